```python
import jax, jax.numpy as jnp
from jax import lax
import numpy as np

D_MODEL = 1024
BATCH = 8
SEQ = 2048
DEPTH = 1

SSD_EXPAND = 2
D_INNER = SSD_EXPAND * D_MODEL
SSD_HEAD_DIM = 64
SSD_HEADS = D_INNER // SSD_HEAD_DIM
SSD_GROUPS = 4
SSD_HEADS_PER_GROUP = SSD_HEADS // SSD_GROUPS
D_STATE = 128
CONV_WIDTH = 4
CHUNK = 128
D_CONV = D_INNER + 2 * SSD_GROUPS * D_STATE

POOL_WIDTH = D_MODEL
POOL_WINDOWS = (2, 4, 8, 16)
POOL_GROUPS = len(POOL_WINDOWS)
POOL_GROUP_DIM = POOL_WIDTH // POOL_GROUPS

N_BRANCHES = 2
D_IN_PROJ = D_INNER + D_CONV + SSD_HEADS + POOL_WIDTH + N_BRANCHES * D_MODEL

N_EXPERTS = 32
TOP_K = 4
D_EXPERT = D_MODEL
SWIGLU_LIMIT = 7.0
SWIGLU_ALPHA = 1.702
MOE_BLOCK = 128

D_PLE = 256
EPS = 1e-6

kernel_name = "hybrid_ssd_pool_moe_block"


def rms_norm(x, g):
    xf = x.astype(jnp.float32)
    y = xf * lax.rsqrt(jnp.mean(xf * xf, axis=-1, keepdims=True) + EPS)
    return (y * g.astype(jnp.float32)).astype(x.dtype)


def causal_depthwise_conv(u, w, b):
    out = lax.conv_general_dilated(
        u, w[:, None, :], window_strides=(1,), padding=[(CONV_WIDTH - 1, 0)],
        dimension_numbers=("NWC", "WIO", "NWC"), feature_group_count=u.shape[-1])
    return out + b


def ssd_scan(xh, dt, a, bm, cm):
    bsz, seq = xh.shape[:2]
    nc = seq // CHUNK
    G, R, P, N = SSD_GROUPS, SSD_HEADS_PER_GROUP, SSD_HEAD_DIM, D_STATE
    xh = xh.astype(jnp.float32).reshape(bsz, nc, CHUNK, G, R, P)
    dt = dt.reshape(bsz, nc, CHUNK, G, R)
    bm = bm.astype(jnp.float32).reshape(bsz, nc, CHUNK, G, N)
    cm = cm.astype(jnp.float32).reshape(bsz, nc, CHUNK, G, N)
    xdt = xh * dt[..., None]
    a_cum = jnp.moveaxis(jnp.cumsum(dt * a, axis=2), 2, -1)
    causal = jnp.tril(jnp.ones((CHUNK, CHUNK), dtype=bool))
    seg = a_cum[..., :, None] - a_cum[..., None, :]
    decay = jnp.exp(jnp.where(causal, seg, -jnp.inf))
    cb = jnp.einsum("bclgn,bcsgn->bcgls", cm, bm)
    y_diag = jnp.einsum("bcgls,bcgrls,bcsgrp->bclgrp", cb, decay, xdt)
    decay_states = jnp.exp(a_cum[..., -1:] - a_cum)
    states = jnp.einsum("bclgn,bcgrl,bclgrp->bcgrpn", bm, decay_states, xdt)
    chunk_decay = jnp.exp(a_cum[..., -1])

    def step(h, inp):
        s_c, d_c = inp
        return h * d_c[..., None, None] + s_c, h

    h0 = jnp.zeros((bsz, G, R, P, N), jnp.float32)
    _, prev = lax.scan(step, h0, (jnp.moveaxis(states, 1, 0), jnp.moveaxis(chunk_decay, 1, 0)))
    prev = jnp.moveaxis(prev, 0, 1)
    y_off = jnp.einsum("bclgn,bcgrpn,bcgrl->bclgrp", cm, prev, jnp.exp(a_cum))
    return (y_diag + y_off).reshape(bsz, seq, G, R, P)


def ssd_branch(z, xbc, dt_raw, conv_w, conv_b, dt_bias, a_log, d_skip, norm_g, w_out):
    bsz, seq = z.shape[:2]
    G, R, P, N = SSD_GROUPS, SSD_HEADS_PER_GROUP, SSD_HEAD_DIM, D_STATE
    xbc = jax.nn.silu(causal_depthwise_conv(xbc, conv_w, conv_b))
    xs, bm, cm = jnp.split(xbc, [D_INNER, D_INNER + G * N], axis=-1)
    xh = xs.reshape(bsz, seq, G, R, P)
    bm = bm.reshape(bsz, seq, G, N)
    cm = cm.reshape(bsz, seq, G, N)
    dt = jax.nn.softplus(dt_raw.astype(jnp.float32) + dt_bias.astype(jnp.float32)).reshape(bsz, seq, G, R)
    a = -jnp.exp(a_log.astype(jnp.float32)).reshape(G, R)
    y = ssd_scan(xh, dt, a, bm, cm)
    y = y + d_skip.astype(jnp.float32).reshape(G, R)[..., None] * xh.astype(jnp.float32)
    y = y.reshape(bsz, seq, D_INNER) * jax.nn.silu(z.astype(jnp.float32))
    yg = y.reshape(bsz, seq, G, D_INNER // G)
    yg = yg * lax.rsqrt(jnp.mean(yg * yg, axis=-1, keepdims=True) + EPS)
    y = yg.reshape(bsz, seq, D_INNER) * norm_g.astype(jnp.float32)
    return y.astype(z.dtype) @ w_out


def pool_branch(u, w_group, scale):
    bsz, seq = u.shape[:2]
    uf = u.astype(jnp.float32).reshape(bsz, seq, POOL_GROUPS, POOL_GROUP_DIM)
    pos = jnp.arange(seq)
    outs = []
    for gi, w in enumerate(POOL_WINDOWS):
        ug = uf[:, :, gi]
        csum = jnp.cumsum(ug, axis=1)
        shifted = jnp.pad(csum, ((0, 0), (w, 0), (0, 0)))[:, :seq]
        count = jnp.minimum(pos + 1, w).astype(jnp.float32)[None, :, None]
        outs.append((csum - shifted) / count - ug)
    pooled = jnp.stack(outs, axis=2).astype(u.dtype)
    mixed = jnp.einsum("bsgi,gio->bsgo", pooled, w_group)
    return mixed.reshape(bsz, seq, POOL_WIDTH) * scale


def moe_ffn(h, w_router, b_router, w_gate_up, b_gate_up, w_down, b_down):
    bsz, seq, d = h.shape
    n_tok = bsz * seq
    hf = h.reshape(n_tok, d)
    logits = (hf @ w_router + b_router).astype(jnp.float32)
    top_val, top_idx = lax.top_k(logits, TOP_K)
    weights = jax.nn.softmax(top_val, axis=-1)
    n_assign = n_tok * TOP_K
    expert_flat = top_idx.reshape(-1).astype(jnp.int32)
    token_flat = jnp.arange(n_assign, dtype=jnp.int32) // TOP_K
    weight_flat = weights.reshape(-1)
    order = jnp.argsort(expert_flat)
    e_sorted = expert_flat[order]
    counts = jnp.bincount(expert_flat, length=N_EXPERTS).astype(jnp.int32)
    start = jnp.cumsum(counts) - counts
    padded = (counts + MOE_BLOCK - 1) // MOE_BLOCK * MOE_BLOCK
    pend = jnp.cumsum(padded)
    pstart = pend - padded
    rank = jnp.arange(n_assign, dtype=jnp.int32) - start[e_sorted]
    dest = pstart[e_sorted] + rank
    n_rows = n_assign + N_EXPERTS * MOE_BLOCK
    n_blocks = n_rows // MOE_BLOCK
    row_token = jnp.zeros((n_rows,), jnp.int32).at[dest].set(token_flat[order])
    row_weight = jnp.zeros((n_rows,), jnp.float32).at[dest].set(weight_flat[order])
    block_start = jnp.arange(n_blocks, dtype=jnp.int32) * MOE_BLOCK
    block_expert = jnp.minimum(jnp.sum(block_start[:, None] >= pend[None, :], axis=1), N_EXPERTS - 1)

    def expert_block(args):
        tok, e = args
        xb = hf[tok]
        gu = xb @ w_gate_up[e] + b_gate_up[e]
        gate, up = jnp.split(gu, 2, axis=-1)
        gate = jnp.minimum(gate, SWIGLU_LIMIT)
        up = jnp.clip(up, -SWIGLU_LIMIT, SWIGLU_LIMIT)
        act = (up + 1.0) * gate * jax.nn.sigmoid(SWIGLU_ALPHA * gate)
        return act @ w_down[e] + b_down[e]

    y_rows = lax.map(expert_block, (row_token.reshape(n_blocks, MOE_BLOCK), block_expert))
    y_rows = y_rows.reshape(n_rows, d).astype(jnp.float32) * row_weight[:, None]
    out = jnp.zeros((n_tok, d), jnp.float32).at[row_token].add(y_rows)
    return out.astype(h.dtype).reshape(bsz, seq, d)


def setup_inputs(seed: int = 0) -> dict:
    key = jax.random.key(seed)
    ks = jax.random.split(key, 32)
    f32 = jnp.float32
    nrm = lambda k, shape, s: jax.random.normal(k, shape, f32) * s
    gain = lambda k, shape: 1.0 + 0.02 * jax.random.normal(k, shape, f32)
    dt_init = jnp.exp(jax.random.uniform(ks[7], (DEPTH, SSD_HEADS), f32, np.log(1e-3), np.log(1e-1)))
    return {
        "x": nrm(ks[0], (BATCH, SEQ, D_MODEL), 1.0),
        "p": nrm(ks[1], (DEPTH, BATCH, SEQ, D_PLE), 1.0),
        "mix_norm_g": gain(ks[2], (DEPTH, D_MODEL)),
        "w_in": nrm(ks[3], (DEPTH, D_MODEL, D_IN_PROJ), D_MODEL ** -0.5),
        "conv_w": nrm(ks[4], (DEPTH, CONV_WIDTH, D_CONV), CONV_WIDTH ** -0.5),
        "conv_b": nrm(ks[5], (DEPTH, D_CONV), 0.02),
        "dt_bias": dt_init + jnp.log(-jnp.expm1(-dt_init)),
        "a_log": jnp.log(jax.random.uniform(ks[8], (DEPTH, SSD_HEADS), f32, 1.0, 16.0)),
        "d_skip": gain(ks[9], (DEPTH, SSD_HEADS)),
        "ssd_norm_g": gain(ks[10], (DEPTH, D_INNER)),
        "w_ssd_out": nrm(ks[11], (DEPTH, D_INNER, D_MODEL), D_INNER ** -0.5),
        "pool_w": nrm(ks[12], (DEPTH, POOL_GROUPS, POOL_GROUP_DIM, POOL_GROUP_DIM), POOL_GROUP_DIM ** -0.5),
        "pool_scale": gain(ks[13], (DEPTH, POOL_WIDTH)),
        "w_mix_out": nrm(ks[14], (DEPTH, D_MODEL, D_MODEL), D_MODEL ** -0.5),
        "ffn_norm_g": gain(ks[15], (DEPTH, D_MODEL)),
        "w_router": nrm(ks[16], (DEPTH, D_MODEL, N_EXPERTS), D_MODEL ** -0.5),
        "b_router": nrm(ks[17], (DEPTH, N_EXPERTS), 0.01),
        "w_gate_up": nrm(ks[18], (DEPTH, N_EXPERTS, D_MODEL, 2 * D_EXPERT), D_MODEL ** -0.5),
        "b_gate_up": nrm(ks[19], (DEPTH, N_EXPERTS, 2 * D_EXPERT), 0.02),
        "w_down": nrm(ks[20], (DEPTH, N_EXPERTS, D_EXPERT, D_MODEL), D_EXPERT ** -0.5),
        "b_down": nrm(ks[21], (DEPTH, N_EXPERTS, D_MODEL), 0.02),
        "ple_norm_g": gain(ks[22], (DEPTH, D_MODEL)),
        "w_ple_gate": nrm(ks[23], (DEPTH, D_MODEL, D_MODEL), D_MODEL ** -0.5),
        "w_ple_proj": nrm(ks[24], (DEPTH, D_PLE, D_MODEL), D_PLE ** -0.5),
        "final_norm_g": gain(ks[25], (D_MODEL,)),
    }


def reference(x, p, mix_norm_g, w_in, conv_w, conv_b, dt_bias, a_log, d_skip, ssd_norm_g,
              w_ssd_out, pool_w, pool_scale, w_mix_out, ffn_norm_g, w_router, b_router,
              w_gate_up, b_gate_up, w_down, b_down, ple_norm_g, w_ple_gate, w_ple_proj,
              final_norm_g):
    split_at = [D_INNER, D_INNER + D_CONV, D_INNER + D_CONV + SSD_HEADS,
                D_INNER + D_CONV + SSD_HEADS + POOL_WIDTH]
    for i in range(DEPTH):
        h = rms_norm(x, mix_norm_g[i])
        proj = h @ w_in[i]
        z, xbc, dt_raw, u, gates = jnp.split(proj, split_at, axis=-1)
        y_ssd = ssd_branch(z, xbc, dt_raw, conv_w[i], conv_b[i], dt_bias[i], a_log[i],
                           d_skip[i], ssd_norm_g[i], w_ssd_out[i])
        y_pool = pool_branch(u, pool_w[i], pool_scale[i])
        g_ssd, g_pool = jnp.split(jax.nn.sigmoid(gates), N_BRANCHES, axis=-1)
        x = x + (g_ssd * y_ssd + g_pool * y_pool) @ w_mix_out[i]
        h = rms_norm(x, ffn_norm_g[i])
        x = x + moe_ffn(h, w_router[i], b_router[i], w_gate_up[i], b_gate_up[i], w_down[i], b_down[i])
        ple_gate = jax.nn.sigmoid(rms_norm(x, ple_norm_g[i]) @ w_ple_gate[i])
        x = x + ple_gate * (p[i] @ w_ple_proj[i])
    return rms_norm(x, final_norm_g)
```

```python
import functools

import jax
import jax.numpy as jnp
from jax import lax
from jax.experimental import pallas as pl
from jax.experimental.pallas import tpu as pltpu

F32 = jnp.float32
BF16 = jnp.bfloat16

D_MODEL = 1024
D_INNER = 2048
HEAD_DIM = 64
HEADS = 32
GROUPS = 4
HEADS_PER_GROUP = HEADS // GROUPS
GROUP_DIM = D_INNER // GROUPS
D_STATE = 128
CONV_WIDTH = 4
CHUNK = 128
D_BC = 2 * GROUPS * D_STATE
D_CONV = D_INNER + D_BC
POOL_WIDTH = D_MODEL
POOL_WINDOWS = (2, 4, 8, 16)
POOL_GROUP_DIM = POOL_WIDTH // len(POOL_WINDOWS)
N_EXPERTS = 32
TOP_K = 4
D_EXPERT = D_MODEL
SWIGLU_LIMIT = 7.0
SWIGLU_ALPHA = 1.702
D_PLE = 256
EPS = 1e-6

LANES = 128
CONV_HALO = 8
POOL_HALO = 16
D_PROJ = D_INNER + D_CONV + POOL_WIDTH + 2 * D_MODEL
MOE_ROWS = 256
VMEM_LIMIT = 56 * 1024 * 1024


def _split2(v):
    hi = v.astype(BF16)
    lo = (v - hi.astype(F32)).astype(BF16)
    return hi, lo


def _split3(v):
    hi = v.astype(BF16)
    r = v - hi.astype(F32)
    mid = r.astype(BF16)
    lo = (r - mid.astype(F32)).astype(BF16)
    return hi, mid, lo


def _rms(x, g):
    return x * lax.rsqrt(jnp.mean(x * x, axis=-1, keepdims=True) + EPS) * g


def _in_proj_kernel(x_ref, g_ref, w_ref, wdt_ref, proj_ref, dt_ref, h_sc):
    @pl.when(pl.program_id(1) == 0)
    def _():
        h = _rms(x_ref[...], g_ref[...])
        hi, lo = _split2(h)
        h_sc[...] = hi
        lhs = jnp.concatenate([hi, lo, hi], axis=1)
        dt_ref[...] = jnp.dot(lhs, wdt_ref[...], preferred_element_type=F32)

    proj_ref[...] = jnp.dot(h_sc[...], w_ref[...], preferred_element_type=F32).astype(BF16)


def _in_proj(x2d, g, w_main, wdt3, tm=1024, tn=1024):
    t = x2d.shape[0]
    return pl.pallas_call(
        _in_proj_kernel,
        grid=(t // tm, D_PROJ // tn),
        in_specs=[
            pl.BlockSpec((tm, D_MODEL), lambda i, j: (i, 0)),
            pl.BlockSpec((1, D_MODEL), lambda i, j: (0, 0)),
            pl.BlockSpec((D_MODEL, tn), lambda i, j: (0, j)),
            pl.BlockSpec((3 * D_MODEL, LANES), lambda i, j: (0, 0)),
        ],
        out_specs=[
            pl.BlockSpec((tm, tn), lambda i, j: (i, j)),
            pl.BlockSpec((tm, LANES), lambda i, j: (i, 0)),
        ],
        out_shape=[
            jax.ShapeDtypeStruct((t, D_PROJ), BF16),
            jax.ShapeDtypeStruct((t, LANES), F32),
        ],
        scratch_shapes=[pltpu.VMEM((tm, D_MODEL), BF16)],
        compiler_params=pltpu.CompilerParams(
            dimension_semantics=("arbitrary", "arbitrary"), vmem_limit_bytes=VMEM_LIMIT),
        name="in_proj",
    )(x2d, g, w_main, wdt3)


def _mixers_kernel(z_ref, xs_ref, bc_ref, u_ref, dt_ref,
                   cw_ref, cb_ref, dtb_ref, alog_ref, dskip_ref, ng_ref, ltri_ref, e2_ref,
                   pw_ref, ps_ref,
                   yssd_ref, ypool_ref,
                   ext_sc, extu_sc, state_sc):
    c = pl.program_id(1)

    @pl.when(c == 0)
    def _():
        ext_sc[0:CONV_HALO, :] = jnp.zeros((CONV_HALO, D_CONV), F32)
        extu_sc[0:POOL_HALO, :] = jnp.zeros((POOL_HALO, POOL_WIDTH), F32)
        state_sc[...] = jnp.zeros_like(state_sc)

    @pl.when(c > 0)
    def _():
        ext_sc[0:CONV_HALO, :] = ext_sc[CHUNK:CHUNK + CONV_HALO, :]
        extu_sc[0:POOL_HALO, :] = extu_sc[CHUNK:CHUNK + POOL_HALO, :]

    ext_sc[CONV_HALO:CONV_HALO + CHUNK, 0:D_INNER] = xs_ref[...].astype(F32)
    ext_sc[CONV_HALO:CONV_HALO + CHUNK, D_INNER:D_CONV] = bc_ref[...].astype(F32)
    conv = cb_ref[...]
    for k in range(CONV_WIDTH):
        off = CONV_HALO - (CONV_WIDTH - 1) + k
        conv = conv + cw_ref[k:k + 1, :] * ext_sc[off:off + CHUNK, :]
    xc = conv * jax.nn.sigmoid(conv)
    xs = xc[:, 0:D_INNER]
    xs_b = xs.astype(BF16)

    dtv = jax.nn.softplus(dt_ref[...] + dtb_ref[...])
    da = dtv * (-jnp.exp(alog_ref[...]))
    a_cum = jnp.dot(ltri_ref[...], jnp.concatenate(_split3(da), axis=0),
                    preferred_element_type=F32)
    expa = jnp.exp(a_cum)
    a_last = a_cum[CHUNK - 1:CHUNK, :]
    wst = dtv * jnp.exp(a_last - a_cum)
    a_cum_t = a_cum.T
    dt_t = dtv.T

    both = jnp.concatenate([wst, expa], axis=0)
    hi, lo = _split2(both)
    expd = jnp.dot(jnp.concatenate([hi, lo], axis=1), e2_ref[...],
                   preferred_element_type=F32)
    wst_x = expd[0:CHUNK, :]
    expa_x = expd[CHUNK:2 * CHUNK, :]
    xw_b = (xs * wst_x).astype(BF16)

    row = lax.broadcasted_iota(jnp.int32, (CHUNK, CHUNK), 0)
    col = lax.broadcasted_iota(jnp.int32, (CHUNK, CHUNK), 1)
    causal = row >= col
    lane = lax.broadcasted_iota(jnp.int32, (CHUNK, LANES), 1)
    low_half = lane < HEAD_DIM

    y_groups = []
    for g in range(GROUPS):
        bg = xc[:, D_INNER + g * D_STATE:D_INNER + (g + 1) * D_STATE]
        cg = xc[:, D_INNER + GROUPS * D_STATE + g * D_STATE:D_INNER + GROUPS * D_STATE + (g + 1) * D_STATE]
        bg_b = bg.astype(BF16)
        cg_b = cg.astype(BF16)
        cbm = lax.dot_general(cg_b, bg_b, (((1,), (1,)), ((), ())), preferred_element_type=F32)
        gsl = slice(g * GROUP_DIM, (g + 1) * GROUP_DIM)

        prev_t = state_sc[g]
        y_off = jnp.dot(cg_b, prev_t.astype(BF16), preferred_element_type=F32) * expa_x[:, gsl]
        st_t = jnp.dot(bg.T.astype(BF16), xw_b[:, gsl], preferred_element_type=F32)
        state_sc[g] = prev_t * expa_x[CHUNK - 1:CHUNK, gsl] + st_t

        pairs = []
        for jp in range(HEADS_PER_GROUP // 2):
            ms = []
            for hh in range(2):
                h = g * HEADS_PER_GROUP + jp * 2 + hh
                seg = a_cum[:, h:h + 1] - a_cum_t[h:h + 1, :]
                dec = jnp.where(causal, jnp.exp(jnp.minimum(seg, 0.0)), 0.0)
                ms.append((cbm * dec * dt_t[h:h + 1, :]).astype(BF16))
            lhs = jnp.concatenate(ms, axis=1)
            c0 = g * GROUP_DIM + jp * LANES
            xp = xs_b[:, c0:c0 + LANES]
            zero = jnp.zeros_like(xp)
            rhs = jnp.concatenate([jnp.where(low_half, xp, zero), jnp.where(low_half, zero, xp)], axis=0)
            pairs.append(jnp.dot(lhs, rhs, preferred_element_type=F32))
        y_diag = jnp.concatenate(pairs, axis=1)

        yg = y_diag + y_off + dskip_ref[:, gsl] * xs[:, gsl]
        zg = z_ref[:, gsl].astype(F32)
        yg = yg * (zg * jax.nn.sigmoid(zg))
        yg = yg * lax.rsqrt(jnp.mean(yg * yg, axis=-1, keepdims=True) + EPS) * ng_ref[:, gsl]
        y_groups.append(yg.astype(BF16))
    yssd_ref[...] = jnp.concatenate(y_groups, axis=1)

    uf = u_ref[...].astype(F32)
    extu_sc[POOL_HALO:POOL_HALO + CHUNK, :] = uf
    pos = c * CHUNK + lax.broadcasted_iota(jnp.int32, (CHUNK, 1), 0)
    outs = []
    for gi, w in enumerate(POOL_WINDOWS):
        psl = slice(gi * POOL_GROUP_DIM, (gi + 1) * POOL_GROUP_DIM)
        s = uf[:, psl]
        for i in range(1, w):
            s = s + extu_sc[POOL_HALO - i:POOL_HALO - i + CHUNK, psl]
        cnt = jnp.minimum(pos + 1, w).astype(F32)
        pooled = s / cnt - uf[:, psl]
        outs.append(jnp.dot(pooled.astype(BF16), pw_ref[gi], preferred_element_type=F32))
    ypool_ref[...] = (jnp.concatenate(outs, axis=1) * ps_ref[...]).astype(BF16)


def _mixers(proj, dt_raw, bsz, seq, cw, cb, dtb, alog, dskip, ng, ltri3, e2, pw, ps):
    nc = seq // CHUNK
    t = bsz * seq
    rowmap = lambda b, c: b * nc + c
    const2 = lambda b, c: (0, 0)
    return pl.pallas_call(
        _mixers_kernel,
        grid=(bsz, nc),
        in_specs=[
            pl.BlockSpec((CHUNK, D_INNER), lambda b, c: (rowmap(b, c), 0)),
            pl.BlockSpec((CHUNK, D_INNER), lambda b, c: (rowmap(b, c), 1)),
            pl.BlockSpec((CHUNK, D_BC), lambda b, c: (rowmap(b, c), 4)),
            pl.BlockSpec((CHUNK, POOL_WIDTH), lambda b, c: (rowmap(b, c), 5)),
            pl.BlockSpec((CHUNK, LANES), lambda b, c: (rowmap(b, c), 0)),
            pl.BlockSpec((CONV_WIDTH, D_CONV), const2),
            pl.BlockSpec((1, D_CONV), const2),
            pl.BlockSpec((1, LANES), const2),
            pl.BlockSpec((1, LANES), const2),
            pl.BlockSpec((1, D_INNER), const2),
            pl.BlockSpec((1, D_INNER), const2),
            pl.BlockSpec((CHUNK, 3 * CHUNK), const2),
            pl.BlockSpec((2 * LANES, D_INNER), const2),
            pl.BlockSpec((len(POOL_WINDOWS), POOL_GROUP_DIM, POOL_GROUP_DIM), lambda b, c: (0, 0, 0)),
            pl.BlockSpec((1, POOL_WIDTH), const2),
        ],
        out_specs=[
            pl.BlockSpec((CHUNK, D_INNER), lambda b, c: (rowmap(b, c), 0)),
            pl.BlockSpec((CHUNK, POOL_WIDTH), lambda b, c: (rowmap(b, c), 0)),
        ],
        out_shape=[
            jax.ShapeDtypeStruct((t, D_INNER), BF16),
            jax.ShapeDtypeStruct((t, POOL_WIDTH), BF16),
        ],
        scratch_shapes=[
            pltpu.VMEM((CONV_HALO + CHUNK, D_CONV), F32),
            pltpu.VMEM((POOL_HALO + CHUNK, POOL_WIDTH), F32),
            pltpu.VMEM((GROUPS, D_STATE, GROUP_DIM), F32),
        ],
        compiler_params=pltpu.CompilerParams(
            dimension_semantics=("arbitrary", "arbitrary"), vmem_limit_bytes=VMEM_LIMIT),
        name="mixers",
    )(proj, proj, proj, proj, dt_raw, cw, cb, dtb, alog, dskip, ng, ltri3, e2, pw, ps)


def _mix_route_kernel(x_ref, yssd_ref, ypool_ref, gates_ref, wso_ref, wmo_ref, fg_ref, wr_ref, br_ref,
                      x1_ref, h2_ref, topw_ref, topi_ref):
    y_ssd = jnp.dot(yssd_ref[...], wso_ref[...], preferred_element_type=F32)
    gates = jax.nn.sigmoid(gates_ref[...].astype(F32))
    mixed = gates[:, 0:D_MODEL] * y_ssd + gates[:, D_MODEL:2 * D_MODEL] * ypool_ref[...].astype(F32)
    x1 = x_ref[...] + jnp.dot(mixed.astype(BF16), wmo_ref[...], preferred_element_type=F32)
    x1_ref[...] = x1
    h2 = _rms(x1, fg_ref[...])
    h2_ref[...] = h2

    hi, lo = _split2(h2)
    logits = jnp.dot(jnp.concatenate([hi, lo, hi], axis=1), wr_ref[...],
                     preferred_element_type=F32) + br_ref[...]
    tm = logits.shape[0]
    lane = lax.broadcasted_iota(jnp.int32, (tm, LANES), 1)
    neg = jnp.float32(-jnp.inf)
    work = jnp.where(lane < N_EXPERTS, logits, neg)
    vals = []
    idxs = []
    for _ in range(TOP_K):
        m = jnp.max(work, axis=-1, keepdims=True)
        idx = jnp.min(jnp.where(work == m, lane, LANES), axis=-1, keepdims=True)
        vals.append(m)
        idxs.append(idx)
        work = jnp.where(lane == idx, neg, work)
    es = [jnp.exp(v - vals[0]) for v in vals]
    den = es[0] + es[1] + es[2] + es[3]
    topw = jnp.zeros((tm, LANES), F32)
    topi = jnp.zeros((tm, LANES), jnp.int32)
    for k in range(TOP_K):
        topw = jnp.where(lane == k, es[k] / den, topw)
        topi = jnp.where(lane == k, idxs[k], topi)
    topw_ref[...] = topw
    topi_ref[...] = topi


def _mix_route(x2d, yssd, ypool, proj, wso, wmo, fg, wr3, br, tm=512):
    t = x2d.shape[0]
    const = lambda i: (0, 0)
    return pl.pallas_call(
        _mix_route_kernel,
        grid=(t // tm,),
        in_specs=[
            pl.BlockSpec((tm, D_MODEL), lambda i: (i, 0)),
            pl.BlockSpec((tm, D_INNER), lambda i: (i, 0)),
            pl.BlockSpec((tm, POOL_WIDTH), lambda i: (i, 0)),
            pl.BlockSpec((tm, 2 * D_MODEL), lambda i: (i, 3)),
            pl.BlockSpec((D_INNER, D_MODEL), const),
            pl.BlockSpec((D_MODEL, D_MODEL), const),
            pl.BlockSpec((1, D_MODEL), const),
            pl.BlockSpec((3 * D_MODEL, LANES), const),
            pl.BlockSpec((1, LANES), const),
        ],
        out_specs=[
            pl.BlockSpec((tm, D_MODEL), lambda i: (i, 0)),
            pl.BlockSpec((tm, D_MODEL), lambda i: (i, 0)),
            pl.BlockSpec((tm, LANES), lambda i: (i, 0)),
            pl.BlockSpec((tm, LANES), lambda i: (i, 0)),
        ],
        out_shape=[
            jax.ShapeDtypeStruct((t, D_MODEL), F32),
            jax.ShapeDtypeStruct((t, D_MODEL), F32),
            jax.ShapeDtypeStruct((t, LANES), F32),
            jax.ShapeDtypeStruct((t, LANES), jnp.int32),
        ],
        compiler_params=pltpu.CompilerParams(
            dimension_semantics=("arbitrary",), vmem_limit_bytes=VMEM_LIMIT),
        name="mix_route",
    )(x2d, yssd, ypool, proj, wso, wmo, fg, wr3, br)


def _moe_kernel(be_ref, nvalid_ref, nreal_ref,
                tok0_ref, toknext_ref, dst_ref,
                h_hbm, wgu_ref, bgu_ref, wd_ref, bd_ref,
                slots_hbm,
                xbuf, ybuf, wgu_bf, wd_bf, gsem, ssem):
    i = pl.program_id(0)
    nvalid = nvalid_ref[0]
    slot = lax.rem(i, 2)

    def gather_copy(tok, r, s):
        return pltpu.make_async_copy(h_hbm.at[pl.ds(tok, 1)], xbuf.at[s, pl.ds(r, 1)], gsem.at[s])

    def scatter_copy(dst, r, s):
        return pltpu.make_async_copy(ybuf.at[s, pl.ds(r, 1)], slots_hbm.at[pl.ds(dst, 1)], ssem.at[s])

    def start_gather(tok_ref, s, n):
        def body(r, carry):
            gather_copy(tok_ref[0, 0, r], r, s).start()
            return carry
        lax.fori_loop(0, n, body, 0)

    def wait_gather(s, n):
        def body(r, carry):
            gather_copy(0, r, s).wait()
            return carry
        lax.fori_loop(0, n, body, 0)

    def wait_scatter(s, n):
        def body(r, carry):
            scatter_copy(0, r, s).wait()
            return carry
        lax.fori_loop(0, n, body, 0)

    @pl.when(i == 0)
    def _():
        xbuf[...] = jnp.zeros_like(xbuf)
        start_gather(tok0_ref, 0, nreal_ref[0])

    @pl.when(i + 1 < nvalid)
    def _():
        start_gather(toknext_ref, 1 - slot, nreal_ref[i + 1])

    @pl.when(i < nvalid)
    def _():
        nreal = nreal_ref[i]
        prev_e = be_ref[jnp.maximum(i - 1, 0)]

        @pl.when(jnp.logical_or(i == 0, be_ref[i] != prev_e))
        def _():
            wgu_bf[...] = wgu_ref[0].astype(BF16)
            wd_bf[...] = wd_ref[0].astype(BF16)

        wait_gather(slot, nreal)

        @pl.when(i >= 2)
        def _():
            wait_scatter(slot, nreal_ref[i - 2])

        xb = xbuf[slot].astype(BF16)
        gu = jnp.dot(xb, wgu_bf[...], preferred_element_type=F32) + bgu_ref[0]
        gate = jnp.minimum(gu[:, 0:D_EXPERT], SWIGLU_LIMIT)
        up = jnp.clip(gu[:, D_EXPERT:2 * D_EXPERT], -SWIGLU_LIMIT, SWIGLU_LIMIT)
        act = (up + 1.0) * gate * jax.nn.sigmoid(SWIGLU_ALPHA * gate)
        ybuf[slot] = jnp.dot(act.astype(BF16), wd_bf[...], preferred_element_type=F32) + bd_ref[0]

        def body(r, carry):
            scatter_copy(dst_ref[0, 0, r], r, slot).start()
            return carry
        lax.fori_loop(0, nreal, body, 0)

        @pl.when(i == nvalid - 1)
        def _():
            wait_scatter(slot, nreal)

            @pl.when(i >= 1)
            def _():
                wait_scatter(1 - slot, nreal_ref[i - 1])


def _moe(block_expert, nvalid, nreal, row_token3, row_dest3, h2, wgu, bgu3, wd, bd3, n_slot_rows):
    nb = block_expert.shape[0]
    grid_spec = pltpu.PrefetchScalarGridSpec(
        num_scalar_prefetch=3,
        grid=(nb,),
        in_specs=[
            pl.BlockSpec((1, 1, MOE_ROWS), lambda i, be, nv, nr: (0, 0, 0), memory_space=pltpu.SMEM),
            pl.BlockSpec((1, 1, MOE_ROWS), lambda i, be, nv, nr: (jnp.minimum(i + 1, nb - 1), 0, 0),
                         memory_space=pltpu.SMEM),
            pl.BlockSpec((1, 1, MOE_ROWS), lambda i, be, nv, nr: (i, 0, 0), memory_space=pltpu.SMEM),
            pl.BlockSpec(memory_space=pl.ANY),
            pl.BlockSpec((1, D_MODEL, 2 * D_EXPERT), lambda i, be, nv, nr: (be[i], 0, 0)),
            pl.BlockSpec((1, 1, 2 * D_EXPERT), lambda i, be, nv, nr: (be[i], 0, 0)),
            pl.BlockSpec((1, D_EXPERT, D_MODEL), lambda i, be, nv, nr: (be[i], 0, 0)),
            pl.BlockSpec((1, 1, D_MODEL), lambda i, be, nv, nr: (be[i], 0, 0)),
        ],
        out_specs=pl.BlockSpec(memory_space=pl.ANY),
        scratch_shapes=[
            pltpu.VMEM((2, MOE_ROWS, D_MODEL), F32),
            pltpu.VMEM((2, MOE_ROWS, D_MODEL), F32),
            pltpu.VMEM((D_MODEL, 2 * D_EXPERT), BF16),
            pltpu.VMEM((D_EXPERT, D_MODEL), BF16),
            pltpu.SemaphoreType.DMA((2,)),
            pltpu.SemaphoreType.DMA((2,)),
        ],
    )
    return pl.pallas_call(
        _moe_kernel,
        grid_spec=grid_spec,
        out_shape=jax.ShapeDtypeStruct((n_slot_rows, D_MODEL), F32),
        compiler_params=pltpu.CompilerParams(
            dimension_semantics=("arbitrary",), vmem_limit_bytes=VMEM_LIMIT),
        name="moe_experts",
    )(block_expert, nvalid, nreal, row_token3, row_token3, row_dest3, h2, wgu, bgu3, wd, bd3)


def _combine_kernel(x1_ref, slots_ref, topw_ref, p_ref, pg_ref, wpg_ref, wpp_ref, fg_ref, out_ref):
    x2 = x1_ref[...]
    topw = topw_ref[...]
    for k in range(TOP_K):
        x2 = x2 + slots_ref[:, k * D_MODEL:(k + 1) * D_MODEL] * topw[:, k:k + 1]
    n = _rms(x2, pg_ref[...])
    gate = jax.nn.sigmoid(jnp.dot(n.astype(BF16), wpg_ref[...], preferred_element_type=F32))
    pp = jnp.dot(p_ref[...].astype(BF16), wpp_ref[...], preferred_element_type=F32)
    x3 = x2 + gate * pp
    out_ref[...] = _rms(x3, fg_ref[...])


def _combine(x1, slots4, topw, p2d, pg, wpg, wpp, fg, tm=512):
    t = x1.shape[0]
    const = lambda i: (0, 0)
    return pl.pallas_call(
        _combine_kernel,
        grid=(t // tm,),
        in_specs=[
            pl.BlockSpec((tm, D_MODEL), lambda i: (i, 0)),
            pl.BlockSpec((tm, TOP_K * D_MODEL), lambda i: (i, 0)),
            pl.BlockSpec((tm, LANES), lambda i: (i, 0)),
            pl.BlockSpec((tm, D_PLE), lambda i: (i, 0)),
            pl.BlockSpec((1, D_MODEL), const),
            pl.BlockSpec((D_MODEL, D_MODEL), const),
            pl.BlockSpec((D_PLE, D_MODEL), const),
            pl.BlockSpec((1, D_MODEL), const),
        ],
        out_specs=pl.BlockSpec((tm, D_MODEL), lambda i: (i, 0)),
        out_shape=jax.ShapeDtypeStruct((t, D_MODEL), F32),
        compiler_params=pltpu.CompilerParams(
            dimension_semantics=("arbitrary",), vmem_limit_bytes=VMEM_LIMIT),
        name="combine_ple",
    )(x1, slots4, topw, p2d, pg, wpg, wpp, fg)


def _routing_tables(top_idx, n_tok):
    n_assign = n_tok * TOP_K
    expert_flat = top_idx.reshape(-1)
    order = jnp.argsort(expert_flat, stable=True).astype(jnp.int32)
    e_sorted = expert_flat[order]
    counts = jnp.bincount(expert_flat, length=N_EXPERTS).astype(jnp.int32)
    start = jnp.cumsum(counts) - counts
    padded = (counts + MOE_ROWS - 1) // MOE_ROWS * MOE_ROWS
    pend = jnp.cumsum(padded)
    pstart = pend - padded
    rank = jnp.arange(n_assign, dtype=jnp.int32) - start[e_sorted]
    dest = pstart[e_sorted] + rank
    n_rows = n_assign + N_EXPERTS * MOE_ROWS
    n_blocks = n_rows // MOE_ROWS
    row_token = jnp.zeros((n_rows,), jnp.int32).at[dest].set(order // TOP_K)
    row_dest = jnp.zeros((n_rows,), jnp.int32).at[dest].set(order)
    block_start = jnp.arange(n_blocks, dtype=jnp.int32) * MOE_ROWS
    block_expert = jnp.minimum(jnp.sum(block_start[:, None] >= pend[None, :], axis=1),
                               N_EXPERTS - 1).astype(jnp.int32)
    nvalid = (pend[-1] // MOE_ROWS).astype(jnp.int32).reshape(1)
    nreal = jnp.clip((pstart + counts)[block_expert] - block_start, 0, MOE_ROWS).astype(jnp.int32)
    return (block_expert, nvalid, nreal, row_token.reshape(n_blocks, 1, MOE_ROWS),
            row_dest.reshape(n_blocks, 1, MOE_ROWS), n_assign)


def _layer(x2d, p2d, bsz, seq, mix_norm_g, w_in, conv_w, conv_b, dt_bias, a_log, d_skip, ssd_norm_g,
           w_ssd_out, pool_w, pool_scale, w_mix_out, ffn_norm_g, w_router, b_router,
           w_gate_up, b_gate_up, w_down, b_down, ple_norm_g, w_ple_gate, w_ple_proj, out_g):
    n_tok = x2d.shape[0]
    dt0 = D_INNER + D_CONV
    w_main = jnp.concatenate([w_in[:, :dt0], w_in[:, dt0 + HEADS:]], axis=1).astype(BF16)
    w_dt = jnp.pad(w_in[:, dt0:dt0 + HEADS], ((0, 0), (0, LANES - HEADS)))
    wdt_hi, wdt_lo = _split2(w_dt)
    wdt3 = jnp.concatenate([wdt_hi, wdt_hi, wdt_lo], axis=0)

    proj, dt_raw = _in_proj(x2d, mix_norm_g[None, :], w_main, wdt3)

    pad_h = lambda v: jnp.pad(v, (0, LANES - HEADS))[None, :]
    ltri = (jnp.arange(CHUNK)[:, None] >= jnp.arange(CHUNK)[None, :]).astype(BF16)
    ltri3 = jnp.concatenate([ltri, ltri, ltri], axis=1)
    e1 = (jnp.arange(LANES)[:, None] == (jnp.arange(D_INNER) // HEAD_DIM)[None, :]).astype(BF16)
    e2 = jnp.concatenate([e1, e1], axis=0)
    yssd, ypool = _mixers(
        proj, dt_raw, bsz, seq, conv_w, conv_b[None, :], pad_h(dt_bias), pad_h(a_log),
        jnp.repeat(d_skip, HEAD_DIM)[None, :], ssd_norm_g[None, :], ltri3, e2,
        pool_w.astype(BF16), pool_scale[None, :])

    wr = jnp.pad(w_router, ((0, 0), (0, LANES - N_EXPERTS)))
    wr_hi, wr_lo = _split2(wr)
    wr3 = jnp.concatenate([wr_hi, wr_hi, wr_lo], axis=0)
    br = jnp.pad(b_router, (0, LANES - N_EXPERTS))[None, :]
    x1, h2, topw, topi = _mix_route(x2d, yssd, ypool, proj, w_ssd_out.astype(BF16),
                                    w_mix_out.astype(BF16), ffn_norm_g[None, :], wr3, br)

    block_expert, nvalid, nreal, row_token3, row_dest3, n_slot_rows = _routing_tables(topi[:, :TOP_K], n_tok)
    slots = _moe(block_expert, nvalid, nreal, row_token3, row_dest3, h2, w_gate_up, b_gate_up[:, None, :],
                 w_down, b_down[:, None, :], n_slot_rows)
    slots4 = slots.reshape(n_slot_rows // TOP_K, TOP_K * D_MODEL)

    return _combine(x1, slots4, topw, p2d, ple_norm_g[None, :], w_ple_gate.astype(BF16),
                    w_ple_proj.astype(BF16), out_g[None, :])


def kernel(x, p, mix_norm_g, w_in, conv_w, conv_b, dt_bias, a_log, d_skip, ssd_norm_g, w_ssd_out, pool_w,
           pool_scale, w_mix_out, ffn_norm_g, w_router, b_router, w_gate_up, b_gate_up, w_down, b_down,
           ple_norm_g, w_ple_gate, w_ple_proj, final_norm_g):
    bsz, seq, d = x.shape
    depth = p.shape[0]
    assert depth == 1 and d == D_MODEL and seq % CHUNK == 0
    x2d = x.reshape(bsz * seq, d)
    out = _layer(x2d, p[0].reshape(bsz * seq, D_PLE), bsz, seq, mix_norm_g[0], w_in[0], conv_w[0], conv_b[0],
                 dt_bias[0], a_log[0], d_skip[0], ssd_norm_g[0], w_ssd_out[0], pool_w[0], pool_scale[0],
                 w_mix_out[0], ffn_norm_g[0], w_router[0], b_router[0], w_gate_up[0], b_gate_up[0],
                 w_down[0], b_down[0], ple_norm_g[0], w_ple_gate[0], w_ple_proj[0], final_norm_g)
    return out.reshape(bsz, seq, d)
```

```python
import functools

import jax
import jax.numpy as jnp
from jax import lax
from jax.experimental import pallas as pl
from jax.experimental.pallas import tpu as pltpu

F32 = jnp.float32
BF16 = jnp.bfloat16

D_MODEL = 1024
D_INNER = 2048
HEAD_DIM = 64
HEADS = 32
GROUPS = 4
HEADS_PER_GROUP = HEADS // GROUPS
GROUP_DIM = D_INNER // GROUPS
D_STATE = 128
CONV_WIDTH = 4
CHUNK = 128
D_BC = 2 * GROUPS * D_STATE
D_CONV = D_INNER + D_BC
POOL_WIDTH = D_MODEL
POOL_WINDOWS = (2, 4, 8, 16)
POOL_GROUP_DIM = POOL_WIDTH // len(POOL_WINDOWS)
N_EXPERTS = 32
TOP_K = 4
D_EXPERT = D_MODEL
SWIGLU_LIMIT = 7.0
SWIGLU_ALPHA = 1.702
D_PLE = 256
EPS = 1e-6

LANES = 128
HALO = 16
D_PROJ = D_INNER + D_CONV + POOL_WIDTH + 2 * D_MODEL
MOE_ROWS = 256
ROW_UNROLL = 8
VMEM_LIMIT = 56 * 1024 * 1024


def _split2(v):
    hi = v.astype(BF16)
    lo = (v - hi.astype(F32)).astype(BF16)
    return hi, lo


def _split3(v):
    hi = v.astype(BF16)
    r = v - hi.astype(F32)
    mid = r.astype(BF16)
    lo = (r - mid.astype(F32)).astype(BF16)
    return hi, mid, lo


def _rms(x, g):
    return x * lax.rsqrt(jnp.mean(x * x, axis=-1, keepdims=True) + EPS) * g


def _in_proj_kernel(x_ref, g_ref, w_ref, wdt_ref, proj_ref, dt_ref, h_sc):
    @pl.when(pl.program_id(1) == 0)
    def _():
        h = _rms(x_ref[...], g_ref[...])
        hi, lo = _split2(h)
        h_sc[...] = hi
        lhs = jnp.concatenate([hi, lo, hi], axis=1)
        dt_ref[...] = jnp.dot(lhs, wdt_ref[...], preferred_element_type=F32)

    proj_ref[...] = jnp.dot(h_sc[...], w_ref[...], preferred_element_type=F32).astype(BF16)


def _in_proj(x2d, g, w_main, wdt3, tm=1024, tn=1024):
    t = x2d.shape[0]
    return pl.pallas_call(
        _in_proj_kernel,
        grid=(t // tm, D_PROJ // tn),
        in_specs=[
            pl.BlockSpec((tm, D_MODEL), lambda i, j: (i, 0)),
            pl.BlockSpec((1, D_MODEL), lambda i, j: (0, 0)),
            pl.BlockSpec((D_MODEL, tn), lambda i, j: (0, j)),
            pl.BlockSpec((3 * D_MODEL, LANES), lambda i, j: (0, 0)),
        ],
        out_specs=[
            pl.BlockSpec((tm, tn), lambda i, j: (i, j)),
            pl.BlockSpec((tm, LANES), lambda i, j: (i, 0)),
        ],
        out_shape=[
            jax.ShapeDtypeStruct((t, D_PROJ), BF16),
            jax.ShapeDtypeStruct((t, LANES), F32),
        ],
        scratch_shapes=[pltpu.VMEM((tm, D_MODEL), BF16)],
        compiler_params=pltpu.CompilerParams(
            dimension_semantics=("arbitrary", "arbitrary"), vmem_limit_bytes=VMEM_LIMIT),
        name="in_proj",
    )(x2d, g, w_main, wdt3)


def _mixers_kernel(z_ref, xs_ref, bc_ref, u_ref, dt_ref,
                   cw_ref, cb_ref, dtb_ref, alog_ref, dskip_ref, ng_ref, ltri_ref, e2_ref,
                   shift_ref, band_ref, pw_ref, ps_ref,
                   yssd_ref, ypool_ref,
                   ext_sc, extu_sc, state_sc):
    c = pl.program_id(1)

    @pl.when(c == 0)
    def _():
        ext_sc[0:HALO, :] = jnp.zeros((HALO, D_CONV), BF16)
        extu_sc[0:HALO, :] = jnp.zeros((HALO, POOL_WIDTH), BF16)
        state_sc[...] = jnp.zeros_like(state_sc)

    @pl.when(c > 0)
    def _():
        ext_sc[0:HALO, :] = ext_sc[CHUNK:CHUNK + HALO, :]
        extu_sc[0:HALO, :] = extu_sc[CHUNK:CHUNK + HALO, :]

    ext_sc[HALO:HALO + CHUNK, 0:D_INNER] = xs_ref[...]
    ext_sc[HALO:HALO + CHUNK, D_INNER:D_CONV] = bc_ref[...]
    ext = ext_sc[...]
    conv = cb_ref[...] + cw_ref[CONV_WIDTH - 1:CONV_WIDTH, :] * ext[HALO:HALO + CHUNK, :].astype(F32)
    for k in range(CONV_WIDTH - 1):
        conv = conv + cw_ref[k:k + 1, :] * jnp.dot(shift_ref[k], ext, preferred_element_type=F32)
    xc = conv * jax.nn.sigmoid(conv)
    xs = xc[:, 0:D_INNER]
    xs_b = xs.astype(BF16)

    dtv = jax.nn.softplus(dt_ref[...] + dtb_ref[...])
    da = dtv * (-jnp.exp(alog_ref[...]))
    a_cum = jnp.dot(ltri_ref[...], jnp.concatenate(_split3(da), axis=0),
                    preferred_element_type=F32)
    expa = jnp.exp(a_cum)
    a_last = a_cum[CHUNK - 1:CHUNK, :]
    wst = dtv * jnp.exp(a_last - a_cum)
    a_cum_t = a_cum.T
    dt_t = dtv.T

    both = jnp.concatenate([wst, expa], axis=0)
    hi, lo = _split2(both)
    expd = jnp.dot(jnp.concatenate([hi, lo], axis=1), e2_ref[...],
                   preferred_element_type=F32)
    wst_x = expd[0:CHUNK, :]
    expa_x = expd[CHUNK:2 * CHUNK, :]
    xw_b = (xs * wst_x).astype(BF16)

    row = lax.broadcasted_iota(jnp.int32, (CHUNK, CHUNK), 0)
    col = lax.broadcasted_iota(jnp.int32, (CHUNK, CHUNK), 1)
    causal = row >= col
    lane = lax.broadcasted_iota(jnp.int32, (CHUNK, LANES), 1)
    low_half = lane < HEAD_DIM

    y_groups = []
    for g in range(GROUPS):
        bg = xc[:, D_INNER + g * D_STATE:D_INNER + (g + 1) * D_STATE]
        cg = xc[:, D_INNER + GROUPS * D_STATE + g * D_STATE:D_INNER + GROUPS * D_STATE + (g + 1) * D_STATE]
        bg_b = bg.astype(BF16)
        cg_b = cg.astype(BF16)
        cbm = lax.dot_general(cg_b, bg_b, (((1,), (1,)), ((), ())), preferred_element_type=F32)
        gsl = slice(g * GROUP_DIM, (g + 1) * GROUP_DIM)

        prev_t = state_sc[g]
        y_off = jnp.dot(cg_b, prev_t.astype(BF16), preferred_element_type=F32) * expa_x[:, gsl]
        st_t = jnp.dot(bg.T.astype(BF16), xw_b[:, gsl], preferred_element_type=F32)
        state_sc[g] = prev_t * expa_x[CHUNK - 1:CHUNK, gsl] + st_t

        pairs = []
        for jp in range(HEADS_PER_GROUP // 2):
            ms = []
            for hh in range(2):
                h = g * HEADS_PER_GROUP + jp * 2 + hh
                seg = a_cum[:, h:h + 1] - a_cum_t[h:h + 1, :]
                dec = jnp.where(causal, jnp.exp(jnp.minimum(seg, 0.0)), 0.0)
                ms.append((cbm * dec * dt_t[h:h + 1, :]).astype(BF16))
            lhs = jnp.concatenate(ms, axis=1)
            c0 = g * GROUP_DIM + jp * LANES
            xp = xs_b[:, c0:c0 + LANES]
            zero = jnp.zeros_like(xp)
            rhs = jnp.concatenate([jnp.where(low_half, xp, zero), jnp.where(low_half, zero, xp)], axis=0)
            pairs.append(jnp.dot(lhs, rhs, preferred_element_type=F32))
        y_diag = jnp.concatenate(pairs, axis=1)

        yg = y_diag + y_off + dskip_ref[:, gsl] * xs[:, gsl]
        zg = z_ref[:, gsl].astype(F32)
        yg = yg * (zg * jax.nn.sigmoid(zg))
        yg = yg * lax.rsqrt(jnp.mean(yg * yg, axis=-1, keepdims=True) + EPS) * ng_ref[:, gsl]
        y_groups.append(yg.astype(BF16))
    yssd_ref[...] = jnp.concatenate(y_groups, axis=1)

    extu_sc[HALO:HALO + CHUNK, :] = u_ref[...]
    pos = c * CHUNK + lax.broadcasted_iota(jnp.int32, (CHUNK, 1), 0)
    outs = []
    for gi, w in enumerate(POOL_WINDOWS):
        psl = slice(gi * POOL_GROUP_DIM, (gi + 1) * POOL_GROUP_DIM)
        s = jnp.dot(band_ref[gi], extu_sc[:, psl], preferred_element_type=F32)
        cnt = jnp.minimum(pos + 1, w).astype(F32)
        pooled = s / cnt - u_ref[:, psl].astype(F32)
        outs.append(jnp.dot(pooled.astype(BF16), pw_ref[gi], preferred_element_type=F32))
    ypool_ref[...] = (jnp.concatenate(outs, axis=1) * ps_ref[...]).astype(BF16)


def _mixers(proj, dt_raw, bsz, seq, cw, cb, dtb, alog, dskip, ng, ltri3, e2, pw, ps):
    nc = seq // CHUNK
    t = bsz * seq
    rowmap = lambda b, c: b * nc + c
    const2 = lambda b, c: (0, 0)
    const3 = lambda b, c: (0, 0, 0)
    trow = jnp.arange(CHUNK)[:, None] + HALO
    jcol = jnp.arange(HALO + CHUNK)[None, :]
    shifts = jnp.stack([(jcol == trow - (CONV_WIDTH - 1) + k) for k in range(CONV_WIDTH - 1)]).astype(BF16)
    bands = jnp.stack([(jcol <= trow) & (jcol > trow - w) for w in POOL_WINDOWS]).astype(BF16)
    return pl.pallas_call(
        _mixers_kernel,
        grid=(bsz, nc),
        in_specs=[
            pl.BlockSpec((CHUNK, D_INNER), lambda b, c: (rowmap(b, c), 0)),
            pl.BlockSpec((CHUNK, D_INNER), lambda b, c: (rowmap(b, c), 1)),
            pl.BlockSpec((CHUNK, D_BC), lambda b, c: (rowmap(b, c), 4)),
            pl.BlockSpec((CHUNK, POOL_WIDTH), lambda b, c: (rowmap(b, c), 5)),
            pl.BlockSpec((CHUNK, LANES), lambda b, c: (rowmap(b, c), 0)),
            pl.BlockSpec((CONV_WIDTH, D_CONV), const2),
            pl.BlockSpec((1, D_CONV), const2),
            pl.BlockSpec((1, LANES), const2),
            pl.BlockSpec((1, LANES), const2),
            pl.BlockSpec((1, D_INNER), const2),
            pl.BlockSpec((1, D_INNER), const2),
            pl.BlockSpec((CHUNK, 3 * CHUNK), const2),
            pl.BlockSpec((2 * LANES, D_INNER), const2),
            pl.BlockSpec((CONV_WIDTH - 1, CHUNK, HALO + CHUNK), const3),
            pl.BlockSpec((len(POOL_WINDOWS), CHUNK, HALO + CHUNK), const3),
            pl.BlockSpec((len(POOL_WINDOWS), POOL_GROUP_DIM, POOL_GROUP_DIM), const3),
            pl.BlockSpec((1, POOL_WIDTH), const2),
        ],
        out_specs=[
            pl.BlockSpec((CHUNK, D_INNER), lambda b, c: (rowmap(b, c), 0)),
            pl.BlockSpec((CHUNK, POOL_WIDTH), lambda b, c: (rowmap(b, c), 0)),
        ],
        out_shape=[
            jax.ShapeDtypeStruct((t, D_INNER), BF16),
            jax.ShapeDtypeStruct((t, POOL_WIDTH), BF16),
        ],
        scratch_shapes=[
            pltpu.VMEM((HALO + CHUNK, D_CONV), BF16),
            pltpu.VMEM((HALO + CHUNK, POOL_WIDTH), BF16),
            pltpu.VMEM((GROUPS, D_STATE, GROUP_DIM), F32),
        ],
        compiler_params=pltpu.CompilerParams(
            dimension_semantics=("arbitrary", "arbitrary"), vmem_limit_bytes=VMEM_LIMIT),
        name="mixers",
    )(proj, proj, proj, proj, dt_raw, cw, cb, dtb, alog, dskip, ng, ltri3, e2, shifts, bands, pw, ps)


def _mix_route_kernel(x_ref, yssd_ref, ypool_ref, gates_ref, wso_ref, wmo_ref, fg_ref, wr_ref, br_ref,
                      x1_ref, h2_ref, topw_ref, topi_ref):
    y_ssd = jnp.dot(yssd_ref[...], wso_ref[...], preferred_element_type=F32)
    gates = jax.nn.sigmoid(gates_ref[...].astype(F32))
    mixed = gates[:, 0:D_MODEL] * y_ssd + gates[:, D_MODEL:2 * D_MODEL] * ypool_ref[...].astype(F32)
    x1 = x_ref[...] + jnp.dot(mixed.astype(BF16), wmo_ref[...], preferred_element_type=F32)
    x1_ref[...] = x1
    h2 = _rms(x1, fg_ref[...])
    h2_ref[...] = h2

    hi, lo = _split2(h2)
    logits = jnp.dot(jnp.concatenate([hi, lo, hi], axis=1), wr_ref[...],
                     preferred_element_type=F32) + br_ref[...]
    tm = logits.shape[0]
    lane = lax.broadcasted_iota(jnp.int32, (tm, LANES), 1)
    neg = jnp.float32(-jnp.inf)
    work = jnp.where(lane < N_EXPERTS, logits, neg)
    vals = []
    idxs = []
    for _ in range(TOP_K):
        m = jnp.max(work, axis=-1, keepdims=True)
        idx = jnp.min(jnp.where(work == m, lane, LANES), axis=-1, keepdims=True)
        vals.append(m)
        idxs.append(idx)
        work = jnp.where(lane == idx, neg, work)
    es = [jnp.exp(v - vals[0]) for v in vals]
    den = es[0] + es[1] + es[2] + es[3]
    topw = jnp.zeros((tm, LANES), F32)
    topi = jnp.zeros((tm, LANES), jnp.int32)
    for k in range(TOP_K):
        topw = jnp.where(lane == k, es[k] / den, topw)
        topi = jnp.where(lane == k, idxs[k], topi)
    topw_ref[...] = topw
    topi_ref[...] = topi


def _mix_route(x2d, yssd, ypool, proj, wso, wmo, fg, wr3, br, tm=512):
    t = x2d.shape[0]
    const = lambda i: (0, 0)
    return pl.pallas_call(
        _mix_route_kernel,
        grid=(t // tm,),
        in_specs=[
            pl.BlockSpec((tm, D_MODEL), lambda i: (i, 0)),
            pl.BlockSpec((tm, D_INNER), lambda i: (i, 0)),
            pl.BlockSpec((tm, POOL_WIDTH), lambda i: (i, 0)),
            pl.BlockSpec((tm, 2 * D_MODEL), lambda i: (i, 3)),
            pl.BlockSpec((D_INNER, D_MODEL), const),
            pl.BlockSpec((D_MODEL, D_MODEL), const),
            pl.BlockSpec((1, D_MODEL), const),
            pl.BlockSpec((3 * D_MODEL, LANES), const),
            pl.BlockSpec((1, LANES), const),
        ],
        out_specs=[
            pl.BlockSpec((tm, D_MODEL), lambda i: (i, 0)),
            pl.BlockSpec((tm, D_MODEL), lambda i: (i, 0)),
            pl.BlockSpec((tm, LANES), lambda i: (i, 0)),
            pl.BlockSpec((tm, LANES), lambda i: (i, 0)),
        ],
        out_shape=[
            jax.ShapeDtypeStruct((t, D_MODEL), F32),
            jax.ShapeDtypeStruct((t, D_MODEL), F32),
            jax.ShapeDtypeStruct((t, LANES), F32),
            jax.ShapeDtypeStruct((t, LANES), jnp.int32),
        ],
        compiler_params=pltpu.CompilerParams(
            dimension_semantics=("arbitrary",), vmem_limit_bytes=VMEM_LIMIT),
        name="mix_route",
    )(x2d, yssd, ypool, proj, wso, wmo, fg, wr3, br)


def _moe_kernel(be_ref, nvalid_ref, nreal_ref,
                tok0_ref, toknext_ref, dst_ref,
                h_hbm, wgu_ref, bgu_ref, wd_ref, bd_ref,
                slots_hbm,
                xbuf, ybuf, wgu_bf, wd_bf, gsem, ssem):
    i = pl.program_id(0)
    nvalid = nvalid_ref[0]
    slot = lax.rem(i, 2)

    def gather_copy(tok, r, s):
        return pltpu.make_async_copy(h_hbm.at[pl.ds(tok, 1)], xbuf.at[s, pl.ds(r, 1)], gsem.at[s])

    def scatter_copy(dst, r, s):
        return pltpu.make_async_copy(ybuf.at[s, pl.ds(r, 1)], slots_hbm.at[pl.ds(dst, 1)], ssem.at[s])

    def for_rows(n, start_one):
        ngroups = lax.shift_right_logical(n, 3)

        def group(gi, carry):
            for u in range(ROW_UNROLL):
                start_one(gi * ROW_UNROLL + u)
            return carry
        lax.fori_loop(0, ngroups, group, 0)

        def single(r, carry):
            start_one(r)
            return carry
        lax.fori_loop(ngroups * ROW_UNROLL, n, single, 0)

    def start_gather(tok_ref, s, n):
        for_rows(n, lambda r: gather_copy(tok_ref[0, 0, r], r, s).start())

    def wait_rows(n, wait_chunk):
        c = MOE_ROWS
        while c >= 1:
            pl.when(lax.bitwise_and(n, c) != 0)(functools.partial(wait_chunk, c))
            c //= 2

    def wait_gather(s, n):
        wait_rows(n, lambda c: pltpu.make_async_copy(
            h_hbm.at[pl.ds(0, c)], xbuf.at[s, pl.ds(0, c)], gsem.at[s]).wait())

    def wait_scatter(s, n):
        wait_rows(n, lambda c: pltpu.make_async_copy(
            ybuf.at[s, pl.ds(0, c)], slots_hbm.at[pl.ds(0, c)], ssem.at[s]).wait())

    @pl.when(i == 0)
    def _():
        xbuf[...] = jnp.zeros_like(xbuf)
        start_gather(tok0_ref, 0, nreal_ref[0])

    @pl.when(i + 1 < nvalid)
    def _():
        start_gather(toknext_ref, 1 - slot, nreal_ref[i + 1])

    @pl.when(i < nvalid)
    def _():
        nreal = nreal_ref[i]
        prev_e = be_ref[jnp.maximum(i - 1, 0)]

        @pl.when(jnp.logical_or(i == 0, be_ref[i] != prev_e))
        def _():
            wgu_bf[...] = wgu_ref[0].astype(BF16)
            wd_bf[...] = wd_ref[0].astype(BF16)

        wait_gather(slot, nreal)

        @pl.when(i >= 2)
        def _():
            wait_scatter(slot, nreal_ref[i - 2])

        xb = xbuf[slot].astype(BF16)
        gu = jnp.dot(xb, wgu_bf[...], preferred_element_type=F32) + bgu_ref[0]
        gate = jnp.minimum(gu[:, 0:D_EXPERT], SWIGLU_LIMIT)
        up = jnp.clip(gu[:, D_EXPERT:2 * D_EXPERT], -SWIGLU_LIMIT, SWIGLU_LIMIT)
        act = (up + 1.0) * gate * jax.nn.sigmoid(SWIGLU_ALPHA * gate)
        ybuf[slot] = jnp.dot(act.astype(BF16), wd_bf[...], preferred_element_type=F32) + bd_ref[0]

        for_rows(nreal, lambda r: scatter_copy(dst_ref[0, 0, r], r, slot).start())

        @pl.when(i == nvalid - 1)
        def _():
            wait_scatter(slot, nreal)

            @pl.when(i >= 1)
            def _():
                wait_scatter(1 - slot, nreal_ref[i - 1])


def _moe(block_expert, nvalid, nreal, row_token3, row_dest3, h2, wgu, bgu3, wd, bd3, n_slot_rows):
    nb = block_expert.shape[0]
    grid_spec = pltpu.PrefetchScalarGridSpec(
        num_scalar_prefetch=3,
        grid=(nb,),
        in_specs=[
            pl.BlockSpec((1, 1, MOE_ROWS), lambda i, be, nv, nr: (0, 0, 0), memory_space=pltpu.SMEM),
            pl.BlockSpec((1, 1, MOE_ROWS), lambda i, be, nv, nr: (jnp.minimum(i + 1, nb - 1), 0, 0),
                         memory_space=pltpu.SMEM),
            pl.BlockSpec((1, 1, MOE_ROWS), lambda i, be, nv, nr: (i, 0, 0), memory_space=pltpu.SMEM),
            pl.BlockSpec(memory_space=pl.ANY),
            pl.BlockSpec((1, D_MODEL, 2 * D_EXPERT), lambda i, be, nv, nr: (be[i], 0, 0)),
            pl.BlockSpec((1, 1, 2 * D_EXPERT), lambda i, be, nv, nr: (be[i], 0, 0)),
            pl.BlockSpec((1, D_EXPERT, D_MODEL), lambda i, be, nv, nr: (be[i], 0, 0)),
            pl.BlockSpec((1, 1, D_MODEL), lambda i, be, nv, nr: (be[i], 0, 0)),
        ],
        out_specs=pl.BlockSpec(memory_space=pl.ANY),
        scratch_shapes=[
            pltpu.VMEM((2, MOE_ROWS, D_MODEL), F32),
            pltpu.VMEM((2, MOE_ROWS, D_MODEL), F32),
            pltpu.VMEM((D_MODEL, 2 * D_EXPERT), BF16),
            pltpu.VMEM((D_EXPERT, D_MODEL), BF16),
            pltpu.SemaphoreType.DMA((2,)),
            pltpu.SemaphoreType.DMA((2,)),
        ],
    )
    return pl.pallas_call(
        _moe_kernel,
        grid_spec=grid_spec,
        out_shape=jax.ShapeDtypeStruct((n_slot_rows, D_MODEL), F32),
        compiler_params=pltpu.CompilerParams(
            dimension_semantics=("arbitrary",), vmem_limit_bytes=VMEM_LIMIT),
        name="moe_experts",
    )(block_expert, nvalid, nreal, row_token3, row_token3, row_dest3, h2, wgu, bgu3, wd, bd3)


def _combine_kernel(x1_ref, s0_ref, s1_ref, s2_ref, s3_ref, topw_ref, p_ref, pg_ref, wpg_ref, wpp_ref, fg_ref,
                    out_ref):
    x2 = x1_ref[...]
    topw = topw_ref[...]
    for k, s_ref in enumerate((s0_ref, s1_ref, s2_ref, s3_ref)):
        x2 = x2 + s_ref[...] * topw[:, k:k + 1]
    n = _rms(x2, pg_ref[...])
    gate = jax.nn.sigmoid(jnp.dot(n.astype(BF16), wpg_ref[...], preferred_element_type=F32))
    pp = jnp.dot(p_ref[...].astype(BF16), wpp_ref[...], preferred_element_type=F32)
    x3 = x2 + gate * pp
    out_ref[...] = _rms(x3, fg_ref[...])


def _combine(x1, slots, topw, p2d, pg, wpg, wpp, fg, tm=512):
    t = x1.shape[0]
    nt = t // tm
    const = lambda i: (0, 0)
    slot_specs = [pl.BlockSpec((tm, D_MODEL), functools.partial(lambda k, i: (k * nt + i, 0), k))
                  for k in range(TOP_K)]
    return pl.pallas_call(
        _combine_kernel,
        grid=(nt,),
        in_specs=[
            pl.BlockSpec((tm, D_MODEL), lambda i: (i, 0)),
            *slot_specs,
            pl.BlockSpec((tm, LANES), lambda i: (i, 0)),
            pl.BlockSpec((tm, D_PLE), lambda i: (i, 0)),
            pl.BlockSpec((1, D_MODEL), const),
            pl.BlockSpec((D_MODEL, D_MODEL), const),
            pl.BlockSpec((D_PLE, D_MODEL), const),
            pl.BlockSpec((1, D_MODEL), const),
        ],
        out_specs=pl.BlockSpec((tm, D_MODEL), lambda i: (i, 0)),
        out_shape=jax.ShapeDtypeStruct((t, D_MODEL), F32),
        compiler_params=pltpu.CompilerParams(
            dimension_semantics=("arbitrary",), vmem_limit_bytes=VMEM_LIMIT),
        name="combine_ple",
    )(x1, slots, slots, slots, slots, topw, p2d, pg, wpg, wpp, fg)


def _routing_tables(top_idx, n_tok):
    n_assign = n_tok * TOP_K
    expert_flat = top_idx.reshape(-1)
    order = jnp.argsort(expert_flat, stable=True).astype(jnp.int32)
    counts = jnp.bincount(expert_flat, length=N_EXPERTS).astype(jnp.int32)
    start = jnp.cumsum(counts) - counts
    padded = (counts + MOE_ROWS - 1) // MOE_ROWS * MOE_ROWS
    pend = jnp.cumsum(padded)
    pstart = pend - padded
    n_rows = n_assign + N_EXPERTS * MOE_ROWS
    n_blocks = n_rows // MOE_ROWS
    block_start = jnp.arange(n_blocks, dtype=jnp.int32) * MOE_ROWS
    block_expert = jnp.minimum(jnp.sum(block_start[:, None] >= pend[None, :], axis=1),
                               N_EXPERTS - 1).astype(jnp.int32)
    nvalid = (pend[-1] // MOE_ROWS).astype(jnp.int32).reshape(1)
    nreal = jnp.clip((pstart + counts)[block_expert] - block_start, 0, MOE_ROWS).astype(jnp.int32)
    sorted_pos = (block_start + (start - pstart)[block_expert])[:, None] + jnp.arange(MOE_ROWS, dtype=jnp.int32)[None, :]
    assign = order[jnp.clip(sorted_pos, 0, n_assign - 1)]
    row_token = assign // TOP_K
    row_dest = (assign % TOP_K) * n_tok + row_token
    return (block_expert, nvalid, nreal, row_token.reshape(n_blocks, 1, MOE_ROWS),
            row_dest.reshape(n_blocks, 1, MOE_ROWS), n_assign)


def _layer(x2d, p2d, bsz, seq, mix_norm_g, w_in, conv_w, conv_b, dt_bias, a_log, d_skip, ssd_norm_g,
           w_ssd_out, pool_w, pool_scale, w_mix_out, ffn_norm_g, w_router, b_router,
           w_gate_up, b_gate_up, w_down, b_down, ple_norm_g, w_ple_gate, w_ple_proj, out_g):
    n_tok = x2d.shape[0]
    dt0 = D_INNER + D_CONV
    w_main = jnp.concatenate([w_in[:, :dt0], w_in[:, dt0 + HEADS:]], axis=1).astype(BF16)
    w_dt = jnp.pad(w_in[:, dt0:dt0 + HEADS], ((0, 0), (0, LANES - HEADS)))
    wdt_hi, wdt_lo = _split2(w_dt)
    wdt3 = jnp.concatenate([wdt_hi, wdt_hi, wdt_lo], axis=0)

    proj, dt_raw = _in_proj(x2d, mix_norm_g[None, :], w_main, wdt3)

    pad_h = lambda v: jnp.pad(v, (0, LANES - HEADS))[None, :]
    ltri = (jnp.arange(CHUNK)[:, None] >= jnp.arange(CHUNK)[None, :]).astype(BF16)
    ltri3 = jnp.concatenate([ltri, ltri, ltri], axis=1)
    e1 = (jnp.arange(LANES)[:, None] == (jnp.arange(D_INNER) // HEAD_DIM)[None, :]).astype(BF16)
    e2 = jnp.concatenate([e1, e1], axis=0)
    yssd, ypool = _mixers(
        proj, dt_raw, bsz, seq, conv_w, conv_b[None, :], pad_h(dt_bias), pad_h(a_log),
        jnp.repeat(d_skip, HEAD_DIM)[None, :], ssd_norm_g[None, :], ltri3, e2,
        pool_w.astype(BF16), pool_scale[None, :])

    wr = jnp.pad(w_router, ((0, 0), (0, LANES - N_EXPERTS)))
    wr_hi, wr_lo = _split2(wr)
    wr3 = jnp.concatenate([wr_hi, wr_hi, wr_lo], axis=0)
    br = jnp.pad(b_router, (0, LANES - N_EXPERTS))[None, :]
    x1, h2, topw, topi = _mix_route(x2d, yssd, ypool, proj, w_ssd_out.astype(BF16),
                                    w_mix_out.astype(BF16), ffn_norm_g[None, :], wr3, br)

    block_expert, nvalid, nreal, row_token3, row_dest3, n_slot_rows = _routing_tables(topi[:, :TOP_K], n_tok)
    slots = _moe(block_expert, nvalid, nreal, row_token3, row_dest3, h2, w_gate_up, b_gate_up[:, None, :],
                 w_down, b_down[:, None, :], n_slot_rows)

    return _combine(x1, slots, topw, p2d, ple_norm_g[None, :], w_ple_gate.astype(BF16),
                    w_ple_proj.astype(BF16), out_g[None, :])


def kernel(x, p, mix_norm_g, w_in, conv_w, conv_b, dt_bias, a_log, d_skip, ssd_norm_g, w_ssd_out, pool_w,
           pool_scale, w_mix_out, ffn_norm_g, w_router, b_router, w_gate_up, b_gate_up, w_down, b_down,
           ple_norm_g, w_ple_gate, w_ple_proj, final_norm_g):
    bsz, seq, d = x.shape
    depth = p.shape[0]
    assert depth == 1 and d == D_MODEL and seq % CHUNK == 0
    x2d = x.reshape(bsz * seq, d)
    out = _layer(x2d, p[0].reshape(bsz * seq, D_PLE), bsz, seq, mix_norm_g[0], w_in[0], conv_w[0], conv_b[0],
                 dt_bias[0], a_log[0], d_skip[0], ssd_norm_g[0], w_ssd_out[0], pool_w[0], pool_scale[0],
                 w_mix_out[0], ffn_norm_g[0], w_router[0], b_router[0], w_gate_up[0], b_gate_up[0],
                 w_down[0], b_down[0], ple_norm_g[0], w_ple_gate[0], w_ple_proj[0], final_norm_g)
    return out.reshape(bsz, seq, d)
```

```python
import functools

import jax
import jax.numpy as jnp
from jax import lax
from jax.experimental import pallas as pl
from jax.experimental.pallas import tpu as pltpu

F32 = jnp.float32
BF16 = jnp.bfloat16

D_MODEL = 1024
D_INNER = 2048
HEAD_DIM = 64
HEADS = 32
GROUPS = 4
HEADS_PER_GROUP = HEADS // GROUPS
GROUP_DIM = D_INNER // GROUPS
D_STATE = 128
CONV_WIDTH = 4
CHUNK = 128
D_BC = 2 * GROUPS * D_STATE
D_CONV = D_INNER + D_BC
POOL_WIDTH = D_MODEL
POOL_WINDOWS = (2, 4, 8, 16)
POOL_GROUP_DIM = POOL_WIDTH // len(POOL_WINDOWS)
N_EXPERTS = 32
TOP_K = 4
D_EXPERT = D_MODEL
SWIGLU_LIMIT = 7.0
SWIGLU_ALPHA = 1.702
D_PLE = 256
EPS = 1e-6

LANES = 128
TILE_ROWS = D_MODEL // LANES
HALO = 16
D_PROJ = D_INNER + D_CONV + POOL_WIDTH + 2 * D_MODEL
MOE_ROWS = 256
ROW_UNROLL = 8
VMEM_LIMIT = 56 * 1024 * 1024


def _split2(v):
    hi = v.astype(BF16)
    lo = (v - hi.astype(F32)).astype(BF16)
    return hi, lo


def _split3(v):
    hi = v.astype(BF16)
    r = v - hi.astype(F32)
    mid = r.astype(BF16)
    lo = (r - mid.astype(F32)).astype(BF16)
    return hi, mid, lo


def _rms(x, g):
    return x * lax.rsqrt(jnp.mean(x * x, axis=-1, keepdims=True) + EPS) * g


def _store_token_tiles(ref2d, val):
    rows = val.shape[0]
    for j in range(TILE_ROWS):
        ref2d[pl.ds(j, rows, stride=TILE_ROWS), :] = val[:, j * LANES:(j + 1) * LANES]


def _load_token_tiles(ref2d, rows):
    return jnp.concatenate([ref2d[pl.ds(j, rows, stride=TILE_ROWS), :] for j in range(TILE_ROWS)], axis=1)


def _in_proj_kernel(x_ref, g_ref, w_ref, wdt_ref, proj_ref, dt_ref, h_sc):
    @pl.when(pl.program_id(1) == 0)
    def _():
        h = _rms(x_ref[...], g_ref[...])
        hi, lo = _split2(h)
        h_sc[...] = hi
        lhs = jnp.concatenate([hi, lo, hi], axis=1)
        dt_ref[...] = jnp.dot(lhs, wdt_ref[...], preferred_element_type=F32)

    proj_ref[...] = jnp.dot(h_sc[...], w_ref[...], preferred_element_type=F32).astype(BF16)


def _in_proj(x2d, g, w_main, wdt3, tm=1024, tn=1024):
    t = x2d.shape[0]
    return pl.pallas_call(
        _in_proj_kernel,
        grid=(t // tm, D_PROJ // tn),
        in_specs=[
            pl.BlockSpec((tm, D_MODEL), lambda i, j: (i, 0)),
            pl.BlockSpec((1, D_MODEL), lambda i, j: (0, 0)),
            pl.BlockSpec((D_MODEL, tn), lambda i, j: (0, j)),
            pl.BlockSpec((3 * D_MODEL, LANES), lambda i, j: (0, 0)),
        ],
        out_specs=[
            pl.BlockSpec((tm, tn), lambda i, j: (i, j)),
            pl.BlockSpec((tm, LANES), lambda i, j: (i, 0)),
        ],
        out_shape=[
            jax.ShapeDtypeStruct((t, D_PROJ), BF16),
            jax.ShapeDtypeStruct((t, LANES), F32),
        ],
        scratch_shapes=[pltpu.VMEM((tm, D_MODEL), BF16)],
        compiler_params=pltpu.CompilerParams(
            dimension_semantics=("arbitrary", "arbitrary"), vmem_limit_bytes=VMEM_LIMIT),
        name="in_proj",
    )(x2d, g, w_main, wdt3)


def _mixers_kernel(z_ref, xs_ref, bc_ref, u_ref, dt_ref,
                   cw_ref, cb_ref, dtb_ref, alog_ref, dskip_ref, ng_ref, ltri_ref, e2_ref,
                   shift_ref, band_ref, pw_ref, ps_ref,
                   yssd_ref, ypool_ref,
                   ext_sc, extu_sc, state_sc):
    c = pl.program_id(1)

    @pl.when(c == 0)
    def _():
        ext_sc[0:HALO, :] = jnp.zeros((HALO, D_CONV), BF16)
        extu_sc[0:HALO, :] = jnp.zeros((HALO, POOL_WIDTH), BF16)
        state_sc[...] = jnp.zeros_like(state_sc)

    @pl.when(c > 0)
    def _():
        ext_sc[0:HALO, :] = ext_sc[CHUNK:CHUNK + HALO, :]
        extu_sc[0:HALO, :] = extu_sc[CHUNK:CHUNK + HALO, :]

    ext_sc[HALO:HALO + CHUNK, 0:D_INNER] = xs_ref[...]
    ext_sc[HALO:HALO + CHUNK, D_INNER:D_CONV] = bc_ref[...]
    ext = ext_sc[...]
    conv = cb_ref[...] + cw_ref[CONV_WIDTH - 1:CONV_WIDTH, :] * ext[HALO:HALO + CHUNK, :].astype(F32)
    for k in range(CONV_WIDTH - 1):
        conv = conv + cw_ref[k:k + 1, :] * jnp.dot(shift_ref[k], ext, preferred_element_type=F32)
    xc = conv * jax.nn.sigmoid(conv)
    xs = xc[:, 0:D_INNER]
    xs_b = xs.astype(BF16)

    dtv = jax.nn.softplus(dt_ref[...] + dtb_ref[...])
    da = dtv * (-jnp.exp(alog_ref[...]))
    a_cum = jnp.dot(ltri_ref[...], jnp.concatenate(_split3(da), axis=0),
                    preferred_element_type=F32)
    expa = jnp.exp(a_cum)
    a_last = a_cum[CHUNK - 1:CHUNK, :]
    wst = dtv * jnp.exp(a_last - a_cum)
    a_cum_t = a_cum.T
    dt_t = dtv.T

    both = jnp.concatenate([wst, expa], axis=0)
    hi, lo = _split2(both)
    expd = jnp.dot(jnp.concatenate([hi, lo], axis=1), e2_ref[...],
                   preferred_element_type=F32)
    wst_x = expd[0:CHUNK, :]
    expa_x = expd[CHUNK:2 * CHUNK, :]
    xw_b = (xs * wst_x).astype(BF16)

    row = lax.broadcasted_iota(jnp.int32, (CHUNK, CHUNK), 0)
    col = lax.broadcasted_iota(jnp.int32, (CHUNK, CHUNK), 1)
    causal = row >= col
    lane = lax.broadcasted_iota(jnp.int32, (CHUNK, LANES), 1)
    low_half = lane < HEAD_DIM

    y_groups = []
    for g in range(GROUPS):
        bg = xc[:, D_INNER + g * D_STATE:D_INNER + (g + 1) * D_STATE]
        cg = xc[:, D_INNER + GROUPS * D_STATE + g * D_STATE:D_INNER + GROUPS * D_STATE + (g + 1) * D_STATE]
        bg_b = bg.astype(BF16)
        cg_b = cg.astype(BF16)
        cbm = lax.dot_general(cg_b, bg_b, (((1,), (1,)), ((), ())), preferred_element_type=F32)
        gsl = slice(g * GROUP_DIM, (g + 1) * GROUP_DIM)

        prev_t = state_sc[g]
        y_off = jnp.dot(cg_b, prev_t.astype(BF16), preferred_element_type=F32) * expa_x[:, gsl]
        st_t = jnp.dot(bg.T.astype(BF16), xw_b[:, gsl], preferred_element_type=F32)
        state_sc[g] = prev_t * expa_x[CHUNK - 1:CHUNK, gsl] + st_t

        pairs = []
        for jp in range(HEADS_PER_GROUP // 2):
            ms = []
            for hh in range(2):
                h = g * HEADS_PER_GROUP + jp * 2 + hh
                seg = a_cum[:, h:h + 1] - a_cum_t[h:h + 1, :]
                dec = jnp.where(causal, jnp.exp(jnp.minimum(seg, 0.0)), 0.0)
                ms.append((cbm * dec * dt_t[h:h + 1, :]).astype(BF16))
            lhs = jnp.concatenate(ms, axis=1)
            c0 = g * GROUP_DIM + jp * LANES
            xp = xs_b[:, c0:c0 + LANES]
            zero = jnp.zeros_like(xp)
            rhs = jnp.concatenate([jnp.where(low_half, xp, zero), jnp.where(low_half, zero, xp)], axis=0)
            pairs.append(jnp.dot(lhs, rhs, preferred_element_type=F32))
        y_diag = jnp.concatenate(pairs, axis=1)

        yg = y_diag + y_off + dskip_ref[:, gsl] * xs[:, gsl]
        zg = z_ref[:, gsl].astype(F32)
        yg = yg * (zg * jax.nn.sigmoid(zg))
        yg = yg * lax.rsqrt(jnp.mean(yg * yg, axis=-1, keepdims=True) + EPS) * ng_ref[:, gsl]
        y_groups.append(yg.astype(BF16))
    yssd_ref[...] = jnp.concatenate(y_groups, axis=1)

    extu_sc[HALO:HALO + CHUNK, :] = u_ref[...]
    pos = c * CHUNK + lax.broadcasted_iota(jnp.int32, (CHUNK, 1), 0)
    outs = []
    for gi, w in enumerate(POOL_WINDOWS):
        psl = slice(gi * POOL_GROUP_DIM, (gi + 1) * POOL_GROUP_DIM)
        s = jnp.dot(band_ref[gi], extu_sc[:, psl], preferred_element_type=F32)
        cnt = jnp.minimum(pos + 1, w).astype(F32)
        pooled = s / cnt - u_ref[:, psl].astype(F32)
        outs.append(jnp.dot(pooled.astype(BF16), pw_ref[gi], preferred_element_type=F32))
    ypool_ref[...] = (jnp.concatenate(outs, axis=1) * ps_ref[...]).astype(BF16)


def _mixers(proj, dt_raw, bsz, seq, cw, cb, dtb, alog, dskip, ng, ltri3, e2, pw, ps):
    nc = seq // CHUNK
    t = bsz * seq
    rowmap = lambda b, c: b * nc + c
    const2 = lambda b, c: (0, 0)
    const3 = lambda b, c: (0, 0, 0)
    trow = jnp.arange(CHUNK)[:, None] + HALO
    jcol = jnp.arange(HALO + CHUNK)[None, :]
    shifts = jnp.stack([(jcol == trow - (CONV_WIDTH - 1) + k) for k in range(CONV_WIDTH - 1)]).astype(BF16)
    bands = jnp.stack([(jcol <= trow) & (jcol > trow - w) for w in POOL_WINDOWS]).astype(BF16)
    return pl.pallas_call(
        _mixers_kernel,
        grid=(bsz, nc),
        in_specs=[
            pl.BlockSpec((CHUNK, D_INNER), lambda b, c: (rowmap(b, c), 0)),
            pl.BlockSpec((CHUNK, D_INNER), lambda b, c: (rowmap(b, c), 1)),
            pl.BlockSpec((CHUNK, D_BC), lambda b, c: (rowmap(b, c), 4)),
            pl.BlockSpec((CHUNK, POOL_WIDTH), lambda b, c: (rowmap(b, c), 5)),
            pl.BlockSpec((CHUNK, LANES), lambda b, c: (rowmap(b, c), 0)),
            pl.BlockSpec((CONV_WIDTH, D_CONV), const2),
            pl.BlockSpec((1, D_CONV), const2),
            pl.BlockSpec((1, LANES), const2),
            pl.BlockSpec((1, LANES), const2),
            pl.BlockSpec((1, D_INNER), const2),
            pl.BlockSpec((1, D_INNER), const2),
            pl.BlockSpec((CHUNK, 3 * CHUNK), const2),
            pl.BlockSpec((2 * LANES, D_INNER), const2),
            pl.BlockSpec((CONV_WIDTH - 1, CHUNK, HALO + CHUNK), const3),
            pl.BlockSpec((len(POOL_WINDOWS), CHUNK, HALO + CHUNK), const3),
            pl.BlockSpec((len(POOL_WINDOWS), POOL_GROUP_DIM, POOL_GROUP_DIM), const3),
            pl.BlockSpec((1, POOL_WIDTH), const2),
        ],
        out_specs=[
            pl.BlockSpec((CHUNK, D_INNER), lambda b, c: (rowmap(b, c), 0)),
            pl.BlockSpec((CHUNK, POOL_WIDTH), lambda b, c: (rowmap(b, c), 0)),
        ],
        out_shape=[
            jax.ShapeDtypeStruct((t, D_INNER), BF16),
            jax.ShapeDtypeStruct((t, POOL_WIDTH), BF16),
        ],
        scratch_shapes=[
            pltpu.VMEM((HALO + CHUNK, D_CONV), BF16),
            pltpu.VMEM((HALO + CHUNK, POOL_WIDTH), BF16),
            pltpu.VMEM((GROUPS, D_STATE, GROUP_DIM), F32),
        ],
        compiler_params=pltpu.CompilerParams(
            dimension_semantics=("arbitrary", "arbitrary"), vmem_limit_bytes=VMEM_LIMIT),
        name="mixers",
    )(proj, proj, proj, proj, dt_raw, cw, cb, dtb, alog, dskip, ng, ltri3, e2, shifts, bands, pw, ps)


def _mix_route_kernel(x_ref, yssd_ref, ypool_ref, gates_ref, wso_ref, wmo_ref, fg_ref, wr_ref, br_ref,
                      x1_ref, h2_ref, topw_ref, topi_ref):
    y_ssd = jnp.dot(yssd_ref[...], wso_ref[...], preferred_element_type=F32)
    gates = jax.nn.sigmoid(gates_ref[...].astype(F32))
    mixed = gates[:, 0:D_MODEL] * y_ssd + gates[:, D_MODEL:2 * D_MODEL] * ypool_ref[...].astype(F32)
    x1 = x_ref[...] + jnp.dot(mixed.astype(BF16), wmo_ref[...], preferred_element_type=F32)
    x1_ref[...] = x1
    h2 = _rms(x1, fg_ref[...])
    _store_token_tiles(h2_ref, h2)

    hi, lo = _split2(h2)
    logits = jnp.dot(jnp.concatenate([hi, lo, hi], axis=1), wr_ref[...],
                     preferred_element_type=F32) + br_ref[...]
    tm = logits.shape[0]
    lane = lax.broadcasted_iota(jnp.int32, (tm, LANES), 1)
    neg = jnp.float32(-jnp.inf)
    work = jnp.where(lane < N_EXPERTS, logits, neg)
    vals = []
    idxs = []
    for _ in range(TOP_K):
        m = jnp.max(work, axis=-1, keepdims=True)
        idx = jnp.min(jnp.where(work == m, lane, LANES), axis=-1, keepdims=True)
        vals.append(m)
        idxs.append(idx)
        work = jnp.where(lane == idx, neg, work)
    es = [jnp.exp(v - vals[0]) for v in vals]
    den = es[0] + es[1] + es[2] + es[3]
    topw = jnp.zeros((tm, LANES), F32)
    topi = jnp.zeros((tm, LANES), jnp.int32)
    for k in range(TOP_K):
        topw = jnp.where(lane == k, es[k] / den, topw)
        topi = jnp.where(lane == k, idxs[k], topi)
    topw_ref[...] = topw
    topi_ref[...] = topi


def _mix_route(x2d, yssd, ypool, proj, wso, wmo, fg, wr3, br, tm=512):
    t = x2d.shape[0]
    const = lambda i: (0, 0)
    return pl.pallas_call(
        _mix_route_kernel,
        grid=(t // tm,),
        in_specs=[
            pl.BlockSpec((tm, D_MODEL), lambda i: (i, 0)),
            pl.BlockSpec((tm, D_INNER), lambda i: (i, 0)),
            pl.BlockSpec((tm, POOL_WIDTH), lambda i: (i, 0)),
            pl.BlockSpec((tm, 2 * D_MODEL), lambda i: (i, 3)),
            pl.BlockSpec((D_INNER, D_MODEL), const),
            pl.BlockSpec((D_MODEL, D_MODEL), const),
            pl.BlockSpec((1, D_MODEL), const),
            pl.BlockSpec((3 * D_MODEL, LANES), const),
            pl.BlockSpec((1, LANES), const),
        ],
        out_specs=[
            pl.BlockSpec((tm, D_MODEL), lambda i: (i, 0)),
            pl.BlockSpec((tm * TILE_ROWS, LANES), lambda i: (i, 0)),
            pl.BlockSpec((tm, LANES), lambda i: (i, 0)),
            pl.BlockSpec((tm, LANES), lambda i: (i, 0)),
        ],
        out_shape=[
            jax.ShapeDtypeStruct((t, D_MODEL), F32),
            jax.ShapeDtypeStruct((t * TILE_ROWS, LANES), F32),
            jax.ShapeDtypeStruct((t, LANES), F32),
            jax.ShapeDtypeStruct((t, LANES), jnp.int32),
        ],
        compiler_params=pltpu.CompilerParams(
            dimension_semantics=("arbitrary",), vmem_limit_bytes=VMEM_LIMIT),
        name="mix_route",
    )(x2d, yssd, ypool, proj, wso, wmo, fg, wr3, br)


def _moe_kernel(be_ref, nvalid_ref, nreal_ref,
                tok0_ref, toknext_ref, dst_ref,
                h_hbm, wgu_ref, bgu_ref, wd_ref, bd_ref,
                slots_hbm,
                xbuf, ybuf, wgu_bf, wd_bf, gsem, ssem):
    i = pl.program_id(0)
    nvalid = nvalid_ref[0]
    slot = lax.rem(i, 2)

    def tile(off, n=1):
        return pl.ds(pl.multiple_of(off, TILE_ROWS), n * TILE_ROWS)

    def gather_copy(tok, r, s):
        return pltpu.make_async_copy(h_hbm.at[tile(tok)], xbuf.at[s, tile(r * TILE_ROWS)], gsem.at[s])

    def scatter_copy(dst, r, s):
        return pltpu.make_async_copy(ybuf.at[s, tile(r * TILE_ROWS)], slots_hbm.at[tile(dst)], ssem.at[s])

    def for_rows(n, start_one):
        ngroups = lax.shift_right_logical(n, 3)

        def group(gi, carry):
            for u in range(ROW_UNROLL):
                start_one(gi * ROW_UNROLL + u)
            return carry
        lax.fori_loop(0, ngroups, group, 0)

        def single(r, carry):
            start_one(r)
            return carry
        lax.fori_loop(ngroups * ROW_UNROLL, n, single, 0)

    def start_gather(tok_ref, s, n):
        for_rows(n, lambda r: gather_copy(tok_ref[0, 0, r], r, s).start())

    def wait_rows(n, wait_chunk):
        c = MOE_ROWS
        while c >= 1:
            pl.when(lax.bitwise_and(n, c) != 0)(functools.partial(wait_chunk, c))
            c //= 2

    def wait_gather(s, n):
        wait_rows(n, lambda c: pltpu.make_async_copy(
            h_hbm.at[tile(0, c)], xbuf.at[s, tile(0, c)], gsem.at[s]).wait())

    def wait_scatter(s, n):
        wait_rows(n, lambda c: pltpu.make_async_copy(
            ybuf.at[s, tile(0, c)], slots_hbm.at[tile(0, c)], ssem.at[s]).wait())

    @pl.when(i == 0)
    def _():
        xbuf[...] = jnp.zeros_like(xbuf)
        start_gather(tok0_ref, 0, nreal_ref[0])

    @pl.when(i + 1 < nvalid)
    def _():
        start_gather(toknext_ref, 1 - slot, nreal_ref[i + 1])

    @pl.when(i < nvalid)
    def _():
        nreal = nreal_ref[i]
        prev_e = be_ref[jnp.maximum(i - 1, 0)]

        @pl.when(jnp.logical_or(i == 0, be_ref[i] != prev_e))
        def _():
            wgu_bf[...] = wgu_ref[0].astype(BF16)
            wd_bf[...] = wd_ref[0].astype(BF16)

        wait_gather(slot, nreal)

        @pl.when(i >= 2)
        def _():
            wait_scatter(slot, nreal_ref[i - 2])

        xb = _load_token_tiles(xbuf.at[slot], MOE_ROWS).astype(BF16)
        gu = jnp.dot(xb, wgu_bf[...], preferred_element_type=F32) + bgu_ref[0]
        gate = jnp.minimum(gu[:, 0:D_EXPERT], SWIGLU_LIMIT)
        up = jnp.clip(gu[:, D_EXPERT:2 * D_EXPERT], -SWIGLU_LIMIT, SWIGLU_LIMIT)
        act = (up + 1.0) * gate * jax.nn.sigmoid(SWIGLU_ALPHA * gate)
        y = jnp.dot(act.astype(BF16), wd_bf[...], preferred_element_type=F32) + bd_ref[0]
        _store_token_tiles(ybuf.at[slot], y)

        for_rows(nreal, lambda r: scatter_copy(dst_ref[0, 0, r], r, slot).start())

        @pl.when(i == nvalid - 1)
        def _():
            wait_scatter(slot, nreal)

            @pl.when(i >= 1)
            def _():
                wait_scatter(1 - slot, nreal_ref[i - 1])


def _moe(block_expert, nvalid, nreal, row_token3, row_dest3, h2, wgu, bgu3, wd, bd3, n_slot_rows):
    nb = block_expert.shape[0]
    grid_spec = pltpu.PrefetchScalarGridSpec(
        num_scalar_prefetch=3,
        grid=(nb,),
        in_specs=[
            pl.BlockSpec((1, 1, MOE_ROWS), lambda i, be, nv, nr: (0, 0, 0), memory_space=pltpu.SMEM),
            pl.BlockSpec((1, 1, MOE_ROWS), lambda i, be, nv, nr: (jnp.minimum(i + 1, nb - 1), 0, 0),
                         memory_space=pltpu.SMEM),
            pl.BlockSpec((1, 1, MOE_ROWS), lambda i, be, nv, nr: (i, 0, 0), memory_space=pltpu.SMEM),
            pl.BlockSpec(memory_space=pl.ANY),
            pl.BlockSpec((1, D_MODEL, 2 * D_EXPERT), lambda i, be, nv, nr: (be[i], 0, 0)),
            pl.BlockSpec((1, 1, 2 * D_EXPERT), lambda i, be, nv, nr: (be[i], 0, 0)),
            pl.BlockSpec((1, D_EXPERT, D_MODEL), lambda i, be, nv, nr: (be[i], 0, 0)),
            pl.BlockSpec((1, 1, D_MODEL), lambda i, be, nv, nr: (be[i], 0, 0)),
        ],
        out_specs=pl.BlockSpec(memory_space=pl.ANY),
        scratch_shapes=[
            pltpu.VMEM((2, MOE_ROWS * TILE_ROWS, LANES), F32),
            pltpu.VMEM((2, MOE_ROWS * TILE_ROWS, LANES), F32),
            pltpu.VMEM((D_MODEL, 2 * D_EXPERT), BF16),
            pltpu.VMEM((D_EXPERT, D_MODEL), BF16),
            pltpu.SemaphoreType.DMA((2,)),
            pltpu.SemaphoreType.DMA((2,)),
        ],
    )
    return pl.pallas_call(
        _moe_kernel,
        grid_spec=grid_spec,
        out_shape=jax.ShapeDtypeStruct((n_slot_rows * TILE_ROWS, LANES), F32),
        compiler_params=pltpu.CompilerParams(
            dimension_semantics=("arbitrary",), vmem_limit_bytes=VMEM_LIMIT),
        name="moe_experts",
    )(block_expert, nvalid, nreal, row_token3, row_token3, row_dest3, h2, wgu, bgu3, wd, bd3)


def _combine_kernel(x1_ref, s0_ref, s1_ref, s2_ref, s3_ref, topw_ref, p_ref, pg_ref, wpg_ref, wpp_ref, fg_ref,
                    out_ref):
    x2 = x1_ref[...]
    topw = topw_ref[...]
    for k, s_ref in enumerate((s0_ref, s1_ref, s2_ref, s3_ref)):
        x2 = x2 + _load_token_tiles(s_ref, x2.shape[0]) * topw[:, k:k + 1]
    n = _rms(x2, pg_ref[...])
    gate = jax.nn.sigmoid(jnp.dot(n.astype(BF16), wpg_ref[...], preferred_element_type=F32))
    pp = jnp.dot(p_ref[...].astype(BF16), wpp_ref[...], preferred_element_type=F32)
    x3 = x2 + gate * pp
    out_ref[...] = _rms(x3, fg_ref[...])


def _combine(x1, slots, topw, p2d, pg, wpg, wpp, fg, tm=512):
    t = x1.shape[0]
    nt = t // tm
    const = lambda i: (0, 0)
    slot_specs = [pl.BlockSpec((tm * TILE_ROWS, LANES), functools.partial(lambda k, i: (k * nt + i, 0), k))
                  for k in range(TOP_K)]
    return pl.pallas_call(
        _combine_kernel,
        grid=(nt,),
        in_specs=[
            pl.BlockSpec((tm, D_MODEL), lambda i: (i, 0)),
            *slot_specs,
            pl.BlockSpec((tm, LANES), lambda i: (i, 0)),
            pl.BlockSpec((tm, D_PLE), lambda i: (i, 0)),
            pl.BlockSpec((1, D_MODEL), const),
            pl.BlockSpec((D_MODEL, D_MODEL), const),
            pl.BlockSpec((D_PLE, D_MODEL), const),
            pl.BlockSpec((1, D_MODEL), const),
        ],
        out_specs=pl.BlockSpec((tm, D_MODEL), lambda i: (i, 0)),
        out_shape=jax.ShapeDtypeStruct((t, D_MODEL), F32),
        compiler_params=pltpu.CompilerParams(
            dimension_semantics=("arbitrary",), vmem_limit_bytes=VMEM_LIMIT),
        name="combine_ple",
    )(x1, slots, slots, slots, slots, topw, p2d, pg, wpg, wpp, fg)


def _routing_tables(top_idx, n_tok):
    n_assign = n_tok * TOP_K
    expert_flat = top_idx.reshape(-1)
    order = jnp.argsort(expert_flat, stable=True).astype(jnp.int32)
    counts = jnp.bincount(expert_flat, length=N_EXPERTS).astype(jnp.int32)
    start = jnp.cumsum(counts) - counts
    padded = (counts + MOE_ROWS - 1) // MOE_ROWS * MOE_ROWS
    pend = jnp.cumsum(padded)
    pstart = pend - padded
    n_rows = n_assign + N_EXPERTS * MOE_ROWS
    n_blocks = n_rows // MOE_ROWS
    block_start = jnp.arange(n_blocks, dtype=jnp.int32) * MOE_ROWS
    block_expert = jnp.minimum(jnp.sum(block_start[:, None] >= pend[None, :], axis=1),
                               N_EXPERTS - 1).astype(jnp.int32)
    nvalid = (pend[-1] // MOE_ROWS).astype(jnp.int32).reshape(1)
    nreal = jnp.clip((pstart + counts)[block_expert] - block_start, 0, MOE_ROWS).astype(jnp.int32)
    sorted_pos = (block_start + (start - pstart)[block_expert])[:, None] + jnp.arange(MOE_ROWS, dtype=jnp.int32)[None, :]
    assign = order[jnp.clip(sorted_pos, 0, n_assign - 1)]
    row_token = assign // TOP_K
    row_dest = (assign % TOP_K) * n_tok + row_token
    return (block_expert, nvalid, nreal, (row_token * TILE_ROWS).reshape(n_blocks, 1, MOE_ROWS),
            (row_dest * TILE_ROWS).reshape(n_blocks, 1, MOE_ROWS), n_assign)


def _layer(x2d, p2d, bsz, seq, mix_norm_g, w_in, conv_w, conv_b, dt_bias, a_log, d_skip, ssd_norm_g,
           w_ssd_out, pool_w, pool_scale, w_mix_out, ffn_norm_g, w_router, b_router,
           w_gate_up, b_gate_up, w_down, b_down, ple_norm_g, w_ple_gate, w_ple_proj, out_g):
    n_tok = x2d.shape[0]
    dt0 = D_INNER + D_CONV
    w_main = jnp.concatenate([w_in[:, :dt0], w_in[:, dt0 + HEADS:]], axis=1).astype(BF16)
    w_dt = jnp.pad(w_in[:, dt0:dt0 + HEADS], ((0, 0), (0, LANES - HEADS)))
    wdt_hi, wdt_lo = _split2(w_dt)
    wdt3 = jnp.concatenate([wdt_hi, wdt_hi, wdt_lo], axis=0)

    proj, dt_raw = _in_proj(x2d, mix_norm_g[None, :], w_main, wdt3)

    pad_h = lambda v: jnp.pad(v, (0, LANES - HEADS))[None, :]
    ltri = (jnp.arange(CHUNK)[:, None] >= jnp.arange(CHUNK)[None, :]).astype(BF16)
    ltri3 = jnp.concatenate([ltri, ltri, ltri], axis=1)
    e1 = (jnp.arange(LANES)[:, None] == (jnp.arange(D_INNER) // HEAD_DIM)[None, :]).astype(BF16)
    e2 = jnp.concatenate([e1, e1], axis=0)
    yssd, ypool = _mixers(
        proj, dt_raw, bsz, seq, conv_w, conv_b[None, :], pad_h(dt_bias), pad_h(a_log),
        jnp.repeat(d_skip, HEAD_DIM)[None, :], ssd_norm_g[None, :], ltri3, e2,
        pool_w.astype(BF16), pool_scale[None, :])

    wr = jnp.pad(w_router, ((0, 0), (0, LANES - N_EXPERTS)))
    wr_hi, wr_lo = _split2(wr)
    wr3 = jnp.concatenate([wr_hi, wr_hi, wr_lo], axis=0)
    br = jnp.pad(b_router, (0, LANES - N_EXPERTS))[None, :]
    x1, h2, topw, topi = _mix_route(x2d, yssd, ypool, proj, w_ssd_out.astype(BF16),
                                    w_mix_out.astype(BF16), ffn_norm_g[None, :], wr3, br)

    block_expert, nvalid, nreal, row_token3, row_dest3, n_slot_rows = _routing_tables(topi[:, :TOP_K], n_tok)
    slots = _moe(block_expert, nvalid, nreal, row_token3, row_dest3, h2, w_gate_up, b_gate_up[:, None, :],
                 w_down, b_down[:, None, :], n_slot_rows)

    return _combine(x1, slots, topw, p2d, ple_norm_g[None, :], w_ple_gate.astype(BF16),
                    w_ple_proj.astype(BF16), out_g[None, :])


def kernel(x, p, mix_norm_g, w_in, conv_w, conv_b, dt_bias, a_log, d_skip, ssd_norm_g, w_ssd_out, pool_w,
           pool_scale, w_mix_out, ffn_norm_g, w_router, b_router, w_gate_up, b_gate_up, w_down, b_down,
           ple_norm_g, w_ple_gate, w_ple_proj, final_norm_g):
    bsz, seq, d = x.shape
    depth = p.shape[0]
    assert depth == 1 and d == D_MODEL and seq % CHUNK == 0
    x2d = x.reshape(bsz * seq, d)
    out = _layer(x2d, p[0].reshape(bsz * seq, D_PLE), bsz, seq, mix_norm_g[0], w_in[0], conv_w[0], conv_b[0],
                 dt_bias[0], a_log[0], d_skip[0], ssd_norm_g[0], w_ssd_out[0], pool_w[0], pool_scale[0],
                 w_mix_out[0], ffn_norm_g[0], w_router[0], b_router[0], w_gate_up[0], b_gate_up[0],
                 w_down[0], b_down[0], ple_norm_g[0], w_ple_gate[0], w_ple_proj[0], final_norm_g)
    return out.reshape(bsz, seq, d)
```

```python
import functools

import jax
import jax.numpy as jnp
from jax import lax
from jax.experimental import pallas as pl
from jax.experimental.pallas import tpu as pltpu

F32 = jnp.float32
BF16 = jnp.bfloat16

D_MODEL = 1024
D_INNER = 2048
HEAD_DIM = 64
HEADS = 32
GROUPS = 4
HEADS_PER_GROUP = HEADS // GROUPS
GROUP_DIM = D_INNER // GROUPS
D_STATE = 128
CONV_WIDTH = 4
CHUNK = 128
D_BC = 2 * GROUPS * D_STATE
D_CONV = D_INNER + D_BC
POOL_WIDTH = D_MODEL
POOL_WINDOWS = (2, 4, 8, 16)
POOL_GROUP_DIM = POOL_WIDTH // len(POOL_WINDOWS)
N_EXPERTS = 32
TOP_K = 4
D_EXPERT = D_MODEL
SWIGLU_LIMIT = 7.0
SWIGLU_ALPHA = 1.702
D_PLE = 256
EPS = 1e-6

LANES = 128
TILE_ROWS = D_MODEL // LANES
ROUTE_SUB = 256
MIX_CHUNKS = 2
HALO = 16
D_PROJ = D_INNER + D_CONV + POOL_WIDTH + 2 * D_MODEL
MOE_ROWS = 256
ROW_UNROLL = 8
VMEM_LIMIT = 56 * 1024 * 1024


def _split2(v):
    hi = v.astype(BF16)
    lo = (v - hi.astype(F32)).astype(BF16)
    return hi, lo


def _split3(v):
    hi = v.astype(BF16)
    r = v - hi.astype(F32)
    mid = r.astype(BF16)
    lo = (r - mid.astype(F32)).astype(BF16)
    return hi, mid, lo


def _rms(x, g):
    return x * lax.rsqrt(jnp.mean(x * x, axis=-1, keepdims=True) + EPS) * g


def _store_token_tiles(ref2d, val, row0=0):
    rows = val.shape[0]
    for j in range(TILE_ROWS):
        ref2d[pl.ds(row0 * TILE_ROWS + j, rows, stride=TILE_ROWS), :] = val[:, j * LANES:(j + 1) * LANES]


def _load_token_tiles(ref2d, rows):
    return jnp.concatenate([ref2d[pl.ds(j, rows, stride=TILE_ROWS), :] for j in range(TILE_ROWS)], axis=1)


def _in_proj_kernel(x_ref, g_ref, w_ref, wdt_ref, proj_ref, dt_ref, h_sc):
    @pl.when(pl.program_id(1) == 0)
    def _():
        h = _rms(x_ref[...], g_ref[...])
        hi, lo = _split2(h)
        h_sc[...] = hi
        lhs = jnp.concatenate([hi, lo, hi], axis=1)
        dt_ref[...] = jnp.dot(lhs, wdt_ref[...], preferred_element_type=F32)

    proj_ref[...] = jnp.dot(h_sc[...], w_ref[...], preferred_element_type=F32).astype(BF16)


def _in_proj(x2d, g, w_main, wdt3, tm=1024, tn=2048):
    t = x2d.shape[0]
    return pl.pallas_call(
        _in_proj_kernel,
        grid=(t // tm, D_PROJ // tn),
        in_specs=[
            pl.BlockSpec((tm, D_MODEL), lambda i, j: (i, 0)),
            pl.BlockSpec((1, D_MODEL), lambda i, j: (0, 0)),
            pl.BlockSpec((D_MODEL, tn), lambda i, j: (0, j)),
            pl.BlockSpec((3 * D_MODEL, LANES), lambda i, j: (0, 0)),
        ],
        out_specs=[
            pl.BlockSpec((tm, tn), lambda i, j: (i, j)),
            pl.BlockSpec((tm, LANES), lambda i, j: (i, 0)),
        ],
        out_shape=[
            jax.ShapeDtypeStruct((t, D_PROJ), BF16),
            jax.ShapeDtypeStruct((t, LANES), F32),
        ],
        scratch_shapes=[pltpu.VMEM((tm, D_MODEL), BF16)],
        compiler_params=pltpu.CompilerParams(
            dimension_semantics=("arbitrary", "arbitrary"), vmem_limit_bytes=VMEM_LIMIT),
        name="in_proj",
    )(x2d, g, w_main, wdt3)


def _mixers_kernel(z_ref, xs_ref, bc_ref, u_ref, dt_ref,
                   cw_ref, cb_ref, dtb_ref, alog_ref, dskip_ref, ng_ref, ltri_ref, e2_ref,
                   shift_ref, band_ref, pw_ref, ps_ref,
                   yssd_ref, ypool_ref,
                   ext_sc, extu_sc, state_sc):
    c = pl.program_id(1)
    rows = MIX_CHUNKS * CHUNK

    @pl.when(c == 0)
    def _():
        ext_sc[0:HALO, :] = jnp.zeros((HALO, D_CONV), BF16)
        extu_sc[0:HALO, :] = jnp.zeros((HALO, POOL_WIDTH), BF16)
        state_sc[...] = jnp.zeros_like(state_sc)

    @pl.when(c > 0)
    def _():
        ext_sc[0:HALO, :] = ext_sc[rows:rows + HALO, :]
        extu_sc[0:HALO, :] = extu_sc[rows:rows + HALO, :]

    ext_sc[HALO:HALO + rows, 0:D_INNER] = xs_ref[...]
    ext_sc[HALO:HALO + rows, D_INNER:D_CONV] = bc_ref[...]
    extu_sc[HALO:HALO + rows, :] = u_ref[...]
    for ci in range(MIX_CHUNKS):
        _mixers_chunk(ci, c * MIX_CHUNKS + ci, z_ref, u_ref, dt_ref, cw_ref, cb_ref, dtb_ref, alog_ref, dskip_ref,
                      ng_ref, ltri_ref, e2_ref, shift_ref, band_ref, pw_ref, ps_ref, yssd_ref, ypool_ref,
                      ext_sc, extu_sc, state_sc)


def _mixers_chunk(ci, chunk_index, z_ref, u_ref, dt_ref, cw_ref, cb_ref, dtb_ref, alog_ref, dskip_ref,
                  ng_ref, ltri_ref, e2_ref, shift_ref, band_ref, pw_ref, ps_ref, yssd_ref, ypool_ref,
                  ext_sc, extu_sc, state_sc):
    r0 = ci * CHUNK
    rs = slice(r0, r0 + CHUNK)
    ext = ext_sc[r0:r0 + HALO + CHUNK, :]
    conv = cb_ref[...] + cw_ref[CONV_WIDTH - 1:CONV_WIDTH, :] * ext[HALO:HALO + CHUNK, :].astype(F32)
    for k in range(CONV_WIDTH - 1):
        conv = conv + cw_ref[k:k + 1, :] * jnp.dot(shift_ref[k], ext, preferred_element_type=F32)
    xc = conv * jax.nn.sigmoid(conv)
    xs = xc[:, 0:D_INNER]
    xs_b = xs.astype(BF16)

    dtv = jax.nn.softplus(dt_ref[rs, :] + dtb_ref[...])
    da = dtv * (-jnp.exp(alog_ref[...]))
    a_cum = jnp.dot(ltri_ref[...], jnp.concatenate(_split3(da), axis=0),
                    preferred_element_type=F32)
    expa = jnp.exp(a_cum)
    a_last = a_cum[CHUNK - 1:CHUNK, :]
    wst = dtv * jnp.exp(a_last - a_cum)
    a_cum_t = a_cum.T
    dt_t = dtv.T

    both = jnp.concatenate([wst, expa], axis=0)
    hi, lo = _split2(both)
    expd = jnp.dot(jnp.concatenate([hi, lo], axis=1), e2_ref[...],
                   preferred_element_type=F32)
    wst_x = expd[0:CHUNK, :]
    expa_x = expd[CHUNK:2 * CHUNK, :]
    xw_b = (xs * wst_x).astype(BF16)

    row = lax.broadcasted_iota(jnp.int32, (CHUNK, CHUNK), 0)
    col = lax.broadcasted_iota(jnp.int32, (CHUNK, CHUNK), 1)
    causal = row >= col
    lane = lax.broadcasted_iota(jnp.int32, (CHUNK, LANES), 1)
    low_half = lane < HEAD_DIM

    y_groups = []
    for g in range(GROUPS):
        bg = xc[:, D_INNER + g * D_STATE:D_INNER + (g + 1) * D_STATE]
        cg = xc[:, D_INNER + GROUPS * D_STATE + g * D_STATE:D_INNER + GROUPS * D_STATE + (g + 1) * D_STATE]
        bg_b = bg.astype(BF16)
        cg_b = cg.astype(BF16)
        cbm = lax.dot_general(cg_b, bg_b, (((1,), (1,)), ((), ())), preferred_element_type=F32)
        gsl = slice(g * GROUP_DIM, (g + 1) * GROUP_DIM)

        prev_t = state_sc[g]
        y_off = jnp.dot(cg_b, prev_t.astype(BF16), preferred_element_type=F32) * expa_x[:, gsl]
        st_t = jnp.dot(bg.T.astype(BF16), xw_b[:, gsl], preferred_element_type=F32)
        state_sc[g] = prev_t * expa_x[CHUNK - 1:CHUNK, gsl] + st_t

        pairs = []
        for jp in range(HEADS_PER_GROUP // 2):
            ms = []
            for hh in range(2):
                h = g * HEADS_PER_GROUP + jp * 2 + hh
                seg = a_cum[:, h:h + 1] - a_cum_t[h:h + 1, :]
                dec = jnp.where(causal, jnp.exp(jnp.minimum(seg, 0.0)), 0.0)
                ms.append((cbm * dec * dt_t[h:h + 1, :]).astype(BF16))
            lhs = jnp.concatenate(ms, axis=1)
            c0 = g * GROUP_DIM + jp * LANES
            xp = xs_b[:, c0:c0 + LANES]
            zero = jnp.zeros_like(xp)
            rhs = jnp.concatenate([jnp.where(low_half, xp, zero), jnp.where(low_half, zero, xp)], axis=0)
            pairs.append(jnp.dot(lhs, rhs, preferred_element_type=F32))
        y_diag = jnp.concatenate(pairs, axis=1)

        yg = y_diag + y_off + dskip_ref[:, gsl] * xs[:, gsl]
        zg = z_ref[rs, gsl].astype(F32)
        yg = yg * (zg * jax.nn.sigmoid(zg))
        yg = yg * lax.rsqrt(jnp.mean(yg * yg, axis=-1, keepdims=True) + EPS) * ng_ref[:, gsl]
        y_groups.append(yg.astype(BF16))
    yssd_ref[rs, :] = jnp.concatenate(y_groups, axis=1)

    pos = chunk_index * CHUNK + lax.broadcasted_iota(jnp.int32, (CHUNK, 1), 0)
    outs = []
    for gi, w in enumerate(POOL_WINDOWS):
        psl = slice(gi * POOL_GROUP_DIM, (gi + 1) * POOL_GROUP_DIM)
        s = jnp.dot(band_ref[gi], extu_sc[r0:r0 + HALO + CHUNK, psl], preferred_element_type=F32)
        cnt = jnp.minimum(pos + 1, w).astype(F32)
        pooled = s / cnt - u_ref[rs, psl].astype(F32)
        outs.append(jnp.dot(pooled.astype(BF16), pw_ref[gi], preferred_element_type=F32))
    ypool_ref[rs, :] = (jnp.concatenate(outs, axis=1) * ps_ref[...]).astype(BF16)


def _mixers(proj, dt_raw, bsz, seq, cw, cb, dtb, alog, dskip, ng, ltri3, e2, pw, ps):
    rows = MIX_CHUNKS * CHUNK
    nc = seq // rows
    t = bsz * seq
    rowmap = lambda b, c: b * nc + c
    const2 = lambda b, c: (0, 0)
    const3 = lambda b, c: (0, 0, 0)
    trow = jnp.arange(CHUNK)[:, None] + HALO
    jcol = jnp.arange(HALO + CHUNK)[None, :]
    shifts = jnp.stack([(jcol == trow - (CONV_WIDTH - 1) + k) for k in range(CONV_WIDTH - 1)]).astype(BF16)
    bands = jnp.stack([(jcol <= trow) & (jcol > trow - w) for w in POOL_WINDOWS]).astype(BF16)
    return pl.pallas_call(
        _mixers_kernel,
        grid=(bsz, nc),
        in_specs=[
            pl.BlockSpec((rows, D_INNER), lambda b, c: (rowmap(b, c), 0)),
            pl.BlockSpec((rows, D_INNER), lambda b, c: (rowmap(b, c), 1)),
            pl.BlockSpec((rows, D_BC), lambda b, c: (rowmap(b, c), 4)),
            pl.BlockSpec((rows, POOL_WIDTH), lambda b, c: (rowmap(b, c), 5)),
            pl.BlockSpec((rows, LANES), lambda b, c: (rowmap(b, c), 0)),
            pl.BlockSpec((CONV_WIDTH, D_CONV), const2),
            pl.BlockSpec((1, D_CONV), const2),
            pl.BlockSpec((1, LANES), const2),
            pl.BlockSpec((1, LANES), const2),
            pl.BlockSpec((1, D_INNER), const2),
            pl.BlockSpec((1, D_INNER), const2),
            pl.BlockSpec((CHUNK, 3 * CHUNK), const2),
            pl.BlockSpec((2 * LANES, D_INNER), const2),
            pl.BlockSpec((CONV_WIDTH - 1, CHUNK, HALO + CHUNK), const3),
            pl.BlockSpec((len(POOL_WINDOWS), CHUNK, HALO + CHUNK), const3),
            pl.BlockSpec((len(POOL_WINDOWS), POOL_GROUP_DIM, POOL_GROUP_DIM), const3),
            pl.BlockSpec((1, POOL_WIDTH), const2),
        ],
        out_specs=[
            pl.BlockSpec((rows, D_INNER), lambda b, c: (rowmap(b, c), 0)),
            pl.BlockSpec((rows, POOL_WIDTH), lambda b, c: (rowmap(b, c), 0)),
        ],
        out_shape=[
            jax.ShapeDtypeStruct((t, D_INNER), BF16),
            jax.ShapeDtypeStruct((t, POOL_WIDTH), BF16),
        ],
        scratch_shapes=[
            pltpu.VMEM((HALO + rows, D_CONV), BF16),
            pltpu.VMEM((HALO + rows, POOL_WIDTH), BF16),
            pltpu.VMEM((GROUPS, D_STATE, GROUP_DIM), F32),
        ],
        compiler_params=pltpu.CompilerParams(
            dimension_semantics=("arbitrary", "arbitrary"), vmem_limit_bytes=VMEM_LIMIT),
        name="mixers",
    )(proj, proj, proj, proj, dt_raw, cw, cb, dtb, alog, dskip, ng, ltri3, e2, shifts, bands, pw, ps)


def _mix_route_kernel(x_ref, yssd_ref, ypool_ref, gates_ref, wso_ref, wmo_ref, fg_ref, wr_ref, br_ref,
                      x1_ref, h2_ref, topw_ref, topi_ref):
    for r0 in range(0, x_ref.shape[0], ROUTE_SUB):
        _mix_route_rows(slice(r0, r0 + ROUTE_SUB), x_ref, yssd_ref, ypool_ref, gates_ref, wso_ref, wmo_ref, fg_ref,
                        wr_ref, br_ref, x1_ref, h2_ref, topw_ref, topi_ref)


def _mix_route_rows(rs, x_ref, yssd_ref, ypool_ref, gates_ref, wso_ref, wmo_ref, fg_ref, wr_ref, br_ref,
                    x1_ref, h2_ref, topw_ref, topi_ref):
    y_ssd = jnp.dot(yssd_ref[rs, :], wso_ref[...], preferred_element_type=F32)
    gates = jax.nn.sigmoid(gates_ref[rs, :].astype(F32))
    mixed = gates[:, 0:D_MODEL] * y_ssd + gates[:, D_MODEL:2 * D_MODEL] * ypool_ref[rs, :].astype(F32)
    x1 = x_ref[rs, :] + jnp.dot(mixed.astype(BF16), wmo_ref[...], preferred_element_type=F32)
    x1_ref[rs, :] = x1
    h2 = _rms(x1, fg_ref[...])
    _store_token_tiles(h2_ref, h2, rs.start)

    hi, lo = _split2(h2)
    logits = jnp.dot(jnp.concatenate([hi, lo, hi], axis=1), wr_ref[...],
                     preferred_element_type=F32) + br_ref[...]
    tm = logits.shape[0]
    lane = lax.broadcasted_iota(jnp.int32, (tm, LANES), 1)
    neg = jnp.float32(-jnp.inf)
    work = jnp.where(lane < N_EXPERTS, logits, neg)
    vals = []
    idxs = []
    for _ in range(TOP_K):
        m = jnp.max(work, axis=-1, keepdims=True)
        idx = jnp.min(jnp.where(work == m, lane, LANES), axis=-1, keepdims=True)
        vals.append(m)
        idxs.append(idx)
        work = jnp.where(lane == idx, neg, work)
    es = [jnp.exp(v - vals[0]) for v in vals]
    den = es[0] + es[1] + es[2] + es[3]
    topw = jnp.zeros((tm, LANES), F32)
    topi = jnp.zeros((tm, LANES), jnp.int32)
    for k in range(TOP_K):
        topw = jnp.where(lane == k, es[k] / den, topw)
        topi = jnp.where(lane == k, idxs[k], topi)
    topw_ref[rs, :] = topw
    topi_ref[rs, :] = topi


def _mix_route(x2d, yssd, ypool, proj, wso, wmo, fg, wr3, br, tm=512):
    t = x2d.shape[0]
    const = lambda i: (0, 0)
    return pl.pallas_call(
        _mix_route_kernel,
        grid=(t // tm,),
        in_specs=[
            pl.BlockSpec((tm, D_MODEL), lambda i: (i, 0)),
            pl.BlockSpec((tm, D_INNER), lambda i: (i, 0)),
            pl.BlockSpec((tm, POOL_WIDTH), lambda i: (i, 0)),
            pl.BlockSpec((tm, 2 * D_MODEL), lambda i: (i, 3)),
            pl.BlockSpec((D_INNER, D_MODEL), const),
            pl.BlockSpec((D_MODEL, D_MODEL), const),
            pl.BlockSpec((1, D_MODEL), const),
            pl.BlockSpec((3 * D_MODEL, LANES), const),
            pl.BlockSpec((1, LANES), const),
        ],
        out_specs=[
            pl.BlockSpec((tm, D_MODEL), lambda i: (i, 0)),
            pl.BlockSpec((tm * TILE_ROWS, LANES), lambda i: (i, 0)),
            pl.BlockSpec((tm, LANES), lambda i: (i, 0)),
            pl.BlockSpec((tm, LANES), lambda i: (i, 0)),
        ],
        out_shape=[
            jax.ShapeDtypeStruct((t, D_MODEL), F32),
            jax.ShapeDtypeStruct((t * TILE_ROWS, LANES), F32),
            jax.ShapeDtypeStruct((t, LANES), F32),
            jax.ShapeDtypeStruct((t, LANES), jnp.int32),
        ],
        compiler_params=pltpu.CompilerParams(
            dimension_semantics=("arbitrary",), vmem_limit_bytes=VMEM_LIMIT),
        name="mix_route",
    )(x2d, yssd, ypool, proj, wso, wmo, fg, wr3, br)


def _moe_kernel(be_ref, nvalid_ref, nreal_ref,
                tok0_ref, toknext_ref, dst_ref,
                h_hbm, wgu_ref, bgu_ref, wd_ref, bd_ref,
                slots_hbm,
                xbuf, ybuf, wgu_bf, wd_bf, gsem, ssem):
    i = pl.program_id(0)
    nvalid = nvalid_ref[0]
    slot = lax.rem(i, 2)

    def tile(off, n=1):
        return pl.ds(pl.multiple_of(off, TILE_ROWS), n * TILE_ROWS)

    def gather_copy(tok, r, s):
        return pltpu.make_async_copy(h_hbm.at[tile(tok)], xbuf.at[s, tile(r * TILE_ROWS)], gsem.at[s])

    def scatter_copy(dst, r, s):
        return pltpu.make_async_copy(ybuf.at[s, tile(r * TILE_ROWS)], slots_hbm.at[tile(dst)], ssem.at[s])

    def for_rows(n, start_one):
        ngroups = lax.shift_right_logical(n, 3)

        def group(gi, carry):
            for u in range(ROW_UNROLL):
                start_one(gi * ROW_UNROLL + u)
            return carry
        lax.fori_loop(0, ngroups, group, 0)

        def single(r, carry):
            start_one(r)
            return carry
        lax.fori_loop(ngroups * ROW_UNROLL, n, single, 0)

    def start_gather(tok_ref, s, n):
        for_rows(n, lambda r: gather_copy(tok_ref[0, 0, r], r, s).start())

    def wait_rows(n, wait_chunk):
        c = MOE_ROWS
        while c >= 1:
            pl.when(lax.bitwise_and(n, c) != 0)(functools.partial(wait_chunk, c))
            c //= 2

    def wait_gather(s, n):
        wait_rows(n, lambda c: pltpu.make_async_copy(
            h_hbm.at[tile(0, c)], xbuf.at[s, tile(0, c)], gsem.at[s]).wait())

    def wait_scatter(s, n):
        wait_rows(n, lambda c: pltpu.make_async_copy(
            ybuf.at[s, tile(0, c)], slots_hbm.at[tile(0, c)], ssem.at[s]).wait())

    @pl.when(i == 0)
    def _():
        xbuf[...] = jnp.zeros_like(xbuf)
        start_gather(tok0_ref, 0, nreal_ref[0])

    @pl.when(i + 1 < nvalid)
    def _():
        start_gather(toknext_ref, 1 - slot, nreal_ref[i + 1])

    @pl.when(i < nvalid)
    def _():
        nreal = nreal_ref[i]
        prev_e = be_ref[jnp.maximum(i - 1, 0)]

        @pl.when(jnp.logical_or(i == 0, be_ref[i] != prev_e))
        def _():
            wgu_bf[...] = wgu_ref[0].astype(BF16)
            wd_bf[...] = wd_ref[0].astype(BF16)

        wait_gather(slot, nreal)

        @pl.when(i >= 2)
        def _():
            wait_scatter(slot, nreal_ref[i - 2])

        xb = _load_token_tiles(xbuf.at[slot], MOE_ROWS).astype(BF16)
        gu = jnp.dot(xb, wgu_bf[...], preferred_element_type=F32) + bgu_ref[0]
        gate = jnp.minimum(gu[:, 0:D_EXPERT], SWIGLU_LIMIT)
        up = jnp.clip(gu[:, D_EXPERT:2 * D_EXPERT], -SWIGLU_LIMIT, SWIGLU_LIMIT)
        act = (up + 1.0) * gate * jax.nn.sigmoid(SWIGLU_ALPHA * gate)
        y = jnp.dot(act.astype(BF16), wd_bf[...], preferred_element_type=F32) + bd_ref[0]
        _store_token_tiles(ybuf.at[slot], y)

        for_rows(nreal, lambda r: scatter_copy(dst_ref[0, 0, r], r, slot).start())

        @pl.when(i == nvalid - 1)
        def _():
            wait_scatter(slot, nreal)

            @pl.when(i >= 1)
            def _():
                wait_scatter(1 - slot, nreal_ref[i - 1])


def _moe(block_expert, nvalid, nreal, row_token3, row_dest3, h2, wgu, bgu3, wd, bd3, n_slot_rows):
    nb = block_expert.shape[0]
    grid_spec = pltpu.PrefetchScalarGridSpec(
        num_scalar_prefetch=3,
        grid=(nb,),
        in_specs=[
            pl.BlockSpec((1, 1, MOE_ROWS), lambda i, be, nv, nr: (0, 0, 0), memory_space=pltpu.SMEM),
            pl.BlockSpec((1, 1, MOE_ROWS), lambda i, be, nv, nr: (jnp.minimum(i + 1, nb - 1), 0, 0),
                         memory_space=pltpu.SMEM),
            pl.BlockSpec((1, 1, MOE_ROWS), lambda i, be, nv, nr: (i, 0, 0), memory_space=pltpu.SMEM),
            pl.BlockSpec(memory_space=pl.ANY),
            pl.BlockSpec((1, D_MODEL, 2 * D_EXPERT), lambda i, be, nv, nr: (be[i], 0, 0)),
            pl.BlockSpec((1, 1, 2 * D_EXPERT), lambda i, be, nv, nr: (be[i], 0, 0)),
            pl.BlockSpec((1, D_EXPERT, D_MODEL), lambda i, be, nv, nr: (be[i], 0, 0)),
            pl.BlockSpec((1, 1, D_MODEL), lambda i, be, nv, nr: (be[i], 0, 0)),
        ],
        out_specs=pl.BlockSpec(memory_space=pl.ANY),
        scratch_shapes=[
            pltpu.VMEM((2, MOE_ROWS * TILE_ROWS, LANES), F32),
            pltpu.VMEM((2, MOE_ROWS * TILE_ROWS, LANES), F32),
            pltpu.VMEM((D_MODEL, 2 * D_EXPERT), BF16),
            pltpu.VMEM((D_EXPERT, D_MODEL), BF16),
            pltpu.SemaphoreType.DMA((2,)),
            pltpu.SemaphoreType.DMA((2,)),
        ],
    )
    return pl.pallas_call(
        _moe_kernel,
        grid_spec=grid_spec,
        out_shape=jax.ShapeDtypeStruct((n_slot_rows * TILE_ROWS, LANES), F32),
        compiler_params=pltpu.CompilerParams(
            dimension_semantics=("arbitrary",), vmem_limit_bytes=VMEM_LIMIT),
        name="moe_experts",
    )(block_expert, nvalid, nreal, row_token3, row_token3, row_dest3, h2, wgu, bgu3, wd, bd3)


def _combine_kernel(x1_ref, s0_ref, s1_ref, s2_ref, s3_ref, topw_ref, p_ref, pg_ref, wpg_ref, wpp_ref, fg_ref,
                    out_ref):
    x2 = x1_ref[...]
    topw = topw_ref[...]
    for k, s_ref in enumerate((s0_ref, s1_ref, s2_ref, s3_ref)):
        x2 = x2 + _load_token_tiles(s_ref, x2.shape[0]) * topw[:, k:k + 1]
    n = _rms(x2, pg_ref[...])
    gate = jax.nn.sigmoid(jnp.dot(n.astype(BF16), wpg_ref[...], preferred_element_type=F32))
    pp = jnp.dot(p_ref[...].astype(BF16), wpp_ref[...], preferred_element_type=F32)
    x3 = x2 + gate * pp
    out_ref[...] = _rms(x3, fg_ref[...])


def _combine(x1, slots, topw, p2d, pg, wpg, wpp, fg, tm=512):
    t = x1.shape[0]
    nt = t // tm
    const = lambda i: (0, 0)
    slot_specs = [pl.BlockSpec((tm * TILE_ROWS, LANES), functools.partial(lambda k, i: (k * nt + i, 0), k))
                  for k in range(TOP_K)]
    return pl.pallas_call(
        _combine_kernel,
        grid=(nt,),
        in_specs=[
            pl.BlockSpec((tm, D_MODEL), lambda i: (i, 0)),
            *slot_specs,
            pl.BlockSpec((tm, LANES), lambda i: (i, 0)),
            pl.BlockSpec((tm, D_PLE), lambda i: (i, 0)),
            pl.BlockSpec((1, D_MODEL), const),
            pl.BlockSpec((D_MODEL, D_MODEL), const),
            pl.BlockSpec((D_PLE, D_MODEL), const),
            pl.BlockSpec((1, D_MODEL), const),
        ],
        out_specs=pl.BlockSpec((tm, D_MODEL), lambda i: (i, 0)),
        out_shape=jax.ShapeDtypeStruct((t, D_MODEL), F32),
        compiler_params=pltpu.CompilerParams(
            dimension_semantics=("arbitrary",), vmem_limit_bytes=VMEM_LIMIT),
        name="combine_ple",
    )(x1, slots, slots, slots, slots, topw, p2d, pg, wpg, wpp, fg)


def _routing_tables(top_idx, n_tok):
    n_assign = n_tok * TOP_K
    expert_flat = top_idx.reshape(-1)
    order = jnp.argsort(expert_flat, stable=True).astype(jnp.int32)
    counts = jnp.bincount(expert_flat, length=N_EXPERTS).astype(jnp.int32)
    start = jnp.cumsum(counts) - counts
    padded = (counts + MOE_ROWS - 1) // MOE_ROWS * MOE_ROWS
    pend = jnp.cumsum(padded)
    pstart = pend - padded
    n_rows = n_assign + N_EXPERTS * MOE_ROWS
    n_blocks = n_rows // MOE_ROWS
    block_start = jnp.arange(n_blocks, dtype=jnp.int32) * MOE_ROWS
    block_expert = jnp.minimum(jnp.sum(block_start[:, None] >= pend[None, :], axis=1),
                               N_EXPERTS - 1).astype(jnp.int32)
    nvalid = (pend[-1] // MOE_ROWS).astype(jnp.int32).reshape(1)
    nreal = jnp.clip((pstart + counts)[block_expert] - block_start, 0, MOE_ROWS).astype(jnp.int32)
    sorted_pos = (block_start + (start - pstart)[block_expert])[:, None] + jnp.arange(MOE_ROWS, dtype=jnp.int32)[None, :]
    assign = order[jnp.clip(sorted_pos, 0, n_assign - 1)]
    row_token = assign // TOP_K
    row_dest = (assign % TOP_K) * n_tok + row_token
    return (block_expert, nvalid, nreal, (row_token * TILE_ROWS).reshape(n_blocks, 1, MOE_ROWS),
            (row_dest * TILE_ROWS).reshape(n_blocks, 1, MOE_ROWS), n_assign)


def _layer(x2d, p2d, bsz, seq, mix_norm_g, w_in, conv_w, conv_b, dt_bias, a_log, d_skip, ssd_norm_g,
           w_ssd_out, pool_w, pool_scale, w_mix_out, ffn_norm_g, w_router, b_router,
           w_gate_up, b_gate_up, w_down, b_down, ple_norm_g, w_ple_gate, w_ple_proj, out_g):
    n_tok = x2d.shape[0]
    dt0 = D_INNER + D_CONV
    w_main = jnp.concatenate([w_in[:, :dt0], w_in[:, dt0 + HEADS:]], axis=1).astype(BF16)
    w_dt = jnp.pad(w_in[:, dt0:dt0 + HEADS], ((0, 0), (0, LANES - HEADS)))
    wdt_hi, wdt_lo = _split2(w_dt)
    wdt3 = jnp.concatenate([wdt_hi, wdt_hi, wdt_lo], axis=0)

    proj, dt_raw = _in_proj(x2d, mix_norm_g[None, :], w_main, wdt3)

    pad_h = lambda v: jnp.pad(v, (0, LANES - HEADS))[None, :]
    ltri = (jnp.arange(CHUNK)[:, None] >= jnp.arange(CHUNK)[None, :]).astype(BF16)
    ltri3 = jnp.concatenate([ltri, ltri, ltri], axis=1)
    e1 = (jnp.arange(LANES)[:, None] == (jnp.arange(D_INNER) // HEAD_DIM)[None, :]).astype(BF16)
    e2 = jnp.concatenate([e1, e1], axis=0)
    yssd, ypool = _mixers(
        proj, dt_raw, bsz, seq, conv_w, conv_b[None, :], pad_h(dt_bias), pad_h(a_log),
        jnp.repeat(d_skip, HEAD_DIM)[None, :], ssd_norm_g[None, :], ltri3, e2,
        pool_w.astype(BF16), pool_scale[None, :])

    wr = jnp.pad(w_router, ((0, 0), (0, LANES - N_EXPERTS)))
    wr_hi, wr_lo = _split2(wr)
    wr3 = jnp.concatenate([wr_hi, wr_hi, wr_lo], axis=0)
    br = jnp.pad(b_router, (0, LANES - N_EXPERTS))[None, :]
    x1, h2, topw, topi = _mix_route(x2d, yssd, ypool, proj, w_ssd_out.astype(BF16),
                                    w_mix_out.astype(BF16), ffn_norm_g[None, :], wr3, br)

    block_expert, nvalid, nreal, row_token3, row_dest3, n_slot_rows = _routing_tables(topi[:, :TOP_K], n_tok)
    slots = _moe(block_expert, nvalid, nreal, row_token3, row_dest3, h2, w_gate_up, b_gate_up[:, None, :],
                 w_down, b_down[:, None, :], n_slot_rows)

    return _combine(x1, slots, topw, p2d, ple_norm_g[None, :], w_ple_gate.astype(BF16),
                    w_ple_proj.astype(BF16), out_g[None, :])


def kernel(x, p, mix_norm_g, w_in, conv_w, conv_b, dt_bias, a_log, d_skip, ssd_norm_g, w_ssd_out, pool_w,
           pool_scale, w_mix_out, ffn_norm_g, w_router, b_router, w_gate_up, b_gate_up, w_down, b_down,
           ple_norm_g, w_ple_gate, w_ple_proj, final_norm_g):
    bsz, seq, d = x.shape
    depth = p.shape[0]
    assert depth == 1 and d == D_MODEL and seq % CHUNK == 0
    x2d = x.reshape(bsz * seq, d)
    out = _layer(x2d, p[0].reshape(bsz * seq, D_PLE), bsz, seq, mix_norm_g[0], w_in[0], conv_w[0], conv_b[0],
                 dt_bias[0], a_log[0], d_skip[0], ssd_norm_g[0], w_ssd_out[0], pool_w[0], pool_scale[0],
                 w_mix_out[0], ffn_norm_g[0], w_router[0], b_router[0], w_gate_up[0], b_gate_up[0],
                 w_down[0], b_down[0], ple_norm_g[0], w_ple_gate[0], w_ple_proj[0], final_norm_g)
    return out.reshape(bsz, seq, d)
```

```python
import functools

import jax
import jax.numpy as jnp
from jax import lax
from jax.experimental import pallas as pl
from jax.experimental.pallas import tpu as pltpu

F32 = jnp.float32
BF16 = jnp.bfloat16

D_MODEL = 1024
D_INNER = 2048
HEAD_DIM = 64
HEADS = 32
GROUPS = 4
HEADS_PER_GROUP = HEADS // GROUPS
GROUP_DIM = D_INNER // GROUPS
D_STATE = 128
CONV_WIDTH = 4
CHUNK = 128
D_BC = 2 * GROUPS * D_STATE
D_CONV = D_INNER + D_BC
POOL_WIDTH = D_MODEL
POOL_WINDOWS = (2, 4, 8, 16)
POOL_GROUP_DIM = POOL_WIDTH // len(POOL_WINDOWS)
N_EXPERTS = 32
TOP_K = 4
D_EXPERT = D_MODEL
SWIGLU_LIMIT = 7.0
SWIGLU_ALPHA = 1.702
D_PLE = 256
EPS = 1e-6

LANES = 128
TILE_ROWS = D_MODEL // LANES
ROUTE_SUB = 256
MIX_CHUNKS = 2
HALO = 16
D_PROJ = D_INNER + D_CONV + POOL_WIDTH + 2 * D_MODEL
MOE_ROWS = 256
MOE_PHASES = 4
ROW_UNROLL = 8
VMEM_LIMIT = 56 * 1024 * 1024


def _split2(v):
    hi = v.astype(BF16)
    lo = (v - hi.astype(F32)).astype(BF16)
    return hi, lo


def _split3(v):
    hi = v.astype(BF16)
    r = v - hi.astype(F32)
    mid = r.astype(BF16)
    lo = (r - mid.astype(F32)).astype(BF16)
    return hi, mid, lo


def _rms(x, g):
    return x * lax.rsqrt(jnp.mean(x * x, axis=-1, keepdims=True) + EPS) * g


def _store_token_tiles(ref2d, val, row0=0):
    rows = val.shape[0]
    for j in range(TILE_ROWS):
        ref2d[pl.ds(row0 * TILE_ROWS + j, rows, stride=TILE_ROWS), :] = val[:, j * LANES:(j + 1) * LANES]


def _load_token_tiles(ref2d, rows):
    return jnp.concatenate([ref2d[pl.ds(j, rows, stride=TILE_ROWS), :] for j in range(TILE_ROWS)], axis=1)


def _in_proj_kernel(x_ref, g_ref, w_ref, wdt_ref, proj_ref, dt_ref, h_sc):
    @pl.when(pl.program_id(1) == 0)
    def _():
        h = _rms(x_ref[...], g_ref[...])
        hi, lo = _split2(h)
        h_sc[...] = hi
        lhs = jnp.concatenate([hi, lo, hi], axis=1)
        dt_ref[...] = jnp.dot(lhs, wdt_ref[...], preferred_element_type=F32)

    proj_ref[...] = jnp.dot(h_sc[...], w_ref[...], preferred_element_type=F32).astype(BF16)


def _in_proj(x2d, g, w_main, wdt3, tm=1024, tn=2048):
    t = x2d.shape[0]
    return pl.pallas_call(
        _in_proj_kernel,
        grid=(t // tm, D_PROJ // tn),
        in_specs=[
            pl.BlockSpec((tm, D_MODEL), lambda i, j: (i, 0)),
            pl.BlockSpec((1, D_MODEL), lambda i, j: (0, 0)),
            pl.BlockSpec((D_MODEL, tn), lambda i, j: (0, j)),
            pl.BlockSpec((3 * D_MODEL, LANES), lambda i, j: (0, 0)),
        ],
        out_specs=[
            pl.BlockSpec((tm, tn), lambda i, j: (i, j)),
            pl.BlockSpec((tm, LANES), lambda i, j: (i, 0)),
        ],
        out_shape=[
            jax.ShapeDtypeStruct((t, D_PROJ), BF16),
            jax.ShapeDtypeStruct((t, LANES), F32),
        ],
        scratch_shapes=[pltpu.VMEM((tm, D_MODEL), BF16)],
        compiler_params=pltpu.CompilerParams(
            dimension_semantics=("arbitrary", "arbitrary"), vmem_limit_bytes=VMEM_LIMIT),
        name="in_proj",
    )(x2d, g, w_main, wdt3)


def _mixers_kernel(z_ref, xs_ref, bc_ref, u_ref, dt_ref,
                   cw_ref, cb_ref, dtb_ref, alog_ref, dskip_ref, ng_ref, ltri_ref, e2_ref,
                   shift_ref, band_ref, pw_ref, ps_ref,
                   yssd_ref, ypool_ref,
                   ext_sc, extu_sc, state_sc):
    c = pl.program_id(1)
    rows = MIX_CHUNKS * CHUNK

    @pl.when(c == 0)
    def _():
        ext_sc[0:HALO, :] = jnp.zeros((HALO, D_CONV), BF16)
        extu_sc[0:HALO, :] = jnp.zeros((HALO, POOL_WIDTH), BF16)
        state_sc[...] = jnp.zeros_like(state_sc)

    @pl.when(c > 0)
    def _():
        ext_sc[0:HALO, :] = ext_sc[rows:rows + HALO, :]
        extu_sc[0:HALO, :] = extu_sc[rows:rows + HALO, :]

    ext_sc[HALO:HALO + rows, 0:D_INNER] = xs_ref[...]
    ext_sc[HALO:HALO + rows, D_INNER:D_CONV] = bc_ref[...]
    extu_sc[HALO:HALO + rows, :] = u_ref[...]
    for ci in range(MIX_CHUNKS):
        _mixers_chunk(ci, c * MIX_CHUNKS + ci, z_ref, u_ref, dt_ref, cw_ref, cb_ref, dtb_ref, alog_ref, dskip_ref,
                      ng_ref, ltri_ref, e2_ref, shift_ref, band_ref, pw_ref, ps_ref, yssd_ref, ypool_ref,
                      ext_sc, extu_sc, state_sc)


def _mixers_chunk(ci, chunk_index, z_ref, u_ref, dt_ref, cw_ref, cb_ref, dtb_ref, alog_ref, dskip_ref,
                  ng_ref, ltri_ref, e2_ref, shift_ref, band_ref, pw_ref, ps_ref, yssd_ref, ypool_ref,
                  ext_sc, extu_sc, state_sc):
    r0 = ci * CHUNK
    rs = slice(r0, r0 + CHUNK)
    ext = ext_sc[r0:r0 + HALO + CHUNK, :]
    conv = cb_ref[...] + cw_ref[CONV_WIDTH - 1:CONV_WIDTH, :] * ext[HALO:HALO + CHUNK, :].astype(F32)
    for k in range(CONV_WIDTH - 1):
        conv = conv + cw_ref[k:k + 1, :] * jnp.dot(shift_ref[k], ext, preferred_element_type=F32)
    xc = conv * jax.nn.sigmoid(conv)
    xs = xc[:, 0:D_INNER]
    xs_b = xs.astype(BF16)

    dtv = jax.nn.softplus(dt_ref[rs, :] + dtb_ref[...])
    da = dtv * (-jnp.exp(alog_ref[...]))
    a_cum = jnp.dot(ltri_ref[...], jnp.concatenate(_split3(da), axis=0),
                    preferred_element_type=F32)
    expa = jnp.exp(a_cum)
    a_last = a_cum[CHUNK - 1:CHUNK, :]
    wst = dtv * jnp.exp(a_last - a_cum)
    a_cum_t = a_cum.T
    dt_t = dtv.T

    both = jnp.concatenate([wst, expa], axis=0)
    hi, lo = _split2(both)
    expd = jnp.dot(jnp.concatenate([hi, lo], axis=1), e2_ref[...],
                   preferred_element_type=F32)
    wst_x = expd[0:CHUNK, :]
    expa_x = expd[CHUNK:2 * CHUNK, :]
    xw_b = (xs * wst_x).astype(BF16)

    row = lax.broadcasted_iota(jnp.int32, (CHUNK, CHUNK), 0)
    col = lax.broadcasted_iota(jnp.int32, (CHUNK, CHUNK), 1)
    causal = row >= col
    lane = lax.broadcasted_iota(jnp.int32, (CHUNK, LANES), 1)
    low_half = lane < HEAD_DIM

    y_groups = []
    for g in range(GROUPS):
        bg = xc[:, D_INNER + g * D_STATE:D_INNER + (g + 1) * D_STATE]
        cg = xc[:, D_INNER + GROUPS * D_STATE + g * D_STATE:D_INNER + GROUPS * D_STATE + (g + 1) * D_STATE]
        bg_b = bg.astype(BF16)
        cg_b = cg.astype(BF16)
        cbm = lax.dot_general(cg_b, bg_b, (((1,), (1,)), ((), ())), preferred_element_type=F32)
        gsl = slice(g * GROUP_DIM, (g + 1) * GROUP_DIM)

        prev_t = state_sc[g]
        y_off = jnp.dot(cg_b, prev_t.astype(BF16), preferred_element_type=F32) * expa_x[:, gsl]
        st_t = jnp.dot(bg.T.astype(BF16), xw_b[:, gsl], preferred_element_type=F32)
        state_sc[g] = prev_t * expa_x[CHUNK - 1:CHUNK, gsl] + st_t

        pairs = []
        for jp in range(HEADS_PER_GROUP // 2):
            ms = []
            for hh in range(2):
                h = g * HEADS_PER_GROUP + jp * 2 + hh
                seg = a_cum[:, h:h + 1] - a_cum_t[h:h + 1, :]
                dec = jnp.where(causal, jnp.exp(jnp.minimum(seg, 0.0)), 0.0)
                ms.append((cbm * dec * dt_t[h:h + 1, :]).astype(BF16))
            lhs = jnp.concatenate(ms, axis=1)
            c0 = g * GROUP_DIM + jp * LANES
            xp = xs_b[:, c0:c0 + LANES]
            zero = jnp.zeros_like(xp)
            rhs = jnp.concatenate([jnp.where(low_half, xp, zero), jnp.where(low_half, zero, xp)], axis=0)
            pairs.append(jnp.dot(lhs, rhs, preferred_element_type=F32))
        y_diag = jnp.concatenate(pairs, axis=1)

        yg = y_diag + y_off + dskip_ref[:, gsl] * xs[:, gsl]
        zg = z_ref[rs, gsl].astype(F32)
        yg = yg * (zg * jax.nn.sigmoid(zg))
        yg = yg * lax.rsqrt(jnp.mean(yg * yg, axis=-1, keepdims=True) + EPS) * ng_ref[:, gsl]
        y_groups.append(yg.astype(BF16))
    yssd_ref[rs, :] = jnp.concatenate(y_groups, axis=1)

    pos = chunk_index * CHUNK + lax.broadcasted_iota(jnp.int32, (CHUNK, 1), 0)
    outs = []
    for gi, w in enumerate(POOL_WINDOWS):
        psl = slice(gi * POOL_GROUP_DIM, (gi + 1) * POOL_GROUP_DIM)
        s = jnp.dot(band_ref[gi], extu_sc[r0:r0 + HALO + CHUNK, psl], preferred_element_type=F32)
        cnt = jnp.minimum(pos + 1, w).astype(F32)
        pooled = s / cnt - u_ref[rs, psl].astype(F32)
        outs.append(jnp.dot(pooled.astype(BF16), pw_ref[gi], preferred_element_type=F32))
    ypool_ref[rs, :] = (jnp.concatenate(outs, axis=1) * ps_ref[...]).astype(BF16)


def _mixers(proj, dt_raw, bsz, seq, cw, cb, dtb, alog, dskip, ng, ltri3, e2, pw, ps):
    rows = MIX_CHUNKS * CHUNK
    nc = seq // rows
    t = bsz * seq
    rowmap = lambda b, c: b * nc + c
    const2 = lambda b, c: (0, 0)
    const3 = lambda b, c: (0, 0, 0)
    trow = jnp.arange(CHUNK)[:, None] + HALO
    jcol = jnp.arange(HALO + CHUNK)[None, :]
    shifts = jnp.stack([(jcol == trow - (CONV_WIDTH - 1) + k) for k in range(CONV_WIDTH - 1)]).astype(BF16)
    bands = jnp.stack([(jcol <= trow) & (jcol > trow - w) for w in POOL_WINDOWS]).astype(BF16)
    return pl.pallas_call(
        _mixers_kernel,
        grid=(bsz, nc),
        in_specs=[
            pl.BlockSpec((rows, D_INNER), lambda b, c: (rowmap(b, c), 0)),
            pl.BlockSpec((rows, D_INNER), lambda b, c: (rowmap(b, c), 1)),
            pl.BlockSpec((rows, D_BC), lambda b, c: (rowmap(b, c), 4)),
            pl.BlockSpec((rows, POOL_WIDTH), lambda b, c: (rowmap(b, c), 5)),
            pl.BlockSpec((rows, LANES), lambda b, c: (rowmap(b, c), 0)),
            pl.BlockSpec((CONV_WIDTH, D_CONV), const2),
            pl.BlockSpec((1, D_CONV), const2),
            pl.BlockSpec((1, LANES), const2),
            pl.BlockSpec((1, LANES), const2),
            pl.BlockSpec((1, D_INNER), const2),
            pl.BlockSpec((1, D_INNER), const2),
            pl.BlockSpec((CHUNK, 3 * CHUNK), const2),
            pl.BlockSpec((2 * LANES, D_INNER), const2),
            pl.BlockSpec((CONV_WIDTH - 1, CHUNK, HALO + CHUNK), const3),
            pl.BlockSpec((len(POOL_WINDOWS), CHUNK, HALO + CHUNK), const3),
            pl.BlockSpec((len(POOL_WINDOWS), POOL_GROUP_DIM, POOL_GROUP_DIM), const3),
            pl.BlockSpec((1, POOL_WIDTH), const2),
        ],
        out_specs=[
            pl.BlockSpec((rows, D_INNER), lambda b, c: (rowmap(b, c), 0)),
            pl.BlockSpec((rows, POOL_WIDTH), lambda b, c: (rowmap(b, c), 0)),
        ],
        out_shape=[
            jax.ShapeDtypeStruct((t, D_INNER), BF16),
            jax.ShapeDtypeStruct((t, POOL_WIDTH), BF16),
        ],
        scratch_shapes=[
            pltpu.VMEM((HALO + rows, D_CONV), BF16),
            pltpu.VMEM((HALO + rows, POOL_WIDTH), BF16),
            pltpu.VMEM((GROUPS, D_STATE, GROUP_DIM), F32),
        ],
        compiler_params=pltpu.CompilerParams(
            dimension_semantics=("arbitrary", "arbitrary"), vmem_limit_bytes=VMEM_LIMIT),
        name="mixers",
    )(proj, proj, proj, proj, dt_raw, cw, cb, dtb, alog, dskip, ng, ltri3, e2, shifts, bands, pw, ps)


def _mix_route_kernel(x_ref, yssd_ref, ypool_ref, gates_ref, wso_ref, wmo_ref, fg_ref, wr_ref, br_ref,
                      x1_ref, h2_ref, topw_ref, topi_ref):
    for r0 in range(0, x_ref.shape[0], ROUTE_SUB):
        _mix_route_rows(slice(r0, r0 + ROUTE_SUB), x_ref, yssd_ref, ypool_ref, gates_ref, wso_ref, wmo_ref, fg_ref,
                        wr_ref, br_ref, x1_ref, h2_ref, topw_ref, topi_ref)


def _mix_route_rows(rs, x_ref, yssd_ref, ypool_ref, gates_ref, wso_ref, wmo_ref, fg_ref, wr_ref, br_ref,
                    x1_ref, h2_ref, topw_ref, topi_ref):
    y_ssd = jnp.dot(yssd_ref[rs, :], wso_ref[...], preferred_element_type=F32)
    gates = jax.nn.sigmoid(gates_ref[rs, :].astype(F32))
    mixed = gates[:, 0:D_MODEL] * y_ssd + gates[:, D_MODEL:2 * D_MODEL] * ypool_ref[rs, :].astype(F32)
    x1 = x_ref[rs, :] + jnp.dot(mixed.astype(BF16), wmo_ref[...], preferred_element_type=F32)
    x1_ref[rs, :] = x1
    h2 = _rms(x1, fg_ref[...])
    _store_token_tiles(h2_ref, h2, rs.start)

    hi, lo = _split2(h2)
    logits = jnp.dot(jnp.concatenate([hi, lo, hi], axis=1), wr_ref[...],
                     preferred_element_type=F32) + br_ref[...]
    tm = logits.shape[0]
    lane = lax.broadcasted_iota(jnp.int32, (tm, LANES), 1)
    neg = jnp.float32(-jnp.inf)
    work = jnp.where(lane < N_EXPERTS, logits, neg)
    vals = []
    idxs = []
    for _ in range(TOP_K):
        m = jnp.max(work, axis=-1, keepdims=True)
        idx = jnp.min(jnp.where(work == m, lane, LANES), axis=-1, keepdims=True)
        vals.append(m)
        idxs.append(idx)
        work = jnp.where(lane == idx, neg, work)
    es = [jnp.exp(v - vals[0]) for v in vals]
    den = es[0] + es[1] + es[2] + es[3]
    topw = jnp.zeros((tm, LANES), F32)
    topi = jnp.zeros((tm, LANES), jnp.int32)
    for k in range(TOP_K):
        topw = jnp.where(lane == k, es[k] / den, topw)
        topi = jnp.where(lane == k, idxs[k], topi)
    topw_ref[rs, :] = topw
    topi_ref[rs, :] = topi


def _mix_route(x2d, yssd, ypool, proj, wso, wmo, fg, wr3, br, tm=512):
    t = x2d.shape[0]
    const = lambda i: (0, 0)
    return pl.pallas_call(
        _mix_route_kernel,
        grid=(t // tm,),
        in_specs=[
            pl.BlockSpec((tm, D_MODEL), lambda i: (i, 0)),
            pl.BlockSpec((tm, D_INNER), lambda i: (i, 0)),
            pl.BlockSpec((tm, POOL_WIDTH), lambda i: (i, 0)),
            pl.BlockSpec((tm, 2 * D_MODEL), lambda i: (i, 3)),
            pl.BlockSpec((D_INNER, D_MODEL), const),
            pl.BlockSpec((D_MODEL, D_MODEL), const),
            pl.BlockSpec((1, D_MODEL), const),
            pl.BlockSpec((3 * D_MODEL, LANES), const),
            pl.BlockSpec((1, LANES), const),
        ],
        out_specs=[
            pl.BlockSpec((tm, D_MODEL), lambda i: (i, 0)),
            pl.BlockSpec((tm * TILE_ROWS, LANES), lambda i: (i, 0)),
            pl.BlockSpec((tm, LANES), lambda i: (i, 0)),
            pl.BlockSpec((tm, LANES), lambda i: (i, 0)),
        ],
        out_shape=[
            jax.ShapeDtypeStruct((t, D_MODEL), F32),
            jax.ShapeDtypeStruct((t * TILE_ROWS, LANES), F32),
            jax.ShapeDtypeStruct((t, LANES), F32),
            jax.ShapeDtypeStruct((t, LANES), jnp.int32),
        ],
        compiler_params=pltpu.CompilerParams(
            dimension_semantics=("arbitrary",), vmem_limit_bytes=VMEM_LIMIT),
        name="mix_route",
    )(x2d, yssd, ypool, proj, wso, wmo, fg, wr3, br)


def _moe_kernel(be_ref, nvalid_ref, nreal_ref,
                tok0_ref, toknext_ref, dstprev_ref,
                h_hbm, wgu_ref, bgu_ref, wd_ref, bd_ref,
                slots_hbm,
                xbuf, ybuf, xb_sc, act_sc, wgu_bf, wd_bf, gsem, ssem, *, nb):
    i = pl.program_id(0)
    nvalid = nvalid_ref[0]
    slot = lax.rem(i, 2)

    def tile(off, n=1):
        return pl.ds(pl.multiple_of(off, TILE_ROWS), n * TILE_ROWS)

    def gather_copy(tok, r, s):
        return pltpu.make_async_copy(h_hbm.at[tile(tok)], xbuf.at[s, tile(r * TILE_ROWS)], gsem.at[s])

    def scatter_copy(dst, r, s):
        return pltpu.make_async_copy(ybuf.at[s, tile(r * TILE_ROWS)], slots_hbm.at[tile(dst)], ssem.at[s])

    def for_rows(base, n, start_one):
        ngroups = lax.shift_right_logical(n, 3)

        def group(gi, carry):
            for u in range(ROW_UNROLL):
                start_one(base + gi * ROW_UNROLL + u, u % 2)
            return carry
        lax.fori_loop(0, ngroups, group, 0)

        def single(r, carry):
            start_one(r, 0)
            return carry
        lax.fori_loop(base + ngroups * ROW_UNROLL, base + n, single, 0)

    def start_gather(tok_ref, s, base, n):
        for_rows(base, n, lambda r, prio: gather_copy(tok_ref[0, 0, r], r, s).start(priority=prio))

    def start_scatter(dst_ref, s, base, n):
        for_rows(base, n, lambda r, prio: scatter_copy(dst_ref[0, 0, r], r, s).start(priority=prio))

    def wait_rows(n, wait_chunk):
        c = MOE_ROWS
        while c >= 1:
            pl.when(lax.bitwise_and(n, c) != 0)(functools.partial(wait_chunk, c))
            c //= 2

    def wait_gather(s, n):
        wait_rows(n, lambda c: pltpu.make_async_copy(
            h_hbm.at[tile(0, c)], xbuf.at[s, tile(0, c)], gsem.at[s]).wait())

    def wait_scatter(s, n):
        wait_rows(n, lambda c: pltpu.make_async_copy(
            ybuf.at[s, tile(0, c)], slots_hbm.at[tile(0, c)], ssem.at[s]).wait())

    prev_block = jnp.maximum(i - 1, 0)
    next_block = jnp.minimum(i + 1, nb - 1)
    n_prev = jnp.where(i >= 1, nreal_ref[prev_block], 0)

    @pl.when(i == 0)
    def _():
        xbuf[...] = jnp.zeros_like(xbuf)
        start_gather(tok0_ref, 0, 0, nreal_ref[0])

    @pl.when(jnp.logical_and(i >= 2, i <= nvalid))
    def _():
        wait_scatter(slot, nreal_ref[jnp.maximum(i - 2, 0)])

    @pl.when(i < nvalid)
    def _():
        nreal = nreal_ref[i]
        n_next = jnp.where(i + 1 < nvalid, nreal_ref[next_block], 0)

        @pl.when(jnp.logical_or(i == 0, be_ref[i] != be_ref[prev_block]))
        def _():
            wgu_bf[...] = wgu_ref[0].astype(BF16)
            wd_bf[...] = wd_ref[0].astype(BF16)

        wait_gather(slot, nreal)
        xb_sc[...] = _load_token_tiles(xbuf.at[slot], MOE_ROWS).astype(BF16)

        phase_rows = MOE_ROWS // MOE_PHASES
        phase_cols = D_EXPERT // MOE_PHASES
        for ph in range(MOE_PHASES):
            base = ph * phase_rows
            start_gather(toknext_ref, 1 - slot, base, jnp.clip(n_next - base, 0, phase_rows))
            start_scatter(dstprev_ref, 1 - slot, base, jnp.clip(n_prev - base, 0, phase_rows))
            xb = xb_sc[...]
            gcols = slice(ph * phase_cols, (ph + 1) * phase_cols)
            ucols = slice(D_EXPERT + ph * phase_cols, D_EXPERT + (ph + 1) * phase_cols)
            gate = jnp.dot(xb, wgu_bf[:, gcols], preferred_element_type=F32) + bgu_ref[0, :, gcols]
            up = jnp.dot(xb, wgu_bf[:, ucols], preferred_element_type=F32) + bgu_ref[0, :, ucols]
            gate = jnp.minimum(gate, SWIGLU_LIMIT)
            up = jnp.clip(up, -SWIGLU_LIMIT, SWIGLU_LIMIT)
            act = (up + 1.0) * gate * jax.nn.sigmoid(SWIGLU_ALPHA * gate)
            act_sc[:, gcols] = act.astype(BF16)

        y = jnp.dot(act_sc[...], wd_bf[...], preferred_element_type=F32) + bd_ref[0]
        _store_token_tiles(ybuf.at[slot], y)

    @pl.when(i == nvalid)
    def _():
        start_scatter(dstprev_ref, 1 - slot, 0, n_prev)
        wait_scatter(1 - slot, n_prev)


def _moe(block_expert, nvalid, nreal, row_token3, row_dest3, h2, wgu, bgu3, wd, bd3, n_slot_rows):
    nb = block_expert.shape[0]
    expert_of = lambda i, be: be[jnp.minimum(i, nb - 1)]
    grid_spec = pltpu.PrefetchScalarGridSpec(
        num_scalar_prefetch=3,
        grid=(nb + 1,),
        in_specs=[
            pl.BlockSpec((1, 1, MOE_ROWS), lambda i, be, nv, nr: (0, 0, 0), memory_space=pltpu.SMEM),
            pl.BlockSpec((1, 1, MOE_ROWS), lambda i, be, nv, nr: (jnp.minimum(i + 1, nb - 1), 0, 0),
                         memory_space=pltpu.SMEM),
            pl.BlockSpec((1, 1, MOE_ROWS), lambda i, be, nv, nr: (jnp.clip(i - 1, 0, nb - 1), 0, 0),
                         memory_space=pltpu.SMEM),
            pl.BlockSpec(memory_space=pl.ANY),
            pl.BlockSpec((1, D_MODEL, 2 * D_EXPERT), lambda i, be, nv, nr: (expert_of(i, be), 0, 0)),
            pl.BlockSpec((1, 1, 2 * D_EXPERT), lambda i, be, nv, nr: (expert_of(i, be), 0, 0)),
            pl.BlockSpec((1, D_EXPERT, D_MODEL), lambda i, be, nv, nr: (expert_of(i, be), 0, 0)),
            pl.BlockSpec((1, 1, D_MODEL), lambda i, be, nv, nr: (expert_of(i, be), 0, 0)),
        ],
        out_specs=pl.BlockSpec(memory_space=pl.ANY),
        scratch_shapes=[
            pltpu.VMEM((2, MOE_ROWS * TILE_ROWS, LANES), F32),
            pltpu.VMEM((2, MOE_ROWS * TILE_ROWS, LANES), F32),
            pltpu.VMEM((MOE_ROWS, D_MODEL), BF16),
            pltpu.VMEM((MOE_ROWS, D_EXPERT), BF16),
            pltpu.VMEM((D_MODEL, 2 * D_EXPERT), BF16),
            pltpu.VMEM((D_EXPERT, D_MODEL), BF16),
            pltpu.SemaphoreType.DMA((2,)),
            pltpu.SemaphoreType.DMA((2,)),
        ],
    )
    return pl.pallas_call(
        functools.partial(_moe_kernel, nb=nb),
        grid_spec=grid_spec,
        out_shape=jax.ShapeDtypeStruct((n_slot_rows * TILE_ROWS, LANES), F32),
        compiler_params=pltpu.CompilerParams(
            dimension_semantics=("arbitrary",), vmem_limit_bytes=VMEM_LIMIT),
        name="moe_experts",
    )(block_expert, nvalid, nreal, row_token3, row_token3, row_dest3, h2, wgu, bgu3, wd, bd3)


def _combine_kernel(x1_ref, s0_ref, s1_ref, s2_ref, s3_ref, topw_ref, p_ref, pg_ref, wpg_ref, wpp_ref, fg_ref,
                    out_ref):
    x2 = x1_ref[...]
    topw = topw_ref[...]
    for k, s_ref in enumerate((s0_ref, s1_ref, s2_ref, s3_ref)):
        x2 = x2 + _load_token_tiles(s_ref, x2.shape[0]) * topw[:, k:k + 1]
    n = _rms(x2, pg_ref[...])
    gate = jax.nn.sigmoid(jnp.dot(n.astype(BF16), wpg_ref[...], preferred_element_type=F32))
    pp = jnp.dot(p_ref[...].astype(BF16), wpp_ref[...], preferred_element_type=F32)
    x3 = x2 + gate * pp
    out_ref[...] = _rms(x3, fg_ref[...])


def _combine(x1, slots, topw, p2d, pg, wpg, wpp, fg, tm=512):
    t = x1.shape[0]
    nt = t // tm
    const = lambda i: (0, 0)
    slot_specs = [pl.BlockSpec((tm * TILE_ROWS, LANES), functools.partial(lambda k, i: (k * nt + i, 0), k))
                  for k in range(TOP_K)]
    return pl.pallas_call(
        _combine_kernel,
        grid=(nt,),
        in_specs=[
            pl.BlockSpec((tm, D_MODEL), lambda i: (i, 0)),
            *slot_specs,
            pl.BlockSpec((tm, LANES), lambda i: (i, 0)),
            pl.BlockSpec((tm, D_PLE), lambda i: (i, 0)),
            pl.BlockSpec((1, D_MODEL), const),
            pl.BlockSpec((D_MODEL, D_MODEL), const),
            pl.BlockSpec((D_PLE, D_MODEL), const),
            pl.BlockSpec((1, D_MODEL), const),
        ],
        out_specs=pl.BlockSpec((tm, D_MODEL), lambda i: (i, 0)),
        out_shape=jax.ShapeDtypeStruct((t, D_MODEL), F32),
        compiler_params=pltpu.CompilerParams(
            dimension_semantics=("arbitrary",), vmem_limit_bytes=VMEM_LIMIT),
        name="combine_ple",
    )(x1, slots, slots, slots, slots, topw, p2d, pg, wpg, wpp, fg)


def _routing_tables(top_idx, n_tok):
    n_assign = n_tok * TOP_K
    expert_flat = top_idx.reshape(-1)
    order = jnp.argsort(expert_flat, stable=True).astype(jnp.int32)
    counts = jnp.bincount(expert_flat, length=N_EXPERTS).astype(jnp.int32)
    start = jnp.cumsum(counts) - counts
    padded = (counts + MOE_ROWS - 1) // MOE_ROWS * MOE_ROWS
    pend = jnp.cumsum(padded)
    pstart = pend - padded
    n_rows = n_assign + N_EXPERTS * MOE_ROWS
    n_blocks = n_rows // MOE_ROWS
    block_start = jnp.arange(n_blocks, dtype=jnp.int32) * MOE_ROWS
    block_expert = jnp.minimum(jnp.sum(block_start[:, None] >= pend[None, :], axis=1),
                               N_EXPERTS - 1).astype(jnp.int32)
    nvalid = (pend[-1] // MOE_ROWS).astype(jnp.int32).reshape(1)
    is_e = block_expert[:, None] == jnp.arange(N_EXPERTS, dtype=jnp.int32)[None, :]
    per_block = lambda v: jnp.sum(jnp.where(is_e, v[None, :], 0), axis=1)
    nreal = jnp.clip(per_block(pstart + counts) - block_start, 0, MOE_ROWS).astype(jnp.int32)
    sorted_pos = (block_start + per_block(start - pstart))[:, None] + jnp.arange(MOE_ROWS, dtype=jnp.int32)[None, :]
    assign = order[jnp.clip(sorted_pos, 0, n_assign - 1)]
    row_token = assign // TOP_K
    row_dest = (assign % TOP_K) * n_tok + row_token
    return (block_expert, nvalid, nreal, (row_token * TILE_ROWS).reshape(n_blocks, 1, MOE_ROWS),
            (row_dest * TILE_ROWS).reshape(n_blocks, 1, MOE_ROWS), n_assign)


def _layer(x2d, p2d, bsz, seq, mix_norm_g, w_in, conv_w, conv_b, dt_bias, a_log, d_skip, ssd_norm_g,
           w_ssd_out, pool_w, pool_scale, w_mix_out, ffn_norm_g, w_router, b_router,
           w_gate_up, b_gate_up, w_down, b_down, ple_norm_g, w_ple_gate, w_ple_proj, out_g):
    n_tok = x2d.shape[0]
    dt0 = D_INNER + D_CONV
    w_main = jnp.concatenate([w_in[:, :dt0], w_in[:, dt0 + HEADS:]], axis=1).astype(BF16)
    w_dt = jnp.pad(w_in[:, dt0:dt0 + HEADS], ((0, 0), (0, LANES - HEADS)))
    wdt_hi, wdt_lo = _split2(w_dt)
    wdt3 = jnp.concatenate([wdt_hi, wdt_hi, wdt_lo], axis=0)

    proj, dt_raw = _in_proj(x2d, mix_norm_g[None, :], w_main, wdt3)

    pad_h = lambda v: jnp.pad(v, (0, LANES - HEADS))[None, :]
    ltri = (jnp.arange(CHUNK)[:, None] >= jnp.arange(CHUNK)[None, :]).astype(BF16)
    ltri3 = jnp.concatenate([ltri, ltri, ltri], axis=1)
    e1 = (jnp.arange(LANES)[:, None] == (jnp.arange(D_INNER) // HEAD_DIM)[None, :]).astype(BF16)
    e2 = jnp.concatenate([e1, e1], axis=0)
    yssd, ypool = _mixers(
        proj, dt_raw, bsz, seq, conv_w, conv_b[None, :], pad_h(dt_bias), pad_h(a_log),
        jnp.repeat(d_skip, HEAD_DIM)[None, :], ssd_norm_g[None, :], ltri3, e2,
        pool_w.astype(BF16), pool_scale[None, :])

    wr = jnp.pad(w_router, ((0, 0), (0, LANES - N_EXPERTS)))
    wr_hi, wr_lo = _split2(wr)
    wr3 = jnp.concatenate([wr_hi, wr_hi, wr_lo], axis=0)
    br = jnp.pad(b_router, (0, LANES - N_EXPERTS))[None, :]
    x1, h2, topw, topi = _mix_route(x2d, yssd, ypool, proj, w_ssd_out.astype(BF16),
                                    w_mix_out.astype(BF16), ffn_norm_g[None, :], wr3, br)

    block_expert, nvalid, nreal, row_token3, row_dest3, n_slot_rows = _routing_tables(topi[:, :TOP_K], n_tok)
    slots = _moe(block_expert, nvalid, nreal, row_token3, row_dest3, h2, w_gate_up, b_gate_up[:, None, :],
                 w_down, b_down[:, None, :], n_slot_rows)

    return _combine(x1, slots, topw, p2d, ple_norm_g[None, :], w_ple_gate.astype(BF16),
                    w_ple_proj.astype(BF16), out_g[None, :])


def kernel(x, p, mix_norm_g, w_in, conv_w, conv_b, dt_bias, a_log, d_skip, ssd_norm_g, w_ssd_out, pool_w,
           pool_scale, w_mix_out, ffn_norm_g, w_router, b_router, w_gate_up, b_gate_up, w_down, b_down,
           ple_norm_g, w_ple_gate, w_ple_proj, final_norm_g):
    bsz, seq, d = x.shape
    depth = p.shape[0]
    assert depth == 1 and d == D_MODEL and seq % CHUNK == 0
    x2d = x.reshape(bsz * seq, d)
    out = _layer(x2d, p[0].reshape(bsz * seq, D_PLE), bsz, seq, mix_norm_g[0], w_in[0], conv_w[0], conv_b[0],
                 dt_bias[0], a_log[0], d_skip[0], ssd_norm_g[0], w_ssd_out[0], pool_w[0], pool_scale[0],
                 w_mix_out[0], ffn_norm_g[0], w_router[0], b_router[0], w_gate_up[0], b_gate_up[0],
                 w_down[0], b_down[0], ple_norm_g[0], w_ple_gate[0], w_ple_proj[0], final_norm_g)
    return out.reshape(bsz, seq, d)
```

```python
import functools

import jax
import jax.numpy as jnp
from jax import lax
from jax.experimental import pallas as pl
from jax.experimental.pallas import tpu as pltpu

F32 = jnp.float32
BF16 = jnp.bfloat16

D_MODEL = 1024
D_INNER = 2048
HEAD_DIM = 64
HEADS = 32
GROUPS = 4
HEADS_PER_GROUP = HEADS // GROUPS
GROUP_DIM = D_INNER // GROUPS
D_STATE = 128
CONV_WIDTH = 4
CHUNK = 128
D_BC = 2 * GROUPS * D_STATE
D_CONV = D_INNER + D_BC
POOL_WIDTH = D_MODEL
POOL_WINDOWS = (2, 4, 8, 16)
POOL_GROUP_DIM = POOL_WIDTH // len(POOL_WINDOWS)
N_EXPERTS = 32
TOP_K = 4
D_EXPERT = D_MODEL
SWIGLU_LIMIT = 7.0
SWIGLU_ALPHA = 1.702
D_PLE = 256
EPS = 1e-6

LANES = 128
TILE_ROWS = D_MODEL // LANES
ROUTE_SUB = 256
MIX_CHUNKS = 2
HALO = 16
D_PROJ = D_INNER + D_CONV + POOL_WIDTH + 2 * D_MODEL
MOE_ROWS = 256
MOE_PHASES = 4
ROW_UNROLL = 8
VMEM_LIMIT = 56 * 1024 * 1024


def _split2(v):
    hi = v.astype(BF16)
    lo = (v - hi.astype(F32)).astype(BF16)
    return hi, lo


def _split3(v):
    hi = v.astype(BF16)
    r = v - hi.astype(F32)
    mid = r.astype(BF16)
    lo = (r - mid.astype(F32)).astype(BF16)
    return hi, mid, lo


def _rms(x, g):
    return x * lax.rsqrt(jnp.mean(x * x, axis=-1, keepdims=True) + EPS) * g


def _store_token_tiles(ref2d, val, row0=0):
    rows = val.shape[0]
    for j in range(TILE_ROWS):
        ref2d[pl.ds(row0 * TILE_ROWS + j, rows, stride=TILE_ROWS), :] = val[:, j * LANES:(j + 1) * LANES]


def _load_token_tiles(ref2d, rows):
    return jnp.concatenate([ref2d[pl.ds(j, rows, stride=TILE_ROWS), :] for j in range(TILE_ROWS)], axis=1)


def _in_proj_kernel(x_ref, g_ref, w_ref, wdt_ref, proj_ref, dt_ref, h_sc):
    @pl.when(pl.program_id(1) == 0)
    def _():
        h = _rms(x_ref[...], g_ref[...])
        hi, lo = _split2(h)
        h_sc[...] = hi
        lhs = jnp.concatenate([hi, lo, hi], axis=1)
        dt_ref[...] = jnp.dot(lhs, wdt_ref[...], preferred_element_type=F32)

    proj_ref[...] = jnp.dot(h_sc[...], w_ref[...], preferred_element_type=F32).astype(BF16)


def _in_proj(x2d, g, w_main, wdt3, tm=1024, tn=2048):
    t = x2d.shape[0]
    return pl.pallas_call(
        _in_proj_kernel,
        grid=(t // tm, D_PROJ // tn),
        in_specs=[
            pl.BlockSpec((tm, D_MODEL), lambda i, j: (i, 0)),
            pl.BlockSpec((1, D_MODEL), lambda i, j: (0, 0)),
            pl.BlockSpec((D_MODEL, tn), lambda i, j: (0, j)),
            pl.BlockSpec((3 * D_MODEL, LANES), lambda i, j: (0, 0)),
        ],
        out_specs=[
            pl.BlockSpec((tm, tn), lambda i, j: (i, j)),
            pl.BlockSpec((tm, LANES), lambda i, j: (i, 0)),
        ],
        out_shape=[
            jax.ShapeDtypeStruct((t, D_PROJ), BF16),
            jax.ShapeDtypeStruct((t, LANES), F32),
        ],
        scratch_shapes=[pltpu.VMEM((tm, D_MODEL), BF16)],
        compiler_params=pltpu.CompilerParams(
            dimension_semantics=("arbitrary", "arbitrary"), vmem_limit_bytes=VMEM_LIMIT),
        name="in_proj",
    )(x2d, g, w_main, wdt3)


def _mixers_kernel(z_ref, xs_ref, bc_ref, u_ref, dt_ref,
                   cw_ref, cb_ref, dtb_ref, alog_ref, dskip_ref, ng_ref, ltri_ref, e2_ref,
                   shift_ref, band_ref, pw_ref, ps_ref,
                   yssd_ref, ypool_ref,
                   ext_sc, extu_sc, state_sc):
    c = pl.program_id(1)
    rows = MIX_CHUNKS * CHUNK

    @pl.when(c == 0)
    def _():
        ext_sc[0:HALO, :] = jnp.zeros((HALO, D_CONV), BF16)
        extu_sc[0:HALO, :] = jnp.zeros((HALO, POOL_WIDTH), BF16)
        state_sc[...] = jnp.zeros_like(state_sc)

    @pl.when(c > 0)
    def _():
        ext_sc[0:HALO, :] = ext_sc[rows:rows + HALO, :]
        extu_sc[0:HALO, :] = extu_sc[rows:rows + HALO, :]

    ext_sc[HALO:HALO + rows, 0:D_INNER] = xs_ref[...]
    ext_sc[HALO:HALO + rows, D_INNER:D_CONV] = bc_ref[...]
    extu_sc[HALO:HALO + rows, :] = u_ref[...]
    for ci in range(MIX_CHUNKS):
        _mixers_chunk(ci, c * MIX_CHUNKS + ci, z_ref, u_ref, dt_ref, cw_ref, cb_ref, dtb_ref, alog_ref, dskip_ref,
                      ng_ref, ltri_ref, e2_ref, shift_ref, band_ref, pw_ref, ps_ref, yssd_ref, ypool_ref,
                      ext_sc, extu_sc, state_sc)


def _mixers_chunk(ci, chunk_index, z_ref, u_ref, dt_ref, cw_ref, cb_ref, dtb_ref, alog_ref, dskip_ref,
                  ng_ref, ltri_ref, e2_ref, shift_ref, band_ref, pw_ref, ps_ref, yssd_ref, ypool_ref,
                  ext_sc, extu_sc, state_sc):
    r0 = ci * CHUNK
    rs = slice(r0, r0 + CHUNK)
    ext = ext_sc[r0:r0 + HALO + CHUNK, :]
    conv = cb_ref[...] + cw_ref[CONV_WIDTH - 1:CONV_WIDTH, :] * ext[HALO:HALO + CHUNK, :].astype(F32)
    for k in range(CONV_WIDTH - 1):
        conv = conv + cw_ref[k:k + 1, :] * jnp.dot(shift_ref[k], ext, preferred_element_type=F32)
    xc = conv * jax.nn.sigmoid(conv)
    xs = xc[:, 0:D_INNER]
    xs_b = xs.astype(BF16)

    dtv = jax.nn.softplus(dt_ref[rs, :] + dtb_ref[...])
    da = dtv * (-jnp.exp(alog_ref[...]))
    a_cum = jnp.dot(ltri_ref[...], jnp.concatenate(_split3(da), axis=0),
                    preferred_element_type=F32)
    expa = jnp.exp(a_cum)
    a_last = a_cum[CHUNK - 1:CHUNK, :]
    wst = dtv * jnp.exp(a_last - a_cum)
    a_cum_t = a_cum.T
    dt_t = dtv.T

    both = jnp.concatenate([wst, expa], axis=0)
    hi, lo = _split2(both)
    expd = jnp.dot(jnp.concatenate([hi, lo], axis=1), e2_ref[...],
                   preferred_element_type=F32)
    wst_x = expd[0:CHUNK, :]
    expa_x = expd[CHUNK:2 * CHUNK, :]
    xw_b = (xs * wst_x).astype(BF16)

    row = lax.broadcasted_iota(jnp.int32, (CHUNK, CHUNK), 0)
    col = lax.broadcasted_iota(jnp.int32, (CHUNK, CHUNK), 1)
    causal = row >= col
    lane = lax.broadcasted_iota(jnp.int32, (CHUNK, LANES), 1)
    low_half = lane < HEAD_DIM

    y_groups = []
    for g in range(GROUPS):
        bg = xc[:, D_INNER + g * D_STATE:D_INNER + (g + 1) * D_STATE]
        cg = xc[:, D_INNER + GROUPS * D_STATE + g * D_STATE:D_INNER + GROUPS * D_STATE + (g + 1) * D_STATE]
        bg_b = bg.astype(BF16)
        cg_b = cg.astype(BF16)
        cbm = lax.dot_general(cg_b, bg_b, (((1,), (1,)), ((), ())), preferred_element_type=F32)
        gsl = slice(g * GROUP_DIM, (g + 1) * GROUP_DIM)

        prev_t = state_sc[g]
        y_off = jnp.dot(cg_b, prev_t.astype(BF16), preferred_element_type=F32) * expa_x[:, gsl]
        st_t = jnp.dot(bg.T.astype(BF16), xw_b[:, gsl], preferred_element_type=F32)
        state_sc[g] = prev_t * expa_x[CHUNK - 1:CHUNK, gsl] + st_t

        pairs = []
        for jp in range(HEADS_PER_GROUP // 2):
            ms = []
            for hh in range(2):
                h = g * HEADS_PER_GROUP + jp * 2 + hh
                seg = a_cum[:, h:h + 1] - a_cum_t[h:h + 1, :]
                dec = jnp.where(causal, jnp.exp(jnp.minimum(seg, 0.0)), 0.0)
                ms.append((cbm * dec * dt_t[h:h + 1, :]).astype(BF16))
            lhs = jnp.concatenate(ms, axis=1)
            c0 = g * GROUP_DIM + jp * LANES
            xp = xs_b[:, c0:c0 + LANES]
            zero = jnp.zeros_like(xp)
            rhs = jnp.concatenate([jnp.where(low_half, xp, zero), jnp.where(low_half, zero, xp)], axis=0)
            pairs.append(jnp.dot(lhs, rhs, preferred_element_type=F32))
        y_diag = jnp.concatenate(pairs, axis=1)

        yg = y_diag + y_off + dskip_ref[:, gsl] * xs[:, gsl]
        zg = z_ref[rs, gsl].astype(F32)
        yg = yg * (zg * jax.nn.sigmoid(zg))
        yg = yg * lax.rsqrt(jnp.mean(yg * yg, axis=-1, keepdims=True) + EPS) * ng_ref[:, gsl]
        y_groups.append(yg.astype(BF16))
    yssd_ref[rs, :] = jnp.concatenate(y_groups, axis=1)

    pos = chunk_index * CHUNK + lax.broadcasted_iota(jnp.int32, (CHUNK, 1), 0)
    outs = []
    for gi, w in enumerate(POOL_WINDOWS):
        psl = slice(gi * POOL_GROUP_DIM, (gi + 1) * POOL_GROUP_DIM)
        s = jnp.dot(band_ref[gi], extu_sc[r0:r0 + HALO + CHUNK, psl], preferred_element_type=F32)
        cnt = jnp.minimum(pos + 1, w).astype(F32)
        pooled = s / cnt - u_ref[rs, psl].astype(F32)
        outs.append(jnp.dot(pooled.astype(BF16), pw_ref[gi], preferred_element_type=F32))
    ypool_ref[rs, :] = (jnp.concatenate(outs, axis=1) * ps_ref[...]).astype(BF16)


def _mixers(proj, dt_raw, bsz, seq, cw, cb, dtb, alog, dskip, ng, ltri3, e2, pw, ps):
    rows = MIX_CHUNKS * CHUNK
    nc = seq // rows
    t = bsz * seq
    rowmap = lambda b, c: b * nc + c
    const2 = lambda b, c: (0, 0)
    const3 = lambda b, c: (0, 0, 0)
    trow = jnp.arange(CHUNK)[:, None] + HALO
    jcol = jnp.arange(HALO + CHUNK)[None, :]
    shifts = jnp.stack([(jcol == trow - (CONV_WIDTH - 1) + k) for k in range(CONV_WIDTH - 1)]).astype(BF16)
    bands = jnp.stack([(jcol <= trow) & (jcol > trow - w) for w in POOL_WINDOWS]).astype(BF16)
    return pl.pallas_call(
        _mixers_kernel,
        grid=(bsz, nc),
        in_specs=[
            pl.BlockSpec((rows, D_INNER), lambda b, c: (rowmap(b, c), 0)),
            pl.BlockSpec((rows, D_INNER), lambda b, c: (rowmap(b, c), 1)),
            pl.BlockSpec((rows, D_BC), lambda b, c: (rowmap(b, c), 4)),
            pl.BlockSpec((rows, POOL_WIDTH), lambda b, c: (rowmap(b, c), 5)),
            pl.BlockSpec((rows, LANES), lambda b, c: (rowmap(b, c), 0)),
            pl.BlockSpec((CONV_WIDTH, D_CONV), const2),
            pl.BlockSpec((1, D_CONV), const2),
            pl.BlockSpec((1, LANES), const2),
            pl.BlockSpec((1, LANES), const2),
            pl.BlockSpec((1, D_INNER), const2),
            pl.BlockSpec((1, D_INNER), const2),
            pl.BlockSpec((CHUNK, 3 * CHUNK), const2),
            pl.BlockSpec((2 * LANES, D_INNER), const2),
            pl.BlockSpec((CONV_WIDTH - 1, CHUNK, HALO + CHUNK), const3),
            pl.BlockSpec((len(POOL_WINDOWS), CHUNK, HALO + CHUNK), const3),
            pl.BlockSpec((len(POOL_WINDOWS), POOL_GROUP_DIM, POOL_GROUP_DIM), const3),
            pl.BlockSpec((1, POOL_WIDTH), const2),
        ],
        out_specs=[
            pl.BlockSpec((rows, D_INNER), lambda b, c: (rowmap(b, c), 0)),
            pl.BlockSpec((rows, POOL_WIDTH), lambda b, c: (rowmap(b, c), 0)),
        ],
        out_shape=[
            jax.ShapeDtypeStruct((t, D_INNER), BF16),
            jax.ShapeDtypeStruct((t, POOL_WIDTH), BF16),
        ],
        scratch_shapes=[
            pltpu.VMEM((HALO + rows, D_CONV), BF16),
            pltpu.VMEM((HALO + rows, POOL_WIDTH), BF16),
            pltpu.VMEM((GROUPS, D_STATE, GROUP_DIM), F32),
        ],
        compiler_params=pltpu.CompilerParams(
            dimension_semantics=("arbitrary", "arbitrary"), vmem_limit_bytes=VMEM_LIMIT),
        name="mixers",
    )(proj, proj, proj, proj, dt_raw, cw, cb, dtb, alog, dskip, ng, ltri3, e2, shifts, bands, pw, ps)


def _mix_route_kernel(x_ref, yssd_ref, ypool_ref, gates_ref, wso_ref, wmo_ref, fg_ref, wr_ref, br_ref,
                      x1_ref, h2_ref, topw_ref, topi_ref):
    for r0 in range(0, x_ref.shape[0], ROUTE_SUB):
        _mix_route_rows(slice(r0, r0 + ROUTE_SUB), x_ref, yssd_ref, ypool_ref, gates_ref, wso_ref, wmo_ref, fg_ref,
                        wr_ref, br_ref, x1_ref, h2_ref, topw_ref, topi_ref)


def _mix_route_rows(rs, x_ref, yssd_ref, ypool_ref, gates_ref, wso_ref, wmo_ref, fg_ref, wr_ref, br_ref,
                    x1_ref, h2_ref, topw_ref, topi_ref):
    y_ssd = jnp.dot(yssd_ref[rs, :], wso_ref[...], preferred_element_type=F32)
    gates = jax.nn.sigmoid(gates_ref[rs, :].astype(F32))
    mixed = gates[:, 0:D_MODEL] * y_ssd + gates[:, D_MODEL:2 * D_MODEL] * ypool_ref[rs, :].astype(F32)
    x1 = x_ref[rs, :] + jnp.dot(mixed.astype(BF16), wmo_ref[...], preferred_element_type=F32)
    x1_ref[rs, :] = x1
    h2 = _rms(x1, fg_ref[...])
    _store_token_tiles(h2_ref, h2, rs.start)

    hi, lo = _split2(h2)
    logits = jnp.dot(jnp.concatenate([hi, lo, hi], axis=1), wr_ref[...],
                     preferred_element_type=F32) + br_ref[...]
    tm = logits.shape[0]
    lane = lax.broadcasted_iota(jnp.int32, (tm, LANES), 1)
    neg = jnp.float32(-jnp.inf)
    work = jnp.where(lane < N_EXPERTS, logits, neg)
    vals = []
    idxs = []
    for _ in range(TOP_K):
        m = jnp.max(work, axis=-1, keepdims=True)
        idx = jnp.min(jnp.where(work == m, lane, LANES), axis=-1, keepdims=True)
        vals.append(m)
        idxs.append(idx)
        work = jnp.where(lane == idx, neg, work)
    es = [jnp.exp(v - vals[0]) for v in vals]
    den = es[0] + es[1] + es[2] + es[3]
    topw = jnp.zeros((tm, LANES), F32)
    topi = jnp.zeros((tm, LANES), jnp.int32)
    for k in range(TOP_K):
        topw = jnp.where(lane == k, es[k] / den, topw)
        topi = jnp.where(lane == k, idxs[k], topi)
    topw_ref[rs, :] = topw
    topi_ref[rs, :] = topi


def _mix_route(x2d, yssd, ypool, proj, wso, wmo, fg, wr3, br, tm=512):
    t = x2d.shape[0]
    const = lambda i: (0, 0)
    return pl.pallas_call(
        _mix_route_kernel,
        grid=(t // tm,),
        in_specs=[
            pl.BlockSpec((tm, D_MODEL), lambda i: (i, 0)),
            pl.BlockSpec((tm, D_INNER), lambda i: (i, 0)),
            pl.BlockSpec((tm, POOL_WIDTH), lambda i: (i, 0)),
            pl.BlockSpec((tm, 2 * D_MODEL), lambda i: (i, 3)),
            pl.BlockSpec((D_INNER, D_MODEL), const),
            pl.BlockSpec((D_MODEL, D_MODEL), const),
            pl.BlockSpec((1, D_MODEL), const),
            pl.BlockSpec((3 * D_MODEL, LANES), const),
            pl.BlockSpec((1, LANES), const),
        ],
        out_specs=[
            pl.BlockSpec((tm, D_MODEL), lambda i: (i, 0)),
            pl.BlockSpec((tm * TILE_ROWS, LANES), lambda i: (i, 0)),
            pl.BlockSpec((tm, LANES), lambda i: (i, 0)),
            pl.BlockSpec((tm, LANES), lambda i: (i, 0)),
        ],
        out_shape=[
            jax.ShapeDtypeStruct((t, D_MODEL), F32),
            jax.ShapeDtypeStruct((t * TILE_ROWS, LANES), F32),
            jax.ShapeDtypeStruct((t, LANES), F32),
            jax.ShapeDtypeStruct((t, LANES), jnp.int32),
        ],
        compiler_params=pltpu.CompilerParams(
            dimension_semantics=("arbitrary",), vmem_limit_bytes=VMEM_LIMIT),
        name="mix_route",
    )(x2d, yssd, ypool, proj, wso, wmo, fg, wr3, br)


def _moe_kernel(be_ref, nvalid_ref, nreal_ref,
                tok0_ref, toknext_ref, dstprev_ref,
                h_hbm, wgu_ref, bgu_ref, wd_ref, bd_ref,
                slots_hbm,
                xbuf, ybuf, xb_sc, act_sc, wgu_bf, wd_bf, gsem, ssem, *, nb):
    i = pl.program_id(0)
    nvalid = nvalid_ref[0]
    slot = lax.rem(i, 2)

    def tile(off, n=1):
        return pl.ds(pl.multiple_of(off, TILE_ROWS), n * TILE_ROWS)

    def gather_copy(tok, r, s):
        return pltpu.make_async_copy(h_hbm.at[tile(tok)], xbuf.at[s, tile(r * TILE_ROWS)], gsem.at[s])

    def scatter_copy(dst, r, s):
        return pltpu.make_async_copy(ybuf.at[s, tile(r * TILE_ROWS)], slots_hbm.at[tile(dst)], ssem.at[s])

    def for_rows(base, n, start_one):
        ngroups = lax.shift_right_logical(n, 3)

        def group(gi, carry):
            for u in range(ROW_UNROLL):
                start_one(base + gi * ROW_UNROLL + u, u % 2)
            return carry
        lax.fori_loop(0, ngroups, group, 0)

        def single(r, carry):
            start_one(r, 0)
            return carry
        lax.fori_loop(base + ngroups * ROW_UNROLL, base + n, single, 0)

    def start_gather(tok_ref, s, base, n):
        for_rows(base, n, lambda r, prio: gather_copy(tok_ref[0, 0, r], r, s).start())

    def start_scatter(dst_ref, s, base, n):
        for_rows(base, n, lambda r, prio: scatter_copy(dst_ref[0, 0, r], r, s).start(priority=prio))

    def wait_rows(n, wait_chunk):
        c = MOE_ROWS
        while c >= 1:
            pl.when(lax.bitwise_and(n, c) != 0)(functools.partial(wait_chunk, c))
            c //= 2

    def wait_gather(s, n):
        wait_rows(n, lambda c: pltpu.make_async_copy(
            h_hbm.at[tile(0, c)], xbuf.at[s, tile(0, c)], gsem.at[s]).wait())

    def wait_scatter(s, n):
        wait_rows(n, lambda c: pltpu.make_async_copy(
            ybuf.at[s, tile(0, c)], slots_hbm.at[tile(0, c)], ssem.at[s]).wait())

    prev_block = jnp.maximum(i - 1, 0)
    next_block = jnp.minimum(i + 1, nb - 1)
    n_prev = jnp.where(i >= 1, nreal_ref[prev_block], 0)

    @pl.when(i == 0)
    def _():
        xbuf[...] = jnp.zeros_like(xbuf)
        start_gather(tok0_ref, 0, 0, nreal_ref[0])

    @pl.when(jnp.logical_and(i >= 2, i <= nvalid))
    def _():
        wait_scatter(slot, nreal_ref[jnp.maximum(i - 2, 0)])

    @pl.when(i < nvalid)
    def _():
        nreal = nreal_ref[i]
        n_next = jnp.where(i + 1 < nvalid, nreal_ref[next_block], 0)

        @pl.when(jnp.logical_or(i == 0, be_ref[i] != be_ref[prev_block]))
        def _():
            wgu_bf[...] = wgu_ref[0].astype(BF16)
            wd_bf[...] = wd_ref[0].astype(BF16)

        wait_gather(slot, nreal)
        xb_sc[...] = _load_token_tiles(xbuf.at[slot], MOE_ROWS).astype(BF16)

        phase_rows = MOE_ROWS // MOE_PHASES
        phase_cols = D_EXPERT // MOE_PHASES
        for ph in range(MOE_PHASES):
            base = ph * phase_rows
            start_gather(toknext_ref, 1 - slot, base, jnp.clip(n_next - base, 0, phase_rows))
            start_scatter(dstprev_ref, 1 - slot, base, jnp.clip(n_prev - base, 0, phase_rows))
            xb = xb_sc[...]
            gcols = slice(ph * phase_cols, (ph + 1) * phase_cols)
            ucols = slice(D_EXPERT + ph * phase_cols, D_EXPERT + (ph + 1) * phase_cols)
            gate = jnp.dot(xb, wgu_bf[:, gcols], preferred_element_type=F32) + bgu_ref[0, :, gcols]
            up = jnp.dot(xb, wgu_bf[:, ucols], preferred_element_type=F32) + bgu_ref[0, :, ucols]
            gate = jnp.minimum(gate, SWIGLU_LIMIT)
            up = jnp.clip(up, -SWIGLU_LIMIT, SWIGLU_LIMIT)
            act = (up + 1.0) * gate * jax.nn.sigmoid(SWIGLU_ALPHA * gate)
            act_sc[:, gcols] = act.astype(BF16)

        y = jnp.dot(act_sc[...], wd_bf[...], preferred_element_type=F32) + bd_ref[0]
        _store_token_tiles(ybuf.at[slot], y)

    @pl.when(i == nvalid)
    def _():
        start_scatter(dstprev_ref, 1 - slot, 0, n_prev)
        wait_scatter(1 - slot, n_prev)


def _moe(block_expert, nvalid, nreal, row_token3, row_dest3, h2, wgu, bgu3, wd, bd3, n_slot_rows):
    nb = block_expert.shape[0]
    expert_of = lambda i, be: be[jnp.minimum(i, nb - 1)]
    grid_spec = pltpu.PrefetchScalarGridSpec(
        num_scalar_prefetch=3,
        grid=(nb + 1,),
        in_specs=[
            pl.BlockSpec((1, 1, MOE_ROWS), lambda i, be, nv, nr: (0, 0, 0), memory_space=pltpu.SMEM),
            pl.BlockSpec((1, 1, MOE_ROWS), lambda i, be, nv, nr: (jnp.minimum(i + 1, nb - 1), 0, 0),
                         memory_space=pltpu.SMEM),
            pl.BlockSpec((1, 1, MOE_ROWS), lambda i, be, nv, nr: (jnp.clip(i - 1, 0, nb - 1), 0, 0),
                         memory_space=pltpu.SMEM),
            pl.BlockSpec(memory_space=pl.ANY),
            pl.BlockSpec((1, D_MODEL, 2 * D_EXPERT), lambda i, be, nv, nr: (expert_of(i, be), 0, 0)),
            pl.BlockSpec((1, 1, 2 * D_EXPERT), lambda i, be, nv, nr: (expert_of(i, be), 0, 0)),
            pl.BlockSpec((1, D_EXPERT, D_MODEL), lambda i, be, nv, nr: (expert_of(i, be), 0, 0)),
            pl.BlockSpec((1, 1, D_MODEL), lambda i, be, nv, nr: (expert_of(i, be), 0, 0)),
        ],
        out_specs=pl.BlockSpec(memory_space=pl.ANY),
        scratch_shapes=[
            pltpu.VMEM((2, MOE_ROWS * TILE_ROWS, LANES), F32),
            pltpu.VMEM((2, MOE_ROWS * TILE_ROWS, LANES), F32),
            pltpu.VMEM((MOE_ROWS, D_MODEL), BF16),
            pltpu.VMEM((MOE_ROWS, D_EXPERT), BF16),
            pltpu.VMEM((D_MODEL, 2 * D_EXPERT), BF16),
            pltpu.VMEM((D_EXPERT, D_MODEL), BF16),
            pltpu.SemaphoreType.DMA((2,)),
            pltpu.SemaphoreType.DMA((2,)),
        ],
    )
    return pl.pallas_call(
        functools.partial(_moe_kernel, nb=nb),
        grid_spec=grid_spec,
        out_shape=jax.ShapeDtypeStruct((n_slot_rows * TILE_ROWS, LANES), F32),
        compiler_params=pltpu.CompilerParams(
            dimension_semantics=("arbitrary",), vmem_limit_bytes=VMEM_LIMIT),
        name="moe_experts",
    )(block_expert, nvalid, nreal, row_token3, row_token3, row_dest3, h2, wgu, bgu3, wd, bd3)


def _combine_kernel(x1_ref, s0_ref, s1_ref, s2_ref, s3_ref, topw_ref, p_ref, pg_ref, wpg_ref, wpp_ref, fg_ref,
                    out_ref):
    x2 = x1_ref[...]
    topw = topw_ref[...]
    for k, s_ref in enumerate((s0_ref, s1_ref, s2_ref, s3_ref)):
        x2 = x2 + _load_token_tiles(s_ref, x2.shape[0]) * topw[:, k:k + 1]
    n = _rms(x2, pg_ref[...])
    gate = jax.nn.sigmoid(jnp.dot(n.astype(BF16), wpg_ref[...], preferred_element_type=F32))
    pp = jnp.dot(p_ref[...].astype(BF16), wpp_ref[...], preferred_element_type=F32)
    x3 = x2 + gate * pp
    out_ref[...] = _rms(x3, fg_ref[...])


def _combine(x1, slots, topw, p2d, pg, wpg, wpp, fg, tm=512):
    t = x1.shape[0]
    nt = t // tm
    const = lambda i: (0, 0)
    slot_specs = [pl.BlockSpec((tm * TILE_ROWS, LANES), functools.partial(lambda k, i: (k * nt + i, 0), k))
                  for k in range(TOP_K)]
    return pl.pallas_call(
        _combine_kernel,
        grid=(nt,),
        in_specs=[
            pl.BlockSpec((tm, D_MODEL), lambda i: (i, 0)),
            *slot_specs,
            pl.BlockSpec((tm, LANES), lambda i: (i, 0)),
            pl.BlockSpec((tm, D_PLE), lambda i: (i, 0)),
            pl.BlockSpec((1, D_MODEL), const),
            pl.BlockSpec((D_MODEL, D_MODEL), const),
            pl.BlockSpec((D_PLE, D_MODEL), const),
            pl.BlockSpec((1, D_MODEL), const),
        ],
        out_specs=pl.BlockSpec((tm, D_MODEL), lambda i: (i, 0)),
        out_shape=jax.ShapeDtypeStruct((t, D_MODEL), F32),
        compiler_params=pltpu.CompilerParams(
            dimension_semantics=("arbitrary",), vmem_limit_bytes=VMEM_LIMIT),
        name="combine_ple",
    )(x1, slots, slots, slots, slots, topw, p2d, pg, wpg, wpp, fg)


def _routing_tables(top_idx, n_tok):
    n_assign = n_tok * TOP_K
    expert_flat = top_idx.reshape(-1)
    order = jnp.argsort(expert_flat, stable=True).astype(jnp.int32)
    counts = jnp.bincount(expert_flat, length=N_EXPERTS).astype(jnp.int32)
    start = jnp.cumsum(counts) - counts
    padded = (counts + MOE_ROWS - 1) // MOE_ROWS * MOE_ROWS
    pend = jnp.cumsum(padded)
    pstart = pend - padded
    n_rows = n_assign + N_EXPERTS * MOE_ROWS
    n_blocks = n_rows // MOE_ROWS
    block_start = jnp.arange(n_blocks, dtype=jnp.int32) * MOE_ROWS
    block_expert = jnp.minimum(jnp.sum(block_start[:, None] >= pend[None, :], axis=1),
                               N_EXPERTS - 1).astype(jnp.int32)
    nvalid = (pend[-1] // MOE_ROWS).astype(jnp.int32).reshape(1)
    is_e = block_expert[:, None] == jnp.arange(N_EXPERTS, dtype=jnp.int32)[None, :]
    per_block = lambda v: jnp.sum(jnp.where(is_e, v[None, :], 0), axis=1)
    nreal = jnp.clip(per_block(pstart + counts) - block_start, 0, MOE_ROWS).astype(jnp.int32)
    sorted_pos = (block_start + per_block(start - pstart))[:, None] + jnp.arange(MOE_ROWS, dtype=jnp.int32)[None, :]
    assign = order[jnp.clip(sorted_pos, 0, n_assign - 1)]
    row_token = assign // TOP_K
    row_dest = (assign % TOP_K) * n_tok + row_token
    return (block_expert, nvalid, nreal, (row_token * TILE_ROWS).reshape(n_blocks, 1, MOE_ROWS),
            (row_dest * TILE_ROWS).reshape(n_blocks, 1, MOE_ROWS), n_assign)


def _layer(x2d, p2d, bsz, seq, mix_norm_g, w_in, conv_w, conv_b, dt_bias, a_log, d_skip, ssd_norm_g,
           w_ssd_out, pool_w, pool_scale, w_mix_out, ffn_norm_g, w_router, b_router,
           w_gate_up, b_gate_up, w_down, b_down, ple_norm_g, w_ple_gate, w_ple_proj, out_g):
    n_tok = x2d.shape[0]
    dt0 = D_INNER + D_CONV
    w_main = jnp.concatenate([w_in[:, :dt0], w_in[:, dt0 + HEADS:]], axis=1).astype(BF16)
    w_dt = jnp.pad(w_in[:, dt0:dt0 + HEADS], ((0, 0), (0, LANES - HEADS)))
    wdt_hi, wdt_lo = _split2(w_dt)
    wdt3 = jnp.concatenate([wdt_hi, wdt_hi, wdt_lo], axis=0)

    proj, dt_raw = _in_proj(x2d, mix_norm_g[None, :], w_main, wdt3)

    pad_h = lambda v: jnp.pad(v, (0, LANES - HEADS))[None, :]
    ltri = (jnp.arange(CHUNK)[:, None] >= jnp.arange(CHUNK)[None, :]).astype(BF16)
    ltri3 = jnp.concatenate([ltri, ltri, ltri], axis=1)
    e1 = (jnp.arange(LANES)[:, None] == (jnp.arange(D_INNER) // HEAD_DIM)[None, :]).astype(BF16)
    e2 = jnp.concatenate([e1, e1], axis=0)
    yssd, ypool = _mixers(
        proj, dt_raw, bsz, seq, conv_w, conv_b[None, :], pad_h(dt_bias), pad_h(a_log),
        jnp.repeat(d_skip, HEAD_DIM)[None, :], ssd_norm_g[None, :], ltri3, e2,
        pool_w.astype(BF16), pool_scale[None, :])

    wr = jnp.pad(w_router, ((0, 0), (0, LANES - N_EXPERTS)))
    wr_hi, wr_lo = _split2(wr)
    wr3 = jnp.concatenate([wr_hi, wr_hi, wr_lo], axis=0)
    br = jnp.pad(b_router, (0, LANES - N_EXPERTS))[None, :]
    x1, h2, topw, topi = _mix_route(x2d, yssd, ypool, proj, w_ssd_out.astype(BF16),
                                    w_mix_out.astype(BF16), ffn_norm_g[None, :], wr3, br)

    block_expert, nvalid, nreal, row_token3, row_dest3, n_slot_rows = _routing_tables(topi[:, :TOP_K], n_tok)
    slots = _moe(block_expert, nvalid, nreal, row_token3, row_dest3, h2, w_gate_up, b_gate_up[:, None, :],
                 w_down, b_down[:, None, :], n_slot_rows)

    return _combine(x1, slots, topw, p2d, ple_norm_g[None, :], w_ple_gate.astype(BF16),
                    w_ple_proj.astype(BF16), out_g[None, :])


def kernel(x, p, mix_norm_g, w_in, conv_w, conv_b, dt_bias, a_log, d_skip, ssd_norm_g, w_ssd_out, pool_w,
           pool_scale, w_mix_out, ffn_norm_g, w_router, b_router, w_gate_up, b_gate_up, w_down, b_down,
           ple_norm_g, w_ple_gate, w_ple_proj, final_norm_g):
    bsz, seq, d = x.shape
    depth = p.shape[0]
    assert depth == 1 and d == D_MODEL and seq % CHUNK == 0
    x2d = x.reshape(bsz * seq, d)
    out = _layer(x2d, p[0].reshape(bsz * seq, D_PLE), bsz, seq, mix_norm_g[0], w_in[0], conv_w[0], conv_b[0],
                 dt_bias[0], a_log[0], d_skip[0], ssd_norm_g[0], w_ssd_out[0], pool_w[0], pool_scale[0],
                 w_mix_out[0], ffn_norm_g[0], w_router[0], b_router[0], w_gate_up[0], b_gate_up[0],
                 w_down[0], b_down[0], ple_norm_g[0], w_ple_gate[0], w_ple_proj[0], final_norm_g)
    return out.reshape(bsz, seq, d)
```

```python
import functools

import jax
import jax.numpy as jnp
from jax import lax
from jax.experimental import pallas as pl
from jax.experimental.pallas import tpu as pltpu

F32 = jnp.float32
BF16 = jnp.bfloat16

D_MODEL = 1024
D_INNER = 2048
HEAD_DIM = 64
HEADS = 32
GROUPS = 4
HEADS_PER_GROUP = HEADS // GROUPS
GROUP_DIM = D_INNER // GROUPS
D_STATE = 128
CONV_WIDTH = 4
CHUNK = 128
D_BC = 2 * GROUPS * D_STATE
D_CONV = D_INNER + D_BC
POOL_WIDTH = D_MODEL
POOL_WINDOWS = (2, 4, 8, 16)
POOL_GROUP_DIM = POOL_WIDTH // len(POOL_WINDOWS)
N_EXPERTS = 32
TOP_K = 4
D_EXPERT = D_MODEL
SWIGLU_LIMIT = 7.0
SWIGLU_ALPHA = 1.702
D_PLE = 256
EPS = 1e-6

LANES = 128
TILE_ROWS = D_MODEL // LANES
ROUTE_SUB = 256
MIX_CHUNKS = 2
HALO = 16
D_PROJ = D_INNER + D_CONV + POOL_WIDTH + 2 * D_MODEL
MOE_ROWS = 256
MOE_PHASES = 1
ROW_UNROLL = 8
VMEM_LIMIT = 56 * 1024 * 1024


def _split2(v):
    hi = v.astype(BF16)
    lo = (v - hi.astype(F32)).astype(BF16)
    return hi, lo


def _split3(v):
    hi = v.astype(BF16)
    r = v - hi.astype(F32)
    mid = r.astype(BF16)
    lo = (r - mid.astype(F32)).astype(BF16)
    return hi, mid, lo


def _rms(x, g):
    return x * lax.rsqrt(jnp.mean(x * x, axis=-1, keepdims=True) + EPS) * g


def _store_token_tiles(ref2d, val, row0=0):
    rows = val.shape[0]
    for j in range(TILE_ROWS):
        ref2d[pl.ds(row0 * TILE_ROWS + j, rows, stride=TILE_ROWS), :] = val[:, j * LANES:(j + 1) * LANES]


def _load_token_tiles(ref2d, rows):
    return jnp.concatenate([ref2d[pl.ds(j, rows, stride=TILE_ROWS), :] for j in range(TILE_ROWS)], axis=1)


def _in_proj_kernel(x_ref, g_ref, w_ref, wdt_ref, proj_ref, dt_ref, h_sc):
    @pl.when(pl.program_id(1) == 0)
    def _():
        h = _rms(x_ref[...], g_ref[...])
        hi, lo = _split2(h)
        h_sc[...] = hi
        lhs = jnp.concatenate([hi, lo, hi], axis=1)
        dt_ref[...] = jnp.dot(lhs, wdt_ref[...], preferred_element_type=F32)

    proj_ref[...] = jnp.dot(h_sc[...], w_ref[...], preferred_element_type=F32).astype(BF16)


def _in_proj(x2d, g, w_main, wdt3, tm=1024, tn=2048):
    t = x2d.shape[0]
    return pl.pallas_call(
        _in_proj_kernel,
        grid=(t // tm, D_PROJ // tn),
        in_specs=[
            pl.BlockSpec((tm, D_MODEL), lambda i, j: (i, 0)),
            pl.BlockSpec((1, D_MODEL), lambda i, j: (0, 0)),
            pl.BlockSpec((D_MODEL, tn), lambda i, j: (0, j)),
            pl.BlockSpec((3 * D_MODEL, LANES), lambda i, j: (0, 0)),
        ],
        out_specs=[
            pl.BlockSpec((tm, tn), lambda i, j: (i, j)),
            pl.BlockSpec((tm, LANES), lambda i, j: (i, 0)),
        ],
        out_shape=[
            jax.ShapeDtypeStruct((t, D_PROJ), BF16),
            jax.ShapeDtypeStruct((t, LANES), F32),
        ],
        scratch_shapes=[pltpu.VMEM((tm, D_MODEL), BF16)],
        compiler_params=pltpu.CompilerParams(
            dimension_semantics=("arbitrary", "arbitrary"), vmem_limit_bytes=VMEM_LIMIT),
        name="in_proj",
    )(x2d, g, w_main, wdt3)


def _mixers_kernel(z_ref, xs_ref, bc_ref, u_ref, dt_ref,
                   cw_ref, cb_ref, dtb_ref, alog_ref, dskip_ref, ng_ref, ltri_ref, e2_ref,
                   shift_ref, band_ref, pw_ref, ps_ref,
                   yssd_ref, ypool_ref,
                   ext_sc, extu_sc, state_sc):
    c = pl.program_id(1)
    rows = MIX_CHUNKS * CHUNK

    @pl.when(c == 0)
    def _():
        ext_sc[0:HALO, :] = jnp.zeros((HALO, D_CONV), BF16)
        extu_sc[0:HALO, :] = jnp.zeros((HALO, POOL_WIDTH), BF16)
        state_sc[...] = jnp.zeros_like(state_sc)

    @pl.when(c > 0)
    def _():
        ext_sc[0:HALO, :] = ext_sc[rows:rows + HALO, :]
        extu_sc[0:HALO, :] = extu_sc[rows:rows + HALO, :]

    ext_sc[HALO:HALO + rows, 0:D_INNER] = xs_ref[...]
    ext_sc[HALO:HALO + rows, D_INNER:D_CONV] = bc_ref[...]
    extu_sc[HALO:HALO + rows, :] = u_ref[...]
    for ci in range(MIX_CHUNKS):
        _mixers_chunk(ci, c * MIX_CHUNKS + ci, z_ref, u_ref, dt_ref, cw_ref, cb_ref, dtb_ref, alog_ref, dskip_ref,
                      ng_ref, ltri_ref, e2_ref, shift_ref, band_ref, pw_ref, ps_ref, yssd_ref, ypool_ref,
                      ext_sc, extu_sc, state_sc)


def _mixers_chunk(ci, chunk_index, z_ref, u_ref, dt_ref, cw_ref, cb_ref, dtb_ref, alog_ref, dskip_ref,
                  ng_ref, ltri_ref, e2_ref, shift_ref, band_ref, pw_ref, ps_ref, yssd_ref, ypool_ref,
                  ext_sc, extu_sc, state_sc):
    r0 = ci * CHUNK
    rs = slice(r0, r0 + CHUNK)
    ext = ext_sc[r0:r0 + HALO + CHUNK, :]
    conv = cb_ref[...] + cw_ref[CONV_WIDTH - 1:CONV_WIDTH, :] * ext[HALO:HALO + CHUNK, :].astype(F32)
    for k in range(CONV_WIDTH - 1):
        conv = conv + cw_ref[k:k + 1, :] * jnp.dot(shift_ref[k], ext, preferred_element_type=F32)
    xc = conv * jax.nn.sigmoid(conv)
    xs = xc[:, 0:D_INNER]
    xs_b = xs.astype(BF16)

    dtv = jax.nn.softplus(dt_ref[rs, :] + dtb_ref[...])
    da = dtv * (-jnp.exp(alog_ref[...]))
    a_cum = jnp.dot(ltri_ref[...], jnp.concatenate(_split3(da), axis=0),
                    preferred_element_type=F32)
    expa = jnp.exp(a_cum)
    a_last = a_cum[CHUNK - 1:CHUNK, :]
    wst = dtv * jnp.exp(a_last - a_cum)
    a_cum_t = a_cum.T
    dt_t = dtv.T

    both = jnp.concatenate([wst, expa], axis=0)
    hi, lo = _split2(both)
    expd = jnp.dot(jnp.concatenate([hi, lo], axis=1), e2_ref[...],
                   preferred_element_type=F32)
    wst_x = expd[0:CHUNK, :]
    expa_x = expd[CHUNK:2 * CHUNK, :]
    xw_b = (xs * wst_x).astype(BF16)

    row = lax.broadcasted_iota(jnp.int32, (CHUNK, CHUNK), 0)
    col = lax.broadcasted_iota(jnp.int32, (CHUNK, CHUNK), 1)
    causal = row >= col
    lane = lax.broadcasted_iota(jnp.int32, (CHUNK, LANES), 1)
    low_half = lane < HEAD_DIM

    y_groups = []
    for g in range(GROUPS):
        bg = xc[:, D_INNER + g * D_STATE:D_INNER + (g + 1) * D_STATE]
        cg = xc[:, D_INNER + GROUPS * D_STATE + g * D_STATE:D_INNER + GROUPS * D_STATE + (g + 1) * D_STATE]
        bg_b = bg.astype(BF16)
        cg_b = cg.astype(BF16)
        cbm = lax.dot_general(cg_b, bg_b, (((1,), (1,)), ((), ())), preferred_element_type=F32)
        gsl = slice(g * GROUP_DIM, (g + 1) * GROUP_DIM)

        prev_t = state_sc[g]
        y_off = jnp.dot(cg_b, prev_t.astype(BF16), preferred_element_type=F32) * expa_x[:, gsl]
        st_t = jnp.dot(bg.T.astype(BF16), xw_b[:, gsl], preferred_element_type=F32)
        state_sc[g] = prev_t * expa_x[CHUNK - 1:CHUNK, gsl] + st_t

        pairs = []
        for jp in range(HEADS_PER_GROUP // 2):
            ms = []
            for hh in range(2):
                h = g * HEADS_PER_GROUP + jp * 2 + hh
                seg = a_cum[:, h:h + 1] - a_cum_t[h:h + 1, :]
                dec = jnp.where(causal, jnp.exp(jnp.minimum(seg, 0.0)), 0.0)
                ms.append((cbm * dec * dt_t[h:h + 1, :]).astype(BF16))
            lhs = jnp.concatenate(ms, axis=1)
            c0 = g * GROUP_DIM + jp * LANES
            xp = xs_b[:, c0:c0 + LANES]
            zero = jnp.zeros_like(xp)
            rhs = jnp.concatenate([jnp.where(low_half, xp, zero), jnp.where(low_half, zero, xp)], axis=0)
            pairs.append(jnp.dot(lhs, rhs, preferred_element_type=F32))
        y_diag = jnp.concatenate(pairs, axis=1)

        yg = y_diag + y_off + dskip_ref[:, gsl] * xs[:, gsl]
        zg = z_ref[rs, gsl].astype(F32)
        yg = yg * (zg * jax.nn.sigmoid(zg))
        yg = yg * lax.rsqrt(jnp.mean(yg * yg, axis=-1, keepdims=True) + EPS) * ng_ref[:, gsl]
        y_groups.append(yg.astype(BF16))
    yssd_ref[rs, :] = jnp.concatenate(y_groups, axis=1)

    pos = chunk_index * CHUNK + lax.broadcasted_iota(jnp.int32, (CHUNK, 1), 0)
    outs = []
    for gi, w in enumerate(POOL_WINDOWS):
        psl = slice(gi * POOL_GROUP_DIM, (gi + 1) * POOL_GROUP_DIM)
        s = jnp.dot(band_ref[gi], extu_sc[r0:r0 + HALO + CHUNK, psl], preferred_element_type=F32)
        cnt = jnp.minimum(pos + 1, w).astype(F32)
        pooled = s / cnt - u_ref[rs, psl].astype(F32)
        outs.append(jnp.dot(pooled.astype(BF16), pw_ref[gi], preferred_element_type=F32))
    ypool_ref[rs, :] = (jnp.concatenate(outs, axis=1) * ps_ref[...]).astype(BF16)


def _mixers(proj, dt_raw, bsz, seq, cw, cb, dtb, alog, dskip, ng, ltri3, e2, pw, ps):
    rows = MIX_CHUNKS * CHUNK
    nc = seq // rows
    t = bsz * seq
    rowmap = lambda b, c: b * nc + c
    const2 = lambda b, c: (0, 0)
    const3 = lambda b, c: (0, 0, 0)
    trow = jnp.arange(CHUNK)[:, None] + HALO
    jcol = jnp.arange(HALO + CHUNK)[None, :]
    shifts = jnp.stack([(jcol == trow - (CONV_WIDTH - 1) + k) for k in range(CONV_WIDTH - 1)]).astype(BF16)
    bands = jnp.stack([(jcol <= trow) & (jcol > trow - w) for w in POOL_WINDOWS]).astype(BF16)
    return pl.pallas_call(
        _mixers_kernel,
        grid=(bsz, nc),
        in_specs=[
            pl.BlockSpec((rows, D_INNER), lambda b, c: (rowmap(b, c), 0)),
            pl.BlockSpec((rows, D_INNER), lambda b, c: (rowmap(b, c), 1)),
            pl.BlockSpec((rows, D_BC), lambda b, c: (rowmap(b, c), 4)),
            pl.BlockSpec((rows, POOL_WIDTH), lambda b, c: (rowmap(b, c), 5)),
            pl.BlockSpec((rows, LANES), lambda b, c: (rowmap(b, c), 0)),
            pl.BlockSpec((CONV_WIDTH, D_CONV), const2),
            pl.BlockSpec((1, D_CONV), const2),
            pl.BlockSpec((1, LANES), const2),
            pl.BlockSpec((1, LANES), const2),
            pl.BlockSpec((1, D_INNER), const2),
            pl.BlockSpec((1, D_INNER), const2),
            pl.BlockSpec((CHUNK, 3 * CHUNK), const2),
            pl.BlockSpec((2 * LANES, D_INNER), const2),
            pl.BlockSpec((CONV_WIDTH - 1, CHUNK, HALO + CHUNK), const3),
            pl.BlockSpec((len(POOL_WINDOWS), CHUNK, HALO + CHUNK), const3),
            pl.BlockSpec((len(POOL_WINDOWS), POOL_GROUP_DIM, POOL_GROUP_DIM), const3),
            pl.BlockSpec((1, POOL_WIDTH), const2),
        ],
        out_specs=[
            pl.BlockSpec((rows, D_INNER), lambda b, c: (rowmap(b, c), 0)),
            pl.BlockSpec((rows, POOL_WIDTH), lambda b, c: (rowmap(b, c), 0)),
        ],
        out_shape=[
            jax.ShapeDtypeStruct((t, D_INNER), BF16),
            jax.ShapeDtypeStruct((t, POOL_WIDTH), BF16),
        ],
        scratch_shapes=[
            pltpu.VMEM((HALO + rows, D_CONV), BF16),
            pltpu.VMEM((HALO + rows, POOL_WIDTH), BF16),
            pltpu.VMEM((GROUPS, D_STATE, GROUP_DIM), F32),
        ],
        compiler_params=pltpu.CompilerParams(
            dimension_semantics=("arbitrary", "arbitrary"), vmem_limit_bytes=VMEM_LIMIT),
        name="mixers",
    )(proj, proj, proj, proj, dt_raw, cw, cb, dtb, alog, dskip, ng, ltri3, e2, shifts, bands, pw, ps)


def _mix_route_kernel(x_ref, yssd_ref, ypool_ref, gates_ref, wso_ref, wmo_ref, fg_ref, wr_ref, br_ref,
                      x1_ref, h2_ref, topw_ref, topi_ref):
    for r0 in range(0, x_ref.shape[0], ROUTE_SUB):
        _mix_route_rows(slice(r0, r0 + ROUTE_SUB), x_ref, yssd_ref, ypool_ref, gates_ref, wso_ref, wmo_ref, fg_ref,
                        wr_ref, br_ref, x1_ref, h2_ref, topw_ref, topi_ref)


def _mix_route_rows(rs, x_ref, yssd_ref, ypool_ref, gates_ref, wso_ref, wmo_ref, fg_ref, wr_ref, br_ref,
                    x1_ref, h2_ref, topw_ref, topi_ref):
    y_ssd = jnp.dot(yssd_ref[rs, :], wso_ref[...], preferred_element_type=F32)
    gates = jax.nn.sigmoid(gates_ref[rs, :].astype(F32))
    mixed = gates[:, 0:D_MODEL] * y_ssd + gates[:, D_MODEL:2 * D_MODEL] * ypool_ref[rs, :].astype(F32)
    x1 = x_ref[rs, :] + jnp.dot(mixed.astype(BF16), wmo_ref[...], preferred_element_type=F32)
    x1_ref[rs, :] = x1
    h2 = _rms(x1, fg_ref[...])
    _store_token_tiles(h2_ref, h2, rs.start)

    hi, lo = _split2(h2)
    logits = jnp.dot(jnp.concatenate([hi, lo, hi], axis=1), wr_ref[...],
                     preferred_element_type=F32) + br_ref[...]
    tm = logits.shape[0]
    lane = lax.broadcasted_iota(jnp.int32, (tm, LANES), 1)
    neg = jnp.float32(-jnp.inf)
    work = jnp.where(lane < N_EXPERTS, logits, neg)
    vals = []
    idxs = []
    for _ in range(TOP_K):
        m = jnp.max(work, axis=-1, keepdims=True)
        idx = jnp.min(jnp.where(work == m, lane, LANES), axis=-1, keepdims=True)
        vals.append(m)
        idxs.append(idx)
        work = jnp.where(lane == idx, neg, work)
    es = [jnp.exp(v - vals[0]) for v in vals]
    den = es[0] + es[1] + es[2] + es[3]
    topw = jnp.zeros((tm, LANES), F32)
    topi = jnp.zeros((tm, LANES), jnp.int32)
    for k in range(TOP_K):
        topw = jnp.where(lane == k, es[k] / den, topw)
        topi = jnp.where(lane == k, idxs[k], topi)
    topw_ref[rs, :] = topw
    topi_ref[rs, :] = topi


def _mix_route(x2d, yssd, ypool, proj, wso, wmo, fg, wr3, br, tm=512):
    t = x2d.shape[0]
    const = lambda i: (0, 0)
    return pl.pallas_call(
        _mix_route_kernel,
        grid=(t // tm,),
        in_specs=[
            pl.BlockSpec((tm, D_MODEL), lambda i: (i, 0)),
            pl.BlockSpec((tm, D_INNER), lambda i: (i, 0)),
            pl.BlockSpec((tm, POOL_WIDTH), lambda i: (i, 0)),
            pl.BlockSpec((tm, 2 * D_MODEL), lambda i: (i, 3)),
            pl.BlockSpec((D_INNER, D_MODEL), const),
            pl.BlockSpec((D_MODEL, D_MODEL), const),
            pl.BlockSpec((1, D_MODEL), const),
            pl.BlockSpec((3 * D_MODEL, LANES), const),
            pl.BlockSpec((1, LANES), const),
        ],
        out_specs=[
            pl.BlockSpec((tm, D_MODEL), lambda i: (i, 0)),
            pl.BlockSpec((tm * TILE_ROWS, LANES), lambda i: (i, 0)),
            pl.BlockSpec((tm, LANES), lambda i: (i, 0)),
            pl.BlockSpec((tm, LANES), lambda i: (i, 0)),
        ],
        out_shape=[
            jax.ShapeDtypeStruct((t, D_MODEL), F32),
            jax.ShapeDtypeStruct((t * TILE_ROWS, LANES), F32),
            jax.ShapeDtypeStruct((t, LANES), F32),
            jax.ShapeDtypeStruct((t, LANES), jnp.int32),
        ],
        compiler_params=pltpu.CompilerParams(
            dimension_semantics=("arbitrary",), vmem_limit_bytes=VMEM_LIMIT),
        name="mix_route",
    )(x2d, yssd, ypool, proj, wso, wmo, fg, wr3, br)


def _moe_kernel(be_ref, nvalid_ref, nreal_ref,
                tok0_ref, toknext_ref, dstprev_ref,
                h_hbm, wgu_ref, bgu_ref, wd_ref, bd_ref,
                slots_hbm,
                xbuf, ybuf, xb_sc, act_sc, wgu_bf, wd_bf, gsem, ssem, *, nb):
    i = pl.program_id(0)
    nvalid = nvalid_ref[0]
    slot = lax.rem(i, 2)

    def tile(off, n=1):
        return pl.ds(pl.multiple_of(off, TILE_ROWS), n * TILE_ROWS)

    def gather_copy(tok, r, s):
        return pltpu.make_async_copy(h_hbm.at[tile(tok)], xbuf.at[s, tile(r * TILE_ROWS)], gsem.at[s])

    def scatter_copy(dst, r, s):
        return pltpu.make_async_copy(ybuf.at[s, tile(r * TILE_ROWS)], slots_hbm.at[tile(dst)], ssem.at[s])

    def for_rows(base, n, start_one):
        ngroups = lax.shift_right_logical(n, 3)

        def group(gi, carry):
            for u in range(ROW_UNROLL):
                start_one(base + gi * ROW_UNROLL + u, u % 2)
            return carry
        lax.fori_loop(0, ngroups, group, 0)

        def single(r, carry):
            start_one(r, 0)
            return carry
        lax.fori_loop(base + ngroups * ROW_UNROLL, base + n, single, 0)

    def start_gather(tok_ref, s, base, n):
        for_rows(base, n, lambda r, prio: gather_copy(tok_ref[0, 0, r], r, s).start())

    def start_scatter(dst_ref, s, base, n):
        for_rows(base, n, lambda r, prio: scatter_copy(dst_ref[0, 0, r], r, s).start(priority=prio))

    def wait_rows(n, wait_chunk):
        c = MOE_ROWS
        while c >= 1:
            pl.when(lax.bitwise_and(n, c) != 0)(functools.partial(wait_chunk, c))
            c //= 2

    def wait_gather(s, n):
        wait_rows(n, lambda c: pltpu.make_async_copy(
            h_hbm.at[tile(0, c)], xbuf.at[s, tile(0, c)], gsem.at[s]).wait())

    def wait_scatter(s, n):
        wait_rows(n, lambda c: pltpu.make_async_copy(
            ybuf.at[s, tile(0, c)], slots_hbm.at[tile(0, c)], ssem.at[s]).wait())

    prev_block = jnp.maximum(i - 1, 0)
    next_block = jnp.minimum(i + 1, nb - 1)
    n_prev = jnp.where(i >= 1, nreal_ref[prev_block], 0)

    @pl.when(i == 0)
    def _():
        xbuf[...] = jnp.zeros_like(xbuf)
        start_gather(tok0_ref, 0, 0, nreal_ref[0])

    @pl.when(jnp.logical_and(i >= 2, i <= nvalid))
    def _():
        wait_scatter(slot, nreal_ref[jnp.maximum(i - 2, 0)])

    @pl.when(i < nvalid)
    def _():
        nreal = nreal_ref[i]
        n_next = jnp.where(i + 1 < nvalid, nreal_ref[next_block], 0)

        @pl.when(jnp.logical_or(i == 0, be_ref[i] != be_ref[prev_block]))
        def _():
            wgu_bf[...] = wgu_ref[0].astype(BF16)
            wd_bf[...] = wd_ref[0].astype(BF16)

        wait_gather(slot, nreal)
        xb_sc[...] = _load_token_tiles(xbuf.at[slot], MOE_ROWS).astype(BF16)

        phase_rows = MOE_ROWS // MOE_PHASES
        phase_cols = D_EXPERT // MOE_PHASES
        for ph in range(MOE_PHASES):
            base = ph * phase_rows
            start_gather(toknext_ref, 1 - slot, base, jnp.clip(n_next - base, 0, phase_rows))
            start_scatter(dstprev_ref, 1 - slot, base, jnp.clip(n_prev - base, 0, phase_rows))
            xb = xb_sc[...]
            gcols = slice(ph * phase_cols, (ph + 1) * phase_cols)
            ucols = slice(D_EXPERT + ph * phase_cols, D_EXPERT + (ph + 1) * phase_cols)
            gate = jnp.dot(xb, wgu_bf[:, gcols], preferred_element_type=F32) + bgu_ref[0, :, gcols]
            up = jnp.dot(xb, wgu_bf[:, ucols], preferred_element_type=F32) + bgu_ref[0, :, ucols]
            gate = jnp.minimum(gate, SWIGLU_LIMIT)
            up = jnp.clip(up, -SWIGLU_LIMIT, SWIGLU_LIMIT)
            act = (up + 1.0) * gate * jax.nn.sigmoid(SWIGLU_ALPHA * gate)
            act_sc[:, gcols] = act.astype(BF16)

        y = jnp.dot(act_sc[...], wd_bf[...], preferred_element_type=F32) + bd_ref[0]
        _store_token_tiles(ybuf.at[slot], y)

    @pl.when(i == nvalid)
    def _():
        start_scatter(dstprev_ref, 1 - slot, 0, n_prev)
        wait_scatter(1 - slot, n_prev)


def _moe(block_expert, nvalid, nreal, row_token3, row_dest3, h2, wgu, bgu3, wd, bd3, n_slot_rows):
    nb = block_expert.shape[0]
    expert_of = lambda i, be: be[jnp.minimum(i, nb - 1)]
    grid_spec = pltpu.PrefetchScalarGridSpec(
        num_scalar_prefetch=3,
        grid=(nb + 1,),
        in_specs=[
            pl.BlockSpec((1, 1, MOE_ROWS), lambda i, be, nv, nr: (0, 0, 0), memory_space=pltpu.SMEM),
            pl.BlockSpec((1, 1, MOE_ROWS), lambda i, be, nv, nr: (jnp.minimum(i + 1, nb - 1), 0, 0),
                         memory_space=pltpu.SMEM),
            pl.BlockSpec((1, 1, MOE_ROWS), lambda i, be, nv, nr: (jnp.clip(i - 1, 0, nb - 1), 0, 0),
                         memory_space=pltpu.SMEM),
            pl.BlockSpec(memory_space=pl.ANY),
            pl.BlockSpec((1, D_MODEL, 2 * D_EXPERT), lambda i, be, nv, nr: (expert_of(i, be), 0, 0)),
            pl.BlockSpec((1, 1, 2 * D_EXPERT), lambda i, be, nv, nr: (expert_of(i, be), 0, 0)),
            pl.BlockSpec((1, D_EXPERT, D_MODEL), lambda i, be, nv, nr: (expert_of(i, be), 0, 0)),
            pl.BlockSpec((1, 1, D_MODEL), lambda i, be, nv, nr: (expert_of(i, be), 0, 0)),
        ],
        out_specs=pl.BlockSpec(memory_space=pl.ANY),
        scratch_shapes=[
            pltpu.VMEM((2, MOE_ROWS * TILE_ROWS, LANES), F32),
            pltpu.VMEM((2, MOE_ROWS * TILE_ROWS, LANES), F32),
            pltpu.VMEM((MOE_ROWS, D_MODEL), BF16),
            pltpu.VMEM((MOE_ROWS, D_EXPERT), BF16),
            pltpu.VMEM((D_MODEL, 2 * D_EXPERT), BF16),
            pltpu.VMEM((D_EXPERT, D_MODEL), BF16),
            pltpu.SemaphoreType.DMA((2,)),
            pltpu.SemaphoreType.DMA((2,)),
        ],
    )
    return pl.pallas_call(
        functools.partial(_moe_kernel, nb=nb),
        grid_spec=grid_spec,
        out_shape=jax.ShapeDtypeStruct((n_slot_rows * TILE_ROWS, LANES), F32),
        compiler_params=pltpu.CompilerParams(
            dimension_semantics=("arbitrary",), vmem_limit_bytes=VMEM_LIMIT),
        name="moe_experts",
    )(block_expert, nvalid, nreal, row_token3, row_token3, row_dest3, h2, wgu, bgu3, wd, bd3)


def _combine_kernel(x1_ref, s0_ref, s1_ref, s2_ref, s3_ref, topw_ref, p_ref, pg_ref, wpg_ref, wpp_ref, fg_ref,
                    out_ref):
    x2 = x1_ref[...]
    topw = topw_ref[...]
    for k, s_ref in enumerate((s0_ref, s1_ref, s2_ref, s3_ref)):
        x2 = x2 + _load_token_tiles(s_ref, x2.shape[0]) * topw[:, k:k + 1]
    n = _rms(x2, pg_ref[...])
    gate = jax.nn.sigmoid(jnp.dot(n.astype(BF16), wpg_ref[...], preferred_element_type=F32))
    pp = jnp.dot(p_ref[...].astype(BF16), wpp_ref[...], preferred_element_type=F32)
    x3 = x2 + gate * pp
    out_ref[...] = _rms(x3, fg_ref[...])


def _combine(x1, slots, topw, p2d, pg, wpg, wpp, fg, tm=512):
    t = x1.shape[0]
    nt = t // tm
    const = lambda i: (0, 0)
    slot_specs = [pl.BlockSpec((tm * TILE_ROWS, LANES), functools.partial(lambda k, i: (k * nt + i, 0), k))
                  for k in range(TOP_K)]
    return pl.pallas_call(
        _combine_kernel,
        grid=(nt,),
        in_specs=[
            pl.BlockSpec((tm, D_MODEL), lambda i: (i, 0)),
            *slot_specs,
            pl.BlockSpec((tm, LANES), lambda i: (i, 0)),
            pl.BlockSpec((tm, D_PLE), lambda i: (i, 0)),
            pl.BlockSpec((1, D_MODEL), const),
            pl.BlockSpec((D_MODEL, D_MODEL), const),
            pl.BlockSpec((D_PLE, D_MODEL), const),
            pl.BlockSpec((1, D_MODEL), const),
        ],
        out_specs=pl.BlockSpec((tm, D_MODEL), lambda i: (i, 0)),
        out_shape=jax.ShapeDtypeStruct((t, D_MODEL), F32),
        compiler_params=pltpu.CompilerParams(
            dimension_semantics=("arbitrary",), vmem_limit_bytes=VMEM_LIMIT),
        name="combine_ple",
    )(x1, slots, slots, slots, slots, topw, p2d, pg, wpg, wpp, fg)


def _routing_tables(top_idx, n_tok):
    n_assign = n_tok * TOP_K
    expert_flat = top_idx.reshape(-1)
    order = jnp.argsort(expert_flat, stable=True).astype(jnp.int32)
    counts = jnp.bincount(expert_flat, length=N_EXPERTS).astype(jnp.int32)
    start = jnp.cumsum(counts) - counts
    padded = (counts + MOE_ROWS - 1) // MOE_ROWS * MOE_ROWS
    pend = jnp.cumsum(padded)
    pstart = pend - padded
    n_rows = n_assign + N_EXPERTS * MOE_ROWS
    n_blocks = n_rows // MOE_ROWS
    block_start = jnp.arange(n_blocks, dtype=jnp.int32) * MOE_ROWS
    block_expert = jnp.minimum(jnp.sum(block_start[:, None] >= pend[None, :], axis=1),
                               N_EXPERTS - 1).astype(jnp.int32)
    nvalid = (pend[-1] // MOE_ROWS).astype(jnp.int32).reshape(1)
    is_e = block_expert[:, None] == jnp.arange(N_EXPERTS, dtype=jnp.int32)[None, :]
    per_block = lambda v: jnp.sum(jnp.where(is_e, v[None, :], 0), axis=1)
    nreal = jnp.clip(per_block(pstart + counts) - block_start, 0, MOE_ROWS).astype(jnp.int32)
    sorted_pos = (block_start + per_block(start - pstart))[:, None] + jnp.arange(MOE_ROWS, dtype=jnp.int32)[None, :]
    assign = order[jnp.clip(sorted_pos, 0, n_assign - 1)]
    row_token = assign // TOP_K
    row_dest = (assign % TOP_K) * n_tok + row_token
    return (block_expert, nvalid, nreal, (row_token * TILE_ROWS).reshape(n_blocks, 1, MOE_ROWS),
            (row_dest * TILE_ROWS).reshape(n_blocks, 1, MOE_ROWS), n_assign)


def _layer(x2d, p2d, bsz, seq, mix_norm_g, w_in, conv_w, conv_b, dt_bias, a_log, d_skip, ssd_norm_g,
           w_ssd_out, pool_w, pool_scale, w_mix_out, ffn_norm_g, w_router, b_router,
           w_gate_up, b_gate_up, w_down, b_down, ple_norm_g, w_ple_gate, w_ple_proj, out_g):
    n_tok = x2d.shape[0]
    dt0 = D_INNER + D_CONV
    w_main = jnp.concatenate([w_in[:, :dt0], w_in[:, dt0 + HEADS:]], axis=1).astype(BF16)
    w_dt = jnp.pad(w_in[:, dt0:dt0 + HEADS], ((0, 0), (0, LANES - HEADS)))
    wdt_hi, wdt_lo = _split2(w_dt)
    wdt3 = jnp.concatenate([wdt_hi, wdt_hi, wdt_lo], axis=0)

    proj, dt_raw = _in_proj(x2d, mix_norm_g[None, :], w_main, wdt3)

    pad_h = lambda v: jnp.pad(v, (0, LANES - HEADS))[None, :]
    ltri = (jnp.arange(CHUNK)[:, None] >= jnp.arange(CHUNK)[None, :]).astype(BF16)
    ltri3 = jnp.concatenate([ltri, ltri, ltri], axis=1)
    e1 = (jnp.arange(LANES)[:, None] == (jnp.arange(D_INNER) // HEAD_DIM)[None, :]).astype(BF16)
    e2 = jnp.concatenate([e1, e1], axis=0)
    yssd, ypool = _mixers(
        proj, dt_raw, bsz, seq, conv_w, conv_b[None, :], pad_h(dt_bias), pad_h(a_log),
        jnp.repeat(d_skip, HEAD_DIM)[None, :], ssd_norm_g[None, :], ltri3, e2,
        pool_w.astype(BF16), pool_scale[None, :])

    wr = jnp.pad(w_router, ((0, 0), (0, LANES - N_EXPERTS)))
    wr_hi, wr_lo = _split2(wr)
    wr3 = jnp.concatenate([wr_hi, wr_hi, wr_lo], axis=0)
    br = jnp.pad(b_router, (0, LANES - N_EXPERTS))[None, :]
    x1, h2, topw, topi = _mix_route(x2d, yssd, ypool, proj, w_ssd_out.astype(BF16),
                                    w_mix_out.astype(BF16), ffn_norm_g[None, :], wr3, br)

    block_expert, nvalid, nreal, row_token3, row_dest3, n_slot_rows = _routing_tables(topi[:, :TOP_K], n_tok)
    slots = _moe(block_expert, nvalid, nreal, row_token3, row_dest3, h2, w_gate_up, b_gate_up[:, None, :],
                 w_down, b_down[:, None, :], n_slot_rows)

    return _combine(x1, slots, topw, p2d, ple_norm_g[None, :], w_ple_gate.astype(BF16),
                    w_ple_proj.astype(BF16), out_g[None, :])


def kernel(x, p, mix_norm_g, w_in, conv_w, conv_b, dt_bias, a_log, d_skip, ssd_norm_g, w_ssd_out, pool_w,
           pool_scale, w_mix_out, ffn_norm_g, w_router, b_router, w_gate_up, b_gate_up, w_down, b_down,
           ple_norm_g, w_ple_gate, w_ple_proj, final_norm_g):
    bsz, seq, d = x.shape
    depth = p.shape[0]
    assert depth == 1 and d == D_MODEL and seq % CHUNK == 0
    x2d = x.reshape(bsz * seq, d)
    out = _layer(x2d, p[0].reshape(bsz * seq, D_PLE), bsz, seq, mix_norm_g[0], w_in[0], conv_w[0], conv_b[0],
                 dt_bias[0], a_log[0], d_skip[0], ssd_norm_g[0], w_ssd_out[0], pool_w[0], pool_scale[0],
                 w_mix_out[0], ffn_norm_g[0], w_router[0], b_router[0], w_gate_up[0], b_gate_up[0],
                 w_down[0], b_down[0], ple_norm_g[0], w_ple_gate[0], w_ple_proj[0], final_norm_g)
    return out.reshape(bsz, seq, d)
```

```python
import functools

import jax
import jax.numpy as jnp
from jax import lax
from jax.experimental import pallas as pl
from jax.experimental.pallas import tpu as pltpu

F32 = jnp.float32
BF16 = jnp.bfloat16

D_MODEL = 1024
D_INNER = 2048
HEAD_DIM = 64
HEADS = 32
GROUPS = 4
HEADS_PER_GROUP = HEADS // GROUPS
GROUP_DIM = D_INNER // GROUPS
D_STATE = 128
CONV_WIDTH = 4
CHUNK = 128
D_BC = 2 * GROUPS * D_STATE
D_CONV = D_INNER + D_BC
POOL_WIDTH = D_MODEL
POOL_WINDOWS = (2, 4, 8, 16)
POOL_GROUP_DIM = POOL_WIDTH // len(POOL_WINDOWS)
N_EXPERTS = 32
TOP_K = 4
D_EXPERT = D_MODEL
SWIGLU_LIMIT = 7.0
SWIGLU_ALPHA = 1.702
D_PLE = 256
EPS = 1e-6

LANES = 128
TILE_ROWS = D_MODEL // LANES
ROUTE_SUB = 256
CONV_BLOCK = 256
MIX_CHUNKS = 2
HALO = 16
D_PROJ = D_INNER + D_CONV + POOL_WIDTH + 2 * D_MODEL
MOE_ROWS = 256
MOE_PHASES = 1
ROW_UNROLL = 8
VMEM_LIMIT = 56 * 1024 * 1024


def _split2(v):
    hi = v.astype(BF16)
    lo = (v - hi.astype(F32)).astype(BF16)
    return hi, lo


def _split3(v):
    hi = v.astype(BF16)
    r = v - hi.astype(F32)
    mid = r.astype(BF16)
    lo = (r - mid.astype(F32)).astype(BF16)
    return hi, mid, lo


def _rms(x, g):
    return x * lax.rsqrt(jnp.mean(x * x, axis=-1, keepdims=True) + EPS) * g


def _store_token_tiles(ref2d, val, row0=0):
    rows = val.shape[0]
    for j in range(TILE_ROWS):
        ref2d[pl.ds(row0 * TILE_ROWS + j, rows, stride=TILE_ROWS), :] = val[:, j * LANES:(j + 1) * LANES]


def _load_token_tiles(ref2d, rows):
    return jnp.concatenate([ref2d[pl.ds(j, rows, stride=TILE_ROWS), :] for j in range(TILE_ROWS)], axis=1)


def _in_proj_kernel(x_ref, g_ref, w_ref, wdt_ref, proj_ref, dt_ref, h_sc):
    @pl.when(pl.program_id(1) == 0)
    def _():
        h = _rms(x_ref[...], g_ref[...])
        hi, lo = _split2(h)
        h_sc[...] = hi
        lhs = jnp.concatenate([hi, lo, hi], axis=1)
        dt_ref[...] = jnp.dot(lhs, wdt_ref[...], preferred_element_type=F32)

    proj_ref[...] = jnp.dot(h_sc[...], w_ref[...], preferred_element_type=F32).astype(BF16)


def _in_proj(x2d, g, w_main, wdt3, tm=1024, tn=2048):
    t = x2d.shape[0]
    return pl.pallas_call(
        _in_proj_kernel,
        grid=(t // tm, D_PROJ // tn),
        in_specs=[
            pl.BlockSpec((tm, D_MODEL), lambda i, j: (i, 0)),
            pl.BlockSpec((1, D_MODEL), lambda i, j: (0, 0)),
            pl.BlockSpec((D_MODEL, tn), lambda i, j: (0, j)),
            pl.BlockSpec((3 * D_MODEL, LANES), lambda i, j: (0, 0)),
        ],
        out_specs=[
            pl.BlockSpec((tm, tn), lambda i, j: (i, j)),
            pl.BlockSpec((tm, LANES), lambda i, j: (i, 0)),
        ],
        out_shape=[
            jax.ShapeDtypeStruct((t, D_PROJ), BF16),
            jax.ShapeDtypeStruct((t, LANES), F32),
        ],
        scratch_shapes=[pltpu.VMEM((tm, D_MODEL), BF16)],
        compiler_params=pltpu.CompilerParams(
            dimension_semantics=("arbitrary", "arbitrary"), vmem_limit_bytes=VMEM_LIMIT),
        name="in_proj",
    )(x2d, g, w_main, wdt3)


def _mixers_kernel(z_ref, xs_ref, bc_ref, u_ref, dt_ref,
                   cw_ref, cb_ref, dtb_ref, alog_ref, dskip_ref, ng_ref, ltri_ref, e2_ref,
                   shift_ref, band_ref, pw_ref, ps_ref,
                   yssd_ref, ypool_ref,
                   ext_sc, extu_sc, state_sc, xc_sc):
    c = pl.program_id(1)
    rows = MIX_CHUNKS * CHUNK

    @pl.when(c == 0)
    def _():
        ext_sc[0:HALO, :] = jnp.zeros((HALO, D_CONV), BF16)
        extu_sc[0:HALO, :] = jnp.zeros((HALO, POOL_WIDTH), BF16)
        state_sc[...] = jnp.zeros_like(state_sc)

    @pl.when(c > 0)
    def _():
        ext_sc[0:HALO, :] = ext_sc[rows:rows + HALO, :]
        extu_sc[0:HALO, :] = extu_sc[rows:rows + HALO, :]

    ext_sc[HALO:HALO + rows, 0:D_INNER] = xs_ref[...]
    ext_sc[HALO:HALO + rows, D_INNER:D_CONV] = bc_ref[...]
    extu_sc[HALO:HALO + rows, :] = u_ref[...]
    for ci in range(MIX_CHUNKS):
        _mixers_chunk(ci, c * MIX_CHUNKS + ci, z_ref, u_ref, dt_ref, cw_ref, cb_ref, dtb_ref, alog_ref, dskip_ref,
                      ng_ref, ltri_ref, e2_ref, shift_ref, band_ref, pw_ref, ps_ref, yssd_ref, ypool_ref,
                      ext_sc, extu_sc, state_sc, xc_sc)


def _mixers_chunk(ci, chunk_index, z_ref, u_ref, dt_ref, cw_ref, cb_ref, dtb_ref, alog_ref, dskip_ref,
                  ng_ref, ltri_ref, e2_ref, shift_ref, band_ref, pw_ref, ps_ref, yssd_ref, ypool_ref,
                  ext_sc, extu_sc, state_sc, xc_sc):
    r0 = ci * CHUNK
    rs = slice(r0, r0 + CHUNK)
    for c0 in range(0, D_CONV, CONV_BLOCK):
        cols = slice(c0, c0 + CONV_BLOCK)
        taps = jnp.dot(shift_ref[...], ext_sc[r0:r0 + HALO + CHUNK, cols], preferred_element_type=F32)
        conv = cb_ref[:, cols]
        for k in range(CONV_WIDTH):
            conv = conv + cw_ref[k:k + 1, cols] * taps[k * CHUNK:(k + 1) * CHUNK, :]
        xc_sc[ci, :, cols] = conv * jax.nn.sigmoid(conv)
    xc = xc_sc.at[ci]
    xs = xc[:, 0:D_INNER]
    xs_b = xs.astype(BF16)

    dtv = jax.nn.softplus(dt_ref[rs, :] + dtb_ref[...])
    da = dtv * (-jnp.exp(alog_ref[...]))
    a_cum = jnp.dot(ltri_ref[...], jnp.concatenate(_split3(da), axis=0),
                    preferred_element_type=F32)
    expa = jnp.exp(a_cum)
    a_last = a_cum[CHUNK - 1:CHUNK, :]
    wst = dtv * jnp.exp(a_last - a_cum)
    a_cum_t = a_cum.T
    dt_t = dtv.T

    both = jnp.concatenate([wst, expa], axis=0)
    hi, lo = _split2(both)
    expd = jnp.dot(jnp.concatenate([hi, lo], axis=1), e2_ref[...],
                   preferred_element_type=F32)
    wst_x = expd[0:CHUNK, :]
    expa_x = expd[CHUNK:2 * CHUNK, :]
    xw_b = (xs * wst_x).astype(BF16)

    row = lax.broadcasted_iota(jnp.int32, (CHUNK, CHUNK), 0)
    col = lax.broadcasted_iota(jnp.int32, (CHUNK, CHUNK), 1)
    causal = row >= col
    lane = lax.broadcasted_iota(jnp.int32, (CHUNK, LANES), 1)
    low_half = lane < HEAD_DIM

    y_groups = []
    for g in range(GROUPS):
        bg = xc[:, D_INNER + g * D_STATE:D_INNER + (g + 1) * D_STATE]
        cg = xc[:, D_INNER + GROUPS * D_STATE + g * D_STATE:D_INNER + GROUPS * D_STATE + (g + 1) * D_STATE]
        bg_b = bg.astype(BF16)
        cg_b = cg.astype(BF16)
        cbm = lax.dot_general(cg_b, bg_b, (((1,), (1,)), ((), ())), preferred_element_type=F32)
        gsl = slice(g * GROUP_DIM, (g + 1) * GROUP_DIM)

        prev_t = state_sc[g]
        y_off = jnp.dot(cg_b, prev_t.astype(BF16), preferred_element_type=F32) * expa_x[:, gsl]
        st_t = jnp.dot(bg.T.astype(BF16), xw_b[:, gsl], preferred_element_type=F32)
        state_sc[g] = prev_t * expa_x[CHUNK - 1:CHUNK, gsl] + st_t

        pairs = []
        for jp in range(HEADS_PER_GROUP // 2):
            ms = []
            for hh in range(2):
                h = g * HEADS_PER_GROUP + jp * 2 + hh
                seg = a_cum[:, h:h + 1] - a_cum_t[h:h + 1, :]
                dec = jnp.where(causal, jnp.exp(jnp.minimum(seg, 0.0)), 0.0)
                ms.append((cbm * dec * dt_t[h:h + 1, :]).astype(BF16))
            lhs = jnp.concatenate(ms, axis=1)
            c0 = g * GROUP_DIM + jp * LANES
            xp = xs_b[:, c0:c0 + LANES]
            zero = jnp.zeros_like(xp)
            rhs = jnp.concatenate([jnp.where(low_half, xp, zero), jnp.where(low_half, zero, xp)], axis=0)
            pairs.append(jnp.dot(lhs, rhs, preferred_element_type=F32))
        y_diag = jnp.concatenate(pairs, axis=1)

        yg = y_diag + y_off + dskip_ref[:, gsl] * xs[:, gsl]
        zg = z_ref[rs, gsl].astype(F32)
        yg = yg * (zg * jax.nn.sigmoid(zg))
        yg = yg * lax.rsqrt(jnp.mean(yg * yg, axis=-1, keepdims=True) + EPS) * ng_ref[:, gsl]
        y_groups.append(yg.astype(BF16))
    yssd_ref[rs, :] = jnp.concatenate(y_groups, axis=1)

    pos = chunk_index * CHUNK + lax.broadcasted_iota(jnp.int32, (CHUNK, 1), 0)
    outs = []
    for gi, w in enumerate(POOL_WINDOWS):
        psl = slice(gi * POOL_GROUP_DIM, (gi + 1) * POOL_GROUP_DIM)
        s = jnp.dot(band_ref[gi], extu_sc[r0:r0 + HALO + CHUNK, psl], preferred_element_type=F32)
        cnt = jnp.minimum(pos + 1, w).astype(F32)
        pooled = s / cnt - u_ref[rs, psl].astype(F32)
        outs.append(jnp.dot(pooled.astype(BF16), pw_ref[gi], preferred_element_type=F32))
    ypool_ref[rs, :] = (jnp.concatenate(outs, axis=1) * ps_ref[...]).astype(BF16)


def _mixers(proj, dt_raw, bsz, seq, cw, cb, dtb, alog, dskip, ng, ltri3, e2, pw, ps):
    rows = MIX_CHUNKS * CHUNK
    nc = seq // rows
    t = bsz * seq
    rowmap = lambda b, c: b * nc + c
    const2 = lambda b, c: (0, 0)
    const3 = lambda b, c: (0, 0, 0)
    trow = jnp.arange(CHUNK)[:, None] + HALO
    jcol = jnp.arange(HALO + CHUNK)[None, :]
    shifts = jnp.concatenate([(jcol == trow - (CONV_WIDTH - 1) + k) for k in range(CONV_WIDTH)], axis=0).astype(BF16)
    bands = jnp.stack([(jcol <= trow) & (jcol > trow - w) for w in POOL_WINDOWS]).astype(BF16)
    return pl.pallas_call(
        _mixers_kernel,
        grid=(bsz, nc),
        in_specs=[
            pl.BlockSpec((rows, D_INNER), lambda b, c: (rowmap(b, c), 0)),
            pl.BlockSpec((rows, D_INNER), lambda b, c: (rowmap(b, c), 1)),
            pl.BlockSpec((rows, D_BC), lambda b, c: (rowmap(b, c), 4)),
            pl.BlockSpec((rows, POOL_WIDTH), lambda b, c: (rowmap(b, c), 5)),
            pl.BlockSpec((rows, LANES), lambda b, c: (rowmap(b, c), 0)),
            pl.BlockSpec((CONV_WIDTH, D_CONV), const2),
            pl.BlockSpec((1, D_CONV), const2),
            pl.BlockSpec((1, LANES), const2),
            pl.BlockSpec((1, LANES), const2),
            pl.BlockSpec((1, D_INNER), const2),
            pl.BlockSpec((1, D_INNER), const2),
            pl.BlockSpec((CHUNK, 3 * CHUNK), const2),
            pl.BlockSpec((2 * LANES, D_INNER), const2),
            pl.BlockSpec((CONV_WIDTH * CHUNK, HALO + CHUNK), const2),
            pl.BlockSpec((len(POOL_WINDOWS), CHUNK, HALO + CHUNK), const3),
            pl.BlockSpec((len(POOL_WINDOWS), POOL_GROUP_DIM, POOL_GROUP_DIM), const3),
            pl.BlockSpec((1, POOL_WIDTH), const2),
        ],
        out_specs=[
            pl.BlockSpec((rows, D_INNER), lambda b, c: (rowmap(b, c), 0)),
            pl.BlockSpec((rows, POOL_WIDTH), lambda b, c: (rowmap(b, c), 0)),
        ],
        out_shape=[
            jax.ShapeDtypeStruct((t, D_INNER), BF16),
            jax.ShapeDtypeStruct((t, POOL_WIDTH), BF16),
        ],
        scratch_shapes=[
            pltpu.VMEM((HALO + rows, D_CONV), BF16),
            pltpu.VMEM((HALO + rows, POOL_WIDTH), BF16),
            pltpu.VMEM((GROUPS, D_STATE, GROUP_DIM), F32),
            pltpu.VMEM((MIX_CHUNKS, CHUNK, D_CONV), F32),
        ],
        compiler_params=pltpu.CompilerParams(
            dimension_semantics=("arbitrary", "arbitrary"), vmem_limit_bytes=VMEM_LIMIT),
        name="mixers",
    )(proj, proj, proj, proj, dt_raw, cw, cb, dtb, alog, dskip, ng, ltri3, e2, shifts, bands, pw, ps)


def _mix_route_kernel(x_ref, yssd_ref, ypool_ref, gates_ref, wso_ref, wmo_ref, fg_ref, wr_ref, br_ref,
                      x1_ref, h2_ref, topw_ref, topi_ref):
    for r0 in range(0, x_ref.shape[0], ROUTE_SUB):
        _mix_route_rows(slice(r0, r0 + ROUTE_SUB), x_ref, yssd_ref, ypool_ref, gates_ref, wso_ref, wmo_ref, fg_ref,
                        wr_ref, br_ref, x1_ref, h2_ref, topw_ref, topi_ref)


def _mix_route_rows(rs, x_ref, yssd_ref, ypool_ref, gates_ref, wso_ref, wmo_ref, fg_ref, wr_ref, br_ref,
                    x1_ref, h2_ref, topw_ref, topi_ref):
    y_ssd = jnp.dot(yssd_ref[rs, :], wso_ref[...], preferred_element_type=F32)
    gates = jax.nn.sigmoid(gates_ref[rs, :].astype(F32))
    mixed = gates[:, 0:D_MODEL] * y_ssd + gates[:, D_MODEL:2 * D_MODEL] * ypool_ref[rs, :].astype(F32)
    x1 = x_ref[rs, :] + jnp.dot(mixed.astype(BF16), wmo_ref[...], preferred_element_type=F32)
    x1_ref[rs, :] = x1
    h2 = _rms(x1, fg_ref[...])
    _store_token_tiles(h2_ref, h2, rs.start)

    hi, lo = _split2(h2)
    logits = jnp.dot(jnp.concatenate([hi, lo, hi], axis=1), wr_ref[...],
                     preferred_element_type=F32) + br_ref[...]
    tm = logits.shape[0]
    lane = lax.broadcasted_iota(jnp.int32, (tm, LANES), 1)
    neg = jnp.float32(-jnp.inf)
    work = jnp.where(lane < N_EXPERTS, logits, neg)
    vals = []
    idxs = []
    for _ in range(TOP_K):
        m = jnp.max(work, axis=-1, keepdims=True)
        idx = jnp.min(jnp.where(work == m, lane, LANES), axis=-1, keepdims=True)
        vals.append(m)
        idxs.append(idx)
        work = jnp.where(lane == idx, neg, work)
    es = [jnp.exp(v - vals[0]) for v in vals]
    den = es[0] + es[1] + es[2] + es[3]
    topw = jnp.zeros((tm, LANES), F32)
    topi = jnp.zeros((tm, LANES), jnp.int32)
    for k in range(TOP_K):
        topw = jnp.where(lane == k, es[k] / den, topw)
        topi = jnp.where(lane == k, idxs[k], topi)
    topw_ref[rs, :] = topw
    topi_ref[rs, :] = topi


def _mix_route(x2d, yssd, ypool, proj, wso, wmo, fg, wr3, br, tm=512):
    t = x2d.shape[0]
    const = lambda i: (0, 0)
    return pl.pallas_call(
        _mix_route_kernel,
        grid=(t // tm,),
        in_specs=[
            pl.BlockSpec((tm, D_MODEL), lambda i: (i, 0)),
            pl.BlockSpec((tm, D_INNER), lambda i: (i, 0)),
            pl.BlockSpec((tm, POOL_WIDTH), lambda i: (i, 0)),
            pl.BlockSpec((tm, 2 * D_MODEL), lambda i: (i, 3)),
            pl.BlockSpec((D_INNER, D_MODEL), const),
            pl.BlockSpec((D_MODEL, D_MODEL), const),
            pl.BlockSpec((1, D_MODEL), const),
            pl.BlockSpec((3 * D_MODEL, LANES), const),
            pl.BlockSpec((1, LANES), const),
        ],
        out_specs=[
            pl.BlockSpec((tm, D_MODEL), lambda i: (i, 0)),
            pl.BlockSpec((tm * TILE_ROWS, LANES), lambda i: (i, 0)),
            pl.BlockSpec((tm, LANES), lambda i: (i, 0)),
            pl.BlockSpec((tm, LANES), lambda i: (i, 0)),
        ],
        out_shape=[
            jax.ShapeDtypeStruct((t, D_MODEL), F32),
            jax.ShapeDtypeStruct((t * TILE_ROWS, LANES), F32),
            jax.ShapeDtypeStruct((t, LANES), F32),
            jax.ShapeDtypeStruct((t, LANES), jnp.int32),
        ],
        compiler_params=pltpu.CompilerParams(
            dimension_semantics=("arbitrary",), vmem_limit_bytes=VMEM_LIMIT),
        name="mix_route",
    )(x2d, yssd, ypool, proj, wso, wmo, fg, wr3, br)


def _moe_kernel(be_ref, nvalid_ref, nreal_ref,
                tok0_ref, toknext_ref, dstprev_ref,
                h_hbm, wgu_ref, bgu_ref, wd_ref, bd_ref,
                slots_hbm,
                xbuf, ybuf, xb_sc, act_sc, wgu_bf, wd_bf, gsem, ssem, *, nb):
    i = pl.program_id(0)
    nvalid = nvalid_ref[0]
    slot = lax.rem(i, 2)

    def tile(off, n=1):
        return pl.ds(pl.multiple_of(off, TILE_ROWS), n * TILE_ROWS)

    def gather_copy(tok, r, s):
        return pltpu.make_async_copy(h_hbm.at[tile(tok)], xbuf.at[s, tile(r * TILE_ROWS)], gsem.at[s])

    def scatter_copy(dst, r, s):
        return pltpu.make_async_copy(ybuf.at[s, tile(r * TILE_ROWS)], slots_hbm.at[tile(dst)], ssem.at[s])

    def for_rows(base, n, start_one):
        ngroups = lax.shift_right_logical(n, 3)

        def group(gi, carry):
            for u in range(ROW_UNROLL):
                start_one(base + gi * ROW_UNROLL + u, u % 2)
            return carry
        lax.fori_loop(0, ngroups, group, 0)

        def single(r, carry):
            start_one(r, 0)
            return carry
        lax.fori_loop(base + ngroups * ROW_UNROLL, base + n, single, 0)

    def start_gather(tok_ref, s, base, n):
        for_rows(base, n, lambda r, prio: gather_copy(tok_ref[0, 0, r], r, s).start())

    def start_scatter(dst_ref, s, base, n):
        for_rows(base, n, lambda r, prio: scatter_copy(dst_ref[0, 0, r], r, s).start(priority=prio))

    def wait_rows(n, wait_chunk):
        c = MOE_ROWS
        while c >= 1:
            pl.when(lax.bitwise_and(n, c) != 0)(functools.partial(wait_chunk, c))
            c //= 2

    def wait_gather(s, n):
        wait_rows(n, lambda c: pltpu.make_async_copy(
            h_hbm.at[tile(0, c)], xbuf.at[s, tile(0, c)], gsem.at[s]).wait())

    def wait_scatter(s, n):
        wait_rows(n, lambda c: pltpu.make_async_copy(
            ybuf.at[s, tile(0, c)], slots_hbm.at[tile(0, c)], ssem.at[s]).wait())

    prev_block = jnp.maximum(i - 1, 0)
    next_block = jnp.minimum(i + 1, nb - 1)
    n_prev = jnp.where(i >= 1, nreal_ref[prev_block], 0)

    @pl.when(i == 0)
    def _():
        xbuf[...] = jnp.zeros_like(xbuf)
        start_gather(tok0_ref, 0, 0, nreal_ref[0])

    @pl.when(jnp.logical_and(i >= 2, i <= nvalid))
    def _():
        wait_scatter(slot, nreal_ref[jnp.maximum(i - 2, 0)])

    @pl.when(i < nvalid)
    def _():
        nreal = nreal_ref[i]
        n_next = jnp.where(i + 1 < nvalid, nreal_ref[next_block], 0)

        @pl.when(jnp.logical_or(i == 0, be_ref[i] != be_ref[prev_block]))
        def _():
            wgu_bf[...] = wgu_ref[0].astype(BF16)
            wd_bf[...] = wd_ref[0].astype(BF16)

        wait_gather(slot, nreal)
        xb_sc[...] = _load_token_tiles(xbuf.at[slot], MOE_ROWS).astype(BF16)

        phase_rows = MOE_ROWS // MOE_PHASES
        phase_cols = D_EXPERT // MOE_PHASES
        for ph in range(MOE_PHASES):
            base = ph * phase_rows
            start_gather(toknext_ref, 1 - slot, base, jnp.clip(n_next - base, 0, phase_rows))
            start_scatter(dstprev_ref, 1 - slot, base, jnp.clip(n_prev - base, 0, phase_rows))
            xb = xb_sc[...]
            gcols = slice(ph * phase_cols, (ph + 1) * phase_cols)
            ucols = slice(D_EXPERT + ph * phase_cols, D_EXPERT + (ph + 1) * phase_cols)
            gate = jnp.dot(xb, wgu_bf[:, gcols], preferred_element_type=F32) + bgu_ref[0, :, gcols]
            up = jnp.dot(xb, wgu_bf[:, ucols], preferred_element_type=F32) + bgu_ref[0, :, ucols]
            gate = jnp.minimum(gate, SWIGLU_LIMIT)
            up = jnp.clip(up, -SWIGLU_LIMIT, SWIGLU_LIMIT)
            act = (up + 1.0) * gate * jax.nn.sigmoid(SWIGLU_ALPHA * gate)
            act_sc[:, gcols] = act.astype(BF16)

        y = jnp.dot(act_sc[...], wd_bf[...], preferred_element_type=F32) + bd_ref[0]
        _store_token_tiles(ybuf.at[slot], y)

    @pl.when(i == nvalid)
    def _():
        start_scatter(dstprev_ref, 1 - slot, 0, n_prev)
        wait_scatter(1 - slot, n_prev)


def _moe(block_expert, nvalid, nreal, row_token3, row_dest3, h2, wgu, bgu3, wd, bd3, n_slot_rows):
    nb = block_expert.shape[0]
    expert_of = lambda i, be: be[jnp.minimum(i, nb - 1)]
    grid_spec = pltpu.PrefetchScalarGridSpec(
        num_scalar_prefetch=3,
        grid=(nb + 1,),
        in_specs=[
            pl.BlockSpec((1, 1, MOE_ROWS), lambda i, be, nv, nr: (0, 0, 0), memory_space=pltpu.SMEM),
            pl.BlockSpec((1, 1, MOE_ROWS), lambda i, be, nv, nr: (jnp.minimum(i + 1, nb - 1), 0, 0),
                         memory_space=pltpu.SMEM),
            pl.BlockSpec((1, 1, MOE_ROWS), lambda i, be, nv, nr: (jnp.clip(i - 1, 0, nb - 1), 0, 0),
                         memory_space=pltpu.SMEM),
            pl.BlockSpec(memory_space=pl.ANY),
            pl.BlockSpec((1, D_MODEL, 2 * D_EXPERT), lambda i, be, nv, nr: (expert_of(i, be), 0, 0)),
            pl.BlockSpec((1, 1, 2 * D_EXPERT), lambda i, be, nv, nr: (expert_of(i, be), 0, 0)),
            pl.BlockSpec((1, D_EXPERT, D_MODEL), lambda i, be, nv, nr: (expert_of(i, be), 0, 0)),
            pl.BlockSpec((1, 1, D_MODEL), lambda i, be, nv, nr: (expert_of(i, be), 0, 0)),
        ],
        out_specs=pl.BlockSpec(memory_space=pl.ANY),
        scratch_shapes=[
            pltpu.VMEM((2, MOE_ROWS * TILE_ROWS, LANES), F32),
            pltpu.VMEM((2, MOE_ROWS * TILE_ROWS, LANES), F32),
            pltpu.VMEM((MOE_ROWS, D_MODEL), BF16),
            pltpu.VMEM((MOE_ROWS, D_EXPERT), BF16),
            pltpu.VMEM((D_MODEL, 2 * D_EXPERT), BF16),
            pltpu.VMEM((D_EXPERT, D_MODEL), BF16),
            pltpu.SemaphoreType.DMA((2,)),
            pltpu.SemaphoreType.DMA((2,)),
        ],
    )
    return pl.pallas_call(
        functools.partial(_moe_kernel, nb=nb),
        grid_spec=grid_spec,
        out_shape=jax.ShapeDtypeStruct((n_slot_rows * TILE_ROWS, LANES), F32),
        compiler_params=pltpu.CompilerParams(
            dimension_semantics=("arbitrary",), vmem_limit_bytes=VMEM_LIMIT),
        name="moe_experts",
    )(block_expert, nvalid, nreal, row_token3, row_token3, row_dest3, h2, wgu, bgu3, wd, bd3)


def _combine_kernel(x1_ref, s0_ref, s1_ref, s2_ref, s3_ref, topw_ref, p_ref, pg_ref, wpg_ref, wpp_ref, fg_ref,
                    out_ref):
    x2 = x1_ref[...]
    topw = topw_ref[...]
    for k, s_ref in enumerate((s0_ref, s1_ref, s2_ref, s3_ref)):
        x2 = x2 + _load_token_tiles(s_ref, x2.shape[0]) * topw[:, k:k + 1]
    n = _rms(x2, pg_ref[...])
    gate = jax.nn.sigmoid(jnp.dot(n.astype(BF16), wpg_ref[...], preferred_element_type=F32))
    pp = jnp.dot(p_ref[...].astype(BF16), wpp_ref[...], preferred_element_type=F32)
    x3 = x2 + gate * pp
    out_ref[...] = _rms(x3, fg_ref[...])


def _combine(x1, slots, topw, p2d, pg, wpg, wpp, fg, tm=512):
    t = x1.shape[0]
    nt = t // tm
    const = lambda i: (0, 0)
    slot_specs = [pl.BlockSpec((tm * TILE_ROWS, LANES), functools.partial(lambda k, i: (k * nt + i, 0), k))
                  for k in range(TOP_K)]
    return pl.pallas_call(
        _combine_kernel,
        grid=(nt,),
        in_specs=[
            pl.BlockSpec((tm, D_MODEL), lambda i: (i, 0)),
            *slot_specs,
            pl.BlockSpec((tm, LANES), lambda i: (i, 0)),
            pl.BlockSpec((tm, D_PLE), lambda i: (i, 0)),
            pl.BlockSpec((1, D_MODEL), const),
            pl.BlockSpec((D_MODEL, D_MODEL), const),
            pl.BlockSpec((D_PLE, D_MODEL), const),
            pl.BlockSpec((1, D_MODEL), const),
        ],
        out_specs=pl.BlockSpec((tm, D_MODEL), lambda i: (i, 0)),
        out_shape=jax.ShapeDtypeStruct((t, D_MODEL), F32),
        compiler_params=pltpu.CompilerParams(
            dimension_semantics=("arbitrary",), vmem_limit_bytes=VMEM_LIMIT),
        name="combine_ple",
    )(x1, slots, slots, slots, slots, topw, p2d, pg, wpg, wpp, fg)


def _routing_tables(top_idx, n_tok):
    n_assign = n_tok * TOP_K
    expert_flat = top_idx.reshape(-1)
    order = jnp.argsort(expert_flat, stable=True).astype(jnp.int32)
    counts = jnp.bincount(expert_flat, length=N_EXPERTS).astype(jnp.int32)
    start = jnp.cumsum(counts) - counts
    padded = (counts + MOE_ROWS - 1) // MOE_ROWS * MOE_ROWS
    pend = jnp.cumsum(padded)
    pstart = pend - padded
    n_rows = n_assign + N_EXPERTS * MOE_ROWS
    n_blocks = n_rows // MOE_ROWS
    block_start = jnp.arange(n_blocks, dtype=jnp.int32) * MOE_ROWS
    block_expert = jnp.minimum(jnp.sum(block_start[:, None] >= pend[None, :], axis=1),
                               N_EXPERTS - 1).astype(jnp.int32)
    nvalid = (pend[-1] // MOE_ROWS).astype(jnp.int32).reshape(1)
    is_e = block_expert[:, None] == jnp.arange(N_EXPERTS, dtype=jnp.int32)[None, :]
    per_block = lambda v: jnp.sum(jnp.where(is_e, v[None, :], 0), axis=1)
    nreal = jnp.clip(per_block(pstart + counts) - block_start, 0, MOE_ROWS).astype(jnp.int32)
    sorted_pos = (block_start + per_block(start - pstart))[:, None] + jnp.arange(MOE_ROWS, dtype=jnp.int32)[None, :]
    assign = order[jnp.clip(sorted_pos, 0, n_assign - 1)]
    row_token = assign // TOP_K
    row_dest = (assign % TOP_K) * n_tok + row_token
    return (block_expert, nvalid, nreal, (row_token * TILE_ROWS).reshape(n_blocks, 1, MOE_ROWS),
            (row_dest * TILE_ROWS).reshape(n_blocks, 1, MOE_ROWS), n_assign)


def _layer(x2d, p2d, bsz, seq, mix_norm_g, w_in, conv_w, conv_b, dt_bias, a_log, d_skip, ssd_norm_g,
           w_ssd_out, pool_w, pool_scale, w_mix_out, ffn_norm_g, w_router, b_router,
           w_gate_up, b_gate_up, w_down, b_down, ple_norm_g, w_ple_gate, w_ple_proj, out_g):
    n_tok = x2d.shape[0]
    dt0 = D_INNER + D_CONV
    w_main = jnp.concatenate([w_in[:, :dt0], w_in[:, dt0 + HEADS:]], axis=1).astype(BF16)
    w_dt = jnp.pad(w_in[:, dt0:dt0 + HEADS], ((0, 0), (0, LANES - HEADS)))
    wdt_hi, wdt_lo = _split2(w_dt)
    wdt3 = jnp.concatenate([wdt_hi, wdt_hi, wdt_lo], axis=0)

    proj, dt_raw = _in_proj(x2d, mix_norm_g[None, :], w_main, wdt3)

    pad_h = lambda v: jnp.pad(v, (0, LANES - HEADS))[None, :]
    ltri = (jnp.arange(CHUNK)[:, None] >= jnp.arange(CHUNK)[None, :]).astype(BF16)
    ltri3 = jnp.concatenate([ltri, ltri, ltri], axis=1)
    e1 = (jnp.arange(LANES)[:, None] == (jnp.arange(D_INNER) // HEAD_DIM)[None, :]).astype(BF16)
    e2 = jnp.concatenate([e1, e1], axis=0)
    yssd, ypool = _mixers(
        proj, dt_raw, bsz, seq, conv_w, conv_b[None, :], pad_h(dt_bias), pad_h(a_log),
        jnp.repeat(d_skip, HEAD_DIM)[None, :], ssd_norm_g[None, :], ltri3, e2,
        pool_w.astype(BF16), pool_scale[None, :])

    wr = jnp.pad(w_router, ((0, 0), (0, LANES - N_EXPERTS)))
    wr_hi, wr_lo = _split2(wr)
    wr3 = jnp.concatenate([wr_hi, wr_hi, wr_lo], axis=0)
    br = jnp.pad(b_router, (0, LANES - N_EXPERTS))[None, :]
    x1, h2, topw, topi = _mix_route(x2d, yssd, ypool, proj, w_ssd_out.astype(BF16),
                                    w_mix_out.astype(BF16), ffn_norm_g[None, :], wr3, br)

    block_expert, nvalid, nreal, row_token3, row_dest3, n_slot_rows = _routing_tables(topi[:, :TOP_K], n_tok)
    slots = _moe(block_expert, nvalid, nreal, row_token3, row_dest3, h2, w_gate_up, b_gate_up[:, None, :],
                 w_down, b_down[:, None, :], n_slot_rows)

    return _combine(x1, slots, topw, p2d, ple_norm_g[None, :], w_ple_gate.astype(BF16),
                    w_ple_proj.astype(BF16), out_g[None, :])


def kernel(x, p, mix_norm_g, w_in, conv_w, conv_b, dt_bias, a_log, d_skip, ssd_norm_g, w_ssd_out, pool_w,
           pool_scale, w_mix_out, ffn_norm_g, w_router, b_router, w_gate_up, b_gate_up, w_down, b_down,
           ple_norm_g, w_ple_gate, w_ple_proj, final_norm_g):
    bsz, seq, d = x.shape
    depth = p.shape[0]
    assert depth == 1 and d == D_MODEL and seq % CHUNK == 0
    x2d = x.reshape(bsz * seq, d)
    out = _layer(x2d, p[0].reshape(bsz * seq, D_PLE), bsz, seq, mix_norm_g[0], w_in[0], conv_w[0], conv_b[0],
                 dt_bias[0], a_log[0], d_skip[0], ssd_norm_g[0], w_ssd_out[0], pool_w[0], pool_scale[0],
                 w_mix_out[0], ffn_norm_g[0], w_router[0], b_router[0], w_gate_up[0], b_gate_up[0],
                 w_down[0], b_down[0], ple_norm_g[0], w_ple_gate[0], w_ple_proj[0], final_norm_g)
    return out.reshape(bsz, seq, d)
```

```python
import functools

import jax
import jax.numpy as jnp
from jax import lax
from jax.experimental import pallas as pl
from jax.experimental.pallas import tpu as pltpu

F32 = jnp.float32
BF16 = jnp.bfloat16

D_MODEL = 1024
D_INNER = 2048
HEAD_DIM = 64
HEADS = 32
GROUPS = 4
HEADS_PER_GROUP = HEADS // GROUPS
GROUP_DIM = D_INNER // GROUPS
D_STATE = 128
CONV_WIDTH = 4
CHUNK = 128
D_BC = 2 * GROUPS * D_STATE
D_CONV = D_INNER + D_BC
POOL_WIDTH = D_MODEL
POOL_WINDOWS = (2, 4, 8, 16)
POOL_GROUP_DIM = POOL_WIDTH // len(POOL_WINDOWS)
N_EXPERTS = 32
TOP_K = 4
D_EXPERT = D_MODEL
SWIGLU_LIMIT = 7.0
SWIGLU_ALPHA = 1.702
D_PLE = 256
EPS = 1e-6

LANES = 128
TILE_ROWS = D_MODEL // LANES
ROUTE_SUB = 256
MIX_CHUNKS = 2
HALO = 16
D_PROJ = D_INNER + D_CONV + POOL_WIDTH + 2 * D_MODEL
MOE_ROWS = 256
ROW_UNROLL = 8
VMEM_LIMIT = 56 * 1024 * 1024


def _split2(v):
    hi = v.astype(BF16)
    lo = (v - hi.astype(F32)).astype(BF16)
    return hi, lo


def _split3(v):
    hi = v.astype(BF16)
    r = v - hi.astype(F32)
    mid = r.astype(BF16)
    lo = (r - mid.astype(F32)).astype(BF16)
    return hi, mid, lo


def _rms(x, g):
    return x * lax.rsqrt(jnp.mean(x * x, axis=-1, keepdims=True) + EPS) * g


def _store_token_tiles(ref2d, val, row0=0):
    rows = val.shape[0]
    for j in range(TILE_ROWS):
        ref2d[pl.ds(row0 * TILE_ROWS + j, rows, stride=TILE_ROWS), :] = val[:, j * LANES:(j + 1) * LANES]


def _load_token_tiles(ref2d, rows):
    return jnp.concatenate([ref2d[pl.ds(j, rows, stride=TILE_ROWS), :] for j in range(TILE_ROWS)], axis=1)


def _in_proj_kernel(x_ref, g_ref, w_ref, wdt_ref, proj_ref, dt_ref, h_sc):
    @pl.when(pl.program_id(1) == 0)
    def _():
        h = _rms(x_ref[...], g_ref[...])
        hi, lo = _split2(h)
        h_sc[...] = hi
        lhs = jnp.concatenate([hi, lo, hi], axis=1)
        dt_ref[...] = jnp.dot(lhs, wdt_ref[...], preferred_element_type=F32)

    proj_ref[...] = jnp.dot(h_sc[...], w_ref[...], preferred_element_type=F32).astype(BF16)


def _in_proj(x2d, g, w_main, wdt3, tm=1024, tn=2048):
    t = x2d.shape[0]
    return pl.pallas_call(
        _in_proj_kernel,
        grid=(t // tm, D_PROJ // tn),
        in_specs=[
            pl.BlockSpec((tm, D_MODEL), lambda i, j: (i, 0)),
            pl.BlockSpec((1, D_MODEL), lambda i, j: (0, 0)),
            pl.BlockSpec((D_MODEL, tn), lambda i, j: (0, j)),
            pl.BlockSpec((3 * D_MODEL, LANES), lambda i, j: (0, 0)),
        ],
        out_specs=[
            pl.BlockSpec((tm, tn), lambda i, j: (i, j)),
            pl.BlockSpec((tm, LANES), lambda i, j: (i, 0)),
        ],
        out_shape=[
            jax.ShapeDtypeStruct((t, D_PROJ), BF16),
            jax.ShapeDtypeStruct((t, LANES), F32),
        ],
        scratch_shapes=[pltpu.VMEM((tm, D_MODEL), BF16)],
        compiler_params=pltpu.CompilerParams(
            dimension_semantics=("arbitrary", "arbitrary"), vmem_limit_bytes=VMEM_LIMIT),
        name="in_proj",
    )(x2d, g, w_main, wdt3)


def _mixers_kernel(z_ref, xs_ref, bc_ref, u_ref, dt_ref,
                   cw_ref, cb_ref, dtb_ref, alog_ref, dskip_ref, ng_ref, ltri_ref, e2_ref,
                   shift_ref, band_ref, pw_ref, ps_ref,
                   yssd_ref, ypool_ref,
                   ext_sc, extu_sc, state_sc):
    c = pl.program_id(1)
    rows = MIX_CHUNKS * CHUNK

    @pl.when(c == 0)
    def _():
        ext_sc[0:HALO, :] = jnp.zeros((HALO, D_CONV), BF16)
        extu_sc[0:HALO, :] = jnp.zeros((HALO, POOL_WIDTH), BF16)
        state_sc[...] = jnp.zeros_like(state_sc)

    @pl.when(c > 0)
    def _():
        ext_sc[0:HALO, :] = ext_sc[rows:rows + HALO, :]
        extu_sc[0:HALO, :] = extu_sc[rows:rows + HALO, :]

    ext_sc[HALO:HALO + rows, 0:D_INNER] = xs_ref[...]
    ext_sc[HALO:HALO + rows, D_INNER:D_CONV] = bc_ref[...]
    extu_sc[HALO:HALO + rows, :] = u_ref[...]
    for ci in range(MIX_CHUNKS):
        _mixers_chunk(ci, c * MIX_CHUNKS + ci, z_ref, u_ref, dt_ref, cw_ref, cb_ref, dtb_ref, alog_ref, dskip_ref,
                      ng_ref, ltri_ref, e2_ref, shift_ref, band_ref, pw_ref, ps_ref, yssd_ref, ypool_ref,
                      ext_sc, extu_sc, state_sc)


def _mixers_chunk(ci, chunk_index, z_ref, u_ref, dt_ref, cw_ref, cb_ref, dtb_ref, alog_ref, dskip_ref,
                  ng_ref, ltri_ref, e2_ref, shift_ref, band_ref, pw_ref, ps_ref, yssd_ref, ypool_ref,
                  ext_sc, extu_sc, state_sc):
    r0 = ci * CHUNK
    rs = slice(r0, r0 + CHUNK)
    ext = ext_sc[r0:r0 + HALO + CHUNK, :]
    conv = cb_ref[...] + cw_ref[CONV_WIDTH - 1:CONV_WIDTH, :] * ext[HALO:HALO + CHUNK, :].astype(F32)
    for k in range(CONV_WIDTH - 1):
        conv = conv + cw_ref[k:k + 1, :] * jnp.dot(shift_ref[k], ext, preferred_element_type=F32)
    xc = conv * jax.nn.sigmoid(conv)
    xs = xc[:, 0:D_INNER]
    xs_b = xs.astype(BF16)

    dtv = jax.nn.softplus(dt_ref[rs, :] + dtb_ref[...])
    da = dtv * (-jnp.exp(alog_ref[...]))
    a_cum = jnp.dot(ltri_ref[...], jnp.concatenate(_split3(da), axis=0),
                    preferred_element_type=F32)
    expa = jnp.exp(a_cum)
    a_last = a_cum[CHUNK - 1:CHUNK, :]
    wst = dtv * jnp.exp(a_last - a_cum)
    a_cum_t = a_cum.T
    dt_t = dtv.T

    both = jnp.concatenate([wst, expa], axis=0)
    hi, lo = _split2(both)
    expd = jnp.dot(jnp.concatenate([hi, lo], axis=1), e2_ref[...],
                   preferred_element_type=F32)
    wst_x = expd[0:CHUNK, :]
    expa_x = expd[CHUNK:2 * CHUNK, :]
    xw_b = (xs * wst_x).astype(BF16)

    row = lax.broadcasted_iota(jnp.int32, (CHUNK, CHUNK), 0)
    col = lax.broadcasted_iota(jnp.int32, (CHUNK, CHUNK), 1)
    causal_bias = jnp.where(row >= col, 0.0, -jnp.inf).astype(F32)
    lane = lax.broadcasted_iota(jnp.int32, (CHUNK, LANES), 1)
    low_half = lane < HEAD_DIM

    y_groups = []
    for g in range(GROUPS):
        bg = xc[:, D_INNER + g * D_STATE:D_INNER + (g + 1) * D_STATE]
        cg = xc[:, D_INNER + GROUPS * D_STATE + g * D_STATE:D_INNER + GROUPS * D_STATE + (g + 1) * D_STATE]
        bg_b = bg.astype(BF16)
        cg_b = cg.astype(BF16)
        cbm = lax.dot_general(cg_b, bg_b, (((1,), (1,)), ((), ())), preferred_element_type=F32)
        gsl = slice(g * GROUP_DIM, (g + 1) * GROUP_DIM)

        prev_t = state_sc[g]
        y_off = jnp.dot(cg_b, prev_t.astype(BF16), preferred_element_type=F32) * expa_x[:, gsl]
        st_t = jnp.dot(bg.T.astype(BF16), xw_b[:, gsl], preferred_element_type=F32)
        state_sc[g] = prev_t * expa_x[CHUNK - 1:CHUNK, gsl] + st_t

        pairs = []
        for jp in range(HEADS_PER_GROUP // 2):
            ms = []
            for hh in range(2):
                h = g * HEADS_PER_GROUP + jp * 2 + hh
                seg = a_cum[:, h:h + 1] - a_cum_t[h:h + 1, :]
                dec = jnp.exp(seg + causal_bias)
                ms.append((cbm * dec * dt_t[h:h + 1, :]).astype(BF16))
            lhs = jnp.concatenate(ms, axis=1)
            c0 = g * GROUP_DIM + jp * LANES
            xp = xs_b[:, c0:c0 + LANES]
            zero = jnp.zeros_like(xp)
            rhs = jnp.concatenate([jnp.where(low_half, xp, zero), jnp.where(low_half, zero, xp)], axis=0)
            pairs.append(jnp.dot(lhs, rhs, preferred_element_type=F32))
        y_diag = jnp.concatenate(pairs, axis=1)

        yg = y_diag + y_off + dskip_ref[:, gsl] * xs[:, gsl]
        zg = z_ref[rs, gsl].astype(F32)
        yg = yg * (zg * jax.nn.sigmoid(zg))
        yg = yg * lax.rsqrt(jnp.mean(yg * yg, axis=-1, keepdims=True) + EPS) * ng_ref[:, gsl]
        y_groups.append(yg.astype(BF16))
    yssd_ref[rs, :] = jnp.concatenate(y_groups, axis=1)

    pos = chunk_index * CHUNK + lax.broadcasted_iota(jnp.int32, (CHUNK, 1), 0)
    outs = []
    for gi, w in enumerate(POOL_WINDOWS):
        psl = slice(gi * POOL_GROUP_DIM, (gi + 1) * POOL_GROUP_DIM)
        s = jnp.dot(band_ref[gi], extu_sc[r0:r0 + HALO + CHUNK, psl], preferred_element_type=F32)
        cnt = jnp.minimum(pos + 1, w).astype(F32)
        pooled = s / cnt - u_ref[rs, psl].astype(F32)
        outs.append(jnp.dot(pooled.astype(BF16), pw_ref[gi], preferred_element_type=F32))
    ypool_ref[rs, :] = (jnp.concatenate(outs, axis=1) * ps_ref[...]).astype(BF16)


def _mixers(proj, dt_raw, bsz, seq, cw, cb, dtb, alog, dskip, ng, ltri3, e2, pw, ps):
    rows = MIX_CHUNKS * CHUNK
    nc = seq // rows
    t = bsz * seq
    rowmap = lambda b, c: b * nc + c
    const2 = lambda b, c: (0, 0)
    const3 = lambda b, c: (0, 0, 0)
    trow = jnp.arange(CHUNK)[:, None] + HALO
    jcol = jnp.arange(HALO + CHUNK)[None, :]
    shifts = jnp.stack([(jcol == trow - (CONV_WIDTH - 1) + k) for k in range(CONV_WIDTH - 1)]).astype(BF16)
    bands = jnp.stack([(jcol <= trow) & (jcol > trow - w) for w in POOL_WINDOWS]).astype(BF16)
    return pl.pallas_call(
        _mixers_kernel,
        grid=(bsz, nc),
        in_specs=[
            pl.BlockSpec((rows, D_INNER), lambda b, c: (rowmap(b, c), 0)),
            pl.BlockSpec((rows, D_INNER), lambda b, c: (rowmap(b, c), 1)),
            pl.BlockSpec((rows, D_BC), lambda b, c: (rowmap(b, c), 4)),
            pl.BlockSpec((rows, POOL_WIDTH), lambda b, c: (rowmap(b, c), 5)),
            pl.BlockSpec((rows, LANES), lambda b, c: (rowmap(b, c), 0)),
            pl.BlockSpec((CONV_WIDTH, D_CONV), const2),
            pl.BlockSpec((1, D_CONV), const2),
            pl.BlockSpec((1, LANES), const2),
            pl.BlockSpec((1, LANES), const2),
            pl.BlockSpec((1, D_INNER), const2),
            pl.BlockSpec((1, D_INNER), const2),
            pl.BlockSpec((CHUNK, 3 * CHUNK), const2),
            pl.BlockSpec((2 * LANES, D_INNER), const2),
            pl.BlockSpec((CONV_WIDTH - 1, CHUNK, HALO + CHUNK), const3),
            pl.BlockSpec((len(POOL_WINDOWS), CHUNK, HALO + CHUNK), const3),
            pl.BlockSpec((len(POOL_WINDOWS), POOL_GROUP_DIM, POOL_GROUP_DIM), const3),
            pl.BlockSpec((1, POOL_WIDTH), const2),
        ],
        out_specs=[
            pl.BlockSpec((rows, D_INNER), lambda b, c: (rowmap(b, c), 0)),
            pl.BlockSpec((rows, POOL_WIDTH), lambda b, c: (rowmap(b, c), 0)),
        ],
        out_shape=[
            jax.ShapeDtypeStruct((t, D_INNER), BF16),
            jax.ShapeDtypeStruct((t, POOL_WIDTH), BF16),
        ],
        scratch_shapes=[
            pltpu.VMEM((HALO + rows, D_CONV), BF16),
            pltpu.VMEM((HALO + rows, POOL_WIDTH), BF16),
            pltpu.VMEM((GROUPS, D_STATE, GROUP_DIM), F32),
        ],
        compiler_params=pltpu.CompilerParams(
            dimension_semantics=("arbitrary", "arbitrary"), vmem_limit_bytes=VMEM_LIMIT),
        name="mixers",
    )(proj, proj, proj, proj, dt_raw, cw, cb, dtb, alog, dskip, ng, ltri3, e2, shifts, bands, pw, ps)


def _mix_route_kernel(x_ref, yssd_ref, ypool_ref, gates_ref, wso_ref, wmo_ref, fg_ref, wr_ref, br_ref,
                      x1_ref, h2_ref, topw_ref, topi_ref):
    for r0 in range(0, x_ref.shape[0], ROUTE_SUB):
        _mix_route_rows(slice(r0, r0 + ROUTE_SUB), x_ref, yssd_ref, ypool_ref, gates_ref, wso_ref, wmo_ref, fg_ref,
                        wr_ref, br_ref, x1_ref, h2_ref, topw_ref, topi_ref)


def _mix_route_rows(rs, x_ref, yssd_ref, ypool_ref, gates_ref, wso_ref, wmo_ref, fg_ref, wr_ref, br_ref,
                    x1_ref, h2_ref, topw_ref, topi_ref):
    y_ssd = jnp.dot(yssd_ref[rs, :], wso_ref[...], preferred_element_type=F32)
    gates = jax.nn.sigmoid(gates_ref[rs, :].astype(F32))
    mixed = gates[:, 0:D_MODEL] * y_ssd + gates[:, D_MODEL:2 * D_MODEL] * ypool_ref[rs, :].astype(F32)
    x1 = x_ref[rs, :] + jnp.dot(mixed.astype(BF16), wmo_ref[...], preferred_element_type=F32)
    x1_ref[rs, :] = x1
    h2 = _rms(x1, fg_ref[...])
    _store_token_tiles(h2_ref, h2, rs.start)

    hi, lo = _split2(h2)
    logits = jnp.dot(jnp.concatenate([hi, lo, hi], axis=1), wr_ref[...],
                     preferred_element_type=F32) + br_ref[...]
    tm = logits.shape[0]
    lane = lax.broadcasted_iota(jnp.int32, (tm, LANES), 1)
    neg = jnp.float32(-jnp.inf)
    work = jnp.where(lane < N_EXPERTS, logits, neg)
    vals = []
    idxs = []
    for _ in range(TOP_K):
        m = jnp.max(work, axis=-1, keepdims=True)
        idx = jnp.min(jnp.where(work == m, lane, LANES), axis=-1, keepdims=True)
        vals.append(m)
        idxs.append(idx)
        work = jnp.where(lane == idx, neg, work)
    es = [jnp.exp(v - vals[0]) for v in vals]
    den = es[0] + es[1] + es[2] + es[3]
    topw = jnp.zeros((tm, LANES), F32)
    topi = jnp.zeros((tm, LANES), jnp.int32)
    for k in range(TOP_K):
        topw = jnp.where(lane == k, es[k] / den, topw)
        topi = jnp.where(lane == k, idxs[k], topi)
    topw_ref[rs, :] = topw
    topi_ref[rs, :] = topi


def _mix_route(x2d, yssd, ypool, proj, wso, wmo, fg, wr3, br, tm=512):
    t = x2d.shape[0]
    const = lambda i: (0, 0)
    return pl.pallas_call(
        _mix_route_kernel,
        grid=(t // tm,),
        in_specs=[
            pl.BlockSpec((tm, D_MODEL), lambda i: (i, 0)),
            pl.BlockSpec((tm, D_INNER), lambda i: (i, 0)),
            pl.BlockSpec((tm, POOL_WIDTH), lambda i: (i, 0)),
            pl.BlockSpec((tm, 2 * D_MODEL), lambda i: (i, 3)),
            pl.BlockSpec((D_INNER, D_MODEL), const),
            pl.BlockSpec((D_MODEL, D_MODEL), const),
            pl.BlockSpec((1, D_MODEL), const),
            pl.BlockSpec((3 * D_MODEL, LANES), const),
            pl.BlockSpec((1, LANES), const),
        ],
        out_specs=[
            pl.BlockSpec((tm, D_MODEL), lambda i: (i, 0)),
            pl.BlockSpec((tm * TILE_ROWS, LANES), lambda i: (i, 0)),
            pl.BlockSpec((tm, LANES), lambda i: (i, 0)),
            pl.BlockSpec((tm, LANES), lambda i: (i, 0)),
        ],
        out_shape=[
            jax.ShapeDtypeStruct((t, D_MODEL), F32),
            jax.ShapeDtypeStruct((t * TILE_ROWS, LANES), F32),
            jax.ShapeDtypeStruct((t, LANES), F32),
            jax.ShapeDtypeStruct((t, LANES), jnp.int32),
        ],
        compiler_params=pltpu.CompilerParams(
            dimension_semantics=("arbitrary",), vmem_limit_bytes=VMEM_LIMIT),
        name="mix_route",
    )(x2d, yssd, ypool, proj, wso, wmo, fg, wr3, br)


def _moe_kernel(be_ref, nvalid_ref, nreal_ref,
                tok0_ref, toknext_ref, dstprev_ref,
                h_hbm, wgu_ref, bgu_ref, wd_ref, bd_ref,
                slots_hbm,
                xbuf, ybuf, xb_sc, wgu_bf, wd_bf, gsem, ssem, *, nb):
    i = pl.program_id(0)
    nvalid = nvalid_ref[0]
    slot = lax.rem(i, 2)

    def tile(off, n=1):
        return pl.ds(pl.multiple_of(off, TILE_ROWS), n * TILE_ROWS)

    def gather_copy(tok, r, s):
        return pltpu.make_async_copy(h_hbm.at[tile(tok)], xbuf.at[s, tile(r * TILE_ROWS)], gsem.at[s])

    def scatter_copy(dst, r, s):
        return pltpu.make_async_copy(ybuf.at[s, tile(r * TILE_ROWS)], slots_hbm.at[tile(dst)], ssem.at[s])

    def for_rows(base, n, start_one):
        ngroups = lax.shift_right_logical(n, 3)

        def group(gi, carry):
            for u in range(ROW_UNROLL):
                start_one(base + gi * ROW_UNROLL + u, u % 2)
            return carry
        lax.fori_loop(0, ngroups, group, 0)

        def single(r, carry):
            start_one(r, 0)
            return carry
        lax.fori_loop(base + ngroups * ROW_UNROLL, base + n, single, 0)

    def start_gather(tok_ref, s, base, n):
        for_rows(base, n, lambda r, prio: gather_copy(tok_ref[0, 0, r], r, s).start())

    def start_scatter(dst_ref, s, base, n):
        for_rows(base, n, lambda r, prio: scatter_copy(dst_ref[0, 0, r], r, s).start(priority=prio))

    def start_gather_and_scatter(tok_ref, dst_ref, s, n_gather, n_scatter):
        ngroups = lax.shift_right_logical(jnp.minimum(n_gather, n_scatter), 3)

        def group(gi, carry):
            for u in range(ROW_UNROLL):
                r = gi * ROW_UNROLL + u
                gather_copy(tok_ref[0, 0, r], r, s).start()
                scatter_copy(dst_ref[0, 0, r], r, s).start()
            return carry
        lax.fori_loop(0, ngroups, group, 0)
        done = ngroups * ROW_UNROLL
        start_gather(tok_ref, s, done, n_gather - done)
        start_scatter(dst_ref, s, done, n_scatter - done)

    def wait_rows(n, wait_chunk):
        c = MOE_ROWS
        while c >= 1:
            pl.when(lax.bitwise_and(n, c) != 0)(functools.partial(wait_chunk, c))
            c //= 2

    def wait_gather(s, n):
        wait_rows(n, lambda c: pltpu.make_async_copy(
            h_hbm.at[tile(0, c)], xbuf.at[s, tile(0, c)], gsem.at[s]).wait())

    def wait_scatter(s, n):
        wait_rows(n, lambda c: pltpu.make_async_copy(
            ybuf.at[s, tile(0, c)], slots_hbm.at[tile(0, c)], ssem.at[s]).wait())

    prev_block = jnp.maximum(i - 1, 0)
    next_block = jnp.minimum(i + 1, nb - 1)
    n_prev = jnp.where(i >= 1, nreal_ref[prev_block], 0)

    @pl.when(i == 0)
    def _():
        xbuf[...] = jnp.zeros_like(xbuf)
        start_gather(tok0_ref, 0, 0, nreal_ref[0])

    @pl.when(jnp.logical_and(i >= 2, i <= nvalid))
    def _():
        wait_scatter(slot, nreal_ref[jnp.maximum(i - 2, 0)])

    @pl.when(i < nvalid)
    def _():
        nreal = nreal_ref[i]
        n_next = jnp.where(i + 1 < nvalid, nreal_ref[next_block], 0)

        @pl.when(jnp.logical_or(i == 0, be_ref[i] != be_ref[prev_block]))
        def _():
            wgu_bf[...] = wgu_ref[0].astype(BF16)
            wd_bf[...] = wd_ref[0].astype(BF16)

        wait_gather(slot, nreal)
        xb_sc[...] = _load_token_tiles(xbuf.at[slot], MOE_ROWS).astype(BF16)

        start_gather_and_scatter(toknext_ref, dstprev_ref, 1 - slot, n_next, n_prev)

        gu = jnp.dot(xb_sc[...], wgu_bf[...], preferred_element_type=F32) + bgu_ref[0]
        gate = jnp.minimum(gu[:, 0:D_EXPERT], SWIGLU_LIMIT)
        up = jnp.clip(gu[:, D_EXPERT:2 * D_EXPERT], -SWIGLU_LIMIT, SWIGLU_LIMIT)
        act = (up + 1.0) * gate * jax.nn.sigmoid(SWIGLU_ALPHA * gate)
        y = jnp.dot(act.astype(BF16), wd_bf[...], preferred_element_type=F32) + bd_ref[0]
        _store_token_tiles(ybuf.at[slot], y)

    @pl.when(i == nvalid)
    def _():
        start_scatter(dstprev_ref, 1 - slot, 0, n_prev)
        wait_scatter(1 - slot, n_prev)


def _moe(block_expert, nvalid, nreal, row_token3, row_dest3, h2, wgu, bgu3, wd, bd3, n_slot_rows):
    nb = block_expert.shape[0]
    expert_of = lambda i, be: be[jnp.minimum(i, nb - 1)]
    grid_spec = pltpu.PrefetchScalarGridSpec(
        num_scalar_prefetch=3,
        grid=(nb + 1,),
        in_specs=[
            pl.BlockSpec((1, 1, MOE_ROWS), lambda i, be, nv, nr: (0, 0, 0), memory_space=pltpu.SMEM),
            pl.BlockSpec((1, 1, MOE_ROWS), lambda i, be, nv, nr: (jnp.minimum(i + 1, nb - 1), 0, 0),
                         memory_space=pltpu.SMEM),
            pl.BlockSpec((1, 1, MOE_ROWS), lambda i, be, nv, nr: (jnp.clip(i - 1, 0, nb - 1), 0, 0),
                         memory_space=pltpu.SMEM),
            pl.BlockSpec(memory_space=pl.ANY),
            pl.BlockSpec((1, D_MODEL, 2 * D_EXPERT), lambda i, be, nv, nr: (expert_of(i, be), 0, 0)),
            pl.BlockSpec((1, 1, 2 * D_EXPERT), lambda i, be, nv, nr: (expert_of(i, be), 0, 0)),
            pl.BlockSpec((1, D_EXPERT, D_MODEL), lambda i, be, nv, nr: (expert_of(i, be), 0, 0)),
            pl.BlockSpec((1, 1, D_MODEL), lambda i, be, nv, nr: (expert_of(i, be), 0, 0)),
        ],
        out_specs=pl.BlockSpec(memory_space=pl.ANY),
        scratch_shapes=[
            pltpu.VMEM((2, MOE_ROWS * TILE_ROWS, LANES), F32),
            pltpu.VMEM((2, MOE_ROWS * TILE_ROWS, LANES), F32),
            pltpu.VMEM((MOE_ROWS, D_MODEL), BF16),
            pltpu.VMEM((D_MODEL, 2 * D_EXPERT), BF16),
            pltpu.VMEM((D_EXPERT, D_MODEL), BF16),
            pltpu.SemaphoreType.DMA((2,)),
            pltpu.SemaphoreType.DMA((2,)),
        ],
    )
    return pl.pallas_call(
        functools.partial(_moe_kernel, nb=nb),
        grid_spec=grid_spec,
        out_shape=jax.ShapeDtypeStruct((n_slot_rows * TILE_ROWS, LANES), F32),
        compiler_params=pltpu.CompilerParams(
            dimension_semantics=("arbitrary",), vmem_limit_bytes=VMEM_LIMIT),
        name="moe_experts",
    )(block_expert, nvalid, nreal, row_token3, row_token3, row_dest3, h2, wgu, bgu3, wd, bd3)


def _combine_kernel(x1_ref, s0_ref, s1_ref, s2_ref, s3_ref, topw_ref, p_ref, pg_ref, wpg_ref, wpp_ref, fg_ref,
                    out_ref):
    x2 = x1_ref[...]
    topw = topw_ref[...]
    for k, s_ref in enumerate((s0_ref, s1_ref, s2_ref, s3_ref)):
        x2 = x2 + _load_token_tiles(s_ref, x2.shape[0]) * topw[:, k:k + 1]
    n = _rms(x2, pg_ref[...])
    gate = jax.nn.sigmoid(jnp.dot(n.astype(BF16), wpg_ref[...], preferred_element_type=F32))
    pp = jnp.dot(p_ref[...].astype(BF16), wpp_ref[...], preferred_element_type=F32)
    x3 = x2 + gate * pp
    out_ref[...] = _rms(x3, fg_ref[...])


def _combine(x1, slots, topw, p2d, pg, wpg, wpp, fg, tm=512):
    t = x1.shape[0]
    nt = t // tm
    const = lambda i: (0, 0)
    slot_specs = [pl.BlockSpec((tm * TILE_ROWS, LANES), functools.partial(lambda k, i: (k * nt + i, 0), k))
                  for k in range(TOP_K)]
    return pl.pallas_call(
        _combine_kernel,
        grid=(nt,),
        in_specs=[
            pl.BlockSpec((tm, D_MODEL), lambda i: (i, 0)),
            *slot_specs,
            pl.BlockSpec((tm, LANES), lambda i: (i, 0)),
            pl.BlockSpec((tm, D_PLE), lambda i: (i, 0)),
            pl.BlockSpec((1, D_MODEL), const),
            pl.BlockSpec((D_MODEL, D_MODEL), const),
            pl.BlockSpec((D_PLE, D_MODEL), const),
            pl.BlockSpec((1, D_MODEL), const),
        ],
        out_specs=pl.BlockSpec((tm, D_MODEL), lambda i: (i, 0)),
        out_shape=jax.ShapeDtypeStruct((t, D_MODEL), F32),
        compiler_params=pltpu.CompilerParams(
            dimension_semantics=("arbitrary",), vmem_limit_bytes=VMEM_LIMIT),
        name="combine_ple",
    )(x1, slots, slots, slots, slots, topw, p2d, pg, wpg, wpp, fg)


def _routing_tables(top_idx, n_tok):
    n_assign = n_tok * TOP_K
    expert_flat = top_idx.reshape(-1)
    order = jnp.argsort(expert_flat, stable=True).astype(jnp.int32)
    counts = jnp.bincount(expert_flat, length=N_EXPERTS).astype(jnp.int32)
    start = jnp.cumsum(counts) - counts
    padded = (counts + MOE_ROWS - 1) // MOE_ROWS * MOE_ROWS
    pend = jnp.cumsum(padded)
    pstart = pend - padded
    n_rows = n_assign + N_EXPERTS * MOE_ROWS
    n_blocks = n_rows // MOE_ROWS
    block_start = jnp.arange(n_blocks, dtype=jnp.int32) * MOE_ROWS
    block_expert = jnp.minimum(jnp.sum(block_start[:, None] >= pend[None, :], axis=1),
                               N_EXPERTS - 1).astype(jnp.int32)
    nvalid = (pend[-1] // MOE_ROWS).astype(jnp.int32).reshape(1)
    is_e = block_expert[:, None] == jnp.arange(N_EXPERTS, dtype=jnp.int32)[None, :]
    per_block = lambda v: jnp.sum(jnp.where(is_e, v[None, :], 0), axis=1)
    nreal = jnp.clip(per_block(pstart + counts) - block_start, 0, MOE_ROWS).astype(jnp.int32)
    sorted_pos = (block_start + per_block(start - pstart))[:, None] + jnp.arange(MOE_ROWS, dtype=jnp.int32)[None, :]
    assign = order[jnp.clip(sorted_pos, 0, n_assign - 1)]
    row_token = assign // TOP_K
    row_dest = (assign % TOP_K) * n_tok + row_token
    return (block_expert, nvalid, nreal, (row_token * TILE_ROWS).reshape(n_blocks, 1, MOE_ROWS),
            (row_dest * TILE_ROWS).reshape(n_blocks, 1, MOE_ROWS), n_assign)


def _layer(x2d, p2d, bsz, seq, mix_norm_g, w_in, conv_w, conv_b, dt_bias, a_log, d_skip, ssd_norm_g,
           w_ssd_out, pool_w, pool_scale, w_mix_out, ffn_norm_g, w_router, b_router,
           w_gate_up, b_gate_up, w_down, b_down, ple_norm_g, w_ple_gate, w_ple_proj, out_g):
    n_tok = x2d.shape[0]
    dt0 = D_INNER + D_CONV
    w_main = jnp.concatenate([w_in[:, :dt0], w_in[:, dt0 + HEADS:]], axis=1).astype(BF16)
    w_dt = jnp.pad(w_in[:, dt0:dt0 + HEADS], ((0, 0), (0, LANES - HEADS)))
    wdt_hi, wdt_lo = _split2(w_dt)
    wdt3 = jnp.concatenate([wdt_hi, wdt_hi, wdt_lo], axis=0)

    proj, dt_raw = _in_proj(x2d, mix_norm_g[None, :], w_main, wdt3)

    pad_h = lambda v: jnp.pad(v, (0, LANES - HEADS))[None, :]
    ltri = (jnp.arange(CHUNK)[:, None] >= jnp.arange(CHUNK)[None, :]).astype(BF16)
    ltri3 = jnp.concatenate([ltri, ltri, ltri], axis=1)
    e1 = (jnp.arange(LANES)[:, None] == (jnp.arange(D_INNER) // HEAD_DIM)[None, :]).astype(BF16)
    e2 = jnp.concatenate([e1, e1], axis=0)
    yssd, ypool = _mixers(
        proj, dt_raw, bsz, seq, conv_w, conv_b[None, :], pad_h(dt_bias), pad_h(a_log),
        jnp.repeat(d_skip, HEAD_DIM)[None, :], ssd_norm_g[None, :], ltri3, e2,
        pool_w.astype(BF16), pool_scale[None, :])

    wr = jnp.pad(w_router, ((0, 0), (0, LANES - N_EXPERTS)))
    wr_hi, wr_lo = _split2(wr)
    wr3 = jnp.concatenate([wr_hi, wr_hi, wr_lo], axis=0)
    br = jnp.pad(b_router, (0, LANES - N_EXPERTS))[None, :]
    x1, h2, topw, topi = _mix_route(x2d, yssd, ypool, proj, w_ssd_out.astype(BF16),
                                    w_mix_out.astype(BF16), ffn_norm_g[None, :], wr3, br)

    block_expert, nvalid, nreal, row_token3, row_dest3, n_slot_rows = _routing_tables(topi[:, :TOP_K], n_tok)
    slots = _moe(block_expert, nvalid, nreal, row_token3, row_dest3, h2, w_gate_up, b_gate_up[:, None, :],
                 w_down, b_down[:, None, :], n_slot_rows)

    return _combine(x1, slots, topw, p2d, ple_norm_g[None, :], w_ple_gate.astype(BF16),
                    w_ple_proj.astype(BF16), out_g[None, :])


def kernel(x, p, mix_norm_g, w_in, conv_w, conv_b, dt_bias, a_log, d_skip, ssd_norm_g, w_ssd_out, pool_w,
           pool_scale, w_mix_out, ffn_norm_g, w_router, b_router, w_gate_up, b_gate_up, w_down, b_down,
           ple_norm_g, w_ple_gate, w_ple_proj, final_norm_g):
    bsz, seq, d = x.shape
    depth = p.shape[0]
    assert depth == 1 and d == D_MODEL and seq % CHUNK == 0
    x2d = x.reshape(bsz * seq, d)
    out = _layer(x2d, p[0].reshape(bsz * seq, D_PLE), bsz, seq, mix_norm_g[0], w_in[0], conv_w[0], conv_b[0],
                 dt_bias[0], a_log[0], d_skip[0], ssd_norm_g[0], w_ssd_out[0], pool_w[0], pool_scale[0],
                 w_mix_out[0], ffn_norm_g[0], w_router[0], b_router[0], w_gate_up[0], b_gate_up[0],
                 w_down[0], b_down[0], ple_norm_g[0], w_ple_gate[0], w_ple_proj[0], final_norm_g)
    return out.reshape(bsz, seq, d)
```

```python
import functools

import jax
import jax.numpy as jnp
from jax import lax
from jax.experimental import pallas as pl
from jax.experimental.pallas import tpu as pltpu

F32 = jnp.float32
BF16 = jnp.bfloat16

D_MODEL = 1024
D_INNER = 2048
HEAD_DIM = 64
HEADS = 32
GROUPS = 4
HEADS_PER_GROUP = HEADS // GROUPS
GROUP_DIM = D_INNER // GROUPS
D_STATE = 128
CONV_WIDTH = 4
CHUNK = 128
D_BC = 2 * GROUPS * D_STATE
D_CONV = D_INNER + D_BC
POOL_WIDTH = D_MODEL
POOL_WINDOWS = (2, 4, 8, 16)
POOL_GROUP_DIM = POOL_WIDTH // len(POOL_WINDOWS)
N_EXPERTS = 32
TOP_K = 4
D_EXPERT = D_MODEL
SWIGLU_LIMIT = 7.0
SWIGLU_ALPHA = 1.702
D_PLE = 256
EPS = 1e-6

LANES = 128
TILE_ROWS = D_MODEL // LANES
ROUTE_SUB = 256
MIX_CHUNKS = 2
HALO = 16
D_PROJ = D_INNER + D_CONV + POOL_WIDTH + 2 * D_MODEL
MOE_ROWS = 256
VMEM_LIMIT = 56 * 1024 * 1024


def _split2(v):
    hi = v.astype(BF16)
    lo = (v - hi.astype(F32)).astype(BF16)
    return hi, lo


def _split3(v):
    hi = v.astype(BF16)
    r = v - hi.astype(F32)
    mid = r.astype(BF16)
    lo = (r - mid.astype(F32)).astype(BF16)
    return hi, mid, lo


def _rms(x, g):
    return x * lax.rsqrt(jnp.mean(x * x, axis=-1, keepdims=True) + EPS) * g


def _store_token_tiles(ref2d, val, row0=0):
    rows = val.shape[0]
    for j in range(TILE_ROWS):
        ref2d[pl.ds(row0 * TILE_ROWS + j, rows, stride=TILE_ROWS), :] = val[:, j * LANES:(j + 1) * LANES]


def _load_token_tiles(ref2d, rows):
    return jnp.concatenate([ref2d[pl.ds(j, rows, stride=TILE_ROWS), :] for j in range(TILE_ROWS)], axis=1)


def _in_proj_kernel(x_ref, g_ref, w_ref, wdt_ref, proj_ref, dt_ref, h_sc):
    @pl.when(pl.program_id(1) == 0)
    def _():
        h = _rms(x_ref[...], g_ref[...])
        hi, lo = _split2(h)
        h_sc[...] = hi
        lhs = jnp.concatenate([hi, lo, hi], axis=1)
        dt_ref[...] = jnp.dot(lhs, wdt_ref[...], preferred_element_type=F32)

    proj_ref[...] = jnp.dot(h_sc[...], w_ref[...], preferred_element_type=F32).astype(BF16)


def _in_proj(x2d, g, w_main, wdt3, tm=1024, tn=2048):
    t = x2d.shape[0]
    return pl.pallas_call(
        _in_proj_kernel,
        grid=(t // tm, D_PROJ // tn),
        in_specs=[
            pl.BlockSpec((tm, D_MODEL), lambda i, j: (i, 0)),
            pl.BlockSpec((1, D_MODEL), lambda i, j: (0, 0)),
            pl.BlockSpec((D_MODEL, tn), lambda i, j: (0, j)),
            pl.BlockSpec((3 * D_MODEL, LANES), lambda i, j: (0, 0)),
        ],
        out_specs=[
            pl.BlockSpec((tm, tn), lambda i, j: (i, j)),
            pl.BlockSpec((tm, LANES), lambda i, j: (i, 0)),
        ],
        out_shape=[
            jax.ShapeDtypeStruct((t, D_PROJ), BF16),
            jax.ShapeDtypeStruct((t, LANES), F32),
        ],
        scratch_shapes=[pltpu.VMEM((tm, D_MODEL), BF16)],
        compiler_params=pltpu.CompilerParams(
            dimension_semantics=("arbitrary", "arbitrary"), vmem_limit_bytes=VMEM_LIMIT),
        name="in_proj",
    )(x2d, g, w_main, wdt3)


def _mixers_kernel(z_ref, xs_ref, bc_ref, u_ref, dt_ref,
                   cw_ref, cb_ref, dtb_ref, alog_ref, dskip_ref, ng_ref, ltri_ref, e2_ref,
                   shift_ref, band_ref, pw_ref, ps_ref,
                   yssd_ref, ypool_ref,
                   ext_sc, extu_sc, state_sc):
    c = pl.program_id(1)
    rows = MIX_CHUNKS * CHUNK

    @pl.when(c == 0)
    def _():
        ext_sc[0:HALO, :] = jnp.zeros((HALO, D_CONV), BF16)
        extu_sc[0:HALO, :] = jnp.zeros((HALO, POOL_WIDTH), BF16)
        state_sc[...] = jnp.zeros_like(state_sc)

    @pl.when(c > 0)
    def _():
        ext_sc[0:HALO, :] = ext_sc[rows:rows + HALO, :]
        extu_sc[0:HALO, :] = extu_sc[rows:rows + HALO, :]

    ext_sc[HALO:HALO + rows, 0:D_INNER] = xs_ref[...]
    ext_sc[HALO:HALO + rows, D_INNER:D_CONV] = bc_ref[...]
    extu_sc[HALO:HALO + rows, :] = u_ref[...]
    for ci in range(MIX_CHUNKS):
        _mixers_chunk(ci, c * MIX_CHUNKS + ci, z_ref, u_ref, dt_ref, cw_ref, cb_ref, dtb_ref, alog_ref, dskip_ref,
                      ng_ref, ltri_ref, e2_ref, shift_ref, band_ref, pw_ref, ps_ref, yssd_ref, ypool_ref,
                      ext_sc, extu_sc, state_sc)


def _mixers_chunk(ci, chunk_index, z_ref, u_ref, dt_ref, cw_ref, cb_ref, dtb_ref, alog_ref, dskip_ref,
                  ng_ref, ltri_ref, e2_ref, shift_ref, band_ref, pw_ref, ps_ref, yssd_ref, ypool_ref,
                  ext_sc, extu_sc, state_sc):
    r0 = ci * CHUNK
    rs = slice(r0, r0 + CHUNK)
    ext = ext_sc[r0:r0 + HALO + CHUNK, :]
    conv = cb_ref[...] + cw_ref[CONV_WIDTH - 1:CONV_WIDTH, :] * ext[HALO:HALO + CHUNK, :].astype(F32)
    for k in range(CONV_WIDTH - 1):
        conv = conv + cw_ref[k:k + 1, :] * jnp.dot(shift_ref[k], ext, preferred_element_type=F32)
    xc = conv * jax.nn.sigmoid(conv)
    xs = xc[:, 0:D_INNER]
    xs_b = xs.astype(BF16)

    dtv = jax.nn.softplus(dt_ref[rs, :] + dtb_ref[...])
    da = dtv * (-jnp.exp(alog_ref[...]))
    a_cum = jnp.dot(ltri_ref[...], jnp.concatenate(_split3(da), axis=0),
                    preferred_element_type=F32)
    expa = jnp.exp(a_cum)
    a_last = a_cum[CHUNK - 1:CHUNK, :]
    wst = dtv * jnp.exp(a_last - a_cum)
    a_cum_t = a_cum.T
    dt_t = dtv.T

    both = jnp.concatenate([wst, expa], axis=0)
    hi, lo = _split2(both)
    expd = jnp.dot(jnp.concatenate([hi, lo], axis=1), e2_ref[...],
                   preferred_element_type=F32)
    wst_x = expd[0:CHUNK, :]
    expa_x = expd[CHUNK:2 * CHUNK, :]
    xw_b = (xs * wst_x).astype(BF16)

    row = lax.broadcasted_iota(jnp.int32, (CHUNK, CHUNK), 0)
    col = lax.broadcasted_iota(jnp.int32, (CHUNK, CHUNK), 1)
    causal_bias = jnp.where(row >= col, 0.0, -jnp.inf).astype(F32)
    lane = lax.broadcasted_iota(jnp.int32, (CHUNK, LANES), 1)
    low_half = lane < HEAD_DIM

    y_groups = []
    for g in range(GROUPS):
        bg = xc[:, D_INNER + g * D_STATE:D_INNER + (g + 1) * D_STATE]
        cg = xc[:, D_INNER + GROUPS * D_STATE + g * D_STATE:D_INNER + GROUPS * D_STATE + (g + 1) * D_STATE]
        bg_b = bg.astype(BF16)
        cg_b = cg.astype(BF16)
        cbm = lax.dot_general(cg_b, bg_b, (((1,), (1,)), ((), ())), preferred_element_type=F32)
        gsl = slice(g * GROUP_DIM, (g + 1) * GROUP_DIM)

        prev_t = state_sc[g]
        y_off = jnp.dot(cg_b, prev_t.astype(BF16), preferred_element_type=F32) * expa_x[:, gsl]
        st_t = jnp.dot(bg.T.astype(BF16), xw_b[:, gsl], preferred_element_type=F32)
        state_sc[g] = prev_t * expa_x[CHUNK - 1:CHUNK, gsl] + st_t

        pairs = []
        for jp in range(HEADS_PER_GROUP // 2):
            ms = []
            for hh in range(2):
                h = g * HEADS_PER_GROUP + jp * 2 + hh
                seg = a_cum[:, h:h + 1] - a_cum_t[h:h + 1, :]
                dec = jnp.exp(seg + causal_bias)
                ms.append((cbm * dec * dt_t[h:h + 1, :]).astype(BF16))
            lhs = jnp.concatenate(ms, axis=1)
            c0 = g * GROUP_DIM + jp * LANES
            xp = xs_b[:, c0:c0 + LANES]
            zero = jnp.zeros_like(xp)
            rhs = jnp.concatenate([jnp.where(low_half, xp, zero), jnp.where(low_half, zero, xp)], axis=0)
            pairs.append(jnp.dot(lhs, rhs, preferred_element_type=F32))
        y_diag = jnp.concatenate(pairs, axis=1)

        yg = y_diag + y_off + dskip_ref[:, gsl] * xs[:, gsl]
        zg = z_ref[rs, gsl].astype(F32)
        yg = yg * (zg * jax.nn.sigmoid(zg))
        yg = yg * lax.rsqrt(jnp.mean(yg * yg, axis=-1, keepdims=True) + EPS) * ng_ref[:, gsl]
        y_groups.append(yg.astype(BF16))
    yssd_ref[rs, :] = jnp.concatenate(y_groups, axis=1)

    pos = chunk_index * CHUNK + lax.broadcasted_iota(jnp.int32, (CHUNK, 1), 0)
    outs = []
    for gi, w in enumerate(POOL_WINDOWS):
        psl = slice(gi * POOL_GROUP_DIM, (gi + 1) * POOL_GROUP_DIM)
        s = jnp.dot(band_ref[gi], extu_sc[r0:r0 + HALO + CHUNK, psl], preferred_element_type=F32)
        cnt = jnp.minimum(pos + 1, w).astype(F32)
        pooled = s / cnt - u_ref[rs, psl].astype(F32)
        outs.append(jnp.dot(pooled.astype(BF16), pw_ref[gi], preferred_element_type=F32))
    ypool_ref[rs, :] = (jnp.concatenate(outs, axis=1) * ps_ref[...]).astype(BF16)


def _mixers(proj, dt_raw, bsz, seq, cw, cb, dtb, alog, dskip, ng, ltri3, e2, pw, ps):
    rows = MIX_CHUNKS * CHUNK
    nc = seq // rows
    t = bsz * seq
    rowmap = lambda b, c: b * nc + c
    const2 = lambda b, c: (0, 0)
    const3 = lambda b, c: (0, 0, 0)
    trow = jnp.arange(CHUNK)[:, None] + HALO
    jcol = jnp.arange(HALO + CHUNK)[None, :]
    shifts = jnp.stack([(jcol == trow - (CONV_WIDTH - 1) + k) for k in range(CONV_WIDTH - 1)]).astype(BF16)
    bands = jnp.stack([(jcol <= trow) & (jcol > trow - w) for w in POOL_WINDOWS]).astype(BF16)
    return pl.pallas_call(
        _mixers_kernel,
        grid=(bsz, nc),
        in_specs=[
            pl.BlockSpec((rows, D_INNER), lambda b, c: (rowmap(b, c), 0)),
            pl.BlockSpec((rows, D_INNER), lambda b, c: (rowmap(b, c), 1)),
            pl.BlockSpec((rows, D_BC), lambda b, c: (rowmap(b, c), 4)),
            pl.BlockSpec((rows, POOL_WIDTH), lambda b, c: (rowmap(b, c), 5)),
            pl.BlockSpec((rows, LANES), lambda b, c: (rowmap(b, c), 0)),
            pl.BlockSpec((CONV_WIDTH, D_CONV), const2),
            pl.BlockSpec((1, D_CONV), const2),
            pl.BlockSpec((1, LANES), const2),
            pl.BlockSpec((1, LANES), const2),
            pl.BlockSpec((1, D_INNER), const2),
            pl.BlockSpec((1, D_INNER), const2),
            pl.BlockSpec((CHUNK, 3 * CHUNK), const2),
            pl.BlockSpec((2 * LANES, D_INNER), const2),
            pl.BlockSpec((CONV_WIDTH - 1, CHUNK, HALO + CHUNK), const3),
            pl.BlockSpec((len(POOL_WINDOWS), CHUNK, HALO + CHUNK), const3),
            pl.BlockSpec((len(POOL_WINDOWS), POOL_GROUP_DIM, POOL_GROUP_DIM), const3),
            pl.BlockSpec((1, POOL_WIDTH), const2),
        ],
        out_specs=[
            pl.BlockSpec((rows, D_INNER), lambda b, c: (rowmap(b, c), 0)),
            pl.BlockSpec((rows, POOL_WIDTH), lambda b, c: (rowmap(b, c), 0)),
        ],
        out_shape=[
            jax.ShapeDtypeStruct((t, D_INNER), BF16),
            jax.ShapeDtypeStruct((t, POOL_WIDTH), BF16),
        ],
        scratch_shapes=[
            pltpu.VMEM((HALO + rows, D_CONV), BF16),
            pltpu.VMEM((HALO + rows, POOL_WIDTH), BF16),
            pltpu.VMEM((GROUPS, D_STATE, GROUP_DIM), F32),
        ],
        compiler_params=pltpu.CompilerParams(
            dimension_semantics=("arbitrary", "arbitrary"), vmem_limit_bytes=VMEM_LIMIT),
        name="mixers",
    )(proj, proj, proj, proj, dt_raw, cw, cb, dtb, alog, dskip, ng, ltri3, e2, shifts, bands, pw, ps)


def _mix_route_kernel(x_ref, yssd_ref, ypool_ref, gates_ref, wso_ref, wmo_ref, fg_ref, wr_ref, br_ref,
                      x1_ref, h2_ref, topw_ref, topi_ref):
    for r0 in range(0, x_ref.shape[0], ROUTE_SUB):
        _mix_route_rows(slice(r0, r0 + ROUTE_SUB), x_ref, yssd_ref, ypool_ref, gates_ref, wso_ref, wmo_ref, fg_ref,
                        wr_ref, br_ref, x1_ref, h2_ref, topw_ref, topi_ref)


def _mix_route_rows(rs, x_ref, yssd_ref, ypool_ref, gates_ref, wso_ref, wmo_ref, fg_ref, wr_ref, br_ref,
                    x1_ref, h2_ref, topw_ref, topi_ref):
    y_ssd = jnp.dot(yssd_ref[rs, :], wso_ref[...], preferred_element_type=F32)
    gates = jax.nn.sigmoid(gates_ref[rs, :].astype(F32))
    mixed = gates[:, 0:D_MODEL] * y_ssd + gates[:, D_MODEL:2 * D_MODEL] * ypool_ref[rs, :].astype(F32)
    x1 = x_ref[rs, :] + jnp.dot(mixed.astype(BF16), wmo_ref[...], preferred_element_type=F32)
    x1_ref[rs, :] = x1
    h2 = _rms(x1, fg_ref[...])
    _store_token_tiles(h2_ref, h2, rs.start)

    hi, lo = _split2(h2)
    logits = jnp.dot(jnp.concatenate([hi, lo, hi], axis=1), wr_ref[...],
                     preferred_element_type=F32) + br_ref[...]
    tm = logits.shape[0]
    lane = lax.broadcasted_iota(jnp.int32, (tm, LANES), 1)
    neg = jnp.float32(-jnp.inf)
    work = jnp.where(lane < N_EXPERTS, logits, neg)
    vals = []
    idxs = []
    for _ in range(TOP_K):
        m = jnp.max(work, axis=-1, keepdims=True)
        idx = jnp.min(jnp.where(work == m, lane, LANES), axis=-1, keepdims=True)
        vals.append(m)
        idxs.append(idx)
        work = jnp.where(lane == idx, neg, work)
    es = [jnp.exp(v - vals[0]) for v in vals]
    den = es[0] + es[1] + es[2] + es[3]
    topw = jnp.zeros((tm, LANES), F32)
    topi = jnp.zeros((tm, LANES), jnp.int32)
    for k in range(TOP_K):
        topw = jnp.where(lane == k, es[k] / den, topw)
        topi = jnp.where(lane == k, idxs[k], topi)
    topw_ref[rs, :] = topw
    topi_ref[rs, :] = topi


def _mix_route(x2d, yssd, ypool, proj, wso, wmo, fg, wr3, br, tm=512):
    t = x2d.shape[0]
    const = lambda i: (0, 0)
    return pl.pallas_call(
        _mix_route_kernel,
        grid=(t // tm,),
        in_specs=[
            pl.BlockSpec((tm, D_MODEL), lambda i: (i, 0)),
            pl.BlockSpec((tm, D_INNER), lambda i: (i, 0)),
            pl.BlockSpec((tm, POOL_WIDTH), lambda i: (i, 0)),
            pl.BlockSpec((tm, 2 * D_MODEL), lambda i: (i, 3)),
            pl.BlockSpec((D_INNER, D_MODEL), const),
            pl.BlockSpec((D_MODEL, D_MODEL), const),
            pl.BlockSpec((1, D_MODEL), const),
            pl.BlockSpec((3 * D_MODEL, LANES), const),
            pl.BlockSpec((1, LANES), const),
        ],
        out_specs=[
            pl.BlockSpec((tm, D_MODEL), lambda i: (i, 0)),
            pl.BlockSpec((tm * TILE_ROWS, LANES), lambda i: (i, 0)),
            pl.BlockSpec((tm, LANES), lambda i: (i, 0)),
            pl.BlockSpec((tm, LANES), lambda i: (i, 0)),
        ],
        out_shape=[
            jax.ShapeDtypeStruct((t, D_MODEL), F32),
            jax.ShapeDtypeStruct((t * TILE_ROWS, LANES), F32),
            jax.ShapeDtypeStruct((t, LANES), F32),
            jax.ShapeDtypeStruct((t, LANES), jnp.int32),
        ],
        compiler_params=pltpu.CompilerParams(
            dimension_semantics=("arbitrary",), vmem_limit_bytes=VMEM_LIMIT),
        name="mix_route",
    )(x2d, yssd, ypool, proj, wso, wmo, fg, wr3, br)


def _moe_kernel(be_ref, nvalid_ref,
                tok0_ref, toknext_ref, dstprev_ref,
                h_hbm, wgu_ref, bgu_ref, wd_ref, bd_ref,
                slots_hbm,
                xbuf, ybuf, xb_sc, wgu_bf, wd_bf, gsem, ssem, *, nb):
    i = pl.program_id(0)
    nvalid = nvalid_ref[0]
    slot = lax.rem(i, 2)
    has_next = i + 1 < nvalid

    def tile(off, n=1):
        return pl.ds(pl.multiple_of(off, TILE_ROWS), n * TILE_ROWS)

    def gather_copy(tok, r, s):
        return pltpu.make_async_copy(h_hbm.at[tile(tok)], xbuf.at[s, tile(r * TILE_ROWS)], gsem.at[s])

    def scatter_copy(dst, r, s):
        return pltpu.make_async_copy(ybuf.at[s, tile(r * TILE_ROWS)], slots_hbm.at[tile(dst)], ssem.at[s])

    def wait_gather(s):
        pltpu.make_async_copy(h_hbm.at[tile(0, MOE_ROWS)], xbuf.at[s, tile(0, MOE_ROWS)], gsem.at[s]).wait()

    def wait_scatter(s):
        pltpu.make_async_copy(ybuf.at[s, tile(0, MOE_ROWS)], slots_hbm.at[tile(0, MOE_ROWS)], ssem.at[s]).wait()

    def issue(s, gather_ref, scatter_ref):
        for r in range(MOE_ROWS):
            if gather_ref is not None:
                gather_copy(gather_ref[0, 0, r], r, s).start()
            if scatter_ref is not None:
                scatter_copy(scatter_ref[0, 0, r], r, s).start()

    @pl.when(i == 0)
    def _():
        ybuf[...] = jnp.zeros_like(ybuf)
        issue(0, tok0_ref, None)

    for par in (0, 1):
        other = 1 - par
        mine = slot == par

        pl.when(jnp.logical_and(mine, i >= 2))(functools.partial(wait_scatter, par))
        pl.when(jnp.logical_and(mine, i < nvalid))(functools.partial(wait_gather, par))
        pl.when(jnp.logical_and(mine, jnp.logical_and(has_next, i == 0)))(
            functools.partial(issue, other, toknext_ref, None))
        pl.when(jnp.logical_and(mine, jnp.logical_and(has_next, i >= 1)))(
            functools.partial(issue, other, toknext_ref, dstprev_ref))
        pl.when(jnp.logical_and(mine, jnp.logical_and(jnp.logical_not(has_next), i >= 1)))(
            functools.partial(issue, other, None, dstprev_ref))
        pl.when(jnp.logical_and(mine, i == nb))(functools.partial(wait_scatter, other))

    @pl.when(i < nvalid)
    def _():
        @pl.when(jnp.logical_or(i == 0, be_ref[i] != be_ref[jnp.maximum(i - 1, 0)]))
        def _():
            wgu_bf[...] = wgu_ref[0].astype(BF16)
            wd_bf[...] = wd_ref[0].astype(BF16)

        xb_sc[...] = _load_token_tiles(xbuf.at[slot], MOE_ROWS).astype(BF16)
        gu = jnp.dot(xb_sc[...], wgu_bf[...], preferred_element_type=F32) + bgu_ref[0]
        gate = jnp.minimum(gu[:, 0:D_EXPERT], SWIGLU_LIMIT)
        up = jnp.clip(gu[:, D_EXPERT:2 * D_EXPERT], -SWIGLU_LIMIT, SWIGLU_LIMIT)
        act = (up + 1.0) * gate * jax.nn.sigmoid(SWIGLU_ALPHA * gate)
        y = jnp.dot(act.astype(BF16), wd_bf[...], preferred_element_type=F32) + bd_ref[0]
        _store_token_tiles(ybuf.at[slot], y)


def _moe(block_expert, nvalid, row_token3, row_dest3, h2, wgu, bgu3, wd, bd3, n_slot_rows):
    nb = block_expert.shape[0]
    expert_of = lambda i, be: be[jnp.minimum(i, nb - 1)]
    grid_spec = pltpu.PrefetchScalarGridSpec(
        num_scalar_prefetch=2,
        grid=(nb + 1,),
        in_specs=[
            pl.BlockSpec((1, 1, MOE_ROWS), lambda i, be, nv: (0, 0, 0), memory_space=pltpu.SMEM),
            pl.BlockSpec((1, 1, MOE_ROWS), lambda i, be, nv: (jnp.minimum(i + 1, nb - 1), 0, 0),
                         memory_space=pltpu.SMEM),
            pl.BlockSpec((1, 1, MOE_ROWS), lambda i, be, nv: (jnp.clip(i - 1, 0, nb - 1), 0, 0),
                         memory_space=pltpu.SMEM),
            pl.BlockSpec(memory_space=pl.ANY),
            pl.BlockSpec((1, D_MODEL, 2 * D_EXPERT), lambda i, be, nv: (expert_of(i, be), 0, 0)),
            pl.BlockSpec((1, 1, 2 * D_EXPERT), lambda i, be, nv: (expert_of(i, be), 0, 0)),
            pl.BlockSpec((1, D_EXPERT, D_MODEL), lambda i, be, nv: (expert_of(i, be), 0, 0)),
            pl.BlockSpec((1, 1, D_MODEL), lambda i, be, nv: (expert_of(i, be), 0, 0)),
        ],
        out_specs=pl.BlockSpec(memory_space=pl.ANY),
        scratch_shapes=[
            pltpu.VMEM((2, MOE_ROWS * TILE_ROWS, LANES), F32),
            pltpu.VMEM((2, MOE_ROWS * TILE_ROWS, LANES), F32),
            pltpu.VMEM((MOE_ROWS, D_MODEL), BF16),
            pltpu.VMEM((D_MODEL, 2 * D_EXPERT), BF16),
            pltpu.VMEM((D_EXPERT, D_MODEL), BF16),
            pltpu.SemaphoreType.DMA((2,)),
            pltpu.SemaphoreType.DMA((2,)),
        ],
    )
    return pl.pallas_call(
        functools.partial(_moe_kernel, nb=nb),
        grid_spec=grid_spec,
        out_shape=jax.ShapeDtypeStruct((n_slot_rows * TILE_ROWS, LANES), F32),
        compiler_params=pltpu.CompilerParams(
            dimension_semantics=("arbitrary",), vmem_limit_bytes=VMEM_LIMIT),
        name="moe_experts",
    )(block_expert, nvalid, row_token3, row_token3, row_dest3, h2, wgu, bgu3, wd, bd3)


def _combine_kernel(x1_ref, s0_ref, s1_ref, s2_ref, s3_ref, topw_ref, p_ref, pg_ref, wpg_ref, wpp_ref, fg_ref,
                    out_ref):
    x2 = x1_ref[...]
    topw = topw_ref[...]
    for k, s_ref in enumerate((s0_ref, s1_ref, s2_ref, s3_ref)):
        x2 = x2 + _load_token_tiles(s_ref, x2.shape[0]) * topw[:, k:k + 1]
    n = _rms(x2, pg_ref[...])
    gate = jax.nn.sigmoid(jnp.dot(n.astype(BF16), wpg_ref[...], preferred_element_type=F32))
    pp = jnp.dot(p_ref[...].astype(BF16), wpp_ref[...], preferred_element_type=F32)
    x3 = x2 + gate * pp
    out_ref[...] = _rms(x3, fg_ref[...])


def _combine(x1, slots, topw, p2d, pg, wpg, wpp, fg, tm=512):
    t = x1.shape[0]
    nt = t // tm
    const = lambda i: (0, 0)
    slot_specs = [pl.BlockSpec((tm * TILE_ROWS, LANES), functools.partial(lambda k, i: (k * nt + i, 0), k))
                  for k in range(TOP_K)]
    return pl.pallas_call(
        _combine_kernel,
        grid=(nt,),
        in_specs=[
            pl.BlockSpec((tm, D_MODEL), lambda i: (i, 0)),
            *slot_specs,
            pl.BlockSpec((tm, LANES), lambda i: (i, 0)),
            pl.BlockSpec((tm, D_PLE), lambda i: (i, 0)),
            pl.BlockSpec((1, D_MODEL), const),
            pl.BlockSpec((D_MODEL, D_MODEL), const),
            pl.BlockSpec((D_PLE, D_MODEL), const),
            pl.BlockSpec((1, D_MODEL), const),
        ],
        out_specs=pl.BlockSpec((tm, D_MODEL), lambda i: (i, 0)),
        out_shape=jax.ShapeDtypeStruct((t, D_MODEL), F32),
        compiler_params=pltpu.CompilerParams(
            dimension_semantics=("arbitrary",), vmem_limit_bytes=VMEM_LIMIT),
        name="combine_ple",
    )(x1, slots, slots, slots, slots, topw, p2d, pg, wpg, wpp, fg)


def _routing_tables(top_idx, n_tok):
    n_assign = n_tok * TOP_K
    expert_flat = top_idx.reshape(-1)
    order = jnp.argsort(expert_flat, stable=True).astype(jnp.int32)
    counts = jnp.bincount(expert_flat, length=N_EXPERTS).astype(jnp.int32)
    start = jnp.cumsum(counts) - counts
    padded = (counts + MOE_ROWS - 1) // MOE_ROWS * MOE_ROWS
    pend = jnp.cumsum(padded)
    pstart = pend - padded
    n_rows = n_assign + N_EXPERTS * MOE_ROWS
    n_blocks = n_rows // MOE_ROWS
    block_start = jnp.arange(n_blocks, dtype=jnp.int32) * MOE_ROWS
    block_expert = jnp.minimum(jnp.sum(block_start[:, None] >= pend[None, :], axis=1),
                               N_EXPERTS - 1).astype(jnp.int32)
    nvalid = (pend[-1] // MOE_ROWS).astype(jnp.int32).reshape(1)
    is_e = block_expert[:, None] == jnp.arange(N_EXPERTS, dtype=jnp.int32)[None, :]
    per_block = lambda v: jnp.sum(jnp.where(is_e, v[None, :], 0), axis=1)
    nreal = jnp.clip(per_block(pstart + counts) - block_start, 0, MOE_ROWS).astype(jnp.int32)
    sorted_pos = (block_start + per_block(start - pstart))[:, None] + jnp.arange(MOE_ROWS, dtype=jnp.int32)[None, :]
    assign = order[jnp.clip(sorted_pos, 0, n_assign - 1)]
    row_token = assign // TOP_K
    is_real = jnp.arange(MOE_ROWS, dtype=jnp.int32)[None, :] < nreal[:, None]
    row_q = block_start[:, None] + jnp.arange(MOE_ROWS, dtype=jnp.int32)[None, :]
    spare = n_assign + row_q - per_block(start + counts)[:, None]
    row_dest = jnp.where(is_real, (assign % TOP_K) * n_tok + row_token, spare)
    return (block_expert, nvalid, (row_token * TILE_ROWS).reshape(n_blocks, 1, MOE_ROWS),
            (row_dest * TILE_ROWS).reshape(n_blocks, 1, MOE_ROWS), n_rows)


def _layer(x2d, p2d, bsz, seq, mix_norm_g, w_in, conv_w, conv_b, dt_bias, a_log, d_skip, ssd_norm_g,
           w_ssd_out, pool_w, pool_scale, w_mix_out, ffn_norm_g, w_router, b_router,
           w_gate_up, b_gate_up, w_down, b_down, ple_norm_g, w_ple_gate, w_ple_proj, out_g):
    n_tok = x2d.shape[0]
    dt0 = D_INNER + D_CONV
    w_main = jnp.concatenate([w_in[:, :dt0], w_in[:, dt0 + HEADS:]], axis=1).astype(BF16)
    w_dt = jnp.pad(w_in[:, dt0:dt0 + HEADS], ((0, 0), (0, LANES - HEADS)))
    wdt_hi, wdt_lo = _split2(w_dt)
    wdt3 = jnp.concatenate([wdt_hi, wdt_hi, wdt_lo], axis=0)

    proj, dt_raw = _in_proj(x2d, mix_norm_g[None, :], w_main, wdt3)

    pad_h = lambda v: jnp.pad(v, (0, LANES - HEADS))[None, :]
    ltri = (jnp.arange(CHUNK)[:, None] >= jnp.arange(CHUNK)[None, :]).astype(BF16)
    ltri3 = jnp.concatenate([ltri, ltri, ltri], axis=1)
    e1 = (jnp.arange(LANES)[:, None] == (jnp.arange(D_INNER) // HEAD_DIM)[None, :]).astype(BF16)
    e2 = jnp.concatenate([e1, e1], axis=0)
    yssd, ypool = _mixers(
        proj, dt_raw, bsz, seq, conv_w, conv_b[None, :], pad_h(dt_bias), pad_h(a_log),
        jnp.repeat(d_skip, HEAD_DIM)[None, :], ssd_norm_g[None, :], ltri3, e2,
        pool_w.astype(BF16), pool_scale[None, :])

    wr = jnp.pad(w_router, ((0, 0), (0, LANES - N_EXPERTS)))
    wr_hi, wr_lo = _split2(wr)
    wr3 = jnp.concatenate([wr_hi, wr_hi, wr_lo], axis=0)
    br = jnp.pad(b_router, (0, LANES - N_EXPERTS))[None, :]
    x1, h2, topw, topi = _mix_route(x2d, yssd, ypool, proj, w_ssd_out.astype(BF16),
                                    w_mix_out.astype(BF16), ffn_norm_g[None, :], wr3, br)

    block_expert, nvalid, row_token3, row_dest3, n_slot_rows = _routing_tables(topi[:, :TOP_K], n_tok)
    slots = _moe(block_expert, nvalid, row_token3, row_dest3, h2, w_gate_up, b_gate_up[:, None, :],
                 w_down, b_down[:, None, :], n_slot_rows)

    return _combine(x1, slots, topw, p2d, ple_norm_g[None, :], w_ple_gate.astype(BF16),
                    w_ple_proj.astype(BF16), out_g[None, :])


def kernel(x, p, mix_norm_g, w_in, conv_w, conv_b, dt_bias, a_log, d_skip, ssd_norm_g, w_ssd_out, pool_w,
           pool_scale, w_mix_out, ffn_norm_g, w_router, b_router, w_gate_up, b_gate_up, w_down, b_down,
           ple_norm_g, w_ple_gate, w_ple_proj, final_norm_g):
    bsz, seq, d = x.shape
    depth = p.shape[0]
    assert depth == 1 and d == D_MODEL and seq % CHUNK == 0
    x2d = x.reshape(bsz * seq, d)
    out = _layer(x2d, p[0].reshape(bsz * seq, D_PLE), bsz, seq, mix_norm_g[0], w_in[0], conv_w[0], conv_b[0],
                 dt_bias[0], a_log[0], d_skip[0], ssd_norm_g[0], w_ssd_out[0], pool_w[0], pool_scale[0],
                 w_mix_out[0], ffn_norm_g[0], w_router[0], b_router[0], w_gate_up[0], b_gate_up[0],
                 w_down[0], b_down[0], ple_norm_g[0], w_ple_gate[0], w_ple_proj[0], final_norm_g)
    return out.reshape(bsz, seq, d)
```

```python
import functools

import jax
import jax.numpy as jnp
from jax import lax
from jax.experimental import pallas as pl
from jax.experimental.pallas import tpu as pltpu

F32 = jnp.float32
BF16 = jnp.bfloat16

D_MODEL = 1024
D_INNER = 2048
HEAD_DIM = 64
HEADS = 32
GROUPS = 4
HEADS_PER_GROUP = HEADS // GROUPS
GROUP_DIM = D_INNER // GROUPS
D_STATE = 128
CONV_WIDTH = 4
CHUNK = 128
D_BC = 2 * GROUPS * D_STATE
D_CONV = D_INNER + D_BC
POOL_WIDTH = D_MODEL
POOL_WINDOWS = (2, 4, 8, 16)
POOL_GROUP_DIM = POOL_WIDTH // len(POOL_WINDOWS)
N_EXPERTS = 32
TOP_K = 4
D_EXPERT = D_MODEL
SWIGLU_LIMIT = 7.0
SWIGLU_ALPHA = 1.702
D_PLE = 256
EPS = 1e-6

LANES = 128
TILE_ROWS = D_MODEL // LANES
ROUTE_SUB = 256
MIX_CHUNKS = 2
HALO = 16
D_PROJ = D_INNER + D_CONV + POOL_WIDTH + 2 * D_MODEL
MOE_ROWS = 256
MOE_COLS = 256
MOE_KSTEP = 256
VMEM_LIMIT = 56 * 1024 * 1024


def _split2(v):
    hi = v.astype(BF16)
    lo = (v - hi.astype(F32)).astype(BF16)
    return hi, lo


def _split3(v):
    hi = v.astype(BF16)
    r = v - hi.astype(F32)
    mid = r.astype(BF16)
    lo = (r - mid.astype(F32)).astype(BF16)
    return hi, mid, lo


def _rms(x, g):
    return x * lax.rsqrt(jnp.mean(x * x, axis=-1, keepdims=True) + EPS) * g


def _store_token_tiles(ref2d, val, row0=0):
    rows = val.shape[0]
    for j in range(TILE_ROWS):
        ref2d[pl.ds(row0 * TILE_ROWS + j, rows, stride=TILE_ROWS), :] = val[:, j * LANES:(j + 1) * LANES]


def _load_token_tiles(ref2d, rows):
    return jnp.concatenate([ref2d[pl.ds(j, rows, stride=TILE_ROWS), :] for j in range(TILE_ROWS)], axis=1)


def _in_proj_kernel(x_ref, g_ref, w_ref, wdt_ref, proj_ref, dt_ref, h_sc):
    @pl.when(pl.program_id(1) == 0)
    def _():
        h = _rms(x_ref[...], g_ref[...])
        hi, lo = _split2(h)
        h_sc[...] = hi
        lhs = jnp.concatenate([hi, lo, hi], axis=1)
        dt_ref[...] = jnp.dot(lhs, wdt_ref[...], preferred_element_type=F32)

    proj_ref[...] = jnp.dot(h_sc[...], w_ref[...], preferred_element_type=F32).astype(BF16)


def _in_proj(x2d, g, w_main, wdt3, tm=1024, tn=2048):
    t = x2d.shape[0]
    return pl.pallas_call(
        _in_proj_kernel,
        grid=(t // tm, D_PROJ // tn),
        in_specs=[
            pl.BlockSpec((tm, D_MODEL), lambda i, j: (i, 0)),
            pl.BlockSpec((1, D_MODEL), lambda i, j: (0, 0)),
            pl.BlockSpec((D_MODEL, tn), lambda i, j: (0, j)),
            pl.BlockSpec((3 * D_MODEL, LANES), lambda i, j: (0, 0)),
        ],
        out_specs=[
            pl.BlockSpec((tm, tn), lambda i, j: (i, j)),
            pl.BlockSpec((tm, LANES), lambda i, j: (i, 0)),
        ],
        out_shape=[
            jax.ShapeDtypeStruct((t, D_PROJ), BF16),
            jax.ShapeDtypeStruct((t, LANES), F32),
        ],
        scratch_shapes=[pltpu.VMEM((tm, D_MODEL), BF16)],
        compiler_params=pltpu.CompilerParams(
            dimension_semantics=("arbitrary", "arbitrary"), vmem_limit_bytes=VMEM_LIMIT),
        name="in_proj",
    )(x2d, g, w_main, wdt3)


def _mixers_kernel(z_ref, xs_ref, bc_ref, u_ref, dt_ref,
                   cw_ref, cb_ref, dtb_ref, alog_ref, dskip_ref, ng_ref, ltri_ref, e2_ref,
                   shift_ref, band_ref, pw_ref, ps_ref,
                   yssd_ref, ypool_ref,
                   ext_sc, extu_sc, state_sc):
    c = pl.program_id(1)
    rows = MIX_CHUNKS * CHUNK

    @pl.when(c == 0)
    def _():
        ext_sc[0:HALO, :] = jnp.zeros((HALO, D_CONV), BF16)
        extu_sc[0:HALO, :] = jnp.zeros((HALO, POOL_WIDTH), BF16)
        state_sc[...] = jnp.zeros_like(state_sc)

    @pl.when(c > 0)
    def _():
        ext_sc[0:HALO, :] = ext_sc[rows:rows + HALO, :]
        extu_sc[0:HALO, :] = extu_sc[rows:rows + HALO, :]

    ext_sc[HALO:HALO + rows, 0:D_INNER] = xs_ref[...]
    ext_sc[HALO:HALO + rows, D_INNER:D_CONV] = bc_ref[...]
    extu_sc[HALO:HALO + rows, :] = u_ref[...]
    for ci in range(MIX_CHUNKS):
        _mixers_chunk(ci, c * MIX_CHUNKS + ci, z_ref, u_ref, dt_ref, cw_ref, cb_ref, dtb_ref, alog_ref, dskip_ref,
                      ng_ref, ltri_ref, e2_ref, shift_ref, band_ref, pw_ref, ps_ref, yssd_ref, ypool_ref,
                      ext_sc, extu_sc, state_sc)


def _mixers_chunk(ci, chunk_index, z_ref, u_ref, dt_ref, cw_ref, cb_ref, dtb_ref, alog_ref, dskip_ref,
                  ng_ref, ltri_ref, e2_ref, shift_ref, band_ref, pw_ref, ps_ref, yssd_ref, ypool_ref,
                  ext_sc, extu_sc, state_sc):
    r0 = ci * CHUNK
    rs = slice(r0, r0 + CHUNK)
    ext = ext_sc[r0:r0 + HALO + CHUNK, :]
    conv = cb_ref[...] + cw_ref[CONV_WIDTH - 1:CONV_WIDTH, :] * ext[HALO:HALO + CHUNK, :].astype(F32)
    for k in range(CONV_WIDTH - 1):
        conv = conv + cw_ref[k:k + 1, :] * jnp.dot(shift_ref[k], ext, preferred_element_type=F32)
    xc = conv * jax.nn.sigmoid(conv)
    xs = xc[:, 0:D_INNER]
    xs_b = xs.astype(BF16)

    dtv = jax.nn.softplus(dt_ref[rs, :] + dtb_ref[...])
    da = dtv * (-jnp.exp(alog_ref[...]))
    a_cum = jnp.dot(ltri_ref[...], jnp.concatenate(_split3(da), axis=0),
                    preferred_element_type=F32)
    expa = jnp.exp(a_cum)
    a_last = a_cum[CHUNK - 1:CHUNK, :]
    wst = dtv * jnp.exp(a_last - a_cum)
    a_cum_t = a_cum.T
    dt_t = dtv.T

    both = jnp.concatenate([wst, expa], axis=0)
    hi, lo = _split2(both)
    expd = jnp.dot(jnp.concatenate([hi, lo], axis=1), e2_ref[...],
                   preferred_element_type=F32)
    wst_x = expd[0:CHUNK, :]
    expa_x = expd[CHUNK:2 * CHUNK, :]
    xw_b = (xs * wst_x).astype(BF16)

    row = lax.broadcasted_iota(jnp.int32, (CHUNK, CHUNK), 0)
    col = lax.broadcasted_iota(jnp.int32, (CHUNK, CHUNK), 1)
    causal_bias = jnp.where(row >= col, 0.0, -jnp.inf).astype(F32)
    lane = lax.broadcasted_iota(jnp.int32, (CHUNK, LANES), 1)
    low_half = lane < HEAD_DIM

    y_groups = []
    for g in range(GROUPS):
        bg = xc[:, D_INNER + g * D_STATE:D_INNER + (g + 1) * D_STATE]
        cg = xc[:, D_INNER + GROUPS * D_STATE + g * D_STATE:D_INNER + GROUPS * D_STATE + (g + 1) * D_STATE]
        bg_b = bg.astype(BF16)
        cg_b = cg.astype(BF16)
        cbm = lax.dot_general(cg_b, bg_b, (((1,), (1,)), ((), ())), preferred_element_type=F32)
        gsl = slice(g * GROUP_DIM, (g + 1) * GROUP_DIM)

        prev_t = state_sc[g]
        y_off = jnp.dot(cg_b, prev_t.astype(BF16), preferred_element_type=F32) * expa_x[:, gsl]
        st_t = jnp.dot(bg.T.astype(BF16), xw_b[:, gsl], preferred_element_type=F32)
        state_sc[g] = prev_t * expa_x[CHUNK - 1:CHUNK, gsl] + st_t

        pairs = []
        for jp in range(HEADS_PER_GROUP // 2):
            ms = []
            for hh in range(2):
                h = g * HEADS_PER_GROUP + jp * 2 + hh
                seg = a_cum[:, h:h + 1] - a_cum_t[h:h + 1, :]
                dec = jnp.exp(seg + causal_bias)
                ms.append((cbm * dec * dt_t[h:h + 1, :]).astype(BF16))
            lhs = jnp.concatenate(ms, axis=1)
            c0 = g * GROUP_DIM + jp * LANES
            xp = xs_b[:, c0:c0 + LANES]
            zero = jnp.zeros_like(xp)
            rhs = jnp.concatenate([jnp.where(low_half, xp, zero), jnp.where(low_half, zero, xp)], axis=0)
            pairs.append(jnp.dot(lhs, rhs, preferred_element_type=F32))
        y_diag = jnp.concatenate(pairs, axis=1)

        yg = y_diag + y_off + dskip_ref[:, gsl] * xs[:, gsl]
        zg = z_ref[rs, gsl].astype(F32)
        yg = yg * (zg * jax.nn.sigmoid(zg))
        yg = yg * lax.rsqrt(jnp.mean(yg * yg, axis=-1, keepdims=True) + EPS) * ng_ref[:, gsl]
        y_groups.append(yg.astype(BF16))
    yssd_ref[rs, :] = jnp.concatenate(y_groups, axis=1)

    pos = chunk_index * CHUNK + lax.broadcasted_iota(jnp.int32, (CHUNK, 1), 0)
    outs = []
    for gi, w in enumerate(POOL_WINDOWS):
        psl = slice(gi * POOL_GROUP_DIM, (gi + 1) * POOL_GROUP_DIM)
        s = jnp.dot(band_ref[gi], extu_sc[r0:r0 + HALO + CHUNK, psl], preferred_element_type=F32)
        cnt = jnp.minimum(pos + 1, w).astype(F32)
        pooled = s / cnt - u_ref[rs, psl].astype(F32)
        outs.append(jnp.dot(pooled.astype(BF16), pw_ref[gi], preferred_element_type=F32))
    ypool_ref[rs, :] = (jnp.concatenate(outs, axis=1) * ps_ref[...]).astype(BF16)


def _mixers(proj, dt_raw, bsz, seq, cw, cb, dtb, alog, dskip, ng, ltri3, e2, pw, ps):
    rows = MIX_CHUNKS * CHUNK
    nc = seq // rows
    t = bsz * seq
    rowmap = lambda b, c: b * nc + c
    const2 = lambda b, c: (0, 0)
    const3 = lambda b, c: (0, 0, 0)
    trow = jnp.arange(CHUNK)[:, None] + HALO
    jcol = jnp.arange(HALO + CHUNK)[None, :]
    shifts = jnp.stack([(jcol == trow - (CONV_WIDTH - 1) + k) for k in range(CONV_WIDTH - 1)]).astype(BF16)
    bands = jnp.stack([(jcol <= trow) & (jcol > trow - w) for w in POOL_WINDOWS]).astype(BF16)
    return pl.pallas_call(
        _mixers_kernel,
        grid=(bsz, nc),
        in_specs=[
            pl.BlockSpec((rows, D_INNER), lambda b, c: (rowmap(b, c), 0)),
            pl.BlockSpec((rows, D_INNER), lambda b, c: (rowmap(b, c), 1)),
            pl.BlockSpec((rows, D_BC), lambda b, c: (rowmap(b, c), 4)),
            pl.BlockSpec((rows, POOL_WIDTH), lambda b, c: (rowmap(b, c), 5)),
            pl.BlockSpec((rows, LANES), lambda b, c: (rowmap(b, c), 0)),
            pl.BlockSpec((CONV_WIDTH, D_CONV), const2),
            pl.BlockSpec((1, D_CONV), const2),
            pl.BlockSpec((1, LANES), const2),
            pl.BlockSpec((1, LANES), const2),
            pl.BlockSpec((1, D_INNER), const2),
            pl.BlockSpec((1, D_INNER), const2),
            pl.BlockSpec((CHUNK, 3 * CHUNK), const2),
            pl.BlockSpec((2 * LANES, D_INNER), const2),
            pl.BlockSpec((CONV_WIDTH - 1, CHUNK, HALO + CHUNK), const3),
            pl.BlockSpec((len(POOL_WINDOWS), CHUNK, HALO + CHUNK), const3),
            pl.BlockSpec((len(POOL_WINDOWS), POOL_GROUP_DIM, POOL_GROUP_DIM), const3),
            pl.BlockSpec((1, POOL_WIDTH), const2),
        ],
        out_specs=[
            pl.BlockSpec((rows, D_INNER), lambda b, c: (rowmap(b, c), 0)),
            pl.BlockSpec((rows, POOL_WIDTH), lambda b, c: (rowmap(b, c), 0)),
        ],
        out_shape=[
            jax.ShapeDtypeStruct((t, D_INNER), BF16),
            jax.ShapeDtypeStruct((t, POOL_WIDTH), BF16),
        ],
        scratch_shapes=[
            pltpu.VMEM((HALO + rows, D_CONV), BF16),
            pltpu.VMEM((HALO + rows, POOL_WIDTH), BF16),
            pltpu.VMEM((GROUPS, D_STATE, GROUP_DIM), F32),
        ],
        compiler_params=pltpu.CompilerParams(
            dimension_semantics=("arbitrary", "arbitrary"), vmem_limit_bytes=VMEM_LIMIT),
        name="mixers",
    )(proj, proj, proj, proj, dt_raw, cw, cb, dtb, alog, dskip, ng, ltri3, e2, shifts, bands, pw, ps)


def _mix_route_kernel(x_ref, yssd_ref, ypool_ref, gates_ref, wso_ref, wmo_ref, fg_ref, wr_ref, br_ref,
                      x1_ref, h2_ref, topw_ref, topi_ref):
    for r0 in range(0, x_ref.shape[0], ROUTE_SUB):
        _mix_route_rows(slice(r0, r0 + ROUTE_SUB), x_ref, yssd_ref, ypool_ref, gates_ref, wso_ref, wmo_ref, fg_ref,
                        wr_ref, br_ref, x1_ref, h2_ref, topw_ref, topi_ref)


def _mix_route_rows(rs, x_ref, yssd_ref, ypool_ref, gates_ref, wso_ref, wmo_ref, fg_ref, wr_ref, br_ref,
                    x1_ref, h2_ref, topw_ref, topi_ref):
    y_ssd = jnp.dot(yssd_ref[rs, :], wso_ref[...], preferred_element_type=F32)
    gates = jax.nn.sigmoid(gates_ref[rs, :].astype(F32))
    mixed = gates[:, 0:D_MODEL] * y_ssd + gates[:, D_MODEL:2 * D_MODEL] * ypool_ref[rs, :].astype(F32)
    x1 = x_ref[rs, :] + jnp.dot(mixed.astype(BF16), wmo_ref[...], preferred_element_type=F32)
    x1_ref[rs, :] = x1
    h2 = _rms(x1, fg_ref[...])
    _store_token_tiles(h2_ref, h2, rs.start)

    hi, lo = _split2(h2)
    logits = jnp.dot(jnp.concatenate([hi, lo, hi], axis=1), wr_ref[...],
                     preferred_element_type=F32) + br_ref[...]
    tm = logits.shape[0]
    lane = lax.broadcasted_iota(jnp.int32, (tm, LANES), 1)
    neg = jnp.float32(-jnp.inf)
    work = jnp.where(lane < N_EXPERTS, logits, neg)
    vals = []
    idxs = []
    for _ in range(TOP_K):
        m = jnp.max(work, axis=-1, keepdims=True)
        idx = jnp.min(jnp.where(work == m, lane, LANES), axis=-1, keepdims=True)
        vals.append(m)
        idxs.append(idx)
        work = jnp.where(lane == idx, neg, work)
    es = [jnp.exp(v - vals[0]) for v in vals]
    den = es[0] + es[1] + es[2] + es[3]
    topw = jnp.zeros((tm, LANES), F32)
    topi = jnp.zeros((tm, LANES), jnp.int32)
    for k in range(TOP_K):
        topw = jnp.where(lane == k, es[k] / den, topw)
        topi = jnp.where(lane == k, idxs[k], topi)
    topw_ref[rs, :] = topw
    topi_ref[rs, :] = topi


def _mix_route(x2d, yssd, ypool, proj, wso, wmo, fg, wr3, br, tm=512):
    t = x2d.shape[0]
    const = lambda i: (0, 0)
    return pl.pallas_call(
        _mix_route_kernel,
        grid=(t // tm,),
        in_specs=[
            pl.BlockSpec((tm, D_MODEL), lambda i: (i, 0)),
            pl.BlockSpec((tm, D_INNER), lambda i: (i, 0)),
            pl.BlockSpec((tm, POOL_WIDTH), lambda i: (i, 0)),
            pl.BlockSpec((tm, 2 * D_MODEL), lambda i: (i, 3)),
            pl.BlockSpec((D_INNER, D_MODEL), const),
            pl.BlockSpec((D_MODEL, D_MODEL), const),
            pl.BlockSpec((1, D_MODEL), const),
            pl.BlockSpec((3 * D_MODEL, LANES), const),
            pl.BlockSpec((1, LANES), const),
        ],
        out_specs=[
            pl.BlockSpec((tm, D_MODEL), lambda i: (i, 0)),
            pl.BlockSpec((tm * TILE_ROWS, LANES), lambda i: (i, 0)),
            pl.BlockSpec((tm, LANES), lambda i: (i, 0)),
            pl.BlockSpec((tm, LANES), lambda i: (i, 0)),
        ],
        out_shape=[
            jax.ShapeDtypeStruct((t, D_MODEL), F32),
            jax.ShapeDtypeStruct((t * TILE_ROWS, LANES), F32),
            jax.ShapeDtypeStruct((t, LANES), F32),
            jax.ShapeDtypeStruct((t, LANES), jnp.int32),
        ],
        compiler_params=pltpu.CompilerParams(
            dimension_semantics=("arbitrary",), vmem_limit_bytes=VMEM_LIMIT),
        name="mix_route",
    )(x2d, yssd, ypool, proj, wso, wmo, fg, wr3, br)


def _moe_kernel(be_ref, nvalid_ref,
                tok0_ref, toknext_ref, dstprev_ref,
                h_hbm, wgu_ref, bgu_ref, wd_ref, bd_ref,
                slots_hbm,
                xbuf, ybuf, xb_sc, act_sc, wgu_bf, wd_bf, gsem, ssem, *, nb):
    i = pl.program_id(0)
    nvalid = nvalid_ref[0]
    slot = lax.rem(i, 2)

    def tile(off, n=1):
        return pl.ds(pl.multiple_of(off, TILE_ROWS), n * TILE_ROWS)

    def gather_copy(tok, r, s):
        return pltpu.make_async_copy(h_hbm.at[tile(tok)], xbuf.at[s, tile(r * TILE_ROWS)], gsem.at[s])

    def scatter_copy(dst, r, s):
        return pltpu.make_async_copy(ybuf.at[s, tile(r * TILE_ROWS)], slots_hbm.at[tile(dst)], ssem.at[s])

    def wait_gather(s):
        pltpu.make_async_copy(h_hbm.at[tile(0, MOE_ROWS)], xbuf.at[s, tile(0, MOE_ROWS)], gsem.at[s]).wait()

    def wait_scatter(s):
        pltpu.make_async_copy(ybuf.at[s, tile(0, MOE_ROWS)], slots_hbm.at[tile(0, MOE_ROWS)], ssem.at[s]).wait()

    def issue(s, gather_ref, scatter_ref):
        for r in range(MOE_ROWS):
            if gather_ref is not None:
                gather_copy(gather_ref[0, 0, r], r, s).start()
            if scatter_ref is not None:
                scatter_copy(scatter_ref[0, 0, r], r, s).start()

    def compute(s, copy_rows):
        n_steps = (2 * D_EXPERT + D_MODEL) // MOE_COLS * (D_MODEL // MOE_KSTEP)
        rows_per_step = -(-MOE_ROWS // n_steps)
        step = [0]

        def chain(lhs_ref, w_ref, cols, bias):
            acc = bias
            for k0 in range(0, D_MODEL, MOE_KSTEP):
                ks = slice(k0, k0 + MOE_KSTEP)
                acc = acc + jnp.dot(lhs_ref[:, ks], w_ref[ks, cols], preferred_element_type=F32)
                if copy_rows is not None:
                    lo = min(step[0] * rows_per_step, MOE_ROWS)
                    copy_rows(lo, min(lo + rows_per_step, MOE_ROWS))
                step[0] += 1
            return acc

        xb_sc[...] = _load_token_tiles(xbuf.at[s], MOE_ROWS).astype(BF16)
        for c0 in range(0, D_EXPERT, MOE_COLS):
            gcols = slice(c0, c0 + MOE_COLS)
            ucols = slice(D_EXPERT + c0, D_EXPERT + c0 + MOE_COLS)
            gate = jnp.minimum(chain(xb_sc, wgu_bf, gcols, bgu_ref[0, :, gcols]), SWIGLU_LIMIT)
            up = jnp.clip(chain(xb_sc, wgu_bf, ucols, bgu_ref[0, :, ucols]), -SWIGLU_LIMIT, SWIGLU_LIMIT)
            act = (up + 1.0) * gate * jax.nn.sigmoid(SWIGLU_ALPHA * gate)
            act_sc[:, gcols] = act.astype(BF16)
        ytiles = ybuf.at[s]
        for c0 in range(0, D_MODEL, MOE_COLS):
            y = chain(act_sc, wd_bf, slice(c0, c0 + MOE_COLS), bd_ref[0, :, c0:c0 + MOE_COLS])
            for j in range(MOE_COLS // LANES):
                ytiles[pl.ds(c0 // LANES + j, MOE_ROWS, stride=TILE_ROWS), :] = y[:, j * LANES:(j + 1) * LANES]

    valid = i < nvalid

    @pl.when(i == 0)
    def _():
        ybuf[...] = jnp.zeros_like(ybuf)
        issue(0, tok0_ref, None)

    expert_changed = be_ref[jnp.minimum(i, nb - 1)] != be_ref[jnp.maximum(i - 1, 0)]

    @pl.when(jnp.logical_and(valid, jnp.logical_or(i == 0, expert_changed)))
    def _():
        wgu_bf[...] = wgu_ref[0].astype(BF16)
        wd_bf[...] = wd_ref[0].astype(BF16)

    for par in (0, 1):
        other = 1 - par
        mine = slot == par

        def copy_rows(lo, hi, other=other):
            for r in range(lo, hi):
                gather_copy(toknext_ref[0, 0, r], r, other).start()
                scatter_copy(dstprev_ref[0, 0, r], r, other).start()

        pl.when(jnp.logical_and(mine, i >= 2))(functools.partial(wait_scatter, par))
        pl.when(jnp.logical_and(mine, i <= nvalid))(functools.partial(wait_gather, par))
        if par == 0:
            @pl.when(jnp.logical_and(valid, i == 0))
            def _():
                issue(1, toknext_ref, None)
                compute(0, None)
        pl.when(jnp.logical_and(mine, jnp.logical_and(valid, i >= 1)))(
            functools.partial(compute, par, copy_rows))
        pl.when(jnp.logical_and(mine, jnp.logical_and(jnp.logical_not(valid), i >= 1)))(
            functools.partial(issue, other, None, dstprev_ref))
        pl.when(jnp.logical_and(mine, i == nb))(functools.partial(wait_scatter, other))


def _moe(block_expert, nvalid, row_token3, row_dest3, h2, wgu, bgu3, wd, bd3, n_slot_rows):
    nb = block_expert.shape[0]
    expert_of = lambda i, be: be[jnp.minimum(i, nb - 1)]
    grid_spec = pltpu.PrefetchScalarGridSpec(
        num_scalar_prefetch=2,
        grid=(nb + 1,),
        in_specs=[
            pl.BlockSpec((1, 1, MOE_ROWS), lambda i, be, nv: (0, 0, 0), memory_space=pltpu.SMEM),
            pl.BlockSpec((1, 1, MOE_ROWS), lambda i, be, nv: (jnp.minimum(i + 1, nb - 1), 0, 0),
                         memory_space=pltpu.SMEM),
            pl.BlockSpec((1, 1, MOE_ROWS), lambda i, be, nv: (jnp.clip(i - 1, 0, nb - 1), 0, 0),
                         memory_space=pltpu.SMEM),
            pl.BlockSpec(memory_space=pl.ANY),
            pl.BlockSpec((1, D_MODEL, 2 * D_EXPERT), lambda i, be, nv: (expert_of(i, be), 0, 0)),
            pl.BlockSpec((1, 1, 2 * D_EXPERT), lambda i, be, nv: (expert_of(i, be), 0, 0)),
            pl.BlockSpec((1, D_EXPERT, D_MODEL), lambda i, be, nv: (expert_of(i, be), 0, 0)),
            pl.BlockSpec((1, 1, D_MODEL), lambda i, be, nv: (expert_of(i, be), 0, 0)),
        ],
        out_specs=pl.BlockSpec(memory_space=pl.ANY),
        scratch_shapes=[
            pltpu.VMEM((2, MOE_ROWS * TILE_ROWS, LANES), F32),
            pltpu.VMEM((2, MOE_ROWS * TILE_ROWS, LANES), F32),
            pltpu.VMEM((MOE_ROWS, D_MODEL), BF16),
            pltpu.VMEM((MOE_ROWS, D_EXPERT), BF16),
            pltpu.VMEM((D_MODEL, 2 * D_EXPERT), BF16),
            pltpu.VMEM((D_EXPERT, D_MODEL), BF16),
            pltpu.SemaphoreType.DMA((2,)),
            pltpu.SemaphoreType.DMA((2,)),
        ],
    )
    return pl.pallas_call(
        functools.partial(_moe_kernel, nb=nb),
        grid_spec=grid_spec,
        out_shape=jax.ShapeDtypeStruct((n_slot_rows * TILE_ROWS, LANES), F32),
        compiler_params=pltpu.CompilerParams(
            dimension_semantics=("arbitrary",), vmem_limit_bytes=VMEM_LIMIT),
        name="moe_experts",
    )(block_expert, nvalid, row_token3, row_token3, row_dest3, h2, wgu, bgu3, wd, bd3)


def _combine_kernel(x1_ref, s0_ref, s1_ref, s2_ref, s3_ref, topw_ref, p_ref, pg_ref, wpg_ref, wpp_ref, fg_ref,
                    out_ref):
    x2 = x1_ref[...]
    topw = topw_ref[...]
    for k, s_ref in enumerate((s0_ref, s1_ref, s2_ref, s3_ref)):
        x2 = x2 + _load_token_tiles(s_ref, x2.shape[0]) * topw[:, k:k + 1]
    n = _rms(x2, pg_ref[...])
    gate = jax.nn.sigmoid(jnp.dot(n.astype(BF16), wpg_ref[...], preferred_element_type=F32))
    pp = jnp.dot(p_ref[...].astype(BF16), wpp_ref[...], preferred_element_type=F32)
    x3 = x2 + gate * pp
    out_ref[...] = _rms(x3, fg_ref[...])


def _combine(x1, slots, topw, p2d, pg, wpg, wpp, fg, tm=512):
    t = x1.shape[0]
    nt = t // tm
    const = lambda i: (0, 0)
    slot_specs = [pl.BlockSpec((tm * TILE_ROWS, LANES), functools.partial(lambda k, i: (k * nt + i, 0), k))
                  for k in range(TOP_K)]
    return pl.pallas_call(
        _combine_kernel,
        grid=(nt,),
        in_specs=[
            pl.BlockSpec((tm, D_MODEL), lambda i: (i, 0)),
            *slot_specs,
            pl.BlockSpec((tm, LANES), lambda i: (i, 0)),
            pl.BlockSpec((tm, D_PLE), lambda i: (i, 0)),
            pl.BlockSpec((1, D_MODEL), const),
            pl.BlockSpec((D_MODEL, D_MODEL), const),
            pl.BlockSpec((D_PLE, D_MODEL), const),
            pl.BlockSpec((1, D_MODEL), const),
        ],
        out_specs=pl.BlockSpec((tm, D_MODEL), lambda i: (i, 0)),
        out_shape=jax.ShapeDtypeStruct((t, D_MODEL), F32),
        compiler_params=pltpu.CompilerParams(
            dimension_semantics=("arbitrary",), vmem_limit_bytes=VMEM_LIMIT),
        name="combine_ple",
    )(x1, slots, slots, slots, slots, topw, p2d, pg, wpg, wpp, fg)


def _routing_tables(top_idx, n_tok):
    n_assign = n_tok * TOP_K
    expert_flat = top_idx.reshape(-1)
    order = jnp.argsort(expert_flat, stable=True).astype(jnp.int32)
    counts = jnp.bincount(expert_flat, length=N_EXPERTS).astype(jnp.int32)
    start = jnp.cumsum(counts) - counts
    padded = (counts + MOE_ROWS - 1) // MOE_ROWS * MOE_ROWS
    pend = jnp.cumsum(padded)
    pstart = pend - padded
    n_rows = n_assign + N_EXPERTS * MOE_ROWS
    n_blocks = n_rows // MOE_ROWS
    block_start = jnp.arange(n_blocks, dtype=jnp.int32) * MOE_ROWS
    block_expert = jnp.minimum(jnp.sum(block_start[:, None] >= pend[None, :], axis=1),
                               N_EXPERTS - 1).astype(jnp.int32)
    nvalid = (pend[-1] // MOE_ROWS).astype(jnp.int32).reshape(1)
    is_e = block_expert[:, None] == jnp.arange(N_EXPERTS, dtype=jnp.int32)[None, :]
    per_block = lambda v: jnp.sum(jnp.where(is_e, v[None, :], 0), axis=1)
    nreal = jnp.clip(per_block(pstart + counts) - block_start, 0, MOE_ROWS).astype(jnp.int32)
    sorted_pos = (block_start + per_block(start - pstart))[:, None] + jnp.arange(MOE_ROWS, dtype=jnp.int32)[None, :]
    assign = order[jnp.clip(sorted_pos, 0, n_assign - 1)]
    row_token = assign // TOP_K
    is_real = jnp.arange(MOE_ROWS, dtype=jnp.int32)[None, :] < nreal[:, None]
    row_q = block_start[:, None] + jnp.arange(MOE_ROWS, dtype=jnp.int32)[None, :]
    spare = n_assign + row_q - per_block(start + counts)[:, None]
    row_dest = jnp.where(is_real, (assign % TOP_K) * n_tok + row_token, spare)
    return (block_expert, nvalid, (row_token * TILE_ROWS).reshape(n_blocks, 1, MOE_ROWS),
            (row_dest * TILE_ROWS).reshape(n_blocks, 1, MOE_ROWS), n_rows)


def _layer(x2d, p2d, bsz, seq, mix_norm_g, w_in, conv_w, conv_b, dt_bias, a_log, d_skip, ssd_norm_g,
           w_ssd_out, pool_w, pool_scale, w_mix_out, ffn_norm_g, w_router, b_router,
           w_gate_up, b_gate_up, w_down, b_down, ple_norm_g, w_ple_gate, w_ple_proj, out_g):
    n_tok = x2d.shape[0]
    dt0 = D_INNER + D_CONV
    w_main = jnp.concatenate([w_in[:, :dt0], w_in[:, dt0 + HEADS:]], axis=1).astype(BF16)
    w_dt = jnp.pad(w_in[:, dt0:dt0 + HEADS], ((0, 0), (0, LANES - HEADS)))
    wdt_hi, wdt_lo = _split2(w_dt)
    wdt3 = jnp.concatenate([wdt_hi, wdt_hi, wdt_lo], axis=0)

    proj, dt_raw = _in_proj(x2d, mix_norm_g[None, :], w_main, wdt3)

    pad_h = lambda v: jnp.pad(v, (0, LANES - HEADS))[None, :]
    ltri = (jnp.arange(CHUNK)[:, None] >= jnp.arange(CHUNK)[None, :]).astype(BF16)
    ltri3 = jnp.concatenate([ltri, ltri, ltri], axis=1)
    e1 = (jnp.arange(LANES)[:, None] == (jnp.arange(D_INNER) // HEAD_DIM)[None, :]).astype(BF16)
    e2 = jnp.concatenate([e1, e1], axis=0)
    yssd, ypool = _mixers(
        proj, dt_raw, bsz, seq, conv_w, conv_b[None, :], pad_h(dt_bias), pad_h(a_log),
        jnp.repeat(d_skip, HEAD_DIM)[None, :], ssd_norm_g[None, :], ltri3, e2,
        pool_w.astype(BF16), pool_scale[None, :])

    wr = jnp.pad(w_router, ((0, 0), (0, LANES - N_EXPERTS)))
    wr_hi, wr_lo = _split2(wr)
    wr3 = jnp.concatenate([wr_hi, wr_hi, wr_lo], axis=0)
    br = jnp.pad(b_router, (0, LANES - N_EXPERTS))[None, :]
    x1, h2, topw, topi = _mix_route(x2d, yssd, ypool, proj, w_ssd_out.astype(BF16),
                                    w_mix_out.astype(BF16), ffn_norm_g[None, :], wr3, br)

    block_expert, nvalid, row_token3, row_dest3, n_slot_rows = _routing_tables(topi[:, :TOP_K], n_tok)
    slots = _moe(block_expert, nvalid, row_token3, row_dest3, h2, w_gate_up, b_gate_up[:, None, :],
                 w_down, b_down[:, None, :], n_slot_rows)

    return _combine(x1, slots, topw, p2d, ple_norm_g[None, :], w_ple_gate.astype(BF16),
                    w_ple_proj.astype(BF16), out_g[None, :])


def kernel(x, p, mix_norm_g, w_in, conv_w, conv_b, dt_bias, a_log, d_skip, ssd_norm_g, w_ssd_out, pool_w,
           pool_scale, w_mix_out, ffn_norm_g, w_router, b_router, w_gate_up, b_gate_up, w_down, b_down,
           ple_norm_g, w_ple_gate, w_ple_proj, final_norm_g):
    bsz, seq, d = x.shape
    depth = p.shape[0]
    assert depth == 1 and d == D_MODEL and seq % CHUNK == 0
    x2d = x.reshape(bsz * seq, d)
    out = _layer(x2d, p[0].reshape(bsz * seq, D_PLE), bsz, seq, mix_norm_g[0], w_in[0], conv_w[0], conv_b[0],
                 dt_bias[0], a_log[0], d_skip[0], ssd_norm_g[0], w_ssd_out[0], pool_w[0], pool_scale[0],
                 w_mix_out[0], ffn_norm_g[0], w_router[0], b_router[0], w_gate_up[0], b_gate_up[0],
                 w_down[0], b_down[0], ple_norm_g[0], w_ple_gate[0], w_ple_proj[0], final_norm_g)
    return out.reshape(bsz, seq, d)
```

```python
import functools

import jax
import jax.numpy as jnp
from jax import lax
from jax.experimental import pallas as pl
from jax.experimental.pallas import tpu as pltpu

F32 = jnp.float32
BF16 = jnp.bfloat16

D_MODEL = 1024
D_INNER = 2048
HEAD_DIM = 64
HEADS = 32
GROUPS = 4
HEADS_PER_GROUP = HEADS // GROUPS
GROUP_DIM = D_INNER // GROUPS
D_STATE = 128
CONV_WIDTH = 4
CHUNK = 128
D_BC = 2 * GROUPS * D_STATE
D_CONV = D_INNER + D_BC
POOL_WIDTH = D_MODEL
POOL_WINDOWS = (2, 4, 8, 16)
POOL_GROUP_DIM = POOL_WIDTH // len(POOL_WINDOWS)
N_EXPERTS = 32
TOP_K = 4
D_EXPERT = D_MODEL
SWIGLU_LIMIT = 7.0
SWIGLU_ALPHA = 1.702
D_PLE = 256
EPS = 1e-6

LANES = 128
TILE_ROWS = D_MODEL // LANES
ROUTE_SUB = 256
MIX_CHUNKS = 4
HALO = 16
D_PROJ = D_INNER + D_CONV + POOL_WIDTH + 2 * D_MODEL
MOE_ROWS = 256
VMEM_LIMIT = 56 * 1024 * 1024


def _split2(v):
    hi = v.astype(BF16)
    lo = (v - hi.astype(F32)).astype(BF16)
    return hi, lo


def _split3(v):
    hi = v.astype(BF16)
    r = v - hi.astype(F32)
    mid = r.astype(BF16)
    lo = (r - mid.astype(F32)).astype(BF16)
    return hi, mid, lo


def _rms(x, g):
    return x * lax.rsqrt(jnp.mean(x * x, axis=-1, keepdims=True) + EPS) * g


def _store_token_tiles(ref2d, val, row0=0):
    rows = val.shape[0]
    for j in range(TILE_ROWS):
        ref2d[pl.ds(row0 * TILE_ROWS + j, rows, stride=TILE_ROWS), :] = val[:, j * LANES:(j + 1) * LANES]


def _load_token_tiles(ref2d, rows):
    return jnp.concatenate([ref2d[pl.ds(j, rows, stride=TILE_ROWS), :] for j in range(TILE_ROWS)], axis=1)


def _in_proj_kernel(x_ref, g_ref, w_ref, wdt_ref, proj_ref, dt_ref, h_sc):
    @pl.when(pl.program_id(1) == 0)
    def _():
        h = _rms(x_ref[...], g_ref[...])
        hi, lo = _split2(h)
        h_sc[...] = hi
        lhs = jnp.concatenate([hi, lo, hi], axis=1)
        dt_ref[...] = jnp.dot(lhs, wdt_ref[...], preferred_element_type=F32)

    proj_ref[...] = jnp.dot(h_sc[...], w_ref[pl.program_id(1)], preferred_element_type=F32).astype(BF16)


def _in_proj(x2d, g, w_main, wdt3, tm=1024, tn=2048):
    t = x2d.shape[0]
    nj = D_PROJ // tn
    w_panels = jnp.stack([w_main[:, k * tn:(k + 1) * tn] for k in range(nj)])
    return pl.pallas_call(
        _in_proj_kernel,
        grid=(t // tm, nj),
        in_specs=[
            pl.BlockSpec((tm, D_MODEL), lambda i, j: (i, 0)),
            pl.BlockSpec((1, D_MODEL), lambda i, j: (0, 0)),
            pl.BlockSpec((nj, D_MODEL, tn), lambda i, j: (0, 0, 0), pipeline_mode=pl.Buffered(1)),
            pl.BlockSpec((3 * D_MODEL, LANES), lambda i, j: (0, 0)),
        ],
        out_specs=[
            pl.BlockSpec((tm, tn), lambda i, j: (i, j)),
            pl.BlockSpec((tm, LANES), lambda i, j: (i, 0)),
        ],
        out_shape=[
            jax.ShapeDtypeStruct((t, D_PROJ), BF16),
            jax.ShapeDtypeStruct((t, LANES), F32),
        ],
        scratch_shapes=[pltpu.VMEM((tm, D_MODEL), BF16)],
        compiler_params=pltpu.CompilerParams(
            dimension_semantics=("arbitrary", "arbitrary"), vmem_limit_bytes=VMEM_LIMIT),
        name="in_proj",
    )(x2d, g, w_panels, wdt3)


def _mixers_kernel(z_ref, xs_ref, bc_ref, u_ref, dt_ref,
                   cw_ref, cb_ref, dtb_ref, alog_ref, dskip_ref, ng_ref, ltri_ref, e2_ref,
                   shift_ref, band_ref, pw_ref, ps_ref,
                   yssd_ref, ypool_ref,
                   ext_sc, extu_sc, state_sc):
    c = pl.program_id(1)
    rows = MIX_CHUNKS * CHUNK

    @pl.when(c == 0)
    def _():
        ext_sc[0:HALO, :] = jnp.zeros((HALO, D_CONV), BF16)
        extu_sc[0:HALO, :] = jnp.zeros((HALO, POOL_WIDTH), BF16)
        state_sc[...] = jnp.zeros_like(state_sc)

    @pl.when(c > 0)
    def _():
        ext_sc[0:HALO, :] = ext_sc[rows:rows + HALO, :]
        extu_sc[0:HALO, :] = extu_sc[rows:rows + HALO, :]

    ext_sc[HALO:HALO + rows, 0:D_INNER] = xs_ref[...]
    ext_sc[HALO:HALO + rows, D_INNER:D_CONV] = bc_ref[...]
    extu_sc[HALO:HALO + rows, :] = u_ref[...]
    for ci in range(MIX_CHUNKS):
        _mixers_chunk(ci, c * MIX_CHUNKS + ci, z_ref, u_ref, dt_ref, cw_ref, cb_ref, dtb_ref, alog_ref, dskip_ref,
                      ng_ref, ltri_ref, e2_ref, shift_ref, band_ref, pw_ref, ps_ref, yssd_ref, ypool_ref,
                      ext_sc, extu_sc, state_sc)


def _mixers_chunk(ci, chunk_index, z_ref, u_ref, dt_ref, cw_ref, cb_ref, dtb_ref, alog_ref, dskip_ref,
                  ng_ref, ltri_ref, e2_ref, shift_ref, band_ref, pw_ref, ps_ref, yssd_ref, ypool_ref,
                  ext_sc, extu_sc, state_sc):
    r0 = ci * CHUNK
    rs = slice(r0, r0 + CHUNK)
    ext = ext_sc[r0:r0 + HALO + CHUNK, :]
    conv = cb_ref[...] + cw_ref[CONV_WIDTH - 1:CONV_WIDTH, :] * ext[HALO:HALO + CHUNK, :].astype(F32)
    for k in range(CONV_WIDTH - 1):
        conv = conv + cw_ref[k:k + 1, :] * jnp.dot(shift_ref[k], ext, preferred_element_type=F32)
    xc = conv * jax.nn.sigmoid(conv)
    xs = xc[:, 0:D_INNER]
    xs_b = xs.astype(BF16)

    dtv = jax.nn.softplus(dt_ref[rs, :] + dtb_ref[...])
    da = dtv * (-jnp.exp(alog_ref[...]))
    a_cum = jnp.dot(ltri_ref[...], jnp.concatenate(_split3(da), axis=0),
                    preferred_element_type=F32)
    expa = jnp.exp(a_cum)
    a_last = a_cum[CHUNK - 1:CHUNK, :]
    wst = dtv * jnp.exp(a_last - a_cum)
    a_cum_t = a_cum.T
    dt_t = dtv.T

    both = jnp.concatenate([wst, expa], axis=0)
    hi, lo = _split2(both)
    expd = jnp.dot(jnp.concatenate([hi, lo], axis=1), e2_ref[...],
                   preferred_element_type=F32)
    wst_x = expd[0:CHUNK, :]
    expa_x = expd[CHUNK:2 * CHUNK, :]
    xw_b = (xs * wst_x).astype(BF16)

    row = lax.broadcasted_iota(jnp.int32, (CHUNK, CHUNK), 0)
    col = lax.broadcasted_iota(jnp.int32, (CHUNK, CHUNK), 1)
    causal_bias = jnp.where(row >= col, 0.0, -jnp.inf).astype(F32)
    lane = lax.broadcasted_iota(jnp.int32, (CHUNK, LANES), 1)
    low_half = lane < HEAD_DIM

    y_groups = []
    for g in range(GROUPS):
        bg = xc[:, D_INNER + g * D_STATE:D_INNER + (g + 1) * D_STATE]
        cg = xc[:, D_INNER + GROUPS * D_STATE + g * D_STATE:D_INNER + GROUPS * D_STATE + (g + 1) * D_STATE]
        bg_b = bg.astype(BF16)
        cg_b = cg.astype(BF16)
        cbm = lax.dot_general(cg_b, bg_b, (((1,), (1,)), ((), ())), preferred_element_type=F32)
        gsl = slice(g * GROUP_DIM, (g + 1) * GROUP_DIM)

        prev_t = state_sc[g]
        y_off = jnp.dot(cg_b, prev_t.astype(BF16), preferred_element_type=F32) * expa_x[:, gsl]
        st_t = jnp.dot(bg.T.astype(BF16), xw_b[:, gsl], preferred_element_type=F32)
        state_sc[g] = prev_t * expa_x[CHUNK - 1:CHUNK, gsl] + st_t

        pairs = []
        for jp in range(HEADS_PER_GROUP // 2):
            ms = []
            for hh in range(2):
                h = g * HEADS_PER_GROUP + jp * 2 + hh
                seg = a_cum[:, h:h + 1] - a_cum_t[h:h + 1, :]
                dec = jnp.exp(seg + causal_bias)
                ms.append((cbm * dec * dt_t[h:h + 1, :]).astype(BF16))
            lhs = jnp.concatenate(ms, axis=1)
            c0 = g * GROUP_DIM + jp * LANES
            xp = xs_b[:, c0:c0 + LANES]
            zero = jnp.zeros_like(xp)
            rhs = jnp.concatenate([jnp.where(low_half, xp, zero), jnp.where(low_half, zero, xp)], axis=0)
            pairs.append(jnp.dot(lhs, rhs, preferred_element_type=F32))
        y_diag = jnp.concatenate(pairs, axis=1)

        yg = y_diag + y_off + dskip_ref[:, gsl] * xs[:, gsl]
        zg = z_ref[rs, gsl].astype(F32)
        yg = yg * (zg * jax.nn.sigmoid(zg))
        yg = yg * lax.rsqrt(jnp.mean(yg * yg, axis=-1, keepdims=True) + EPS) * ng_ref[:, gsl]
        y_groups.append(yg.astype(BF16))
    yssd_ref[rs, :] = jnp.concatenate(y_groups, axis=1)

    pos = chunk_index * CHUNK + lax.broadcasted_iota(jnp.int32, (CHUNK, 1), 0)
    outs = []
    for gi, w in enumerate(POOL_WINDOWS):
        psl = slice(gi * POOL_GROUP_DIM, (gi + 1) * POOL_GROUP_DIM)
        s = jnp.dot(band_ref[gi], extu_sc[r0:r0 + HALO + CHUNK, psl], preferred_element_type=F32)
        cnt = jnp.minimum(pos + 1, w).astype(F32)
        pooled = s / cnt - u_ref[rs, psl].astype(F32)
        outs.append(jnp.dot(pooled.astype(BF16), pw_ref[gi], preferred_element_type=F32))
    ypool_ref[rs, :] = (jnp.concatenate(outs, axis=1) * ps_ref[...]).astype(BF16)


def _mixers(proj, dt_raw, bsz, seq, cw, cb, dtb, alog, dskip, ng, ltri3, e2, pw, ps):
    rows = MIX_CHUNKS * CHUNK
    nc = seq // rows
    t = bsz * seq
    rowmap = lambda b, c: b * nc + c
    const2 = lambda b, c: (0, 0)
    const3 = lambda b, c: (0, 0, 0)
    trow = jnp.arange(CHUNK)[:, None] + HALO
    jcol = jnp.arange(HALO + CHUNK)[None, :]
    shifts = jnp.stack([(jcol == trow - (CONV_WIDTH - 1) + k) for k in range(CONV_WIDTH - 1)]).astype(BF16)
    bands = jnp.stack([(jcol <= trow) & (jcol > trow - w) for w in POOL_WINDOWS]).astype(BF16)
    return pl.pallas_call(
        _mixers_kernel,
        grid=(bsz, nc),
        in_specs=[
            pl.BlockSpec((rows, D_INNER), lambda b, c: (rowmap(b, c), 0)),
            pl.BlockSpec((rows, D_INNER), lambda b, c: (rowmap(b, c), 1)),
            pl.BlockSpec((rows, D_BC), lambda b, c: (rowmap(b, c), 4)),
            pl.BlockSpec((rows, POOL_WIDTH), lambda b, c: (rowmap(b, c), 5)),
            pl.BlockSpec((rows, LANES), lambda b, c: (rowmap(b, c), 0)),
            pl.BlockSpec((CONV_WIDTH, D_CONV), const2),
            pl.BlockSpec((1, D_CONV), const2),
            pl.BlockSpec((1, LANES), const2),
            pl.BlockSpec((1, LANES), const2),
            pl.BlockSpec((1, D_INNER), const2),
            pl.BlockSpec((1, D_INNER), const2),
            pl.BlockSpec((CHUNK, 3 * CHUNK), const2),
            pl.BlockSpec((2 * LANES, D_INNER), const2),
            pl.BlockSpec((CONV_WIDTH - 1, CHUNK, HALO + CHUNK), const3),
            pl.BlockSpec((len(POOL_WINDOWS), CHUNK, HALO + CHUNK), const3),
            pl.BlockSpec((len(POOL_WINDOWS), POOL_GROUP_DIM, POOL_GROUP_DIM), const3),
            pl.BlockSpec((1, POOL_WIDTH), const2),
        ],
        out_specs=[
            pl.BlockSpec((rows, D_INNER), lambda b, c: (rowmap(b, c), 0)),
            pl.BlockSpec((rows, POOL_WIDTH), lambda b, c: (rowmap(b, c), 0)),
        ],
        out_shape=[
            jax.ShapeDtypeStruct((t, D_INNER), BF16),
            jax.ShapeDtypeStruct((t, POOL_WIDTH), BF16),
        ],
        scratch_shapes=[
            pltpu.VMEM((HALO + rows, D_CONV), BF16),
            pltpu.VMEM((HALO + rows, POOL_WIDTH), BF16),
            pltpu.VMEM((GROUPS, D_STATE, GROUP_DIM), F32),
        ],
        compiler_params=pltpu.CompilerParams(
            dimension_semantics=("arbitrary", "arbitrary"), vmem_limit_bytes=VMEM_LIMIT),
        name="mixers",
    )(proj, proj, proj, proj, dt_raw, cw, cb, dtb, alog, dskip, ng, ltri3, e2, shifts, bands, pw, ps)


def _mix_route_kernel(x_ref, yssd_ref, ypool_ref, gates_ref, wso_ref, wmo_ref, fg_ref, wr_ref, br_ref,
                      x1_ref, h2_ref, topw_ref, topi_ref):
    for r0 in range(0, x_ref.shape[0], ROUTE_SUB):
        _mix_route_rows(slice(r0, r0 + ROUTE_SUB), x_ref, yssd_ref, ypool_ref, gates_ref, wso_ref, wmo_ref, fg_ref,
                        wr_ref, br_ref, x1_ref, h2_ref, topw_ref, topi_ref)


def _mix_route_rows(rs, x_ref, yssd_ref, ypool_ref, gates_ref, wso_ref, wmo_ref, fg_ref, wr_ref, br_ref,
                    x1_ref, h2_ref, topw_ref, topi_ref):
    y_ssd = jnp.dot(yssd_ref[rs, :], wso_ref[...], preferred_element_type=F32)
    gates = jax.nn.sigmoid(gates_ref[rs, :].astype(F32))
    mixed = gates[:, 0:D_MODEL] * y_ssd + gates[:, D_MODEL:2 * D_MODEL] * ypool_ref[rs, :].astype(F32)
    x1 = x_ref[rs, :] + jnp.dot(mixed.astype(BF16), wmo_ref[...], preferred_element_type=F32)
    x1_ref[rs, :] = x1
    h2 = _rms(x1, fg_ref[...])
    _store_token_tiles(h2_ref, h2, rs.start)

    hi, lo = _split2(h2)
    logits = jnp.dot(jnp.concatenate([hi, lo, hi], axis=1), wr_ref[...],
                     preferred_element_type=F32) + br_ref[...]
    tm = logits.shape[0]
    lane = lax.broadcasted_iota(jnp.int32, (tm, LANES), 1)
    neg = jnp.float32(-jnp.inf)
    work = jnp.where(lane < N_EXPERTS, logits, neg)
    vals = []
    idxs = []
    for _ in range(TOP_K):
        m = jnp.max(work, axis=-1, keepdims=True)
        idx = jnp.min(jnp.where(work == m, lane, LANES), axis=-1, keepdims=True)
        vals.append(m)
        idxs.append(idx)
        work = jnp.where(lane == idx, neg, work)
    es = [jnp.exp(v - vals[0]) for v in vals]
    den = es[0] + es[1] + es[2] + es[3]
    topw = jnp.zeros((tm, LANES), F32)
    topi = jnp.zeros((tm, LANES), jnp.int32)
    for k in range(TOP_K):
        topw = jnp.where(lane == k, es[k] / den, topw)
        topi = jnp.where(lane == k, idxs[k], topi)
    topw_ref[rs, :] = topw
    topi_ref[rs, :] = topi


def _mix_route(x2d, yssd, ypool, proj, wso, wmo, fg, wr3, br, tm=512):
    t = x2d.shape[0]
    const = lambda i: (0, 0)
    return pl.pallas_call(
        _mix_route_kernel,
        grid=(t // tm,),
        in_specs=[
            pl.BlockSpec((tm, D_MODEL), lambda i: (i, 0)),
            pl.BlockSpec((tm, D_INNER), lambda i: (i, 0)),
            pl.BlockSpec((tm, POOL_WIDTH), lambda i: (i, 0)),
            pl.BlockSpec((tm, 2 * D_MODEL), lambda i: (i, 3)),
            pl.BlockSpec((D_INNER, D_MODEL), const),
            pl.BlockSpec((D_MODEL, D_MODEL), const),
            pl.BlockSpec((1, D_MODEL), const),
            pl.BlockSpec((3 * D_MODEL, LANES), const),
            pl.BlockSpec((1, LANES), const),
        ],
        out_specs=[
            pl.BlockSpec((tm, D_MODEL), lambda i: (i, 0)),
            pl.BlockSpec((tm * TILE_ROWS, LANES), lambda i: (i, 0)),
            pl.BlockSpec((tm, LANES), lambda i: (i, 0)),
            pl.BlockSpec((tm, LANES), lambda i: (i, 0)),
        ],
        out_shape=[
            jax.ShapeDtypeStruct((t, D_MODEL), F32),
            jax.ShapeDtypeStruct((t * TILE_ROWS, LANES), F32),
            jax.ShapeDtypeStruct((t, LANES), F32),
            jax.ShapeDtypeStruct((t, LANES), jnp.int32),
        ],
        compiler_params=pltpu.CompilerParams(
            dimension_semantics=("arbitrary",), vmem_limit_bytes=VMEM_LIMIT),
        name="mix_route",
    )(x2d, yssd, ypool, proj, wso, wmo, fg, wr3, br)


def _moe_kernel(be_ref, nvalid_ref,
                tok0_ref, toknext_ref, dstprev_ref,
                h_hbm, wgu_ref, bgu_ref, wd_ref, bd_ref,
                slots_hbm,
                xbuf, ybuf, xb_sc, wgu_bf, wd_bf, gsem, ssem, *, nb):
    i = pl.program_id(0)
    nvalid = nvalid_ref[0]
    slot = lax.rem(i, 2)
    has_next = i + 1 < nvalid

    def tile(off, n=1):
        return pl.ds(pl.multiple_of(off, TILE_ROWS), n * TILE_ROWS)

    def gather_copy(tok, r, s):
        return pltpu.make_async_copy(h_hbm.at[tile(tok)], xbuf.at[s, tile(r * TILE_ROWS)], gsem.at[s])

    def scatter_copy(dst, r, s):
        return pltpu.make_async_copy(ybuf.at[s, tile(r * TILE_ROWS)], slots_hbm.at[tile(dst)], ssem.at[s])

    def wait_gather(s):
        pltpu.make_async_copy(h_hbm.at[tile(0, MOE_ROWS)], xbuf.at[s, tile(0, MOE_ROWS)], gsem.at[s]).wait()

    def wait_scatter(s):
        pltpu.make_async_copy(ybuf.at[s, tile(0, MOE_ROWS)], slots_hbm.at[tile(0, MOE_ROWS)], ssem.at[s]).wait()

    def issue(s, gather_ref, scatter_ref):
        for r in range(MOE_ROWS):
            if gather_ref is not None:
                gather_copy(gather_ref[0, 0, r], r, s).start()
            if scatter_ref is not None:
                scatter_copy(scatter_ref[0, 0, r], r, s).start()

    @pl.when(i == 0)
    def _():
        ybuf[...] = jnp.zeros_like(ybuf)
        issue(0, tok0_ref, None)

    for par in (0, 1):
        other = 1 - par
        mine = slot == par

        pl.when(jnp.logical_and(mine, i >= 2))(functools.partial(wait_scatter, par))
        pl.when(jnp.logical_and(mine, i < nvalid))(functools.partial(wait_gather, par))
        pl.when(jnp.logical_and(mine, jnp.logical_and(has_next, i == 0)))(
            functools.partial(issue, other, toknext_ref, None))
        pl.when(jnp.logical_and(mine, jnp.logical_and(has_next, i >= 1)))(
            functools.partial(issue, other, toknext_ref, dstprev_ref))
        pl.when(jnp.logical_and(mine, jnp.logical_and(jnp.logical_not(has_next), i >= 1)))(
            functools.partial(issue, other, None, dstprev_ref))
        pl.when(jnp.logical_and(mine, i == nb))(functools.partial(wait_scatter, other))

    @pl.when(i < nvalid)
    def _():
        @pl.when(jnp.logical_or(i == 0, be_ref[i] != be_ref[jnp.maximum(i - 1, 0)]))
        def _():
            wgu_bf[...] = wgu_ref[0].astype(BF16)
            wd_bf[...] = wd_ref[0].astype(BF16)

        xb_sc[...] = _load_token_tiles(xbuf.at[slot], MOE_ROWS).astype(BF16)
        gu = jnp.dot(xb_sc[...], wgu_bf[...], preferred_element_type=F32) + bgu_ref[0]
        gate = jnp.minimum(gu[:, 0:D_EXPERT], SWIGLU_LIMIT)
        up = jnp.clip(gu[:, D_EXPERT:2 * D_EXPERT], -SWIGLU_LIMIT, SWIGLU_LIMIT)
        act = (up + 1.0) * gate * jax.nn.sigmoid(SWIGLU_ALPHA * gate)
        y = jnp.dot(act.astype(BF16), wd_bf[...], preferred_element_type=F32) + bd_ref[0]
        _store_token_tiles(ybuf.at[slot], y)


def _moe(block_expert, nvalid, row_token3, row_dest3, h2, wgu, bgu3, wd, bd3, n_slot_rows):
    nb = block_expert.shape[0]
    expert_of = lambda i, be: be[jnp.minimum(i, nb - 1)]
    grid_spec = pltpu.PrefetchScalarGridSpec(
        num_scalar_prefetch=2,
        grid=(nb + 1,),
        in_specs=[
            pl.BlockSpec((1, 1, MOE_ROWS), lambda i, be, nv: (0, 0, 0), memory_space=pltpu.SMEM),
            pl.BlockSpec((1, 1, MOE_ROWS), lambda i, be, nv: (jnp.minimum(i + 1, nb - 1), 0, 0),
                         memory_space=pltpu.SMEM),
            pl.BlockSpec((1, 1, MOE_ROWS), lambda i, be, nv: (jnp.clip(i - 1, 0, nb - 1), 0, 0),
                         memory_space=pltpu.SMEM),
            pl.BlockSpec(memory_space=pl.ANY),
            pl.BlockSpec((1, D_MODEL, 2 * D_EXPERT), lambda i, be, nv: (expert_of(i, be), 0, 0)),
            pl.BlockSpec((1, 1, 2 * D_EXPERT), lambda i, be, nv: (expert_of(i, be), 0, 0)),
            pl.BlockSpec((1, D_EXPERT, D_MODEL), lambda i, be, nv: (expert_of(i, be), 0, 0)),
            pl.BlockSpec((1, 1, D_MODEL), lambda i, be, nv: (expert_of(i, be), 0, 0)),
        ],
        out_specs=pl.BlockSpec(memory_space=pl.ANY),
        scratch_shapes=[
            pltpu.VMEM((2, MOE_ROWS * TILE_ROWS, LANES), F32),
            pltpu.VMEM((2, MOE_ROWS * TILE_ROWS, LANES), F32),
            pltpu.VMEM((MOE_ROWS, D_MODEL), BF16),
            pltpu.VMEM((D_MODEL, 2 * D_EXPERT), BF16),
            pltpu.VMEM((D_EXPERT, D_MODEL), BF16),
            pltpu.SemaphoreType.DMA((2,)),
            pltpu.SemaphoreType.DMA((2,)),
        ],
    )
    return pl.pallas_call(
        functools.partial(_moe_kernel, nb=nb),
        grid_spec=grid_spec,
        out_shape=jax.ShapeDtypeStruct((n_slot_rows * TILE_ROWS, LANES), F32),
        compiler_params=pltpu.CompilerParams(
            dimension_semantics=("arbitrary",), vmem_limit_bytes=VMEM_LIMIT),
        name="moe_experts",
    )(block_expert, nvalid, row_token3, row_token3, row_dest3, h2, wgu, bgu3, wd, bd3)


def _combine_kernel(x1_ref, s0_ref, s1_ref, s2_ref, s3_ref, topw_ref, p_ref, pg_ref, wpg_ref, wpp_ref, fg_ref,
                    out_ref):
    x2 = x1_ref[...]
    topw = topw_ref[...]
    for k, s_ref in enumerate((s0_ref, s1_ref, s2_ref, s3_ref)):
        x2 = x2 + _load_token_tiles(s_ref, x2.shape[0]) * topw[:, k:k + 1]
    n = _rms(x2, pg_ref[...])
    gate = jax.nn.sigmoid(jnp.dot(n.astype(BF16), wpg_ref[...], preferred_element_type=F32))
    pp = jnp.dot(p_ref[...].astype(BF16), wpp_ref[...], preferred_element_type=F32)
    x3 = x2 + gate * pp
    out_ref[...] = _rms(x3, fg_ref[...])


def _combine(x1, slots, topw, p2d, pg, wpg, wpp, fg, tm=512):
    t = x1.shape[0]
    nt = t // tm
    const = lambda i: (0, 0)
    slot_specs = [pl.BlockSpec((tm * TILE_ROWS, LANES), functools.partial(lambda k, i: (k * nt + i, 0), k))
                  for k in range(TOP_K)]
    return pl.pallas_call(
        _combine_kernel,
        grid=(nt,),
        in_specs=[
            pl.BlockSpec((tm, D_MODEL), lambda i: (i, 0)),
            *slot_specs,
            pl.BlockSpec((tm, LANES), lambda i: (i, 0)),
            pl.BlockSpec((tm, D_PLE), lambda i: (i, 0)),
            pl.BlockSpec((1, D_MODEL), const),
            pl.BlockSpec((D_MODEL, D_MODEL), const),
            pl.BlockSpec((D_PLE, D_MODEL), const),
            pl.BlockSpec((1, D_MODEL), const),
        ],
        out_specs=pl.BlockSpec((tm, D_MODEL), lambda i: (i, 0)),
        out_shape=jax.ShapeDtypeStruct((t, D_MODEL), F32),
        compiler_params=pltpu.CompilerParams(
            dimension_semantics=("arbitrary",), vmem_limit_bytes=VMEM_LIMIT),
        name="combine_ple",
    )(x1, slots, slots, slots, slots, topw, p2d, pg, wpg, wpp, fg)


def _routing_tables(top_idx, n_tok):
    n_assign = n_tok * TOP_K
    expert_flat = top_idx.reshape(-1)
    order = jnp.argsort(expert_flat, stable=True).astype(jnp.int32)
    counts = jnp.bincount(expert_flat, length=N_EXPERTS).astype(jnp.int32)
    start = jnp.cumsum(counts) - counts
    padded = (counts + MOE_ROWS - 1) // MOE_ROWS * MOE_ROWS
    pend = jnp.cumsum(padded)
    pstart = pend - padded
    n_rows = n_assign + N_EXPERTS * MOE_ROWS
    n_blocks = n_rows // MOE_ROWS
    block_start = jnp.arange(n_blocks, dtype=jnp.int32) * MOE_ROWS
    block_expert = jnp.minimum(jnp.sum(block_start[:, None] >= pend[None, :], axis=1),
                               N_EXPERTS - 1).astype(jnp.int32)
    nvalid = (pend[-1] // MOE_ROWS).astype(jnp.int32).reshape(1)
    is_e = block_expert[:, None] == jnp.arange(N_EXPERTS, dtype=jnp.int32)[None, :]
    per_block = lambda v: jnp.sum(jnp.where(is_e, v[None, :], 0), axis=1)
    nreal = jnp.clip(per_block(pstart + counts) - block_start, 0, MOE_ROWS).astype(jnp.int32)
    sorted_pos = (block_start + per_block(start - pstart))[:, None] + jnp.arange(MOE_ROWS, dtype=jnp.int32)[None, :]
    assign = order[jnp.clip(sorted_pos, 0, n_assign - 1)]
    row_token = assign // TOP_K
    is_real = jnp.arange(MOE_ROWS, dtype=jnp.int32)[None, :] < nreal[:, None]
    row_q = block_start[:, None] + jnp.arange(MOE_ROWS, dtype=jnp.int32)[None, :]
    spare = n_assign + row_q - per_block(start + counts)[:, None]
    row_dest = jnp.where(is_real, (assign % TOP_K) * n_tok + row_token, spare)
    return (block_expert, nvalid, (row_token * TILE_ROWS).reshape(n_blocks, 1, MOE_ROWS),
            (row_dest * TILE_ROWS).reshape(n_blocks, 1, MOE_ROWS), n_rows)


def _layer(x2d, p2d, bsz, seq, mix_norm_g, w_in, conv_w, conv_b, dt_bias, a_log, d_skip, ssd_norm_g,
           w_ssd_out, pool_w, pool_scale, w_mix_out, ffn_norm_g, w_router, b_router,
           w_gate_up, b_gate_up, w_down, b_down, ple_norm_g, w_ple_gate, w_ple_proj, out_g):
    n_tok = x2d.shape[0]
    dt0 = D_INNER + D_CONV
    w_main = jnp.concatenate([w_in[:, :dt0], w_in[:, dt0 + HEADS:]], axis=1).astype(BF16)
    w_dt = jnp.pad(w_in[:, dt0:dt0 + HEADS], ((0, 0), (0, LANES - HEADS)))
    wdt_hi, wdt_lo = _split2(w_dt)
    wdt3 = jnp.concatenate([wdt_hi, wdt_hi, wdt_lo], axis=0)

    proj, dt_raw = _in_proj(x2d, mix_norm_g[None, :], w_main, wdt3)

    pad_h = lambda v: jnp.pad(v, (0, LANES - HEADS))[None, :]
    ltri = (jnp.arange(CHUNK)[:, None] >= jnp.arange(CHUNK)[None, :]).astype(BF16)
    ltri3 = jnp.concatenate([ltri, ltri, ltri], axis=1)
    e1 = (jnp.arange(LANES)[:, None] == (jnp.arange(D_INNER) // HEAD_DIM)[None, :]).astype(BF16)
    e2 = jnp.concatenate([e1, e1], axis=0)
    yssd, ypool = _mixers(
        proj, dt_raw, bsz, seq, conv_w, conv_b[None, :], pad_h(dt_bias), pad_h(a_log),
        jnp.repeat(d_skip, HEAD_DIM)[None, :], ssd_norm_g[None, :], ltri3, e2,
        pool_w.astype(BF16), pool_scale[None, :])

    wr = jnp.pad(w_router, ((0, 0), (0, LANES - N_EXPERTS)))
    wr_hi, wr_lo = _split2(wr)
    wr3 = jnp.concatenate([wr_hi, wr_hi, wr_lo], axis=0)
    br = jnp.pad(b_router, (0, LANES - N_EXPERTS))[None, :]
    x1, h2, topw, topi = _mix_route(x2d, yssd, ypool, proj, w_ssd_out.astype(BF16),
                                    w_mix_out.astype(BF16), ffn_norm_g[None, :], wr3, br)

    block_expert, nvalid, row_token3, row_dest3, n_slot_rows = _routing_tables(topi[:, :TOP_K], n_tok)
    slots = _moe(block_expert, nvalid, row_token3, row_dest3, h2, w_gate_up, b_gate_up[:, None, :],
                 w_down, b_down[:, None, :], n_slot_rows)

    return _combine(x1, slots, topw, p2d, ple_norm_g[None, :], w_ple_gate.astype(BF16),
                    w_ple_proj.astype(BF16), out_g[None, :])


def kernel(x, p, mix_norm_g, w_in, conv_w, conv_b, dt_bias, a_log, d_skip, ssd_norm_g, w_ssd_out, pool_w,
           pool_scale, w_mix_out, ffn_norm_g, w_router, b_router, w_gate_up, b_gate_up, w_down, b_down,
           ple_norm_g, w_ple_gate, w_ple_proj, final_norm_g):
    bsz, seq, d = x.shape
    depth = p.shape[0]
    assert depth == 1 and d == D_MODEL and seq % CHUNK == 0
    x2d = x.reshape(bsz * seq, d)
    out = _layer(x2d, p[0].reshape(bsz * seq, D_PLE), bsz, seq, mix_norm_g[0], w_in[0], conv_w[0], conv_b[0],
                 dt_bias[0], a_log[0], d_skip[0], ssd_norm_g[0], w_ssd_out[0], pool_w[0], pool_scale[0],
                 w_mix_out[0], ffn_norm_g[0], w_router[0], b_router[0], w_gate_up[0], b_gate_up[0],
                 w_down[0], b_down[0], ple_norm_g[0], w_ple_gate[0], w_ple_proj[0], final_norm_g)
    return out.reshape(bsz, seq, d)
```

```python
import functools

import jax
import jax.numpy as jnp
from jax import lax
from jax.experimental import pallas as pl
from jax.experimental.pallas import tpu as pltpu

F32 = jnp.float32
BF16 = jnp.bfloat16

D_MODEL = 1024
D_INNER = 2048
HEAD_DIM = 64
HEADS = 32
GROUPS = 4
HEADS_PER_GROUP = HEADS // GROUPS
GROUP_DIM = D_INNER // GROUPS
D_STATE = 128
CONV_WIDTH = 4
CHUNK = 128
D_BC = 2 * GROUPS * D_STATE
D_CONV = D_INNER + D_BC
POOL_WIDTH = D_MODEL
POOL_WINDOWS = (2, 4, 8, 16)
POOL_GROUP_DIM = POOL_WIDTH // len(POOL_WINDOWS)
N_EXPERTS = 32
TOP_K = 4
D_EXPERT = D_MODEL
SWIGLU_LIMIT = 7.0
SWIGLU_ALPHA = 1.702
D_PLE = 256
EPS = 1e-6

LANES = 128
TILE_ROWS = D_MODEL // (2 * LANES)
ROUTE_SUB = 256
MIX_CHUNKS = 4
HALO = 16
D_PROJ = D_INNER + D_CONV + POOL_WIDTH + 2 * D_MODEL
MOE_ROWS = 256
VMEM_LIMIT = 56 * 1024 * 1024


def _split2(v):
    hi = v.astype(BF16)
    lo = (v - hi.astype(F32)).astype(BF16)
    return hi, lo


def _split3(v):
    hi = v.astype(BF16)
    r = v - hi.astype(F32)
    mid = r.astype(BF16)
    lo = (r - mid.astype(F32)).astype(BF16)
    return hi, mid, lo


def _rms(x, g):
    return x * lax.rsqrt(jnp.mean(x * x, axis=-1, keepdims=True) + EPS) * g


def _store_token_tiles(ref2d, val, row0=0):
    rows = val.shape[0]
    half = D_MODEL // 2
    hi = lax.bitcast_convert_type(val[:, :half].astype(BF16).astype(F32), jnp.uint32)
    lo = lax.bitcast_convert_type(val[:, half:].astype(BF16).astype(F32), jnp.uint32)
    words = hi | lax.shift_right_logical(lo, jnp.uint32(16))
    for j in range(TILE_ROWS):
        ref2d[pl.ds(row0 * TILE_ROWS + j, rows, stride=TILE_ROWS), :] = words[:, j * LANES:(j + 1) * LANES]


def _load_token_tiles(ref2d, rows):
    words = jnp.concatenate([ref2d[pl.ds(j, rows, stride=TILE_ROWS), :] for j in range(TILE_ROWS)], axis=1)
    hi = lax.bitcast_convert_type(words & jnp.uint32(0xFFFF0000), F32)
    lo = lax.bitcast_convert_type(lax.shift_left(words, jnp.uint32(16)), F32)
    return jnp.concatenate([hi, lo], axis=1)


def _in_proj_kernel(x_ref, g_ref, w_ref, wdt_ref, proj_ref, dt_ref, h_sc):
    @pl.when(pl.program_id(1) == 0)
    def _():
        h = _rms(x_ref[...], g_ref[...])
        hi, lo = _split2(h)
        h_sc[...] = hi
        lhs = jnp.concatenate([hi, lo, hi], axis=1)
        dt_ref[...] = jnp.dot(lhs, wdt_ref[...], preferred_element_type=F32)

    proj_ref[...] = jnp.dot(h_sc[...], w_ref[...], preferred_element_type=F32).astype(BF16)


def _in_proj(x2d, g, w_main, wdt3, tm=1024, tn=2048):
    t = x2d.shape[0]
    return pl.pallas_call(
        _in_proj_kernel,
        grid=(t // tm, D_PROJ // tn),
        in_specs=[
            pl.BlockSpec((tm, D_MODEL), lambda i, j: (i, 0)),
            pl.BlockSpec((1, D_MODEL), lambda i, j: (0, 0)),
            pl.BlockSpec((D_MODEL, tn), lambda i, j: (0, j)),
            pl.BlockSpec((3 * D_MODEL, LANES), lambda i, j: (0, 0)),
        ],
        out_specs=[
            pl.BlockSpec((tm, tn), lambda i, j: (i, j)),
            pl.BlockSpec((tm, LANES), lambda i, j: (i, 0)),
        ],
        out_shape=[
            jax.ShapeDtypeStruct((t, D_PROJ), BF16),
            jax.ShapeDtypeStruct((t, LANES), F32),
        ],
        scratch_shapes=[pltpu.VMEM((tm, D_MODEL), BF16)],
        compiler_params=pltpu.CompilerParams(
            dimension_semantics=("arbitrary", "arbitrary"), vmem_limit_bytes=VMEM_LIMIT),
        name="in_proj",
    )(x2d, g, w_main, wdt3)


def _mixers_kernel(z_ref, xs_ref, bc_ref, u_ref, dt_ref,
                   cw_ref, cb_ref, dtb_ref, alog_ref, dskip_ref, ng_ref, ltri_ref, e2_ref,
                   shift_ref, band_ref, pw_ref, ps_ref,
                   yssd_ref, ypool_ref,
                   ext_sc, extu_sc, state_sc):
    c = pl.program_id(1)
    rows = MIX_CHUNKS * CHUNK

    @pl.when(c == 0)
    def _():
        ext_sc[0:HALO, :] = jnp.zeros((HALO, D_CONV), BF16)
        extu_sc[0:HALO, :] = jnp.zeros((HALO, POOL_WIDTH), BF16)
        state_sc[...] = jnp.zeros_like(state_sc)

    @pl.when(c > 0)
    def _():
        ext_sc[0:HALO, :] = ext_sc[rows:rows + HALO, :]
        extu_sc[0:HALO, :] = extu_sc[rows:rows + HALO, :]

    ext_sc[HALO:HALO + rows, 0:D_INNER] = xs_ref[...]
    ext_sc[HALO:HALO + rows, D_INNER:D_CONV] = bc_ref[...]
    extu_sc[HALO:HALO + rows, :] = u_ref[...]
    for ci in range(MIX_CHUNKS):
        _mixers_chunk(ci, c * MIX_CHUNKS + ci, z_ref, u_ref, dt_ref, cw_ref, cb_ref, dtb_ref, alog_ref, dskip_ref,
                      ng_ref, ltri_ref, e2_ref, shift_ref, band_ref, pw_ref, ps_ref, yssd_ref, ypool_ref,
                      ext_sc, extu_sc, state_sc)


def _mixers_chunk(ci, chunk_index, z_ref, u_ref, dt_ref, cw_ref, cb_ref, dtb_ref, alog_ref, dskip_ref,
                  ng_ref, ltri_ref, e2_ref, shift_ref, band_ref, pw_ref, ps_ref, yssd_ref, ypool_ref,
                  ext_sc, extu_sc, state_sc):
    r0 = ci * CHUNK
    rs = slice(r0, r0 + CHUNK)
    ext = ext_sc[r0:r0 + HALO + CHUNK, :]
    conv = cb_ref[...] + cw_ref[CONV_WIDTH - 1:CONV_WIDTH, :] * ext[HALO:HALO + CHUNK, :].astype(F32)
    for k in range(CONV_WIDTH - 1):
        conv = conv + cw_ref[k:k + 1, :] * jnp.dot(shift_ref[k], ext, preferred_element_type=F32)
    xc = conv * jax.nn.sigmoid(conv)
    xs = xc[:, 0:D_INNER]
    xs_b = xs.astype(BF16)

    dtv = jax.nn.softplus(dt_ref[rs, :] + dtb_ref[...])
    da = dtv * (-jnp.exp(alog_ref[...]))
    a_cum = jnp.dot(ltri_ref[...], jnp.concatenate(_split3(da), axis=0),
                    preferred_element_type=F32)
    expa = jnp.exp(a_cum)
    a_last = a_cum[CHUNK - 1:CHUNK, :]
    wst = dtv * jnp.exp(a_last - a_cum)
    a_cum_t = a_cum.T
    dt_t = dtv.T

    both = jnp.concatenate([wst, expa], axis=0)
    hi, lo = _split2(both)
    expd = jnp.dot(jnp.concatenate([hi, lo], axis=1), e2_ref[...],
                   preferred_element_type=F32)
    wst_x = expd[0:CHUNK, :]
    expa_x = expd[CHUNK:2 * CHUNK, :]
    xw_b = (xs * wst_x).astype(BF16)

    row = lax.broadcasted_iota(jnp.int32, (CHUNK, CHUNK), 0)
    col = lax.broadcasted_iota(jnp.int32, (CHUNK, CHUNK), 1)
    causal_bias = jnp.where(row >= col, 0.0, -jnp.inf).astype(F32)
    lane = lax.broadcasted_iota(jnp.int32, (CHUNK, LANES), 1)
    low_half = lane < HEAD_DIM

    y_groups = []
    for g in range(GROUPS):
        bg = xc[:, D_INNER + g * D_STATE:D_INNER + (g + 1) * D_STATE]
        cg = xc[:, D_INNER + GROUPS * D_STATE + g * D_STATE:D_INNER + GROUPS * D_STATE + (g + 1) * D_STATE]
        bg_b = bg.astype(BF16)
        cg_b = cg.astype(BF16)
        cbm = lax.dot_general(cg_b, bg_b, (((1,), (1,)), ((), ())), preferred_element_type=F32)
        gsl = slice(g * GROUP_DIM, (g + 1) * GROUP_DIM)

        prev_t = state_sc[g]
        y_off = jnp.dot(cg_b, prev_t.astype(BF16), preferred_element_type=F32) * expa_x[:, gsl]
        st_t = jnp.dot(bg.T.astype(BF16), xw_b[:, gsl], preferred_element_type=F32)
        state_sc[g] = prev_t * expa_x[CHUNK - 1:CHUNK, gsl] + st_t

        pairs = []
        for jp in range(HEADS_PER_GROUP // 2):
            ms = []
            for hh in range(2):
                h = g * HEADS_PER_GROUP + jp * 2 + hh
                seg = a_cum[:, h:h + 1] - a_cum_t[h:h + 1, :]
                dec = jnp.exp(seg + causal_bias)
                ms.append((cbm * dec * dt_t[h:h + 1, :]).astype(BF16))
            lhs = jnp.concatenate(ms, axis=1)
            c0 = g * GROUP_DIM + jp * LANES
            xp = xs_b[:, c0:c0 + LANES]
            zero = jnp.zeros_like(xp)
            rhs = jnp.concatenate([jnp.where(low_half, xp, zero), jnp.where(low_half, zero, xp)], axis=0)
            pairs.append(jnp.dot(lhs, rhs, preferred_element_type=F32))
        y_diag = jnp.concatenate(pairs, axis=1)

        yg = y_diag + y_off + dskip_ref[:, gsl] * xs[:, gsl]
        zg = z_ref[rs, gsl].astype(F32)
        yg = yg * (zg * jax.nn.sigmoid(zg))
        yg = yg * lax.rsqrt(jnp.mean(yg * yg, axis=-1, keepdims=True) + EPS) * ng_ref[:, gsl]
        y_groups.append(yg.astype(BF16))
    yssd_ref[rs, :] = jnp.concatenate(y_groups, axis=1)

    pos = chunk_index * CHUNK + lax.broadcasted_iota(jnp.int32, (CHUNK, 1), 0)
    outs = []
    for gi, w in enumerate(POOL_WINDOWS):
        psl = slice(gi * POOL_GROUP_DIM, (gi + 1) * POOL_GROUP_DIM)
        s = jnp.dot(band_ref[gi], extu_sc[r0:r0 + HALO + CHUNK, psl], preferred_element_type=F32)
        cnt = jnp.minimum(pos + 1, w).astype(F32)
        pooled = s / cnt - u_ref[rs, psl].astype(F32)
        outs.append(jnp.dot(pooled.astype(BF16), pw_ref[gi], preferred_element_type=F32))
    ypool_ref[rs, :] = (jnp.concatenate(outs, axis=1) * ps_ref[...]).astype(BF16)


def _mixers(proj, dt_raw, bsz, seq, cw, cb, dtb, alog, dskip, ng, ltri3, e2, pw, ps):
    rows = MIX_CHUNKS * CHUNK
    nc = seq // rows
    t = bsz * seq
    rowmap = lambda b, c: b * nc + c
    const2 = lambda b, c: (0, 0)
    const3 = lambda b, c: (0, 0, 0)
    trow = jnp.arange(CHUNK)[:, None] + HALO
    jcol = jnp.arange(HALO + CHUNK)[None, :]
    shifts = jnp.stack([(jcol == trow - (CONV_WIDTH - 1) + k) for k in range(CONV_WIDTH - 1)]).astype(BF16)
    bands = jnp.stack([(jcol <= trow) & (jcol > trow - w) for w in POOL_WINDOWS]).astype(BF16)
    return pl.pallas_call(
        _mixers_kernel,
        grid=(bsz, nc),
        in_specs=[
            pl.BlockSpec((rows, D_INNER), lambda b, c: (rowmap(b, c), 0)),
            pl.BlockSpec((rows, D_INNER), lambda b, c: (rowmap(b, c), 1)),
            pl.BlockSpec((rows, D_BC), lambda b, c: (rowmap(b, c), 4)),
            pl.BlockSpec((rows, POOL_WIDTH), lambda b, c: (rowmap(b, c), 5)),
            pl.BlockSpec((rows, LANES), lambda b, c: (rowmap(b, c), 0)),
            pl.BlockSpec((CONV_WIDTH, D_CONV), const2),
            pl.BlockSpec((1, D_CONV), const2),
            pl.BlockSpec((1, LANES), const2),
            pl.BlockSpec((1, LANES), const2),
            pl.BlockSpec((1, D_INNER), const2),
            pl.BlockSpec((1, D_INNER), const2),
            pl.BlockSpec((CHUNK, 3 * CHUNK), const2),
            pl.BlockSpec((2 * LANES, D_INNER), const2),
            pl.BlockSpec((CONV_WIDTH - 1, CHUNK, HALO + CHUNK), const3),
            pl.BlockSpec((len(POOL_WINDOWS), CHUNK, HALO + CHUNK), const3),
            pl.BlockSpec((len(POOL_WINDOWS), POOL_GROUP_DIM, POOL_GROUP_DIM), const3),
            pl.BlockSpec((1, POOL_WIDTH), const2),
        ],
        out_specs=[
            pl.BlockSpec((rows, D_INNER), lambda b, c: (rowmap(b, c), 0)),
            pl.BlockSpec((rows, POOL_WIDTH), lambda b, c: (rowmap(b, c), 0)),
        ],
        out_shape=[
            jax.ShapeDtypeStruct((t, D_INNER), BF16),
            jax.ShapeDtypeStruct((t, POOL_WIDTH), BF16),
        ],
        scratch_shapes=[
            pltpu.VMEM((HALO + rows, D_CONV), BF16),
            pltpu.VMEM((HALO + rows, POOL_WIDTH), BF16),
            pltpu.VMEM((GROUPS, D_STATE, GROUP_DIM), F32),
        ],
        compiler_params=pltpu.CompilerParams(
            dimension_semantics=("arbitrary", "arbitrary"), vmem_limit_bytes=VMEM_LIMIT),
        name="mixers",
    )(proj, proj, proj, proj, dt_raw, cw, cb, dtb, alog, dskip, ng, ltri3, e2, shifts, bands, pw, ps)


def _mix_route_kernel(x_ref, yssd_ref, ypool_ref, gates_ref, wso_ref, wmo_ref, fg_ref, wr_ref, br_ref,
                      x1_ref, h2_ref, topw_ref, topi_ref):
    for r0 in range(0, x_ref.shape[0], ROUTE_SUB):
        _mix_route_rows(slice(r0, r0 + ROUTE_SUB), x_ref, yssd_ref, ypool_ref, gates_ref, wso_ref, wmo_ref, fg_ref,
                        wr_ref, br_ref, x1_ref, h2_ref, topw_ref, topi_ref)


def _mix_route_rows(rs, x_ref, yssd_ref, ypool_ref, gates_ref, wso_ref, wmo_ref, fg_ref, wr_ref, br_ref,
                    x1_ref, h2_ref, topw_ref, topi_ref):
    y_ssd = jnp.dot(yssd_ref[rs, :], wso_ref[...], preferred_element_type=F32)
    gates = jax.nn.sigmoid(gates_ref[rs, :].astype(F32))
    mixed = gates[:, 0:D_MODEL] * y_ssd + gates[:, D_MODEL:2 * D_MODEL] * ypool_ref[rs, :].astype(F32)
    x1 = x_ref[rs, :] + jnp.dot(mixed.astype(BF16), wmo_ref[...], preferred_element_type=F32)
    x1_ref[rs, :] = x1
    h2 = _rms(x1, fg_ref[...])
    _store_token_tiles(h2_ref, h2, rs.start)

    hi, lo = _split2(h2)
    logits = jnp.dot(jnp.concatenate([hi, lo, hi], axis=1), wr_ref[...],
                     preferred_element_type=F32) + br_ref[...]
    tm = logits.shape[0]
    lane = lax.broadcasted_iota(jnp.int32, (tm, LANES), 1)
    neg = jnp.float32(-jnp.inf)
    work = jnp.where(lane < N_EXPERTS, logits, neg)
    vals = []
    idxs = []
    for _ in range(TOP_K):
        m = jnp.max(work, axis=-1, keepdims=True)
        idx = jnp.min(jnp.where(work == m, lane, LANES), axis=-1, keepdims=True)
        vals.append(m)
        idxs.append(idx)
        work = jnp.where(lane == idx, neg, work)
    es = [jnp.exp(v - vals[0]) for v in vals]
    den = es[0] + es[1] + es[2] + es[3]
    topw = jnp.zeros((tm, LANES), F32)
    topi = jnp.zeros((tm, LANES), jnp.int32)
    for k in range(TOP_K):
        topw = jnp.where(lane == k, es[k] / den, topw)
        topi = jnp.where(lane == k, idxs[k], topi)
    topw_ref[rs, :] = topw
    topi_ref[rs, :] = topi


def _mix_route(x2d, yssd, ypool, proj, wso, wmo, fg, wr3, br, tm=512):
    t = x2d.shape[0]
    const = lambda i: (0, 0)
    return pl.pallas_call(
        _mix_route_kernel,
        grid=(t // tm,),
        in_specs=[
            pl.BlockSpec((tm, D_MODEL), lambda i: (i, 0)),
            pl.BlockSpec((tm, D_INNER), lambda i: (i, 0)),
            pl.BlockSpec((tm, POOL_WIDTH), lambda i: (i, 0)),
            pl.BlockSpec((tm, 2 * D_MODEL), lambda i: (i, 3)),
            pl.BlockSpec((D_INNER, D_MODEL), const),
            pl.BlockSpec((D_MODEL, D_MODEL), const),
            pl.BlockSpec((1, D_MODEL), const),
            pl.BlockSpec((3 * D_MODEL, LANES), const),
            pl.BlockSpec((1, LANES), const),
        ],
        out_specs=[
            pl.BlockSpec((tm, D_MODEL), lambda i: (i, 0)),
            pl.BlockSpec((tm * TILE_ROWS, LANES), lambda i: (i, 0)),
            pl.BlockSpec((tm, LANES), lambda i: (i, 0)),
            pl.BlockSpec((tm, LANES), lambda i: (i, 0)),
        ],
        out_shape=[
            jax.ShapeDtypeStruct((t, D_MODEL), F32),
            jax.ShapeDtypeStruct((t * TILE_ROWS, LANES), jnp.uint32),
            jax.ShapeDtypeStruct((t, LANES), F32),
            jax.ShapeDtypeStruct((t, LANES), jnp.int32),
        ],
        compiler_params=pltpu.CompilerParams(
            dimension_semantics=("arbitrary",), vmem_limit_bytes=VMEM_LIMIT),
        name="mix_route",
    )(x2d, yssd, ypool, proj, wso, wmo, fg, wr3, br)


def _moe_kernel(be_ref, nvalid_ref,
                tok0_ref, toknext_ref, dstprev_ref,
                h_hbm, wgu_ref, bgu_ref, wd_ref, bd_ref,
                slots_hbm,
                xbuf, ybuf, xb_sc, wgu_bf, wd_bf, gsem, ssem, *, nb):
    i = pl.program_id(0)
    nvalid = nvalid_ref[0]
    slot = lax.rem(i, 2)
    has_next = i + 1 < nvalid

    def tile(off, n=1):
        return pl.ds(pl.multiple_of(off, TILE_ROWS), n * TILE_ROWS)

    def gather_copy(tok, r, s):
        return pltpu.make_async_copy(h_hbm.at[tile(tok)], xbuf.at[s, tile(r * TILE_ROWS)], gsem.at[s])

    def scatter_copy(dst, r, s):
        return pltpu.make_async_copy(ybuf.at[s, tile(r * TILE_ROWS)], slots_hbm.at[tile(dst)], ssem.at[s])

    def wait_gather(s):
        pltpu.make_async_copy(h_hbm.at[tile(0, MOE_ROWS)], xbuf.at[s, tile(0, MOE_ROWS)], gsem.at[s]).wait()

    def wait_scatter(s):
        pltpu.make_async_copy(ybuf.at[s, tile(0, MOE_ROWS)], slots_hbm.at[tile(0, MOE_ROWS)], ssem.at[s]).wait()

    def issue(s, gather_ref, scatter_ref):
        for r in range(MOE_ROWS):
            if gather_ref is not None:
                gather_copy(gather_ref[0, 0, r], r, s).start()
            if scatter_ref is not None:
                scatter_copy(scatter_ref[0, 0, r], r, s).start()

    @pl.when(i == 0)
    def _():
        ybuf[...] = jnp.zeros_like(ybuf)
        issue(0, tok0_ref, None)

    for par in (0, 1):
        other = 1 - par
        mine = slot == par

        pl.when(jnp.logical_and(mine, i >= 2))(functools.partial(wait_scatter, par))
        pl.when(jnp.logical_and(mine, i < nvalid))(functools.partial(wait_gather, par))
        pl.when(jnp.logical_and(mine, jnp.logical_and(has_next, i == 0)))(
            functools.partial(issue, other, toknext_ref, None))
        pl.when(jnp.logical_and(mine, jnp.logical_and(has_next, i >= 1)))(
            functools.partial(issue, other, toknext_ref, dstprev_ref))
        pl.when(jnp.logical_and(mine, jnp.logical_and(jnp.logical_not(has_next), i >= 1)))(
            functools.partial(issue, other, None, dstprev_ref))
        pl.when(jnp.logical_and(mine, i == nb))(functools.partial(wait_scatter, other))

    @pl.when(i < nvalid)
    def _():
        @pl.when(jnp.logical_or(i == 0, be_ref[i] != be_ref[jnp.maximum(i - 1, 0)]))
        def _():
            wgu_bf[...] = wgu_ref[0].astype(BF16)
            wd_bf[...] = wd_ref[0].astype(BF16)

        xb_sc[...] = _load_token_tiles(xbuf.at[slot], MOE_ROWS).astype(BF16)
        gu = jnp.dot(xb_sc[...], wgu_bf[...], preferred_element_type=F32) + bgu_ref[0]
        gate = jnp.minimum(gu[:, 0:D_EXPERT], SWIGLU_LIMIT)
        up = jnp.clip(gu[:, D_EXPERT:2 * D_EXPERT], -SWIGLU_LIMIT, SWIGLU_LIMIT)
        act = (up + 1.0) * gate * jax.nn.sigmoid(SWIGLU_ALPHA * gate)
        y = jnp.dot(act.astype(BF16), wd_bf[...], preferred_element_type=F32) + bd_ref[0]
        _store_token_tiles(ybuf.at[slot], y)


def _moe(block_expert, nvalid, row_token3, row_dest3, h2, wgu, bgu3, wd, bd3, n_slot_rows):
    nb = block_expert.shape[0]
    expert_of = lambda i, be: be[jnp.minimum(i, nb - 1)]
    grid_spec = pltpu.PrefetchScalarGridSpec(
        num_scalar_prefetch=2,
        grid=(nb + 1,),
        in_specs=[
            pl.BlockSpec((1, 1, MOE_ROWS), lambda i, be, nv: (0, 0, 0), memory_space=pltpu.SMEM),
            pl.BlockSpec((1, 1, MOE_ROWS), lambda i, be, nv: (jnp.minimum(i + 1, nb - 1), 0, 0),
                         memory_space=pltpu.SMEM),
            pl.BlockSpec((1, 1, MOE_ROWS), lambda i, be, nv: (jnp.clip(i - 1, 0, nb - 1), 0, 0),
                         memory_space=pltpu.SMEM),
            pl.BlockSpec(memory_space=pl.ANY),
            pl.BlockSpec((1, D_MODEL, 2 * D_EXPERT), lambda i, be, nv: (expert_of(i, be), 0, 0)),
            pl.BlockSpec((1, 1, 2 * D_EXPERT), lambda i, be, nv: (expert_of(i, be), 0, 0)),
            pl.BlockSpec((1, D_EXPERT, D_MODEL), lambda i, be, nv: (expert_of(i, be), 0, 0)),
            pl.BlockSpec((1, 1, D_MODEL), lambda i, be, nv: (expert_of(i, be), 0, 0)),
        ],
        out_specs=pl.BlockSpec(memory_space=pl.ANY),
        scratch_shapes=[
            pltpu.VMEM((2, MOE_ROWS * TILE_ROWS, LANES), jnp.uint32),
            pltpu.VMEM((2, MOE_ROWS * TILE_ROWS, LANES), jnp.uint32),
            pltpu.VMEM((MOE_ROWS, D_MODEL), BF16),
            pltpu.VMEM((D_MODEL, 2 * D_EXPERT), BF16),
            pltpu.VMEM((D_EXPERT, D_MODEL), BF16),
            pltpu.SemaphoreType.DMA((2,)),
            pltpu.SemaphoreType.DMA((2,)),
        ],
    )
    return pl.pallas_call(
        functools.partial(_moe_kernel, nb=nb),
        grid_spec=grid_spec,
        out_shape=jax.ShapeDtypeStruct((n_slot_rows * TILE_ROWS, LANES), jnp.uint32),
        compiler_params=pltpu.CompilerParams(
            dimension_semantics=("arbitrary",), vmem_limit_bytes=VMEM_LIMIT),
        name="moe_experts",
    )(block_expert, nvalid, row_token3, row_token3, row_dest3, h2, wgu, bgu3, wd, bd3)


def _combine_kernel(x1_ref, s0_ref, s1_ref, s2_ref, s3_ref, topw_ref, p_ref, pg_ref, wpg_ref, wpp_ref, fg_ref,
                    out_ref):
    x2 = x1_ref[...]
    topw = topw_ref[...]
    for k, s_ref in enumerate((s0_ref, s1_ref, s2_ref, s3_ref)):
        x2 = x2 + _load_token_tiles(s_ref, x2.shape[0]) * topw[:, k:k + 1]
    n = _rms(x2, pg_ref[...])
    gate = jax.nn.sigmoid(jnp.dot(n.astype(BF16), wpg_ref[...], preferred_element_type=F32))
    pp = jnp.dot(p_ref[...].astype(BF16), wpp_ref[...], preferred_element_type=F32)
    x3 = x2 + gate * pp
    out_ref[...] = _rms(x3, fg_ref[...])


def _combine(x1, slots, topw, p2d, pg, wpg, wpp, fg, tm=512):
    t = x1.shape[0]
    nt = t // tm
    const = lambda i: (0, 0)
    slot_specs = [pl.BlockSpec((tm * TILE_ROWS, LANES), functools.partial(lambda k, i: (k * nt + i, 0), k))
                  for k in range(TOP_K)]
    return pl.pallas_call(
        _combine_kernel,
        grid=(nt,),
        in_specs=[
            pl.BlockSpec((tm, D_MODEL), lambda i: (i, 0)),
            *slot_specs,
            pl.BlockSpec((tm, LANES), lambda i: (i, 0)),
            pl.BlockSpec((tm, D_PLE), lambda i: (i, 0)),
            pl.BlockSpec((1, D_MODEL), const),
            pl.BlockSpec((D_MODEL, D_MODEL), const),
            pl.BlockSpec((D_PLE, D_MODEL), const),
            pl.BlockSpec((1, D_MODEL), const),
        ],
        out_specs=pl.BlockSpec((tm, D_MODEL), lambda i: (i, 0)),
        out_shape=jax.ShapeDtypeStruct((t, D_MODEL), F32),
        compiler_params=pltpu.CompilerParams(
            dimension_semantics=("arbitrary",), vmem_limit_bytes=VMEM_LIMIT),
        name="combine_ple",
    )(x1, slots, slots, slots, slots, topw, p2d, pg, wpg, wpp, fg)


def _routing_tables(top_idx, n_tok):
    n_assign = n_tok * TOP_K
    expert_flat = top_idx.reshape(-1)
    order = jnp.argsort(expert_flat, stable=True).astype(jnp.int32)
    counts = jnp.bincount(expert_flat, length=N_EXPERTS).astype(jnp.int32)
    start = jnp.cumsum(counts) - counts
    padded = (counts + MOE_ROWS - 1) // MOE_ROWS * MOE_ROWS
    pend = jnp.cumsum(padded)
    pstart = pend - padded
    n_rows = n_assign + N_EXPERTS * MOE_ROWS
    n_blocks = n_rows // MOE_ROWS
    block_start = jnp.arange(n_blocks, dtype=jnp.int32) * MOE_ROWS
    block_expert = jnp.minimum(jnp.sum(block_start[:, None] >= pend[None, :], axis=1),
                               N_EXPERTS - 1).astype(jnp.int32)
    nvalid = (pend[-1] // MOE_ROWS).astype(jnp.int32).reshape(1)
    is_e = block_expert[:, None] == jnp.arange(N_EXPERTS, dtype=jnp.int32)[None, :]
    per_block = lambda v: jnp.sum(jnp.where(is_e, v[None, :], 0), axis=1)
    nreal = jnp.clip(per_block(pstart + counts) - block_start, 0, MOE_ROWS).astype(jnp.int32)
    sorted_pos = (block_start + per_block(start - pstart))[:, None] + jnp.arange(MOE_ROWS, dtype=jnp.int32)[None, :]
    assign = order[jnp.clip(sorted_pos, 0, n_assign - 1)]
    row_token = assign // TOP_K
    is_real = jnp.arange(MOE_ROWS, dtype=jnp.int32)[None, :] < nreal[:, None]
    row_q = block_start[:, None] + jnp.arange(MOE_ROWS, dtype=jnp.int32)[None, :]
    spare = n_assign + row_q - per_block(start + counts)[:, None]
    row_dest = jnp.where(is_real, (assign % TOP_K) * n_tok + row_token, spare)
    return (block_expert, nvalid, (row_token * TILE_ROWS).reshape(n_blocks, 1, MOE_ROWS),
            (row_dest * TILE_ROWS).reshape(n_blocks, 1, MOE_ROWS), n_rows)


def _layer(x2d, p2d, bsz, seq, mix_norm_g, w_in, conv_w, conv_b, dt_bias, a_log, d_skip, ssd_norm_g,
           w_ssd_out, pool_w, pool_scale, w_mix_out, ffn_norm_g, w_router, b_router,
           w_gate_up, b_gate_up, w_down, b_down, ple_norm_g, w_ple_gate, w_ple_proj, out_g):
    n_tok = x2d.shape[0]
    dt0 = D_INNER + D_CONV
    w_main = jnp.concatenate([w_in[:, :dt0], w_in[:, dt0 + HEADS:]], axis=1).astype(BF16)
    w_dt = jnp.pad(w_in[:, dt0:dt0 + HEADS], ((0, 0), (0, LANES - HEADS)))
    wdt_hi, wdt_lo = _split2(w_dt)
    wdt3 = jnp.concatenate([wdt_hi, wdt_hi, wdt_lo], axis=0)

    proj, dt_raw = _in_proj(x2d, mix_norm_g[None, :], w_main, wdt3)

    pad_h = lambda v: jnp.pad(v, (0, LANES - HEADS))[None, :]
    ltri = (jnp.arange(CHUNK)[:, None] >= jnp.arange(CHUNK)[None, :]).astype(BF16)
    ltri3 = jnp.concatenate([ltri, ltri, ltri], axis=1)
    e1 = (jnp.arange(LANES)[:, None] == (jnp.arange(D_INNER) // HEAD_DIM)[None, :]).astype(BF16)
    e2 = jnp.concatenate([e1, e1], axis=0)
    yssd, ypool = _mixers(
        proj, dt_raw, bsz, seq, conv_w, conv_b[None, :], pad_h(dt_bias), pad_h(a_log),
        jnp.repeat(d_skip, HEAD_DIM)[None, :], ssd_norm_g[None, :], ltri3, e2,
        pool_w.astype(BF16), pool_scale[None, :])

    wr = jnp.pad(w_router, ((0, 0), (0, LANES - N_EXPERTS)))
    wr_hi, wr_lo = _split2(wr)
    wr3 = jnp.concatenate([wr_hi, wr_hi, wr_lo], axis=0)
    br = jnp.pad(b_router, (0, LANES - N_EXPERTS))[None, :]
    x1, h2, topw, topi = _mix_route(x2d, yssd, ypool, proj, w_ssd_out.astype(BF16),
                                    w_mix_out.astype(BF16), ffn_norm_g[None, :], wr3, br)

    block_expert, nvalid, row_token3, row_dest3, n_slot_rows = _routing_tables(topi[:, :TOP_K], n_tok)
    slots = _moe(block_expert, nvalid, row_token3, row_dest3, h2, w_gate_up, b_gate_up[:, None, :],
                 w_down, b_down[:, None, :], n_slot_rows)

    return _combine(x1, slots, topw, p2d, ple_norm_g[None, :], w_ple_gate.astype(BF16),
                    w_ple_proj.astype(BF16), out_g[None, :])


def kernel(x, p, mix_norm_g, w_in, conv_w, conv_b, dt_bias, a_log, d_skip, ssd_norm_g, w_ssd_out, pool_w,
           pool_scale, w_mix_out, ffn_norm_g, w_router, b_router, w_gate_up, b_gate_up, w_down, b_down,
           ple_norm_g, w_ple_gate, w_ple_proj, final_norm_g):
    bsz, seq, d = x.shape
    depth = p.shape[0]
    assert depth == 1 and d == D_MODEL and seq % CHUNK == 0
    x2d = x.reshape(bsz * seq, d)
    out = _layer(x2d, p[0].reshape(bsz * seq, D_PLE), bsz, seq, mix_norm_g[0], w_in[0], conv_w[0], conv_b[0],
                 dt_bias[0], a_log[0], d_skip[0], ssd_norm_g[0], w_ssd_out[0], pool_w[0], pool_scale[0],
                 w_mix_out[0], ffn_norm_g[0], w_router[0], b_router[0], w_gate_up[0], b_gate_up[0],
                 w_down[0], b_down[0], ple_norm_g[0], w_ple_gate[0], w_ple_proj[0], final_norm_g)
    return out.reshape(bsz, seq, d)
```

```python
import functools

import jax
import jax.numpy as jnp
from jax import lax
from jax.experimental import pallas as pl
from jax.experimental.pallas import tpu as pltpu

F32 = jnp.float32
BF16 = jnp.bfloat16

D_MODEL = 1024
D_INNER = 2048
HEAD_DIM = 64
HEADS = 32
GROUPS = 4
HEADS_PER_GROUP = HEADS // GROUPS
GROUP_DIM = D_INNER // GROUPS
D_STATE = 128
CONV_WIDTH = 4
CHUNK = 128
D_BC = 2 * GROUPS * D_STATE
D_CONV = D_INNER + D_BC
POOL_WIDTH = D_MODEL
POOL_WINDOWS = (2, 4, 8, 16)
POOL_GROUP_DIM = POOL_WIDTH // len(POOL_WINDOWS)
N_EXPERTS = 32
TOP_K = 4
D_EXPERT = D_MODEL
SWIGLU_LIMIT = 7.0
SWIGLU_ALPHA = 1.702
D_PLE = 256
EPS = 1e-6

LANES = 128
TILE_ROWS = D_MODEL // (2 * LANES)
TOPI_ROWS = 8
ROUTE_SUB = 256
MIX_CHUNKS = 4
HALO = 16
D_PROJ = D_INNER + D_CONV + POOL_WIDTH + 2 * D_MODEL
MOE_ROWS = 256
VMEM_LIMIT = 56 * 1024 * 1024


def _split2(v):
    hi = v.astype(BF16)
    lo = (v - hi.astype(F32)).astype(BF16)
    return hi, lo


def _split3(v):
    hi = v.astype(BF16)
    r = v - hi.astype(F32)
    mid = r.astype(BF16)
    lo = (r - mid.astype(F32)).astype(BF16)
    return hi, mid, lo


def _rms(x, g):
    return x * lax.rsqrt(jnp.mean(x * x, axis=-1, keepdims=True) + EPS) * g


def _store_token_tiles(ref2d, val, row0=0):
    rows = val.shape[0]
    half = D_MODEL // 2
    hi = lax.bitcast_convert_type(val[:, :half].astype(BF16).astype(F32), jnp.uint32)
    lo = lax.bitcast_convert_type(val[:, half:].astype(BF16).astype(F32), jnp.uint32)
    words = hi | lax.shift_right_logical(lo, jnp.uint32(16))
    for j in range(TILE_ROWS):
        ref2d[pl.ds(row0 * TILE_ROWS + j, rows, stride=TILE_ROWS), :] = words[:, j * LANES:(j + 1) * LANES]


def _load_token_tiles(ref2d, rows):
    words = jnp.concatenate([ref2d[pl.ds(j, rows, stride=TILE_ROWS), :] for j in range(TILE_ROWS)], axis=1)
    hi = lax.bitcast_convert_type(words & jnp.uint32(0xFFFF0000), F32)
    lo = lax.bitcast_convert_type(lax.shift_left(words, jnp.uint32(16)), F32)
    return jnp.concatenate([hi, lo], axis=1)


def _in_proj_kernel(x_ref, g_ref, w_ref, wdt_ref, proj_ref, dt_ref, h_sc):
    @pl.when(pl.program_id(1) == 0)
    def _():
        h = _rms(x_ref[...], g_ref[...])
        hi, lo = _split2(h)
        h_sc[...] = hi
        lhs = jnp.concatenate([hi, lo, hi], axis=1)
        dt_ref[...] = jnp.dot(lhs, wdt_ref[...], preferred_element_type=F32)

    proj_ref[...] = jnp.dot(h_sc[...], w_ref[...], preferred_element_type=F32).astype(BF16)


def _in_proj(x2d, g, w_main, wdt3, tm=1024, tn=2048):
    t = x2d.shape[0]
    return pl.pallas_call(
        _in_proj_kernel,
        grid=(t // tm, D_PROJ // tn),
        in_specs=[
            pl.BlockSpec((tm, D_MODEL), lambda i, j: (i, 0)),
            pl.BlockSpec((1, D_MODEL), lambda i, j: (0, 0)),
            pl.BlockSpec((D_MODEL, tn), lambda i, j: (0, j)),
            pl.BlockSpec((3 * D_MODEL, LANES), lambda i, j: (0, 0)),
        ],
        out_specs=[
            pl.BlockSpec((tm, tn), lambda i, j: (i, j)),
            pl.BlockSpec((tm, LANES), lambda i, j: (i, 0)),
        ],
        out_shape=[
            jax.ShapeDtypeStruct((t, D_PROJ), BF16),
            jax.ShapeDtypeStruct((t, LANES), F32),
        ],
        scratch_shapes=[pltpu.VMEM((tm, D_MODEL), BF16)],
        compiler_params=pltpu.CompilerParams(
            dimension_semantics=("arbitrary", "arbitrary"), vmem_limit_bytes=VMEM_LIMIT),
        name="in_proj",
    )(x2d, g, w_main, wdt3)


def _mixers_kernel(z_ref, xs_ref, bc_ref, u_ref, dt_ref,
                   cw_ref, cb_ref, dtb_ref, alog_ref, dskip_ref, ng_ref, ltri_ref, e2_ref,
                   shift_ref, band_ref, pw_ref, ps_ref,
                   yssd_ref, ypool_ref,
                   ext_sc, extu_sc, state_sc):
    c = pl.program_id(1)
    rows = MIX_CHUNKS * CHUNK

    @pl.when(c == 0)
    def _():
        ext_sc[0:HALO, :] = jnp.zeros((HALO, D_CONV), BF16)
        extu_sc[0:HALO, :] = jnp.zeros((HALO, POOL_WIDTH), BF16)
        state_sc[...] = jnp.zeros_like(state_sc)

    @pl.when(c > 0)
    def _():
        ext_sc[0:HALO, :] = ext_sc[rows:rows + HALO, :]
        extu_sc[0:HALO, :] = extu_sc[rows:rows + HALO, :]

    ext_sc[HALO:HALO + rows, 0:D_INNER] = xs_ref[...]
    ext_sc[HALO:HALO + rows, D_INNER:D_CONV] = bc_ref[...]
    extu_sc[HALO:HALO + rows, :] = u_ref[...]
    for ci in range(MIX_CHUNKS):
        _mixers_chunk(ci, c * MIX_CHUNKS + ci, z_ref, u_ref, dt_ref, cw_ref, cb_ref, dtb_ref, alog_ref, dskip_ref,
                      ng_ref, ltri_ref, e2_ref, shift_ref, band_ref, pw_ref, ps_ref, yssd_ref, ypool_ref,
                      ext_sc, extu_sc, state_sc)


def _mixers_chunk(ci, chunk_index, z_ref, u_ref, dt_ref, cw_ref, cb_ref, dtb_ref, alog_ref, dskip_ref,
                  ng_ref, ltri_ref, e2_ref, shift_ref, band_ref, pw_ref, ps_ref, yssd_ref, ypool_ref,
                  ext_sc, extu_sc, state_sc):
    r0 = ci * CHUNK
    rs = slice(r0, r0 + CHUNK)
    ext = ext_sc[r0:r0 + HALO + CHUNK, :]
    conv = cb_ref[...] + cw_ref[CONV_WIDTH - 1:CONV_WIDTH, :] * ext[HALO:HALO + CHUNK, :].astype(F32)
    for k in range(CONV_WIDTH - 1):
        conv = conv + cw_ref[k:k + 1, :] * jnp.dot(shift_ref[k], ext, preferred_element_type=F32)
    xc = conv * jax.nn.sigmoid(conv)
    xs = xc[:, 0:D_INNER]
    xs_b = xs.astype(BF16)

    dtv = jax.nn.softplus(dt_ref[rs, :] + dtb_ref[...])
    da = dtv * (-jnp.exp(alog_ref[...]))
    a_cum = jnp.dot(ltri_ref[...], jnp.concatenate(_split3(da), axis=0),
                    preferred_element_type=F32)
    expa = jnp.exp(a_cum)
    a_last = a_cum[CHUNK - 1:CHUNK, :]
    wst = dtv * jnp.exp(a_last - a_cum)
    a_cum_t = a_cum.T
    dt_t = dtv.T

    both = jnp.concatenate([wst, expa], axis=0)
    hi, lo = _split2(both)
    expd = jnp.dot(jnp.concatenate([hi, lo], axis=1), e2_ref[...],
                   preferred_element_type=F32)
    wst_x = expd[0:CHUNK, :]
    expa_x = expd[CHUNK:2 * CHUNK, :]
    xw_b = (xs * wst_x).astype(BF16)

    row = lax.broadcasted_iota(jnp.int32, (CHUNK, CHUNK), 0)
    col = lax.broadcasted_iota(jnp.int32, (CHUNK, CHUNK), 1)
    causal_bias = jnp.where(row >= col, 0.0, -jnp.inf).astype(F32)
    lane = lax.broadcasted_iota(jnp.int32, (CHUNK, LANES), 1)
    low_half = lane < HEAD_DIM

    y_groups = []
    for g in range(GROUPS):
        bg = xc[:, D_INNER + g * D_STATE:D_INNER + (g + 1) * D_STATE]
        cg = xc[:, D_INNER + GROUPS * D_STATE + g * D_STATE:D_INNER + GROUPS * D_STATE + (g + 1) * D_STATE]
        bg_b = bg.astype(BF16)
        cg_b = cg.astype(BF16)
        cbm = lax.dot_general(cg_b, bg_b, (((1,), (1,)), ((), ())), preferred_element_type=F32)
        gsl = slice(g * GROUP_DIM, (g + 1) * GROUP_DIM)

        prev_t = state_sc[g]
        y_off = jnp.dot(cg_b, prev_t.astype(BF16), preferred_element_type=F32) * expa_x[:, gsl]
        st_t = jnp.dot(bg.T.astype(BF16), xw_b[:, gsl], preferred_element_type=F32)
        state_sc[g] = prev_t * expa_x[CHUNK - 1:CHUNK, gsl] + st_t

        pairs = []
        for jp in range(HEADS_PER_GROUP // 2):
            ms = []
            for hh in range(2):
                h = g * HEADS_PER_GROUP + jp * 2 + hh
                seg = a_cum[:, h:h + 1] - a_cum_t[h:h + 1, :]
                dec = jnp.exp(seg + causal_bias)
                ms.append((cbm * dec * dt_t[h:h + 1, :]).astype(BF16))
            lhs = jnp.concatenate(ms, axis=1)
            c0 = g * GROUP_DIM + jp * LANES
            xp = xs_b[:, c0:c0 + LANES]
            zero = jnp.zeros_like(xp)
            rhs = jnp.concatenate([jnp.where(low_half, xp, zero), jnp.where(low_half, zero, xp)], axis=0)
            pairs.append(jnp.dot(lhs, rhs, preferred_element_type=F32))
        y_diag = jnp.concatenate(pairs, axis=1)

        yg = y_diag + y_off + dskip_ref[:, gsl] * xs[:, gsl]
        zg = z_ref[rs, gsl].astype(F32)
        yg = yg * (zg * jax.nn.sigmoid(zg))
        yg = yg * lax.rsqrt(jnp.mean(yg * yg, axis=-1, keepdims=True) + EPS) * ng_ref[:, gsl]
        y_groups.append(yg.astype(BF16))
    yssd_ref[rs, :] = jnp.concatenate(y_groups, axis=1)

    pos = chunk_index * CHUNK + lax.broadcasted_iota(jnp.int32, (CHUNK, 1), 0)
    outs = []
    for gi, w in enumerate(POOL_WINDOWS):
        psl = slice(gi * POOL_GROUP_DIM, (gi + 1) * POOL_GROUP_DIM)
        s = jnp.dot(band_ref[gi], extu_sc[r0:r0 + HALO + CHUNK, psl], preferred_element_type=F32)
        cnt = jnp.minimum(pos + 1, w).astype(F32)
        pooled = s / cnt - u_ref[rs, psl].astype(F32)
        outs.append(jnp.dot(pooled.astype(BF16), pw_ref[gi], preferred_element_type=F32))
    ypool_ref[rs, :] = (jnp.concatenate(outs, axis=1) * ps_ref[...]).astype(BF16)


def _mixers(proj, dt_raw, bsz, seq, cw, cb, dtb, alog, dskip, ng, ltri3, e2, pw, ps):
    rows = MIX_CHUNKS * CHUNK
    nc = seq // rows
    t = bsz * seq
    rowmap = lambda b, c: b * nc + c
    const2 = lambda b, c: (0, 0)
    const3 = lambda b, c: (0, 0, 0)
    trow = jnp.arange(CHUNK)[:, None] + HALO
    jcol = jnp.arange(HALO + CHUNK)[None, :]
    shifts = jnp.stack([(jcol == trow - (CONV_WIDTH - 1) + k) for k in range(CONV_WIDTH - 1)]).astype(BF16)
    bands = jnp.stack([(jcol <= trow) & (jcol > trow - w) for w in POOL_WINDOWS]).astype(BF16)
    return pl.pallas_call(
        _mixers_kernel,
        grid=(bsz, nc),
        in_specs=[
            pl.BlockSpec((rows, D_INNER), lambda b, c: (rowmap(b, c), 0)),
            pl.BlockSpec((rows, D_INNER), lambda b, c: (rowmap(b, c), 1)),
            pl.BlockSpec((rows, D_BC), lambda b, c: (rowmap(b, c), 4)),
            pl.BlockSpec((rows, POOL_WIDTH), lambda b, c: (rowmap(b, c), 5)),
            pl.BlockSpec((rows, LANES), lambda b, c: (rowmap(b, c), 0)),
            pl.BlockSpec((CONV_WIDTH, D_CONV), const2),
            pl.BlockSpec((1, D_CONV), const2),
            pl.BlockSpec((1, LANES), const2),
            pl.BlockSpec((1, LANES), const2),
            pl.BlockSpec((1, D_INNER), const2),
            pl.BlockSpec((1, D_INNER), const2),
            pl.BlockSpec((CHUNK, 3 * CHUNK), const2),
            pl.BlockSpec((2 * LANES, D_INNER), const2),
            pl.BlockSpec((CONV_WIDTH - 1, CHUNK, HALO + CHUNK), const3),
            pl.BlockSpec((len(POOL_WINDOWS), CHUNK, HALO + CHUNK), const3),
            pl.BlockSpec((len(POOL_WINDOWS), POOL_GROUP_DIM, POOL_GROUP_DIM), const3),
            pl.BlockSpec((1, POOL_WIDTH), const2),
        ],
        out_specs=[
            pl.BlockSpec((rows, D_INNER), lambda b, c: (rowmap(b, c), 0)),
            pl.BlockSpec((rows, POOL_WIDTH), lambda b, c: (rowmap(b, c), 0)),
        ],
        out_shape=[
            jax.ShapeDtypeStruct((t, D_INNER), BF16),
            jax.ShapeDtypeStruct((t, POOL_WIDTH), BF16),
        ],
        scratch_shapes=[
            pltpu.VMEM((HALO + rows, D_CONV), BF16),
            pltpu.VMEM((HALO + rows, POOL_WIDTH), BF16),
            pltpu.VMEM((GROUPS, D_STATE, GROUP_DIM), F32),
        ],
        compiler_params=pltpu.CompilerParams(
            dimension_semantics=("arbitrary", "arbitrary"), vmem_limit_bytes=VMEM_LIMIT),
        name="mixers",
    )(proj, proj, proj, proj, dt_raw, cw, cb, dtb, alog, dskip, ng, ltri3, e2, shifts, bands, pw, ps)


def _mix_route_kernel(x_ref, yssd_ref, ypool_ref, gates_ref, wso_ref, wmo_ref, fg_ref, wr_ref, br_ref,
                      x1_ref, h2_ref, topw_ref, topi_ref):
    for r0 in range(0, x_ref.shape[0], ROUTE_SUB):
        _mix_route_rows(slice(r0, r0 + ROUTE_SUB), x_ref, yssd_ref, ypool_ref, gates_ref, wso_ref, wmo_ref, fg_ref,
                        wr_ref, br_ref, x1_ref, h2_ref, topw_ref, topi_ref)


def _mix_route_rows(rs, x_ref, yssd_ref, ypool_ref, gates_ref, wso_ref, wmo_ref, fg_ref, wr_ref, br_ref,
                    x1_ref, h2_ref, topw_ref, topi_ref):
    y_ssd = jnp.dot(yssd_ref[rs, :], wso_ref[...], preferred_element_type=F32)
    gates = jax.nn.sigmoid(gates_ref[rs, :].astype(F32))
    mixed = gates[:, 0:D_MODEL] * y_ssd + gates[:, D_MODEL:2 * D_MODEL] * ypool_ref[rs, :].astype(F32)
    x1 = x_ref[rs, :] + jnp.dot(mixed.astype(BF16), wmo_ref[...], preferred_element_type=F32)
    x1_ref[rs, :] = x1
    h2 = _rms(x1, fg_ref[...])
    _store_token_tiles(h2_ref, h2, rs.start)

    hi, lo = _split2(h2)
    logits = jnp.dot(jnp.concatenate([hi, lo, hi], axis=1), wr_ref[...],
                     preferred_element_type=F32) + br_ref[...]
    tm = logits.shape[0]
    lane = lax.broadcasted_iota(jnp.int32, (tm, LANES), 1)
    neg = jnp.float32(-jnp.inf)
    work = jnp.where(lane < N_EXPERTS, logits, neg)
    vals = []
    idxs = []
    for _ in range(TOP_K):
        m = jnp.max(work, axis=-1, keepdims=True)
        idx = jnp.min(jnp.where(work == m, lane, LANES), axis=-1, keepdims=True)
        vals.append(m)
        idxs.append(idx)
        work = jnp.where(lane == idx, neg, work)
    es = [jnp.exp(v - vals[0]) for v in vals]
    den = es[0] + es[1] + es[2] + es[3]
    topw = jnp.zeros((tm, LANES), F32)
    topi = jnp.zeros((tm, LANES), jnp.int32)
    for k in range(TOP_K):
        topw = jnp.where(lane == k, es[k] / den, topw)
        topi = jnp.where(lane == k, idxs[k], topi)
    topw_ref[rs, :] = topw
    topi_ref[:, rs] = topi.T[0:TOPI_ROWS, :]


def _mix_route(x2d, yssd, ypool, proj, wso, wmo, fg, wr3, br, tm=512):
    t = x2d.shape[0]
    const = lambda i: (0, 0)
    return pl.pallas_call(
        _mix_route_kernel,
        grid=(t // tm,),
        in_specs=[
            pl.BlockSpec((tm, D_MODEL), lambda i: (i, 0)),
            pl.BlockSpec((tm, D_INNER), lambda i: (i, 0)),
            pl.BlockSpec((tm, POOL_WIDTH), lambda i: (i, 0)),
            pl.BlockSpec((tm, 2 * D_MODEL), lambda i: (i, 3)),
            pl.BlockSpec((D_INNER, D_MODEL), const),
            pl.BlockSpec((D_MODEL, D_MODEL), const),
            pl.BlockSpec((1, D_MODEL), const),
            pl.BlockSpec((3 * D_MODEL, LANES), const),
            pl.BlockSpec((1, LANES), const),
        ],
        out_specs=[
            pl.BlockSpec((tm, D_MODEL), lambda i: (i, 0)),
            pl.BlockSpec((tm * TILE_ROWS, LANES), lambda i: (i, 0)),
            pl.BlockSpec((tm, LANES), lambda i: (i, 0)),
            pl.BlockSpec((TOPI_ROWS, tm), lambda i: (0, i)),
        ],
        out_shape=[
            jax.ShapeDtypeStruct((t, D_MODEL), F32),
            jax.ShapeDtypeStruct((t * TILE_ROWS, LANES), jnp.uint32),
            jax.ShapeDtypeStruct((t, LANES), F32),
            jax.ShapeDtypeStruct((TOPI_ROWS, t), jnp.int32),
        ],
        compiler_params=pltpu.CompilerParams(
            dimension_semantics=("arbitrary",), vmem_limit_bytes=VMEM_LIMIT),
        name="mix_route",
    )(x2d, yssd, ypool, proj, wso, wmo, fg, wr3, br)


def _moe_kernel(be_ref, nvalid_ref,
                tok0_ref, toknext_ref, dstprev_ref,
                h_hbm, wgu_ref, bgu_ref, wd_ref, bd_ref,
                slots_hbm,
                xbuf, ybuf, xb_sc, wgu_bf, wd_bf, gsem, ssem, *, nb):
    i = pl.program_id(0)
    nvalid = nvalid_ref[0]
    slot = lax.rem(i, 2)
    has_next = i + 1 < nvalid

    def tile(off, n=1):
        return pl.ds(pl.multiple_of(off, TILE_ROWS), n * TILE_ROWS)

    def gather_copy(tok, r, s):
        return pltpu.make_async_copy(h_hbm.at[tile(tok)], xbuf.at[s, tile(r * TILE_ROWS)], gsem.at[s])

    def scatter_copy(dst, r, s):
        return pltpu.make_async_copy(ybuf.at[s, tile(r * TILE_ROWS)], slots_hbm.at[tile(dst)], ssem.at[s])

    def wait_gather(s):
        pltpu.make_async_copy(h_hbm.at[tile(0, MOE_ROWS)], xbuf.at[s, tile(0, MOE_ROWS)], gsem.at[s]).wait()

    def wait_scatter(s):
        pltpu.make_async_copy(ybuf.at[s, tile(0, MOE_ROWS)], slots_hbm.at[tile(0, MOE_ROWS)], ssem.at[s]).wait()

    def scatter_whole_block(s):
        pltpu.make_async_copy(ybuf.at[s, tile(0, MOE_ROWS)], slots_hbm.at[tile(dstprev_ref[0, 0, 0], MOE_ROWS)],
                              ssem.at[s]).start()

    def issue(s, gather_ref, scatter_ref):
        for r in range(MOE_ROWS):
            if gather_ref is not None:
                gather_copy(gather_ref[0, 0, r], r, s).start()
            if scatter_ref is not None:
                scatter_copy(scatter_ref[0, 0, r], r, s).start()

    @pl.when(i == 0)
    def _():
        ybuf[...] = jnp.zeros_like(ybuf)
        issue(0, tok0_ref, None)

    for par in (0, 1):
        other = 1 - par
        mine = slot == par

        pl.when(jnp.logical_and(mine, i >= 2))(functools.partial(wait_scatter, par))
        pl.when(jnp.logical_and(mine, i < nvalid))(functools.partial(wait_gather, par))
        pl.when(jnp.logical_and(mine, jnp.logical_and(has_next, i == 0)))(
            functools.partial(issue, other, toknext_ref, None))
        pl.when(jnp.logical_and(mine, jnp.logical_and(has_next, i >= 1)))(
            functools.partial(issue, other, toknext_ref, dstprev_ref))
        pl.when(jnp.logical_and(mine, jnp.logical_and(jnp.logical_not(has_next), jnp.logical_and(i >= 1, i <= nvalid))))(
            functools.partial(issue, other, None, dstprev_ref))
        pl.when(jnp.logical_and(mine, i > nvalid))(functools.partial(scatter_whole_block, other))
        pl.when(jnp.logical_and(mine, i == nb))(functools.partial(wait_scatter, other))

    @pl.when(i < nvalid)
    def _():
        @pl.when(jnp.logical_or(i == 0, be_ref[i] != be_ref[jnp.maximum(i - 1, 0)]))
        def _():
            wgu_bf[...] = wgu_ref[0].astype(BF16)
            wd_bf[...] = wd_ref[0].astype(BF16)

        xb_sc[...] = _load_token_tiles(xbuf.at[slot], MOE_ROWS).astype(BF16)
        gu = jnp.dot(xb_sc[...], wgu_bf[...], preferred_element_type=F32) + bgu_ref[0]
        gate = jnp.minimum(gu[:, 0:D_EXPERT], SWIGLU_LIMIT)
        up = jnp.clip(gu[:, D_EXPERT:2 * D_EXPERT], -SWIGLU_LIMIT, SWIGLU_LIMIT)
        act = (up + 1.0) * gate * jax.nn.sigmoid(SWIGLU_ALPHA * gate)
        y = jnp.dot(act.astype(BF16), wd_bf[...], preferred_element_type=F32) + bd_ref[0]
        _store_token_tiles(ybuf.at[slot], y)


def _moe(block_expert, nvalid, row_token3, row_dest3, h2, wgu, bgu3, wd, bd3, n_slot_rows):
    nb = block_expert.shape[0]
    expert_of = lambda i, be: be[jnp.minimum(i, nb - 1)]
    grid_spec = pltpu.PrefetchScalarGridSpec(
        num_scalar_prefetch=2,
        grid=(nb + 1,),
        in_specs=[
            pl.BlockSpec((1, 1, MOE_ROWS), lambda i, be, nv: (0, 0, 0), memory_space=pltpu.SMEM),
            pl.BlockSpec((1, 1, MOE_ROWS), lambda i, be, nv: (jnp.minimum(i + 1, nb - 1), 0, 0),
                         memory_space=pltpu.SMEM),
            pl.BlockSpec((1, 1, MOE_ROWS), lambda i, be, nv: (jnp.clip(i - 1, 0, nb - 1), 0, 0),
                         memory_space=pltpu.SMEM),
            pl.BlockSpec(memory_space=pl.ANY),
            pl.BlockSpec((1, D_MODEL, 2 * D_EXPERT), lambda i, be, nv: (expert_of(i, be), 0, 0)),
            pl.BlockSpec((1, 1, 2 * D_EXPERT), lambda i, be, nv: (expert_of(i, be), 0, 0)),
            pl.BlockSpec((1, D_EXPERT, D_MODEL), lambda i, be, nv: (expert_of(i, be), 0, 0)),
            pl.BlockSpec((1, 1, D_MODEL), lambda i, be, nv: (expert_of(i, be), 0, 0)),
        ],
        out_specs=pl.BlockSpec(memory_space=pl.ANY),
        scratch_shapes=[
            pltpu.VMEM((2, MOE_ROWS * TILE_ROWS, LANES), jnp.uint32),
            pltpu.VMEM((2, MOE_ROWS * TILE_ROWS, LANES), jnp.uint32),
            pltpu.VMEM((MOE_ROWS, D_MODEL), BF16),
            pltpu.VMEM((D_MODEL, 2 * D_EXPERT), BF16),
            pltpu.VMEM((D_EXPERT, D_MODEL), BF16),
            pltpu.SemaphoreType.DMA((2,)),
            pltpu.SemaphoreType.DMA((2,)),
        ],
    )
    return pl.pallas_call(
        functools.partial(_moe_kernel, nb=nb),
        grid_spec=grid_spec,
        out_shape=jax.ShapeDtypeStruct((n_slot_rows * TILE_ROWS, LANES), jnp.uint32),
        compiler_params=pltpu.CompilerParams(
            dimension_semantics=("arbitrary",), vmem_limit_bytes=VMEM_LIMIT),
        name="moe_experts",
    )(block_expert, nvalid, row_token3, row_token3, row_dest3, h2, wgu, bgu3, wd, bd3)


def _combine_kernel(x1_ref, s0_ref, s1_ref, s2_ref, s3_ref, topw_ref, p_ref, pg_ref, wpg_ref, wpp_ref, fg_ref,
                    out_ref):
    x2 = x1_ref[...]
    topw = topw_ref[...]
    for k, s_ref in enumerate((s0_ref, s1_ref, s2_ref, s3_ref)):
        x2 = x2 + _load_token_tiles(s_ref, x2.shape[0]) * topw[:, k:k + 1]
    n = _rms(x2, pg_ref[...])
    gate = jax.nn.sigmoid(jnp.dot(n.astype(BF16), wpg_ref[...], preferred_element_type=F32))
    pp = jnp.dot(p_ref[...].astype(BF16), wpp_ref[...], preferred_element_type=F32)
    x3 = x2 + gate * pp
    out_ref[...] = _rms(x3, fg_ref[...])


def _combine(x1, slots, topw, p2d, pg, wpg, wpp, fg, tm=512):
    t = x1.shape[0]
    nt = t // tm
    const = lambda i: (0, 0)
    slot_specs = [pl.BlockSpec((tm * TILE_ROWS, LANES), functools.partial(lambda k, i: (k * nt + i, 0), k))
                  for k in range(TOP_K)]
    return pl.pallas_call(
        _combine_kernel,
        grid=(nt,),
        in_specs=[
            pl.BlockSpec((tm, D_MODEL), lambda i: (i, 0)),
            *slot_specs,
            pl.BlockSpec((tm, LANES), lambda i: (i, 0)),
            pl.BlockSpec((tm, D_PLE), lambda i: (i, 0)),
            pl.BlockSpec((1, D_MODEL), const),
            pl.BlockSpec((D_MODEL, D_MODEL), const),
            pl.BlockSpec((D_PLE, D_MODEL), const),
            pl.BlockSpec((1, D_MODEL), const),
        ],
        out_specs=pl.BlockSpec((tm, D_MODEL), lambda i: (i, 0)),
        out_shape=jax.ShapeDtypeStruct((t, D_MODEL), F32),
        compiler_params=pltpu.CompilerParams(
            dimension_semantics=("arbitrary",), vmem_limit_bytes=VMEM_LIMIT),
        name="combine_ple",
    )(x1, slots, slots, slots, slots, topw, p2d, pg, wpg, wpp, fg)


def _routing_tables(top_idx_t, n_tok):
    n_assign = n_tok * TOP_K
    expert_flat = top_idx_t.reshape(-1)
    order = jnp.argsort(expert_flat, stable=True).astype(jnp.int32)
    counts = jnp.bincount(expert_flat, length=N_EXPERTS).astype(jnp.int32)
    start = jnp.cumsum(counts) - counts
    padded = (counts + MOE_ROWS - 1) // MOE_ROWS * MOE_ROWS
    pend = jnp.cumsum(padded)
    pstart = pend - padded
    n_rows = n_assign + N_EXPERTS * MOE_ROWS
    n_blocks = n_rows // MOE_ROWS
    block_start = jnp.arange(n_blocks, dtype=jnp.int32) * MOE_ROWS
    block_expert = jnp.minimum(jnp.sum(block_start[:, None] >= pend[None, :], axis=1),
                               N_EXPERTS - 1).astype(jnp.int32)
    nvalid = (pend[-1] // MOE_ROWS).astype(jnp.int32).reshape(1)
    is_e = block_expert[:, None] == jnp.arange(N_EXPERTS, dtype=jnp.int32)[None, :]
    per_block = lambda v: jnp.sum(jnp.where(is_e, v[None, :], 0), axis=1)
    nreal = jnp.clip(per_block(pstart + counts) - block_start, 0, MOE_ROWS).astype(jnp.int32)
    sorted_pos = (block_start + per_block(start - pstart))[:, None] + jnp.arange(MOE_ROWS, dtype=jnp.int32)[None, :]
    assign = order[jnp.clip(sorted_pos, 0, n_assign - 1)]
    row_token = assign % n_tok
    is_real = jnp.arange(MOE_ROWS, dtype=jnp.int32)[None, :] < nreal[:, None]
    row_q = block_start[:, None] + jnp.arange(MOE_ROWS, dtype=jnp.int32)[None, :]
    spare = n_assign + row_q - per_block(start + counts)[:, None]
    row_dest = jnp.where(is_real, assign, spare)
    return (block_expert, nvalid, (row_token * TILE_ROWS).reshape(n_blocks, 1, MOE_ROWS),
            (row_dest * TILE_ROWS).reshape(n_blocks, 1, MOE_ROWS), n_rows)


def _layer(x2d, p2d, bsz, seq, mix_norm_g, w_in, conv_w, conv_b, dt_bias, a_log, d_skip, ssd_norm_g,
           w_ssd_out, pool_w, pool_scale, w_mix_out, ffn_norm_g, w_router, b_router,
           w_gate_up, b_gate_up, w_down, b_down, ple_norm_g, w_ple_gate, w_ple_proj, out_g):
    n_tok = x2d.shape[0]
    dt0 = D_INNER + D_CONV
    w_main = jnp.concatenate([w_in[:, :dt0], w_in[:, dt0 + HEADS:]], axis=1).astype(BF16)
    w_dt = jnp.pad(w_in[:, dt0:dt0 + HEADS], ((0, 0), (0, LANES - HEADS)))
    wdt_hi, wdt_lo = _split2(w_dt)
    wdt3 = jnp.concatenate([wdt_hi, wdt_hi, wdt_lo], axis=0)

    proj, dt_raw = _in_proj(x2d, mix_norm_g[None, :], w_main, wdt3)

    pad_h = lambda v: jnp.pad(v, (0, LANES - HEADS))[None, :]
    ltri = (jnp.arange(CHUNK)[:, None] >= jnp.arange(CHUNK)[None, :]).astype(BF16)
    ltri3 = jnp.concatenate([ltri, ltri, ltri], axis=1)
    e1 = (jnp.arange(LANES)[:, None] == (jnp.arange(D_INNER) // HEAD_DIM)[None, :]).astype(BF16)
    e2 = jnp.concatenate([e1, e1], axis=0)
    yssd, ypool = _mixers(
        proj, dt_raw, bsz, seq, conv_w, conv_b[None, :], pad_h(dt_bias), pad_h(a_log),
        jnp.repeat(d_skip, HEAD_DIM)[None, :], ssd_norm_g[None, :], ltri3, e2,
        pool_w.astype(BF16), pool_scale[None, :])

    wr = jnp.pad(w_router, ((0, 0), (0, LANES - N_EXPERTS)))
    wr_hi, wr_lo = _split2(wr)
    wr3 = jnp.concatenate([wr_hi, wr_hi, wr_lo], axis=0)
    br = jnp.pad(b_router, (0, LANES - N_EXPERTS))[None, :]
    x1, h2, topw, topi = _mix_route(x2d, yssd, ypool, proj, w_ssd_out.astype(BF16),
                                    w_mix_out.astype(BF16), ffn_norm_g[None, :], wr3, br)

    block_expert, nvalid, row_token3, row_dest3, n_slot_rows = _routing_tables(topi[:TOP_K], n_tok)
    slots = _moe(block_expert, nvalid, row_token3, row_dest3, h2, w_gate_up, b_gate_up[:, None, :],
                 w_down, b_down[:, None, :], n_slot_rows)

    return _combine(x1, slots, topw, p2d, ple_norm_g[None, :], w_ple_gate.astype(BF16),
                    w_ple_proj.astype(BF16), out_g[None, :])


def kernel(x, p, mix_norm_g, w_in, conv_w, conv_b, dt_bias, a_log, d_skip, ssd_norm_g, w_ssd_out, pool_w,
           pool_scale, w_mix_out, ffn_norm_g, w_router, b_router, w_gate_up, b_gate_up, w_down, b_down,
           ple_norm_g, w_ple_gate, w_ple_proj, final_norm_g):
    bsz, seq, d = x.shape
    depth = p.shape[0]
    assert depth == 1 and d == D_MODEL and seq % CHUNK == 0
    x2d = x.reshape(bsz * seq, d)
    out = _layer(x2d, p[0].reshape(bsz * seq, D_PLE), bsz, seq, mix_norm_g[0], w_in[0], conv_w[0], conv_b[0],
                 dt_bias[0], a_log[0], d_skip[0], ssd_norm_g[0], w_ssd_out[0], pool_w[0], pool_scale[0],
                 w_mix_out[0], ffn_norm_g[0], w_router[0], b_router[0], w_gate_up[0], b_gate_up[0],
                 w_down[0], b_down[0], ple_norm_g[0], w_ple_gate[0], w_ple_proj[0], final_norm_g)
    return out.reshape(bsz, seq, d)
```

```python
import functools

import jax
import jax.numpy as jnp
from jax import lax
from jax.experimental import pallas as pl
from jax.experimental.pallas import tpu as pltpu

F32 = jnp.float32
BF16 = jnp.bfloat16

D_MODEL = 1024
D_INNER = 2048
HEAD_DIM = 64
HEADS = 32
GROUPS = 4
HEADS_PER_GROUP = HEADS // GROUPS
GROUP_DIM = D_INNER // GROUPS
D_STATE = 128
CONV_WIDTH = 4
CHUNK = 128
D_BC = 2 * GROUPS * D_STATE
D_CONV = D_INNER + D_BC
POOL_WIDTH = D_MODEL
POOL_WINDOWS = (2, 4, 8, 16)
POOL_GROUP_DIM = POOL_WIDTH // len(POOL_WINDOWS)
N_EXPERTS = 32
TOP_K = 4
D_EXPERT = D_MODEL
SWIGLU_LIMIT = 7.0
SWIGLU_ALPHA = 1.702
D_PLE = 256
EPS = 1e-6

LANES = 128
TILE_ROWS = D_MODEL // (2 * LANES)
TOPI_ROWS = 8
ROUTE_SUB = 256
MIX_CHUNKS = 4
HALO = 16
D_PROJ = D_INNER + D_CONV + POOL_WIDTH + 2 * D_MODEL
MOE_ROWS = 256
VMEM_LIMIT = 56 * 1024 * 1024


def _split2(v):
    hi = v.astype(BF16)
    lo = (v - hi.astype(F32)).astype(BF16)
    return hi, lo


def _split3(v):
    hi = v.astype(BF16)
    r = v - hi.astype(F32)
    mid = r.astype(BF16)
    lo = (r - mid.astype(F32)).astype(BF16)
    return hi, mid, lo


def _hilo_weight(w):
    n = w.shape[1]
    hi, lo = _split2(w)
    top = jnp.pad(jnp.concatenate([hi, lo], axis=1), ((0, 0), (0, LANES - 2 * n)))
    bottom = jnp.pad(hi, ((0, 0), (0, LANES - n)))
    return jnp.concatenate([top, bottom], axis=0)


def _hilo_dot(hi, lo, w2_ref, n):
    k = hi.shape[1]
    a = jnp.dot(hi, w2_ref[0:k, :], preferred_element_type=F32)
    b = jnp.dot(lo, w2_ref[k:2 * k, :], preferred_element_type=F32)
    return a + pltpu.roll(a, LANES - n, axis=1) + b


def _rms(x, g):
    return x * lax.rsqrt(jnp.mean(x * x, axis=-1, keepdims=True) + EPS) * g


def _store_token_tiles(ref2d, val, row0=0):
    rows = val.shape[0]
    half = D_MODEL // 2
    hi = lax.bitcast_convert_type(val[:, :half].astype(BF16).astype(F32), jnp.uint32)
    lo = lax.bitcast_convert_type(val[:, half:].astype(BF16).astype(F32), jnp.uint32)
    words = hi | lax.shift_right_logical(lo, jnp.uint32(16))
    for j in range(TILE_ROWS):
        ref2d[pl.ds(row0 * TILE_ROWS + j, rows, stride=TILE_ROWS), :] = words[:, j * LANES:(j + 1) * LANES]


def _load_token_tiles(ref2d, rows):
    words = jnp.concatenate([ref2d[pl.ds(j, rows, stride=TILE_ROWS), :] for j in range(TILE_ROWS)], axis=1)
    hi = lax.bitcast_convert_type(words & jnp.uint32(0xFFFF0000), F32)
    lo = lax.bitcast_convert_type(lax.shift_left(words, jnp.uint32(16)), F32)
    return jnp.concatenate([hi, lo], axis=1)


def _in_proj_kernel(x_ref, g_ref, w_ref, wdt_ref, proj_ref, dt_ref, h_sc):
    @pl.when(pl.program_id(1) == 0)
    def _():
        h = _rms(x_ref[...], g_ref[...])
        hi, lo = _split2(h)
        h_sc[...] = hi
        dt_ref[...] = _hilo_dot(hi, lo, wdt_ref, HEADS)

    proj_ref[...] = jnp.dot(h_sc[...], w_ref[...], preferred_element_type=F32).astype(BF16)


def _in_proj(x2d, g, w_main, wdt2, tm=1024, tn=2048):
    t = x2d.shape[0]
    return pl.pallas_call(
        _in_proj_kernel,
        grid=(t // tm, D_PROJ // tn),
        in_specs=[
            pl.BlockSpec((tm, D_MODEL), lambda i, j: (i, 0)),
            pl.BlockSpec((1, D_MODEL), lambda i, j: (0, 0)),
            pl.BlockSpec((D_MODEL, tn), lambda i, j: (0, j)),
            pl.BlockSpec((2 * D_MODEL, LANES), lambda i, j: (0, 0)),
        ],
        out_specs=[
            pl.BlockSpec((tm, tn), lambda i, j: (i, j)),
            pl.BlockSpec((tm, LANES), lambda i, j: (i, 0)),
        ],
        out_shape=[
            jax.ShapeDtypeStruct((t, D_PROJ), BF16),
            jax.ShapeDtypeStruct((t, LANES), F32),
        ],
        scratch_shapes=[pltpu.VMEM((tm, D_MODEL), BF16)],
        compiler_params=pltpu.CompilerParams(
            dimension_semantics=("arbitrary", "arbitrary"), vmem_limit_bytes=VMEM_LIMIT),
        name="in_proj",
    )(x2d, g, w_main, wdt2)


def _mixers_kernel(z_ref, xs_ref, bc_ref, u_ref, dt_ref,
                   cw_ref, cb_ref, dtb_ref, alog_ref, dskip_ref, ng_ref, ltri_ref, e2_ref,
                   shift_ref, band_ref, pw_ref, ps_ref,
                   yssd_ref, ypool_ref,
                   ext_sc, extu_sc, state_sc):
    c = pl.program_id(1)
    rows = MIX_CHUNKS * CHUNK

    @pl.when(c == 0)
    def _():
        ext_sc[0:HALO, :] = jnp.zeros((HALO, D_CONV), BF16)
        extu_sc[0:HALO, :] = jnp.zeros((HALO, POOL_WIDTH), BF16)
        state_sc[...] = jnp.zeros_like(state_sc)

    @pl.when(c > 0)
    def _():
        ext_sc[0:HALO, :] = ext_sc[rows:rows + HALO, :]
        extu_sc[0:HALO, :] = extu_sc[rows:rows + HALO, :]

    ext_sc[HALO:HALO + rows, 0:D_INNER] = xs_ref[...]
    ext_sc[HALO:HALO + rows, D_INNER:D_CONV] = bc_ref[...]
    extu_sc[HALO:HALO + rows, :] = u_ref[...]
    for ci in range(MIX_CHUNKS):
        _mixers_chunk(ci, c * MIX_CHUNKS + ci, z_ref, u_ref, dt_ref, cw_ref, cb_ref, dtb_ref, alog_ref, dskip_ref,
                      ng_ref, ltri_ref, e2_ref, shift_ref, band_ref, pw_ref, ps_ref, yssd_ref, ypool_ref,
                      ext_sc, extu_sc, state_sc)


def _mixers_chunk(ci, chunk_index, z_ref, u_ref, dt_ref, cw_ref, cb_ref, dtb_ref, alog_ref, dskip_ref,
                  ng_ref, ltri_ref, e2_ref, shift_ref, band_ref, pw_ref, ps_ref, yssd_ref, ypool_ref,
                  ext_sc, extu_sc, state_sc):
    r0 = ci * CHUNK
    rs = slice(r0, r0 + CHUNK)
    ext = ext_sc[r0:r0 + HALO + CHUNK, :]
    conv = cb_ref[...] + cw_ref[CONV_WIDTH - 1:CONV_WIDTH, :] * ext[HALO:HALO + CHUNK, :].astype(F32)
    for k in range(CONV_WIDTH - 1):
        conv = conv + cw_ref[k:k + 1, :] * jnp.dot(shift_ref[k], ext, preferred_element_type=F32)
    xc = conv * jax.nn.sigmoid(conv)
    xs = xc[:, 0:D_INNER]
    xs_b = xs.astype(BF16)

    dtv = jax.nn.softplus(dt_ref[rs, :] + dtb_ref[...])
    da = dtv * (-jnp.exp(alog_ref[...]))
    a_cum = jnp.dot(ltri_ref[...], jnp.concatenate(_split3(da), axis=0),
                    preferred_element_type=F32)
    expa = jnp.exp(a_cum)
    a_last = a_cum[CHUNK - 1:CHUNK, :]
    wst = dtv * jnp.exp(a_last - a_cum)
    a_cum_t = a_cum.T
    dt_t = dtv.T

    both = jnp.concatenate([wst, expa], axis=0)
    hi, lo = _split2(both)
    expd = jnp.dot(jnp.concatenate([hi, lo], axis=1), e2_ref[...],
                   preferred_element_type=F32)
    wst_x = expd[0:CHUNK, :]
    expa_x = expd[CHUNK:2 * CHUNK, :]
    xw_b = (xs * wst_x).astype(BF16)

    row = lax.broadcasted_iota(jnp.int32, (CHUNK, CHUNK), 0)
    col = lax.broadcasted_iota(jnp.int32, (CHUNK, CHUNK), 1)
    causal_bias = jnp.where(row >= col, 0.0, -jnp.inf).astype(F32)
    lane = lax.broadcasted_iota(jnp.int32, (CHUNK, LANES), 1)
    low_half = lane < HEAD_DIM

    y_groups = []
    for g in range(GROUPS):
        bg = xc[:, D_INNER + g * D_STATE:D_INNER + (g + 1) * D_STATE]
        cg = xc[:, D_INNER + GROUPS * D_STATE + g * D_STATE:D_INNER + GROUPS * D_STATE + (g + 1) * D_STATE]
        bg_b = bg.astype(BF16)
        cg_b = cg.astype(BF16)
        cbm = lax.dot_general(cg_b, bg_b, (((1,), (1,)), ((), ())), preferred_element_type=F32)
        gsl = slice(g * GROUP_DIM, (g + 1) * GROUP_DIM)

        prev_t = state_sc[g]
        y_off = jnp.dot(cg_b, prev_t.astype(BF16), preferred_element_type=F32) * expa_x[:, gsl]
        st_t = jnp.dot(bg.T.astype(BF16), xw_b[:, gsl], preferred_element_type=F32)
        state_sc[g] = prev_t * expa_x[CHUNK - 1:CHUNK, gsl] + st_t

        pairs = []
        for jp in range(HEADS_PER_GROUP // 2):
            ms = []
            for hh in range(2):
                h = g * HEADS_PER_GROUP + jp * 2 + hh
                seg = a_cum[:, h:h + 1] - a_cum_t[h:h + 1, :]
                dec = jnp.exp(seg + causal_bias)
                ms.append((cbm * dec * dt_t[h:h + 1, :]).astype(BF16))
            lhs = jnp.concatenate(ms, axis=1)
            c0 = g * GROUP_DIM + jp * LANES
            xp = xs_b[:, c0:c0 + LANES]
            zero = jnp.zeros_like(xp)
            rhs = jnp.concatenate([jnp.where(low_half, xp, zero), jnp.where(low_half, zero, xp)], axis=0)
            pairs.append(jnp.dot(lhs, rhs, preferred_element_type=F32))
        y_diag = jnp.concatenate(pairs, axis=1)

        yg = y_diag + y_off + dskip_ref[:, gsl] * xs[:, gsl]
        zg = z_ref[rs, gsl].astype(F32)
        yg = yg * (zg * jax.nn.sigmoid(zg))
        yg = yg * lax.rsqrt(jnp.mean(yg * yg, axis=-1, keepdims=True) + EPS) * ng_ref[:, gsl]
        y_groups.append(yg.astype(BF16))
    yssd_ref[rs, :] = jnp.concatenate(y_groups, axis=1)

    pos = chunk_index * CHUNK + lax.broadcasted_iota(jnp.int32, (CHUNK, 1), 0)
    outs = []
    for gi, w in enumerate(POOL_WINDOWS):
        psl = slice(gi * POOL_GROUP_DIM, (gi + 1) * POOL_GROUP_DIM)
        s = jnp.dot(band_ref[gi], extu_sc[r0:r0 + HALO + CHUNK, psl], preferred_element_type=F32)
        cnt = jnp.minimum(pos + 1, w).astype(F32)
        pooled = s / cnt - u_ref[rs, psl].astype(F32)
        outs.append(jnp.dot(pooled.astype(BF16), pw_ref[gi], preferred_element_type=F32))
    ypool_ref[rs, :] = (jnp.concatenate(outs, axis=1) * ps_ref[...]).astype(BF16)


def _mixers(proj, dt_raw, bsz, seq, cw, cb, dtb, alog, dskip, ng, ltri3, e2, pw, ps):
    rows = MIX_CHUNKS * CHUNK
    nc = seq // rows
    t = bsz * seq
    rowmap = lambda b, c: b * nc + c
    const2 = lambda b, c: (0, 0)
    const3 = lambda b, c: (0, 0, 0)
    trow = jnp.arange(CHUNK)[:, None] + HALO
    jcol = jnp.arange(HALO + CHUNK)[None, :]
    shifts = jnp.stack([(jcol == trow - (CONV_WIDTH - 1) + k) for k in range(CONV_WIDTH - 1)]).astype(BF16)
    bands = jnp.stack([(jcol <= trow) & (jcol > trow - w) for w in POOL_WINDOWS]).astype(BF16)
    return pl.pallas_call(
        _mixers_kernel,
        grid=(bsz, nc),
        in_specs=[
            pl.BlockSpec((rows, D_INNER), lambda b, c: (rowmap(b, c), 0)),
            pl.BlockSpec((rows, D_INNER), lambda b, c: (rowmap(b, c), 1)),
            pl.BlockSpec((rows, D_BC), lambda b, c: (rowmap(b, c), 4)),
            pl.BlockSpec((rows, POOL_WIDTH), lambda b, c: (rowmap(b, c), 5)),
            pl.BlockSpec((rows, LANES), lambda b, c: (rowmap(b, c), 0)),
            pl.BlockSpec((CONV_WIDTH, D_CONV), const2),
            pl.BlockSpec((1, D_CONV), const2),
            pl.BlockSpec((1, LANES), const2),
            pl.BlockSpec((1, LANES), const2),
            pl.BlockSpec((1, D_INNER), const2),
            pl.BlockSpec((1, D_INNER), const2),
            pl.BlockSpec((CHUNK, 3 * CHUNK), const2),
            pl.BlockSpec((2 * LANES, D_INNER), const2),
            pl.BlockSpec((CONV_WIDTH - 1, CHUNK, HALO + CHUNK), const3),
            pl.BlockSpec((len(POOL_WINDOWS), CHUNK, HALO + CHUNK), const3),
            pl.BlockSpec((len(POOL_WINDOWS), POOL_GROUP_DIM, POOL_GROUP_DIM), const3),
            pl.BlockSpec((1, POOL_WIDTH), const2),
        ],
        out_specs=[
            pl.BlockSpec((rows, D_INNER), lambda b, c: (rowmap(b, c), 0)),
            pl.BlockSpec((rows, POOL_WIDTH), lambda b, c: (rowmap(b, c), 0)),
        ],
        out_shape=[
            jax.ShapeDtypeStruct((t, D_INNER), BF16),
            jax.ShapeDtypeStruct((t, POOL_WIDTH), BF16),
        ],
        scratch_shapes=[
            pltpu.VMEM((HALO + rows, D_CONV), BF16),
            pltpu.VMEM((HALO + rows, POOL_WIDTH), BF16),
            pltpu.VMEM((GROUPS, D_STATE, GROUP_DIM), F32),
        ],
        compiler_params=pltpu.CompilerParams(
            dimension_semantics=("arbitrary", "arbitrary"), vmem_limit_bytes=VMEM_LIMIT),
        name="mixers",
    )(proj, proj, proj, proj, dt_raw, cw, cb, dtb, alog, dskip, ng, ltri3, e2, shifts, bands, pw, ps)


def _mix_route_kernel(x_ref, yssd_ref, ypool_ref, gates_ref, wso_ref, wmo_ref, fg_ref, wr_ref, br_ref,
                      x1_ref, h2_ref, topw_ref, topi_ref):
    for r0 in range(0, x_ref.shape[0], ROUTE_SUB):
        _mix_route_rows(slice(r0, r0 + ROUTE_SUB), x_ref, yssd_ref, ypool_ref, gates_ref, wso_ref, wmo_ref, fg_ref,
                        wr_ref, br_ref, x1_ref, h2_ref, topw_ref, topi_ref)


def _mix_route_rows(rs, x_ref, yssd_ref, ypool_ref, gates_ref, wso_ref, wmo_ref, fg_ref, wr_ref, br_ref,
                    x1_ref, h2_ref, topw_ref, topi_ref):
    y_ssd = jnp.dot(yssd_ref[rs, :], wso_ref[...], preferred_element_type=F32)
    gates = jax.nn.sigmoid(gates_ref[rs, :].astype(F32))
    mixed = gates[:, 0:D_MODEL] * y_ssd + gates[:, D_MODEL:2 * D_MODEL] * ypool_ref[rs, :].astype(F32)
    x1 = x_ref[rs, :] + jnp.dot(mixed.astype(BF16), wmo_ref[...], preferred_element_type=F32)
    x1_ref[rs, :] = x1
    h2 = _rms(x1, fg_ref[...])
    _store_token_tiles(h2_ref, h2, rs.start)

    hi, lo = _split2(h2)
    logits = _hilo_dot(hi, lo, wr_ref, N_EXPERTS) + br_ref[...]
    tm = logits.shape[0]
    lane = lax.broadcasted_iota(jnp.int32, (tm, LANES), 1)
    neg = jnp.float32(-jnp.inf)
    work = jnp.where(lane < N_EXPERTS, logits, neg)
    vals = []
    idxs = []
    for _ in range(TOP_K):
        m = jnp.max(work, axis=-1, keepdims=True)
        idx = jnp.min(jnp.where(work == m, lane, LANES), axis=-1, keepdims=True)
        vals.append(m)
        idxs.append(idx)
        work = jnp.where(lane == idx, neg, work)
    es = [jnp.exp(v - vals[0]) for v in vals]
    den = es[0] + es[1] + es[2] + es[3]
    topw = jnp.zeros((tm, LANES), F32)
    topi = jnp.zeros((tm, LANES), jnp.int32)
    for k in range(TOP_K):
        topw = jnp.where(lane == k, es[k] / den, topw)
        topi = jnp.where(lane == k, idxs[k], topi)
    topw_ref[rs, :] = topw
    topi_ref[:, rs] = topi.T[0:TOPI_ROWS, :]


def _mix_route(x2d, yssd, ypool, proj, wso, wmo, fg, wr2, br, tm=512):
    t = x2d.shape[0]
    const = lambda i: (0, 0)
    return pl.pallas_call(
        _mix_route_kernel,
        grid=(t // tm,),
        in_specs=[
            pl.BlockSpec((tm, D_MODEL), lambda i: (i, 0)),
            pl.BlockSpec((tm, D_INNER), lambda i: (i, 0)),
            pl.BlockSpec((tm, POOL_WIDTH), lambda i: (i, 0)),
            pl.BlockSpec((tm, 2 * D_MODEL), lambda i: (i, 3)),
            pl.BlockSpec((D_INNER, D_MODEL), const),
            pl.BlockSpec((D_MODEL, D_MODEL), const),
            pl.BlockSpec((1, D_MODEL), const),
            pl.BlockSpec((2 * D_MODEL, LANES), const),
            pl.BlockSpec((1, LANES), const),
        ],
        out_specs=[
            pl.BlockSpec((tm, D_MODEL), lambda i: (i, 0)),
            pl.BlockSpec((tm * TILE_ROWS, LANES), lambda i: (i, 0)),
            pl.BlockSpec((tm, LANES), lambda i: (i, 0)),
            pl.BlockSpec((TOPI_ROWS, tm), lambda i: (0, i)),
        ],
        out_shape=[
            jax.ShapeDtypeStruct((t, D_MODEL), F32),
            jax.ShapeDtypeStruct((t * TILE_ROWS, LANES), jnp.uint32),
            jax.ShapeDtypeStruct((t, LANES), F32),
            jax.ShapeDtypeStruct((TOPI_ROWS, t), jnp.int32),
        ],
        compiler_params=pltpu.CompilerParams(
            dimension_semantics=("arbitrary",), vmem_limit_bytes=VMEM_LIMIT),
        name="mix_route",
    )(x2d, yssd, ypool, proj, wso, wmo, fg, wr2, br)


def _moe_kernel(be_ref, nvalid_ref,
                tok0_ref, toknext_ref, dstprev_ref,
                h_hbm, wgu_ref, bgu_ref, wd_ref, bd_ref,
                slots_hbm,
                xbuf, ybuf, xb_sc, wgu_bf, wd_bf, gsem, ssem, *, nb):
    i = pl.program_id(0)
    nvalid = nvalid_ref[0]
    slot = lax.rem(i, 2)
    has_next = i + 1 < nvalid

    def tile(off, n=1):
        return pl.ds(pl.multiple_of(off, TILE_ROWS), n * TILE_ROWS)

    def gather_copy(tok, r, s):
        return pltpu.make_async_copy(h_hbm.at[tile(tok)], xbuf.at[s, tile(r * TILE_ROWS)], gsem.at[s])

    def scatter_copy(dst, r, s):
        return pltpu.make_async_copy(ybuf.at[s, tile(r * TILE_ROWS)], slots_hbm.at[tile(dst)], ssem.at[s])

    def wait_gather(s):
        pltpu.make_async_copy(h_hbm.at[tile(0, MOE_ROWS)], xbuf.at[s, tile(0, MOE_ROWS)], gsem.at[s]).wait()

    def wait_scatter(s):
        pltpu.make_async_copy(ybuf.at[s, tile(0, MOE_ROWS)], slots_hbm.at[tile(0, MOE_ROWS)], ssem.at[s]).wait()

    def scatter_whole_block(s):
        pltpu.make_async_copy(ybuf.at[s, tile(0, MOE_ROWS)], slots_hbm.at[tile(dstprev_ref[0, 0, 0], MOE_ROWS)],
                              ssem.at[s]).start()

    def issue(s, gather_ref, scatter_ref):
        for r in range(MOE_ROWS):
            if gather_ref is not None:
                gather_copy(gather_ref[0, 0, r], r, s).start()
            if scatter_ref is not None:
                scatter_copy(scatter_ref[0, 0, r], r, s).start()

    @pl.when(i == 0)
    def _():
        ybuf[...] = jnp.zeros_like(ybuf)
        issue(0, tok0_ref, None)

    for par in (0, 1):
        other = 1 - par
        mine = slot == par

        pl.when(jnp.logical_and(mine, i >= 2))(functools.partial(wait_scatter, par))
        pl.when(jnp.logical_and(mine, i < nvalid))(functools.partial(wait_gather, par))
        pl.when(jnp.logical_and(mine, jnp.logical_and(has_next, i == 0)))(
            functools.partial(issue, other, toknext_ref, None))
        pl.when(jnp.logical_and(mine, jnp.logical_and(has_next, i >= 1)))(
            functools.partial(issue, other, toknext_ref, dstprev_ref))
        pl.when(jnp.logical_and(mine, jnp.logical_and(jnp.logical_not(has_next), jnp.logical_and(i >= 1, i <= nvalid))))(
            functools.partial(issue, other, None, dstprev_ref))
        pl.when(jnp.logical_and(mine, i > nvalid))(functools.partial(scatter_whole_block, other))
        pl.when(jnp.logical_and(mine, i == nb))(functools.partial(wait_scatter, other))

    @pl.when(i < nvalid)
    def _():
        @pl.when(jnp.logical_or(i == 0, be_ref[i] != be_ref[jnp.maximum(i - 1, 0)]))
        def _():
            wgu_bf[...] = wgu_ref[0].astype(BF16)
            wd_bf[...] = wd_ref[0].astype(BF16)

        xb_sc[...] = _load_token_tiles(xbuf.at[slot], MOE_ROWS).astype(BF16)
        gu = jnp.dot(xb_sc[...], wgu_bf[...], preferred_element_type=F32) + bgu_ref[0]
        gate = jnp.minimum(gu[:, 0:D_EXPERT], SWIGLU_LIMIT)
        up = jnp.clip(gu[:, D_EXPERT:2 * D_EXPERT], -SWIGLU_LIMIT, SWIGLU_LIMIT)
        act = (up + 1.0) * gate * jax.nn.sigmoid(SWIGLU_ALPHA * gate)
        y = jnp.dot(act.astype(BF16), wd_bf[...], preferred_element_type=F32) + bd_ref[0]
        _store_token_tiles(ybuf.at[slot], y)


def _moe(block_expert, nvalid, row_token3, row_dest3, h2, wgu, bgu3, wd, bd3, n_slot_rows):
    nb = block_expert.shape[0]
    expert_of = lambda i, be: be[jnp.minimum(i, nb - 1)]
    grid_spec = pltpu.PrefetchScalarGridSpec(
        num_scalar_prefetch=2,
        grid=(nb + 1,),
        in_specs=[
            pl.BlockSpec((1, 1, MOE_ROWS), lambda i, be, nv: (0, 0, 0), memory_space=pltpu.SMEM),
            pl.BlockSpec((1, 1, MOE_ROWS), lambda i, be, nv: (jnp.minimum(i + 1, nb - 1), 0, 0),
                         memory_space=pltpu.SMEM),
            pl.BlockSpec((1, 1, MOE_ROWS), lambda i, be, nv: (jnp.clip(i - 1, 0, nb - 1), 0, 0),
                         memory_space=pltpu.SMEM),
            pl.BlockSpec(memory_space=pl.ANY),
            pl.BlockSpec((1, D_MODEL, 2 * D_EXPERT), lambda i, be, nv: (expert_of(i, be), 0, 0)),
            pl.BlockSpec((1, 1, 2 * D_EXPERT), lambda i, be, nv: (expert_of(i, be), 0, 0)),
            pl.BlockSpec((1, D_EXPERT, D_MODEL), lambda i, be, nv: (expert_of(i, be), 0, 0)),
            pl.BlockSpec((1, 1, D_MODEL), lambda i, be, nv: (expert_of(i, be), 0, 0)),
        ],
        out_specs=pl.BlockSpec(memory_space=pl.ANY),
        scratch_shapes=[
            pltpu.VMEM((2, MOE_ROWS * TILE_ROWS, LANES), jnp.uint32),
            pltpu.VMEM((2, MOE_ROWS * TILE_ROWS, LANES), jnp.uint32),
            pltpu.VMEM((MOE_ROWS, D_MODEL), BF16),
            pltpu.VMEM((D_MODEL, 2 * D_EXPERT), BF16),
            pltpu.VMEM((D_EXPERT, D_MODEL), BF16),
            pltpu.SemaphoreType.DMA((2,)),
            pltpu.SemaphoreType.DMA((2,)),
        ],
    )
    return pl.pallas_call(
        functools.partial(_moe_kernel, nb=nb),
        grid_spec=grid_spec,
        out_shape=jax.ShapeDtypeStruct((n_slot_rows * TILE_ROWS, LANES), jnp.uint32),
        compiler_params=pltpu.CompilerParams(
            dimension_semantics=("arbitrary",), vmem_limit_bytes=VMEM_LIMIT),
        name="moe_experts",
    )(block_expert, nvalid, row_token3, row_token3, row_dest3, h2, wgu, bgu3, wd, bd3)


def _combine_kernel(x1_ref, s0_ref, s1_ref, s2_ref, s3_ref, topw_ref, p_ref, pg_ref, wpg_ref, wpp_ref, fg_ref,
                    out_ref):
    x2 = x1_ref[...]
    topw = topw_ref[...]
    for k, s_ref in enumerate((s0_ref, s1_ref, s2_ref, s3_ref)):
        x2 = x2 + _load_token_tiles(s_ref, x2.shape[0]) * topw[:, k:k + 1]
    n = _rms(x2, pg_ref[...])
    gate = jax.nn.sigmoid(jnp.dot(n.astype(BF16), wpg_ref[...], preferred_element_type=F32))
    pp = jnp.dot(p_ref[...].astype(BF16), wpp_ref[...], preferred_element_type=F32)
    x3 = x2 + gate * pp
    out_ref[...] = _rms(x3, fg_ref[...])


def _combine(x1, slots, topw, p2d, pg, wpg, wpp, fg, tm=512):
    t = x1.shape[0]
    nt = t // tm
    const = lambda i: (0, 0)
    slot_specs = [pl.BlockSpec((tm * TILE_ROWS, LANES), functools.partial(lambda k, i: (k * nt + i, 0), k))
                  for k in range(TOP_K)]
    return pl.pallas_call(
        _combine_kernel,
        grid=(nt,),
        in_specs=[
            pl.BlockSpec((tm, D_MODEL), lambda i: (i, 0)),
            *slot_specs,
            pl.BlockSpec((tm, LANES), lambda i: (i, 0)),
            pl.BlockSpec((tm, D_PLE), lambda i: (i, 0)),
            pl.BlockSpec((1, D_MODEL), const),
            pl.BlockSpec((D_MODEL, D_MODEL), const),
            pl.BlockSpec((D_PLE, D_MODEL), const),
            pl.BlockSpec((1, D_MODEL), const),
        ],
        out_specs=pl.BlockSpec((tm, D_MODEL), lambda i: (i, 0)),
        out_shape=jax.ShapeDtypeStruct((t, D_MODEL), F32),
        compiler_params=pltpu.CompilerParams(
            dimension_semantics=("arbitrary",), vmem_limit_bytes=VMEM_LIMIT),
        name="combine_ple",
    )(x1, slots, slots, slots, slots, topw, p2d, pg, wpg, wpp, fg)


def _routing_tables(top_idx_t, n_tok):
    n_assign = n_tok * TOP_K
    expert_flat = top_idx_t.reshape(-1)
    order = jnp.argsort(expert_flat, stable=True).astype(jnp.int32)
    counts = jnp.bincount(expert_flat, length=N_EXPERTS).astype(jnp.int32)
    start = jnp.cumsum(counts) - counts
    padded = (counts + MOE_ROWS - 1) // MOE_ROWS * MOE_ROWS
    pend = jnp.cumsum(padded)
    pstart = pend - padded
    n_rows = n_assign + N_EXPERTS * MOE_ROWS
    n_blocks = n_rows // MOE_ROWS
    block_start = jnp.arange(n_blocks, dtype=jnp.int32) * MOE_ROWS
    block_expert = jnp.minimum(jnp.sum(block_start[:, None] >= pend[None, :], axis=1),
                               N_EXPERTS - 1).astype(jnp.int32)
    nvalid = (pend[-1] // MOE_ROWS).astype(jnp.int32).reshape(1)
    is_e = block_expert[:, None] == jnp.arange(N_EXPERTS, dtype=jnp.int32)[None, :]
    per_block = lambda v: jnp.sum(jnp.where(is_e, v[None, :], 0), axis=1)
    nreal = jnp.clip(per_block(pstart + counts) - block_start, 0, MOE_ROWS).astype(jnp.int32)
    sorted_pos = (block_start + per_block(start - pstart))[:, None] + jnp.arange(MOE_ROWS, dtype=jnp.int32)[None, :]
    assign = order[jnp.clip(sorted_pos, 0, n_assign - 1)]
    row_token = assign % n_tok
    is_real = jnp.arange(MOE_ROWS, dtype=jnp.int32)[None, :] < nreal[:, None]
    row_q = block_start[:, None] + jnp.arange(MOE_ROWS, dtype=jnp.int32)[None, :]
    spare = n_assign + row_q - per_block(start + counts)[:, None]
    row_dest = jnp.where(is_real, assign, spare)
    return (block_expert, nvalid, (row_token * TILE_ROWS).reshape(n_blocks, 1, MOE_ROWS),
            (row_dest * TILE_ROWS).reshape(n_blocks, 1, MOE_ROWS), n_rows)


def _layer(x2d, p2d, bsz, seq, mix_norm_g, w_in, conv_w, conv_b, dt_bias, a_log, d_skip, ssd_norm_g,
           w_ssd_out, pool_w, pool_scale, w_mix_out, ffn_norm_g, w_router, b_router,
           w_gate_up, b_gate_up, w_down, b_down, ple_norm_g, w_ple_gate, w_ple_proj, out_g):
    n_tok = x2d.shape[0]
    dt0 = D_INNER + D_CONV
    w_main = jnp.concatenate([w_in[:, :dt0], w_in[:, dt0 + HEADS:]], axis=1).astype(BF16)
    proj, dt_raw = _in_proj(x2d, mix_norm_g[None, :], w_main, _hilo_weight(w_in[:, dt0:dt0 + HEADS]))

    pad_h = lambda v: jnp.pad(v, (0, LANES - HEADS))[None, :]
    ltri = (jnp.arange(CHUNK)[:, None] >= jnp.arange(CHUNK)[None, :]).astype(BF16)
    ltri3 = jnp.concatenate([ltri, ltri, ltri], axis=1)
    e1 = (jnp.arange(LANES)[:, None] == (jnp.arange(D_INNER) // HEAD_DIM)[None, :]).astype(BF16)
    e2 = jnp.concatenate([e1, e1], axis=0)
    yssd, ypool = _mixers(
        proj, dt_raw, bsz, seq, conv_w, conv_b[None, :], pad_h(dt_bias), pad_h(a_log),
        jnp.repeat(d_skip, HEAD_DIM)[None, :], ssd_norm_g[None, :], ltri3, e2,
        pool_w.astype(BF16), pool_scale[None, :])

    br = jnp.pad(b_router, (0, LANES - N_EXPERTS))[None, :]
    x1, h2, topw, topi = _mix_route(x2d, yssd, ypool, proj, w_ssd_out.astype(BF16),
                                    w_mix_out.astype(BF16), ffn_norm_g[None, :], _hilo_weight(w_router), br)

    block_expert, nvalid, row_token3, row_dest3, n_slot_rows = _routing_tables(topi[:TOP_K], n_tok)
    slots = _moe(block_expert, nvalid, row_token3, row_dest3, h2, w_gate_up, b_gate_up[:, None, :],
                 w_down, b_down[:, None, :], n_slot_rows)

    return _combine(x1, slots, topw, p2d, ple_norm_g[None, :], w_ple_gate.astype(BF16),
                    w_ple_proj.astype(BF16), out_g[None, :])


def kernel(x, p, mix_norm_g, w_in, conv_w, conv_b, dt_bias, a_log, d_skip, ssd_norm_g, w_ssd_out, pool_w,
           pool_scale, w_mix_out, ffn_norm_g, w_router, b_router, w_gate_up, b_gate_up, w_down, b_down,
           ple_norm_g, w_ple_gate, w_ple_proj, final_norm_g):
    bsz, seq, d = x.shape
    depth = p.shape[0]
    assert depth == 1 and d == D_MODEL and p.shape[-1] == D_PLE
    assert seq % (MIX_CHUNKS * CHUNK) == 0 and (bsz * seq) % 1024 == 0
    x2d = x.reshape(bsz * seq, d)
    out = _layer(x2d, p[0].reshape(bsz * seq, D_PLE), bsz, seq, mix_norm_g[0], w_in[0], conv_w[0], conv_b[0],
                 dt_bias[0], a_log[0], d_skip[0], ssd_norm_g[0], w_ssd_out[0], pool_w[0], pool_scale[0],
                 w_mix_out[0], ffn_norm_g[0], w_router[0], b_router[0], w_gate_up[0], b_gate_up[0],
                 w_down[0], b_down[0], ple_norm_g[0], w_ple_gate[0], w_ple_proj[0], final_norm_g)
    return out.reshape(bsz, seq, d)
```

```python
import functools

import jax
import jax.numpy as jnp
from jax import lax
from jax.experimental import pallas as pl
from jax.experimental.pallas import tpu as pltpu

F32 = jnp.float32
BF16 = jnp.bfloat16

D_MODEL = 1024
D_INNER = 2048
HEAD_DIM = 64
HEADS = 32
GROUPS = 4
HEADS_PER_GROUP = HEADS // GROUPS
GROUP_DIM = D_INNER // GROUPS
D_STATE = 128
CONV_WIDTH = 4
CHUNK = 128
D_BC = 2 * GROUPS * D_STATE
D_CONV = D_INNER + D_BC
POOL_WIDTH = D_MODEL
POOL_WINDOWS = (2, 4, 8, 16)
POOL_GROUP_DIM = POOL_WIDTH // len(POOL_WINDOWS)
N_EXPERTS = 32
TOP_K = 4
D_EXPERT = D_MODEL
SWIGLU_LIMIT = 7.0
SWIGLU_ALPHA = 1.702
D_PLE = 256
EPS = 1e-6

LANES = 128
TILE_ROWS = D_MODEL // (2 * LANES)
TOPI_ROWS = 8
ROUTE_SUB = 256
MIX_CHUNKS = 4
HALO = 16
D_PROJ = D_INNER + D_CONV + POOL_WIDTH + 2 * D_MODEL
MOE_ROWS = 256
VMEM_LIMIT = 56 * 1024 * 1024


def _split2(v):
    hi = v.astype(BF16)
    lo = (v - hi.astype(F32)).astype(BF16)
    return hi, lo


def _split3(v):
    hi = v.astype(BF16)
    r = v - hi.astype(F32)
    mid = r.astype(BF16)
    lo = (r - mid.astype(F32)).astype(BF16)
    return hi, mid, lo


def _hilo_weight(w):
    n = w.shape[1]
    hi, lo = _split2(w)
    top = jnp.pad(jnp.concatenate([hi, lo], axis=1), ((0, 0), (0, LANES - 2 * n)))
    bottom = jnp.pad(hi, ((0, 0), (0, LANES - n)))
    return jnp.concatenate([top, bottom], axis=0)


def _hilo_dot(hi, lo, w2_ref, n):
    k = hi.shape[1]
    a = jnp.dot(hi, w2_ref[0:k, :], preferred_element_type=F32)
    b = jnp.dot(lo, w2_ref[k:2 * k, :], preferred_element_type=F32)
    return a + pltpu.roll(a, LANES - n, axis=1) + b


def _rms(x, g):
    return x * lax.rsqrt(jnp.mean(x * x, axis=-1, keepdims=True) + EPS) * g


def _store_token_tiles(ref2d, val, row0=0):
    rows = val.shape[0]
    for j in range(TILE_ROWS):
        c0 = 2 * j * LANES
        hi = lax.bitcast_convert_type(val[:, c0:c0 + LANES].astype(BF16).astype(F32), jnp.uint32)
        lo = lax.bitcast_convert_type(val[:, c0 + LANES:c0 + 2 * LANES].astype(BF16).astype(F32), jnp.uint32)
        ref2d[pl.ds(row0 * TILE_ROWS + j, rows, stride=TILE_ROWS), :] = hi | lax.shift_right_logical(lo, jnp.uint32(16))


def _load_token_tiles(ref2d, rows):
    pieces = []
    for j in range(TILE_ROWS):
        words = ref2d[pl.ds(j, rows, stride=TILE_ROWS), :]
        pieces.append(lax.bitcast_convert_type(words & jnp.uint32(0xFFFF0000), F32))
        pieces.append(lax.bitcast_convert_type(lax.shift_left(words, jnp.uint32(16)), F32))
    return jnp.concatenate(pieces, axis=1)


def _in_proj_kernel(x_ref, g_ref, w_ref, wdt_ref, proj_ref, dt_ref, h_sc):
    @pl.when(pl.program_id(1) == 0)
    def _():
        h = _rms(x_ref[...], g_ref[...])
        hi, lo = _split2(h)
        h_sc[...] = hi
        dt_ref[...] = _hilo_dot(hi, lo, wdt_ref, HEADS)

    proj_ref[...] = jnp.dot(h_sc[...], w_ref[...], preferred_element_type=F32).astype(BF16)


def _in_proj(x2d, g, w_main, wdt2, tm=1024, tn=2048):
    t = x2d.shape[0]
    return pl.pallas_call(
        _in_proj_kernel,
        grid=(t // tm, D_PROJ // tn),
        in_specs=[
            pl.BlockSpec((tm, D_MODEL), lambda i, j: (i, 0)),
            pl.BlockSpec((1, D_MODEL), lambda i, j: (0, 0)),
            pl.BlockSpec((D_MODEL, tn), lambda i, j: (0, j)),
            pl.BlockSpec((2 * D_MODEL, LANES), lambda i, j: (0, 0)),
        ],
        out_specs=[
            pl.BlockSpec((tm, tn), lambda i, j: (i, j)),
            pl.BlockSpec((tm, LANES), lambda i, j: (i, 0)),
        ],
        out_shape=[
            jax.ShapeDtypeStruct((t, D_PROJ), BF16),
            jax.ShapeDtypeStruct((t, LANES), F32),
        ],
        scratch_shapes=[pltpu.VMEM((tm, D_MODEL), BF16)],
        compiler_params=pltpu.CompilerParams(
            dimension_semantics=("arbitrary", "arbitrary"), vmem_limit_bytes=VMEM_LIMIT),
        name="in_proj",
    )(x2d, g, w_main, wdt2)


def _mixers_kernel(z_ref, xs_ref, bc_ref, u_ref, dt_ref,
                   cw_ref, cb_ref, dtb_ref, alog_ref, dskip_ref, ng_ref, ltri_ref, e2_ref,
                   shift_ref, band_ref, pw_ref, ps_ref,
                   yssd_ref, ypool_ref,
                   ext_sc, extu_sc, state_sc):
    c = pl.program_id(1)
    rows = MIX_CHUNKS * CHUNK

    @pl.when(c == 0)
    def _():
        ext_sc[0:HALO, :] = jnp.zeros((HALO, D_CONV), BF16)
        extu_sc[0:HALO, :] = jnp.zeros((HALO, POOL_WIDTH), BF16)
        state_sc[...] = jnp.zeros_like(state_sc)

    @pl.when(c > 0)
    def _():
        ext_sc[0:HALO, :] = ext_sc[rows:rows + HALO, :]
        extu_sc[0:HALO, :] = extu_sc[rows:rows + HALO, :]

    ext_sc[HALO:HALO + rows, 0:D_INNER] = xs_ref[...]
    ext_sc[HALO:HALO + rows, D_INNER:D_CONV] = bc_ref[...]
    extu_sc[HALO:HALO + rows, :] = u_ref[...]
    for ci in range(MIX_CHUNKS):
        _mixers_chunk(ci, c * MIX_CHUNKS + ci, z_ref, u_ref, dt_ref, cw_ref, cb_ref, dtb_ref, alog_ref, dskip_ref,
                      ng_ref, ltri_ref, e2_ref, shift_ref, band_ref, pw_ref, ps_ref, yssd_ref, ypool_ref,
                      ext_sc, extu_sc, state_sc)


def _mixers_chunk(ci, chunk_index, z_ref, u_ref, dt_ref, cw_ref, cb_ref, dtb_ref, alog_ref, dskip_ref,
                  ng_ref, ltri_ref, e2_ref, shift_ref, band_ref, pw_ref, ps_ref, yssd_ref, ypool_ref,
                  ext_sc, extu_sc, state_sc):
    r0 = ci * CHUNK
    rs = slice(r0, r0 + CHUNK)
    ext = ext_sc[r0:r0 + HALO + CHUNK, :]
    conv = cb_ref[...] + cw_ref[CONV_WIDTH - 1:CONV_WIDTH, :] * ext[HALO:HALO + CHUNK, :].astype(F32)
    for k in range(CONV_WIDTH - 1):
        conv = conv + cw_ref[k:k + 1, :] * jnp.dot(shift_ref[k], ext, preferred_element_type=F32)
    xc = conv * jax.nn.sigmoid(conv)
    xs = xc[:, 0:D_INNER]
    xs_b = xs.astype(BF16)

    dtv = jax.nn.softplus(dt_ref[rs, :] + dtb_ref[...])
    da = dtv * (-jnp.exp(alog_ref[...]))
    a_cum = jnp.dot(ltri_ref[...], jnp.concatenate(_split3(da), axis=0),
                    preferred_element_type=F32)
    expa = jnp.exp(a_cum)
    a_last = a_cum[CHUNK - 1:CHUNK, :]
    wst = dtv * jnp.exp(a_last - a_cum)
    a_cum_t = a_cum.T
    dt_t = dtv.T

    both = jnp.concatenate([wst, expa], axis=0)
    hi, lo = _split2(both)
    expd = jnp.dot(jnp.concatenate([hi, lo], axis=1), e2_ref[...],
                   preferred_element_type=F32)
    wst_x = expd[0:CHUNK, :]
    expa_x = expd[CHUNK:2 * CHUNK, :]
    xw_b = (xs * wst_x).astype(BF16)

    row = lax.broadcasted_iota(jnp.int32, (CHUNK, CHUNK), 0)
    col = lax.broadcasted_iota(jnp.int32, (CHUNK, CHUNK), 1)
    causal_bias = jnp.where(row >= col, 0.0, -jnp.inf).astype(F32)
    lane = lax.broadcasted_iota(jnp.int32, (CHUNK, LANES), 1)
    low_half = lane < HEAD_DIM

    y_groups = []
    for g in range(GROUPS):
        bg = xc[:, D_INNER + g * D_STATE:D_INNER + (g + 1) * D_STATE]
        cg = xc[:, D_INNER + GROUPS * D_STATE + g * D_STATE:D_INNER + GROUPS * D_STATE + (g + 1) * D_STATE]
        bg_b = bg.astype(BF16)
        cg_b = cg.astype(BF16)
        cbm = lax.dot_general(cg_b, bg_b, (((1,), (1,)), ((), ())), preferred_element_type=F32)
        gsl = slice(g * GROUP_DIM, (g + 1) * GROUP_DIM)

        prev_t = state_sc[g]
        y_off = jnp.dot(cg_b, prev_t.astype(BF16), preferred_element_type=F32) * expa_x[:, gsl]
        st_t = jnp.dot(bg.T.astype(BF16), xw_b[:, gsl], preferred_element_type=F32)
        state_sc[g] = prev_t * expa_x[CHUNK - 1:CHUNK, gsl] + st_t

        pairs = []
        for jp in range(HEADS_PER_GROUP // 2):
            ms = []
            for hh in range(2):
                h = g * HEADS_PER_GROUP + jp * 2 + hh
                seg = a_cum[:, h:h + 1] - a_cum_t[h:h + 1, :]
                dec = jnp.exp(seg + causal_bias)
                ms.append((cbm * dec * dt_t[h:h + 1, :]).astype(BF16))
            lhs = jnp.concatenate(ms, axis=1)
            c0 = g * GROUP_DIM + jp * LANES
            xp = xs_b[:, c0:c0 + LANES]
            zero = jnp.zeros_like(xp)
            rhs = jnp.concatenate([jnp.where(low_half, xp, zero), jnp.where(low_half, zero, xp)], axis=0)
            pairs.append(jnp.dot(lhs, rhs, preferred_element_type=F32))
        y_diag = jnp.concatenate(pairs, axis=1)

        yg = y_diag + y_off + dskip_ref[:, gsl] * xs[:, gsl]
        zg = z_ref[rs, gsl].astype(F32)
        yg = yg * (zg * jax.nn.sigmoid(zg))
        yg = yg * lax.rsqrt(jnp.mean(yg * yg, axis=-1, keepdims=True) + EPS) * ng_ref[:, gsl]
        y_groups.append(yg.astype(BF16))
    yssd_ref[rs, :] = jnp.concatenate(y_groups, axis=1)

    pos = chunk_index * CHUNK + lax.broadcasted_iota(jnp.int32, (CHUNK, 1), 0)
    outs = []
    for gi, w in enumerate(POOL_WINDOWS):
        psl = slice(gi * POOL_GROUP_DIM, (gi + 1) * POOL_GROUP_DIM)
        s = jnp.dot(band_ref[gi], extu_sc[r0:r0 + HALO + CHUNK, psl], preferred_element_type=F32)
        cnt = jnp.minimum(pos + 1, w).astype(F32)
        pooled = s / cnt - u_ref[rs, psl].astype(F32)
        outs.append(jnp.dot(pooled.astype(BF16), pw_ref[gi], preferred_element_type=F32))
    ypool_ref[rs, :] = (jnp.concatenate(outs, axis=1) * ps_ref[...]).astype(BF16)


def _mixers(proj, dt_raw, bsz, seq, cw, cb, dtb, alog, dskip, ng, ltri3, e2, pw, ps):
    rows = MIX_CHUNKS * CHUNK
    nc = seq // rows
    t = bsz * seq
    rowmap = lambda b, c: b * nc + c
    const2 = lambda b, c: (0, 0)
    const3 = lambda b, c: (0, 0, 0)
    trow = jnp.arange(CHUNK)[:, None] + HALO
    jcol = jnp.arange(HALO + CHUNK)[None, :]
    shifts = jnp.stack([(jcol == trow - (CONV_WIDTH - 1) + k) for k in range(CONV_WIDTH - 1)]).astype(BF16)
    bands = jnp.stack([(jcol <= trow) & (jcol > trow - w) for w in POOL_WINDOWS]).astype(BF16)
    return pl.pallas_call(
        _mixers_kernel,
        grid=(bsz, nc),
        in_specs=[
            pl.BlockSpec((rows, D_INNER), lambda b, c: (rowmap(b, c), 0)),
            pl.BlockSpec((rows, D_INNER), lambda b, c: (rowmap(b, c), 1)),
            pl.BlockSpec((rows, D_BC), lambda b, c: (rowmap(b, c), 4)),
            pl.BlockSpec((rows, POOL_WIDTH), lambda b, c: (rowmap(b, c), 5)),
            pl.BlockSpec((rows, LANES), lambda b, c: (rowmap(b, c), 0)),
            pl.BlockSpec((CONV_WIDTH, D_CONV), const2),
            pl.BlockSpec((1, D_CONV), const2),
            pl.BlockSpec((1, LANES), const2),
            pl.BlockSpec((1, LANES), const2),
            pl.BlockSpec((1, D_INNER), const2),
            pl.BlockSpec((1, D_INNER), const2),
            pl.BlockSpec((CHUNK, 3 * CHUNK), const2),
            pl.BlockSpec((2 * LANES, D_INNER), const2),
            pl.BlockSpec((CONV_WIDTH - 1, CHUNK, HALO + CHUNK), const3),
            pl.BlockSpec((len(POOL_WINDOWS), CHUNK, HALO + CHUNK), const3),
            pl.BlockSpec((len(POOL_WINDOWS), POOL_GROUP_DIM, POOL_GROUP_DIM), const3),
            pl.BlockSpec((1, POOL_WIDTH), const2),
        ],
        out_specs=[
            pl.BlockSpec((rows, D_INNER), lambda b, c: (rowmap(b, c), 0)),
            pl.BlockSpec((rows, POOL_WIDTH), lambda b, c: (rowmap(b, c), 0)),
        ],
        out_shape=[
            jax.ShapeDtypeStruct((t, D_INNER), BF16),
            jax.ShapeDtypeStruct((t, POOL_WIDTH), BF16),
        ],
        scratch_shapes=[
            pltpu.VMEM((HALO + rows, D_CONV), BF16),
            pltpu.VMEM((HALO + rows, POOL_WIDTH), BF16),
            pltpu.VMEM((GROUPS, D_STATE, GROUP_DIM), F32),
        ],
        compiler_params=pltpu.CompilerParams(
            dimension_semantics=("arbitrary", "arbitrary"), vmem_limit_bytes=VMEM_LIMIT),
        name="mixers",
    )(proj, proj, proj, proj, dt_raw, cw, cb, dtb, alog, dskip, ng, ltri3, e2, shifts, bands, pw, ps)


def _mix_route_kernel(x_ref, yssd_ref, ypool_ref, gates_ref, wso_ref, wmo_ref, fg_ref, wr_ref, br_ref,
                      x1_ref, h2_ref, topw_ref, topi_ref):
    for r0 in range(0, x_ref.shape[0], ROUTE_SUB):
        _mix_route_rows(slice(r0, r0 + ROUTE_SUB), x_ref, yssd_ref, ypool_ref, gates_ref, wso_ref, wmo_ref, fg_ref,
                        wr_ref, br_ref, x1_ref, h2_ref, topw_ref, topi_ref)


def _mix_route_rows(rs, x_ref, yssd_ref, ypool_ref, gates_ref, wso_ref, wmo_ref, fg_ref, wr_ref, br_ref,
                    x1_ref, h2_ref, topw_ref, topi_ref):
    y_ssd = jnp.dot(yssd_ref[rs, :], wso_ref[...], preferred_element_type=F32)
    gates = jax.nn.sigmoid(gates_ref[rs, :].astype(F32))
    mixed = gates[:, 0:D_MODEL] * y_ssd + gates[:, D_MODEL:2 * D_MODEL] * ypool_ref[rs, :].astype(F32)
    x1 = x_ref[rs, :] + jnp.dot(mixed.astype(BF16), wmo_ref[...], preferred_element_type=F32)
    x1_ref[rs, :] = x1
    h2 = _rms(x1, fg_ref[...])
    _store_token_tiles(h2_ref, h2, rs.start)

    hi, lo = _split2(h2)
    logits = _hilo_dot(hi, lo, wr_ref, N_EXPERTS) + br_ref[...]
    tm = logits.shape[0]
    lane = lax.broadcasted_iota(jnp.int32, (tm, LANES), 1)
    neg = jnp.float32(-jnp.inf)
    work = jnp.where(lane < N_EXPERTS, logits, neg)
    vals = []
    idxs = []
    for _ in range(TOP_K):
        m = jnp.max(work, axis=-1, keepdims=True)
        idx = jnp.min(jnp.where(work == m, lane, LANES), axis=-1, keepdims=True)
        vals.append(m)
        idxs.append(idx)
        work = jnp.where(lane == idx, neg, work)
    es = [jnp.exp(v - vals[0]) for v in vals]
    den = es[0] + es[1] + es[2] + es[3]
    topw = jnp.zeros((tm, LANES), F32)
    topi = jnp.zeros((tm, LANES), jnp.int32)
    for k in range(TOP_K):
        topw = jnp.where(lane == k, es[k] / den, topw)
        topi = jnp.where(lane == k, idxs[k], topi)
    topw_ref[rs, :] = topw
    topi_ref[:, rs] = topi.T[0:TOPI_ROWS, :]


def _mix_route(x2d, yssd, ypool, proj, wso, wmo, fg, wr2, br, tm=512):
    t = x2d.shape[0]
    const = lambda i: (0, 0)
    return pl.pallas_call(
        _mix_route_kernel,
        grid=(t // tm,),
        in_specs=[
            pl.BlockSpec((tm, D_MODEL), lambda i: (i, 0)),
            pl.BlockSpec((tm, D_INNER), lambda i: (i, 0)),
            pl.BlockSpec((tm, POOL_WIDTH), lambda i: (i, 0)),
            pl.BlockSpec((tm, 2 * D_MODEL), lambda i: (i, 3)),
            pl.BlockSpec((D_INNER, D_MODEL), const),
            pl.BlockSpec((D_MODEL, D_MODEL), const),
            pl.BlockSpec((1, D_MODEL), const),
            pl.BlockSpec((2 * D_MODEL, LANES), const),
            pl.BlockSpec((1, LANES), const),
        ],
        out_specs=[
            pl.BlockSpec((tm, D_MODEL), lambda i: (i, 0)),
            pl.BlockSpec((tm * TILE_ROWS, LANES), lambda i: (i, 0)),
            pl.BlockSpec((tm, LANES), lambda i: (i, 0)),
            pl.BlockSpec((TOPI_ROWS, tm), lambda i: (0, i)),
        ],
        out_shape=[
            jax.ShapeDtypeStruct((t, D_MODEL), F32),
            jax.ShapeDtypeStruct((t * TILE_ROWS, LANES), jnp.uint32),
            jax.ShapeDtypeStruct((t, LANES), F32),
            jax.ShapeDtypeStruct((TOPI_ROWS, t), jnp.int32),
        ],
        compiler_params=pltpu.CompilerParams(
            dimension_semantics=("arbitrary",), vmem_limit_bytes=VMEM_LIMIT),
        name="mix_route",
    )(x2d, yssd, ypool, proj, wso, wmo, fg, wr2, br)


def _moe_kernel(be_ref, nvalid_ref,
                tok0_ref, toknext_ref, dstprev_ref,
                h_hbm, wgu_ref, bgu_ref, wd_ref, bd_ref,
                slots_hbm,
                xbuf, ybuf, xb_sc, wgu_bf, wd_bf, gsem, ssem, *, nb):
    i = pl.program_id(0)
    nvalid = nvalid_ref[0]
    slot = lax.rem(i, 2)
    has_next = i + 1 < nvalid

    def tile(off, n=1):
        return pl.ds(pl.multiple_of(off, TILE_ROWS), n * TILE_ROWS)

    def gather_copy(tok, r, s):
        return pltpu.make_async_copy(h_hbm.at[tile(tok)], xbuf.at[s, tile(r * TILE_ROWS)], gsem.at[s])

    def scatter_copy(dst, r, s):
        return pltpu.make_async_copy(ybuf.at[s, tile(r * TILE_ROWS)], slots_hbm.at[tile(dst)], ssem.at[s])

    def wait_gather(s):
        pltpu.make_async_copy(h_hbm.at[tile(0, MOE_ROWS)], xbuf.at[s, tile(0, MOE_ROWS)], gsem.at[s]).wait()

    def wait_scatter(s):
        pltpu.make_async_copy(ybuf.at[s, tile(0, MOE_ROWS)], slots_hbm.at[tile(0, MOE_ROWS)], ssem.at[s]).wait()

    def scatter_whole_block(s):
        pltpu.make_async_copy(ybuf.at[s, tile(0, MOE_ROWS)], slots_hbm.at[tile(dstprev_ref[0, 0, 0], MOE_ROWS)],
                              ssem.at[s]).start()

    def issue(s, gather_ref, scatter_ref):
        for r in range(MOE_ROWS):
            if gather_ref is not None:
                gather_copy(gather_ref[0, 0, r], r, s).start()
            if scatter_ref is not None:
                scatter_copy(scatter_ref[0, 0, r], r, s).start()

    @pl.when(i == 0)
    def _():
        ybuf[...] = jnp.zeros_like(ybuf)
        issue(0, tok0_ref, None)

    for par in (0, 1):
        other = 1 - par
        mine = slot == par

        pl.when(jnp.logical_and(mine, i >= 2))(functools.partial(wait_scatter, par))
        pl.when(jnp.logical_and(mine, i < nvalid))(functools.partial(wait_gather, par))
        pl.when(jnp.logical_and(mine, jnp.logical_and(has_next, i == 0)))(
            functools.partial(issue, other, toknext_ref, None))
        pl.when(jnp.logical_and(mine, jnp.logical_and(has_next, i >= 1)))(
            functools.partial(issue, other, toknext_ref, dstprev_ref))
        pl.when(jnp.logical_and(mine, jnp.logical_and(jnp.logical_not(has_next), jnp.logical_and(i >= 1, i <= nvalid))))(
            functools.partial(issue, other, None, dstprev_ref))
        pl.when(jnp.logical_and(mine, i > nvalid))(functools.partial(scatter_whole_block, other))
        pl.when(jnp.logical_and(mine, i == nb))(functools.partial(wait_scatter, other))

    @pl.when(i < nvalid)
    def _():
        @pl.when(jnp.logical_or(i == 0, be_ref[i] != be_ref[jnp.maximum(i - 1, 0)]))
        def _():
            wgu_bf[...] = wgu_ref[0].astype(BF16)
            wd_bf[...] = wd_ref[0].astype(BF16)

        xb_sc[...] = _load_token_tiles(xbuf.at[slot], MOE_ROWS).astype(BF16)
        gu = jnp.dot(xb_sc[...], wgu_bf[...], preferred_element_type=F32) + bgu_ref[0]
        gate = jnp.minimum(gu[:, 0:D_EXPERT], SWIGLU_LIMIT)
        up = jnp.clip(gu[:, D_EXPERT:2 * D_EXPERT], -SWIGLU_LIMIT, SWIGLU_LIMIT)
        act = (up + 1.0) * gate * jax.nn.sigmoid(SWIGLU_ALPHA * gate)
        y = jnp.dot(act.astype(BF16), wd_bf[...], preferred_element_type=F32) + bd_ref[0]
        _store_token_tiles(ybuf.at[slot], y)


def _moe(block_expert, nvalid, row_token3, row_dest3, h2, wgu, bgu3, wd, bd3, n_slot_rows):
    nb = block_expert.shape[0]
    expert_of = lambda i, be: be[jnp.minimum(i, nb - 1)]
    grid_spec = pltpu.PrefetchScalarGridSpec(
        num_scalar_prefetch=2,
        grid=(nb + 1,),
        in_specs=[
            pl.BlockSpec((1, 1, MOE_ROWS), lambda i, be, nv: (0, 0, 0), memory_space=pltpu.SMEM),
            pl.BlockSpec((1, 1, MOE_ROWS), lambda i, be, nv: (jnp.minimum(i + 1, nb - 1), 0, 0),
                         memory_space=pltpu.SMEM),
            pl.BlockSpec((1, 1, MOE_ROWS), lambda i, be, nv: (jnp.clip(i - 1, 0, nb - 1), 0, 0),
                         memory_space=pltpu.SMEM),
            pl.BlockSpec(memory_space=pl.ANY),
            pl.BlockSpec((1, D_MODEL, 2 * D_EXPERT), lambda i, be, nv: (expert_of(i, be), 0, 0)),
            pl.BlockSpec((1, 1, 2 * D_EXPERT), lambda i, be, nv: (expert_of(i, be), 0, 0)),
            pl.BlockSpec((1, D_EXPERT, D_MODEL), lambda i, be, nv: (expert_of(i, be), 0, 0)),
            pl.BlockSpec((1, 1, D_MODEL), lambda i, be, nv: (expert_of(i, be), 0, 0)),
        ],
        out_specs=pl.BlockSpec(memory_space=pl.ANY),
        scratch_shapes=[
            pltpu.VMEM((2, MOE_ROWS * TILE_ROWS, LANES), jnp.uint32),
            pltpu.VMEM((2, MOE_ROWS * TILE_ROWS, LANES), jnp.uint32),
            pltpu.VMEM((MOE_ROWS, D_MODEL), BF16),
            pltpu.VMEM((D_MODEL, 2 * D_EXPERT), BF16),
            pltpu.VMEM((D_EXPERT, D_MODEL), BF16),
            pltpu.SemaphoreType.DMA((2,)),
            pltpu.SemaphoreType.DMA((2,)),
        ],
    )
    return pl.pallas_call(
        functools.partial(_moe_kernel, nb=nb),
        grid_spec=grid_spec,
        out_shape=jax.ShapeDtypeStruct((n_slot_rows * TILE_ROWS, LANES), jnp.uint32),
        compiler_params=pltpu.CompilerParams(
            dimension_semantics=("arbitrary",), vmem_limit_bytes=VMEM_LIMIT),
        name="moe_experts",
    )(block_expert, nvalid, row_token3, row_token3, row_dest3, h2, wgu, bgu3, wd, bd3)


def _combine_kernel(x1_ref, s0_ref, s1_ref, s2_ref, s3_ref, topw_ref, p_ref, pg_ref, wpg_ref, wpp_ref, fg_ref,
                    out_ref):
    x2 = x1_ref[...]
    topw = topw_ref[...]
    for k, s_ref in enumerate((s0_ref, s1_ref, s2_ref, s3_ref)):
        x2 = x2 + _load_token_tiles(s_ref, x2.shape[0]) * topw[:, k:k + 1]
    n = _rms(x2, pg_ref[...])
    gate = jax.nn.sigmoid(jnp.dot(n.astype(BF16), wpg_ref[...], preferred_element_type=F32))
    pp = jnp.dot(p_ref[...].astype(BF16), wpp_ref[...], preferred_element_type=F32)
    x3 = x2 + gate * pp
    out_ref[...] = _rms(x3, fg_ref[...])


def _combine(x1, slots, topw, p2d, pg, wpg, wpp, fg, tm=512):
    t = x1.shape[0]
    nt = t // tm
    const = lambda i: (0, 0)
    slot_specs = [pl.BlockSpec((tm * TILE_ROWS, LANES), functools.partial(lambda k, i: (k * nt + i, 0), k))
                  for k in range(TOP_K)]
    return pl.pallas_call(
        _combine_kernel,
        grid=(nt,),
        in_specs=[
            pl.BlockSpec((tm, D_MODEL), lambda i: (i, 0)),
            *slot_specs,
            pl.BlockSpec((tm, LANES), lambda i: (i, 0)),
            pl.BlockSpec((tm, D_PLE), lambda i: (i, 0)),
            pl.BlockSpec((1, D_MODEL), const),
            pl.BlockSpec((D_MODEL, D_MODEL), const),
            pl.BlockSpec((D_PLE, D_MODEL), const),
            pl.BlockSpec((1, D_MODEL), const),
        ],
        out_specs=pl.BlockSpec((tm, D_MODEL), lambda i: (i, 0)),
        out_shape=jax.ShapeDtypeStruct((t, D_MODEL), F32),
        compiler_params=pltpu.CompilerParams(
            dimension_semantics=("arbitrary",), vmem_limit_bytes=VMEM_LIMIT),
        name="combine_ple",
    )(x1, slots, slots, slots, slots, topw, p2d, pg, wpg, wpp, fg)


def _routing_tables(top_idx_t, n_tok):
    n_assign = n_tok * TOP_K
    expert_flat = top_idx_t.reshape(-1)
    order = jnp.argsort(expert_flat, stable=True).astype(jnp.int32)
    counts = jnp.bincount(expert_flat, length=N_EXPERTS).astype(jnp.int32)
    start = jnp.cumsum(counts) - counts
    padded = (counts + MOE_ROWS - 1) // MOE_ROWS * MOE_ROWS
    pend = jnp.cumsum(padded)
    pstart = pend - padded
    n_rows = n_assign + N_EXPERTS * MOE_ROWS
    n_blocks = n_rows // MOE_ROWS
    block_start = jnp.arange(n_blocks, dtype=jnp.int32) * MOE_ROWS
    block_expert = jnp.minimum(jnp.sum(block_start[:, None] >= pend[None, :], axis=1),
                               N_EXPERTS - 1).astype(jnp.int32)
    nvalid = (pend[-1] // MOE_ROWS).astype(jnp.int32).reshape(1)
    is_e = block_expert[:, None] == jnp.arange(N_EXPERTS, dtype=jnp.int32)[None, :]
    per_block = lambda v: jnp.sum(jnp.where(is_e, v[None, :], 0), axis=1)
    nreal = jnp.clip(per_block(pstart + counts) - block_start, 0, MOE_ROWS).astype(jnp.int32)
    sorted_pos = (block_start + per_block(start - pstart))[:, None] + jnp.arange(MOE_ROWS, dtype=jnp.int32)[None, :]
    assign = order[jnp.clip(sorted_pos, 0, n_assign - 1)]
    row_token = assign % n_tok
    is_real = jnp.arange(MOE_ROWS, dtype=jnp.int32)[None, :] < nreal[:, None]
    row_q = block_start[:, None] + jnp.arange(MOE_ROWS, dtype=jnp.int32)[None, :]
    spare = n_assign + row_q - per_block(start + counts)[:, None]
    row_dest = jnp.where(is_real, assign, spare)
    return (block_expert, nvalid, (row_token * TILE_ROWS).reshape(n_blocks, 1, MOE_ROWS),
            (row_dest * TILE_ROWS).reshape(n_blocks, 1, MOE_ROWS), n_rows)


def _layer(x2d, p2d, bsz, seq, mix_norm_g, w_in, conv_w, conv_b, dt_bias, a_log, d_skip, ssd_norm_g,
           w_ssd_out, pool_w, pool_scale, w_mix_out, ffn_norm_g, w_router, b_router,
           w_gate_up, b_gate_up, w_down, b_down, ple_norm_g, w_ple_gate, w_ple_proj, out_g):
    n_tok = x2d.shape[0]
    dt0 = D_INNER + D_CONV
    w_main = jnp.concatenate([w_in[:, :dt0], w_in[:, dt0 + HEADS:]], axis=1).astype(BF16)
    proj, dt_raw = _in_proj(x2d, mix_norm_g[None, :], w_main, _hilo_weight(w_in[:, dt0:dt0 + HEADS]))

    pad_h = lambda v: jnp.pad(v, (0, LANES - HEADS))[None, :]
    ltri = (jnp.arange(CHUNK)[:, None] >= jnp.arange(CHUNK)[None, :]).astype(BF16)
    ltri3 = jnp.concatenate([ltri, ltri, ltri], axis=1)
    e1 = (jnp.arange(LANES)[:, None] == (jnp.arange(D_INNER) // HEAD_DIM)[None, :]).astype(BF16)
    e2 = jnp.concatenate([e1, e1], axis=0)
    yssd, ypool = _mixers(
        proj, dt_raw, bsz, seq, conv_w, conv_b[None, :], pad_h(dt_bias), pad_h(a_log),
        jnp.repeat(d_skip, HEAD_DIM)[None, :], ssd_norm_g[None, :], ltri3, e2,
        pool_w.astype(BF16), pool_scale[None, :])

    br = jnp.pad(b_router, (0, LANES - N_EXPERTS))[None, :]
    x1, h2, topw, topi = _mix_route(x2d, yssd, ypool, proj, w_ssd_out.astype(BF16),
                                    w_mix_out.astype(BF16), ffn_norm_g[None, :], _hilo_weight(w_router), br)

    block_expert, nvalid, row_token3, row_dest3, n_slot_rows = _routing_tables(topi[:TOP_K], n_tok)
    slots = _moe(block_expert, nvalid, row_token3, row_dest3, h2, w_gate_up, b_gate_up[:, None, :],
                 w_down, b_down[:, None, :], n_slot_rows)

    return _combine(x1, slots, topw, p2d, ple_norm_g[None, :], w_ple_gate.astype(BF16),
                    w_ple_proj.astype(BF16), out_g[None, :])


def kernel(x, p, mix_norm_g, w_in, conv_w, conv_b, dt_bias, a_log, d_skip, ssd_norm_g, w_ssd_out, pool_w,
           pool_scale, w_mix_out, ffn_norm_g, w_router, b_router, w_gate_up, b_gate_up, w_down, b_down,
           ple_norm_g, w_ple_gate, w_ple_proj, final_norm_g):
    bsz, seq, d = x.shape
    depth = p.shape[0]
    assert depth == 1 and d == D_MODEL and p.shape[-1] == D_PLE
    assert seq % (MIX_CHUNKS * CHUNK) == 0 and (bsz * seq) % 1024 == 0
    x2d = x.reshape(bsz * seq, d)
    out = _layer(x2d, p[0].reshape(bsz * seq, D_PLE), bsz, seq, mix_norm_g[0], w_in[0], conv_w[0], conv_b[0],
                 dt_bias[0], a_log[0], d_skip[0], ssd_norm_g[0], w_ssd_out[0], pool_w[0], pool_scale[0],
                 w_mix_out[0], ffn_norm_g[0], w_router[0], b_router[0], w_gate_up[0], b_gate_up[0],
                 w_down[0], b_down[0], ple_norm_g[0], w_ple_gate[0], w_ple_proj[0], final_norm_g)
    return out.reshape(bsz, seq, d)
```

```python
import functools

import jax
import jax.numpy as jnp
from jax import lax
from jax.experimental import pallas as pl
from jax.experimental.pallas import tpu as pltpu

F32 = jnp.float32
BF16 = jnp.bfloat16

D_MODEL = 1024
D_INNER = 2048
HEAD_DIM = 64
HEADS = 32
GROUPS = 4
HEADS_PER_GROUP = HEADS // GROUPS
GROUP_DIM = D_INNER // GROUPS
D_STATE = 128
CONV_WIDTH = 4
CHUNK = 128
D_BC = 2 * GROUPS * D_STATE
D_CONV = D_INNER + D_BC
POOL_WIDTH = D_MODEL
POOL_WINDOWS = (2, 4, 8, 16)
POOL_GROUP_DIM = POOL_WIDTH // len(POOL_WINDOWS)
N_EXPERTS = 32
TOP_K = 4
D_EXPERT = D_MODEL
SWIGLU_LIMIT = 7.0
SWIGLU_ALPHA = 1.702
D_PLE = 256
EPS = 1e-6

LANES = 128
TILE_ROWS = D_MODEL // (2 * LANES)
TOPI_ROWS = 8
ROUTE_SUB = 256
MIX_CHUNKS = 4
HALO = 16
D_PROJ = D_INNER + D_CONV + POOL_WIDTH + 2 * D_MODEL
MOE_ROWS = 256
VMEM_LIMIT = 56 * 1024 * 1024


def _split2(v):
    hi = v.astype(BF16)
    lo = (v - hi.astype(F32)).astype(BF16)
    return hi, lo


def _split3(v):
    hi = v.astype(BF16)
    r = v - hi.astype(F32)
    mid = r.astype(BF16)
    lo = (r - mid.astype(F32)).astype(BF16)
    return hi, mid, lo


def _hilo_weight(w):
    n = w.shape[1]
    hi, lo = _split2(w)
    top = jnp.pad(jnp.concatenate([hi, lo], axis=1), ((0, 0), (0, LANES - 2 * n)))
    bottom = jnp.pad(hi, ((0, 0), (0, LANES - n)))
    return jnp.concatenate([top, bottom], axis=0)


def _hilo_dot(hi, lo, w2_ref, n):
    k = hi.shape[1]
    a = jnp.dot(hi, w2_ref[0:k, :], preferred_element_type=F32)
    b = jnp.dot(lo, w2_ref[k:2 * k, :], preferred_element_type=F32)
    return a + pltpu.roll(a, LANES - n, axis=1) + b


def _rms(x, g):
    return x * lax.rsqrt(jnp.mean(x * x, axis=-1, keepdims=True) + EPS) * g


def _store_token_tiles(ref2d, val, row0=0):
    rows = val.shape[0]
    for j in range(TILE_ROWS):
        c0 = 2 * j * LANES
        hi = lax.bitcast_convert_type(val[:, c0:c0 + LANES].astype(BF16).astype(F32), jnp.uint32)
        lo = lax.bitcast_convert_type(val[:, c0 + LANES:c0 + 2 * LANES].astype(BF16).astype(F32), jnp.uint32)
        ref2d[pl.ds(row0 * TILE_ROWS + j, rows, stride=TILE_ROWS), :] = hi | lax.shift_right_logical(lo, jnp.uint32(16))


def _load_token_tiles(ref2d, rows):
    pieces = []
    for j in range(TILE_ROWS):
        words = ref2d[pl.ds(j, rows, stride=TILE_ROWS), :]
        pieces.append(lax.bitcast_convert_type(words & jnp.uint32(0xFFFF0000), F32))
        pieces.append(lax.bitcast_convert_type(lax.shift_left(words, jnp.uint32(16)), F32))
    return jnp.concatenate(pieces, axis=1)


def _in_proj_kernel(x_ref, g_ref, w_ref, wdt_ref, proj_ref, dt_ref, h_sc):
    @pl.when(pl.program_id(1) == 0)
    def _():
        h = _rms(x_ref[...], g_ref[...])
        hi, lo = _split2(h)
        h_sc[...] = hi
        dt_ref[...] = _hilo_dot(hi, lo, wdt_ref, HEADS)

    proj_ref[...] = jnp.dot(h_sc[...], w_ref[...], preferred_element_type=F32).astype(BF16)


def _in_proj(x2d, g, w_main, wdt2, tm=1024, tn=2048):
    t = x2d.shape[0]
    return pl.pallas_call(
        _in_proj_kernel,
        grid=(t // tm, D_PROJ // tn),
        in_specs=[
            pl.BlockSpec((tm, D_MODEL), lambda i, j: (i, 0)),
            pl.BlockSpec((1, D_MODEL), lambda i, j: (0, 0)),
            pl.BlockSpec((D_MODEL, tn), lambda i, j: (0, j)),
            pl.BlockSpec((2 * D_MODEL, LANES), lambda i, j: (0, 0)),
        ],
        out_specs=[
            pl.BlockSpec((tm, tn), lambda i, j: (i, j)),
            pl.BlockSpec((tm, LANES), lambda i, j: (i, 0)),
        ],
        out_shape=[
            jax.ShapeDtypeStruct((t, D_PROJ), BF16),
            jax.ShapeDtypeStruct((t, LANES), F32),
        ],
        scratch_shapes=[pltpu.VMEM((tm, D_MODEL), BF16)],
        compiler_params=pltpu.CompilerParams(
            dimension_semantics=("arbitrary", "arbitrary"), vmem_limit_bytes=VMEM_LIMIT),
        name="in_proj",
    )(x2d, g, w_main, wdt2)


def _mixers_kernel(z_ref, xs_ref, bc_ref, u_ref, dt_ref,
                   cw_ref, cb_ref, dtb_ref, alog_ref, dskip_ref, ng_ref, ltri_ref, e2_ref,
                   shift_ref, band_ref, pw_ref, ps_ref,
                   yssd_ref, ypool_ref,
                   ext_sc, extu_sc, state_sc):
    c = pl.program_id(1)
    rows = MIX_CHUNKS * CHUNK

    @pl.when(c == 0)
    def _():
        ext_sc[0:HALO, :] = jnp.zeros((HALO, D_CONV), BF16)
        extu_sc[0:HALO, :] = jnp.zeros((HALO, POOL_WIDTH), BF16)
        state_sc[...] = jnp.zeros_like(state_sc)

    @pl.when(c > 0)
    def _():
        ext_sc[0:HALO, :] = ext_sc[rows:rows + HALO, :]
        extu_sc[0:HALO, :] = extu_sc[rows:rows + HALO, :]

    ext_sc[HALO:HALO + rows, 0:D_INNER] = xs_ref[...]
    ext_sc[HALO:HALO + rows, D_INNER:D_CONV] = bc_ref[...]
    extu_sc[HALO:HALO + rows, :] = u_ref[...]
    for ci in range(MIX_CHUNKS):
        _mixers_chunk(ci, c * MIX_CHUNKS + ci, z_ref, u_ref, dt_ref, cw_ref, cb_ref, dtb_ref, alog_ref, dskip_ref,
                      ng_ref, ltri_ref, e2_ref, shift_ref, band_ref, pw_ref, ps_ref, yssd_ref, ypool_ref,
                      ext_sc, extu_sc, state_sc)


def _mixers_chunk(ci, chunk_index, z_ref, u_ref, dt_ref, cw_ref, cb_ref, dtb_ref, alog_ref, dskip_ref,
                  ng_ref, ltri_ref, e2_ref, shift_ref, band_ref, pw_ref, ps_ref, yssd_ref, ypool_ref,
                  ext_sc, extu_sc, state_sc):
    r0 = ci * CHUNK
    rs = slice(r0, r0 + CHUNK)
    ext = ext_sc[r0:r0 + HALO + CHUNK, :]
    conv = cb_ref[...] + cw_ref[CONV_WIDTH - 1:CONV_WIDTH, :] * ext[HALO:HALO + CHUNK, :].astype(F32)
    for k in range(CONV_WIDTH - 1):
        conv = conv + cw_ref[k:k + 1, :] * jnp.dot(shift_ref[k], ext, preferred_element_type=F32)
    xc = conv * jax.nn.sigmoid(conv)
    xs = xc[:, 0:D_INNER]
    xs_b = xs.astype(BF16)

    dtv = jax.nn.softplus(dt_ref[rs, :] + dtb_ref[...])
    da = dtv * (-jnp.exp(alog_ref[...]))
    a_cum = jnp.dot(ltri_ref[...], jnp.concatenate(_split3(da), axis=0),
                    preferred_element_type=F32)
    expa = jnp.exp(a_cum)
    a_last = a_cum[CHUNK - 1:CHUNK, :]
    wst = dtv * jnp.exp(a_last - a_cum)
    a_cum_t = a_cum.T
    dt_t = dtv.T

    both = jnp.concatenate([wst, expa], axis=0)
    hi, lo = _split2(both)
    expd = jnp.dot(jnp.concatenate([hi, lo], axis=1), e2_ref[...],
                   preferred_element_type=F32)
    wst_x = expd[0:CHUNK, :]
    expa_x = expd[CHUNK:2 * CHUNK, :]
    xw_b = (xs * wst_x).astype(BF16)

    row = lax.broadcasted_iota(jnp.int32, (CHUNK, CHUNK), 0)
    col = lax.broadcasted_iota(jnp.int32, (CHUNK, CHUNK), 1)
    causal_bias = jnp.where(row >= col, 0.0, -jnp.inf).astype(F32)
    lane = lax.broadcasted_iota(jnp.int32, (CHUNK, LANES), 1)
    low_half = lane < HEAD_DIM

    y_groups = []
    for g in range(GROUPS):
        bg = xc[:, D_INNER + g * D_STATE:D_INNER + (g + 1) * D_STATE]
        cg = xc[:, D_INNER + GROUPS * D_STATE + g * D_STATE:D_INNER + GROUPS * D_STATE + (g + 1) * D_STATE]
        bg_b = bg.astype(BF16)
        cg_b = cg.astype(BF16)
        cbm = lax.dot_general(cg_b, bg_b, (((1,), (1,)), ((), ())), preferred_element_type=F32)
        gsl = slice(g * GROUP_DIM, (g + 1) * GROUP_DIM)

        prev_t = state_sc[g]
        y_off = jnp.dot(cg_b, prev_t.astype(BF16), preferred_element_type=F32) * expa_x[:, gsl]
        st_t = jnp.dot(bg.T.astype(BF16), xw_b[:, gsl], preferred_element_type=F32)
        state_sc[g] = prev_t * expa_x[CHUNK - 1:CHUNK, gsl] + st_t

        pairs = []
        for jp in range(HEADS_PER_GROUP // 2):
            ms = []
            for hh in range(2):
                h = g * HEADS_PER_GROUP + jp * 2 + hh
                seg = a_cum[:, h:h + 1] - a_cum_t[h:h + 1, :]
                dec = jnp.exp(seg + causal_bias)
                ms.append((cbm * dec * dt_t[h:h + 1, :]).astype(BF16))
            lhs = jnp.concatenate(ms, axis=1)
            c0 = g * GROUP_DIM + jp * LANES
            xp = xs_b[:, c0:c0 + LANES]
            zero = jnp.zeros_like(xp)
            rhs = jnp.concatenate([jnp.where(low_half, xp, zero), jnp.where(low_half, zero, xp)], axis=0)
            pairs.append(jnp.dot(lhs, rhs, preferred_element_type=F32))
        y_diag = jnp.concatenate(pairs, axis=1)

        yg = y_diag + y_off + dskip_ref[:, gsl] * xs[:, gsl]
        zg = z_ref[rs, gsl].astype(F32)
        yg = yg * (zg * jax.nn.sigmoid(zg))
        yg = yg * lax.rsqrt(jnp.mean(yg * yg, axis=-1, keepdims=True) + EPS) * ng_ref[:, gsl]
        y_groups.append(yg.astype(BF16))
    yssd_ref[rs, :] = jnp.concatenate(y_groups, axis=1)

    pos = chunk_index * CHUNK + lax.broadcasted_iota(jnp.int32, (CHUNK, 1), 0)
    outs = []
    for gi, w in enumerate(POOL_WINDOWS):
        psl = slice(gi * POOL_GROUP_DIM, (gi + 1) * POOL_GROUP_DIM)
        s = jnp.dot(band_ref[gi], extu_sc[r0:r0 + HALO + CHUNK, psl], preferred_element_type=F32)
        cnt = jnp.minimum(pos + 1, w).astype(F32)
        pooled = s / cnt - u_ref[rs, psl].astype(F32)
        outs.append(jnp.dot(pooled.astype(BF16), pw_ref[gi], preferred_element_type=F32))
    ypool_ref[rs, :] = (jnp.concatenate(outs, axis=1) * ps_ref[...]).astype(BF16)


def _mixers(proj, dt_raw, bsz, seq, cw, cb, dtb, alog, dskip, ng, ltri3, e2, pw, ps):
    rows = MIX_CHUNKS * CHUNK
    nc = seq // rows
    t = bsz * seq
    rowmap = lambda b, c: b * nc + c
    const2 = lambda b, c: (0, 0)
    const3 = lambda b, c: (0, 0, 0)
    trow = jnp.arange(CHUNK)[:, None] + HALO
    jcol = jnp.arange(HALO + CHUNK)[None, :]
    shifts = jnp.stack([(jcol == trow - (CONV_WIDTH - 1) + k) for k in range(CONV_WIDTH - 1)]).astype(BF16)
    bands = jnp.stack([(jcol <= trow) & (jcol > trow - w) for w in POOL_WINDOWS]).astype(BF16)
    return pl.pallas_call(
        _mixers_kernel,
        grid=(bsz, nc),
        in_specs=[
            pl.BlockSpec((rows, D_INNER), lambda b, c: (rowmap(b, c), 0)),
            pl.BlockSpec((rows, D_INNER), lambda b, c: (rowmap(b, c), 1)),
            pl.BlockSpec((rows, D_BC), lambda b, c: (rowmap(b, c), 4)),
            pl.BlockSpec((rows, POOL_WIDTH), lambda b, c: (rowmap(b, c), 5)),
            pl.BlockSpec((rows, LANES), lambda b, c: (rowmap(b, c), 0)),
            pl.BlockSpec((CONV_WIDTH, D_CONV), const2),
            pl.BlockSpec((1, D_CONV), const2),
            pl.BlockSpec((1, LANES), const2),
            pl.BlockSpec((1, LANES), const2),
            pl.BlockSpec((1, D_INNER), const2),
            pl.BlockSpec((1, D_INNER), const2),
            pl.BlockSpec((CHUNK, 3 * CHUNK), const2),
            pl.BlockSpec((2 * LANES, D_INNER), const2),
            pl.BlockSpec((CONV_WIDTH - 1, CHUNK, HALO + CHUNK), const3),
            pl.BlockSpec((len(POOL_WINDOWS), CHUNK, HALO + CHUNK), const3),
            pl.BlockSpec((len(POOL_WINDOWS), POOL_GROUP_DIM, POOL_GROUP_DIM), const3),
            pl.BlockSpec((1, POOL_WIDTH), const2),
        ],
        out_specs=[
            pl.BlockSpec((rows, D_INNER), lambda b, c: (rowmap(b, c), 0)),
            pl.BlockSpec((rows, POOL_WIDTH), lambda b, c: (rowmap(b, c), 0)),
        ],
        out_shape=[
            jax.ShapeDtypeStruct((t, D_INNER), BF16),
            jax.ShapeDtypeStruct((t, POOL_WIDTH), BF16),
        ],
        scratch_shapes=[
            pltpu.VMEM((HALO + rows, D_CONV), BF16),
            pltpu.VMEM((HALO + rows, POOL_WIDTH), BF16),
            pltpu.VMEM((GROUPS, D_STATE, GROUP_DIM), F32),
        ],
        compiler_params=pltpu.CompilerParams(
            dimension_semantics=("arbitrary", "arbitrary"), vmem_limit_bytes=VMEM_LIMIT),
        name="mixers",
    )(proj, proj, proj, proj, dt_raw, cw, cb, dtb, alog, dskip, ng, ltri3, e2, shifts, bands, pw, ps)


def _mix_route_kernel(x_ref, yssd_ref, ypool_ref, gates_ref, wso_ref, wmo_ref, fg_ref, wr_ref, br_ref,
                      x1_ref, h2_ref, topw_ref, topi_ref):
    for r0 in range(0, x_ref.shape[0], ROUTE_SUB):
        _mix_route_rows(slice(r0, r0 + ROUTE_SUB), x_ref, yssd_ref, ypool_ref, gates_ref, wso_ref, wmo_ref, fg_ref,
                        wr_ref, br_ref, x1_ref, h2_ref, topw_ref, topi_ref)


def _mix_route_rows(rs, x_ref, yssd_ref, ypool_ref, gates_ref, wso_ref, wmo_ref, fg_ref, wr_ref, br_ref,
                    x1_ref, h2_ref, topw_ref, topi_ref):
    y_ssd = jnp.dot(yssd_ref[rs, :], wso_ref[...], preferred_element_type=F32)
    gates = jax.nn.sigmoid(gates_ref[rs, :].astype(F32))
    mixed = gates[:, 0:D_MODEL] * y_ssd + gates[:, D_MODEL:2 * D_MODEL] * ypool_ref[rs, :].astype(F32)
    x1 = x_ref[rs, :] + jnp.dot(mixed.astype(BF16), wmo_ref[...], preferred_element_type=F32)
    x1_ref[rs, :] = x1
    h2 = _rms(x1, fg_ref[...])
    _store_token_tiles(h2_ref, h2, rs.start)

    hi, lo = _split2(h2)
    logits = _hilo_dot(hi, lo, wr_ref, N_EXPERTS) + br_ref[...]
    tm = logits.shape[0]
    lane = lax.broadcasted_iota(jnp.int32, (tm, LANES), 1)
    neg = jnp.float32(-jnp.inf)
    work = jnp.where(lane < N_EXPERTS, logits, neg)
    vals = []
    idxs = []
    for _ in range(TOP_K):
        m = jnp.max(work, axis=-1, keepdims=True)
        idx = jnp.min(jnp.where(work == m, lane, LANES), axis=-1, keepdims=True)
        vals.append(m)
        idxs.append(idx)
        work = jnp.where(lane == idx, neg, work)
    es = [jnp.exp(v - vals[0]) for v in vals]
    den = es[0] + es[1] + es[2] + es[3]
    topw = jnp.zeros((tm, LANES), F32)
    topi = jnp.zeros((tm, LANES), jnp.int32)
    for k in range(TOP_K):
        topw = jnp.where(lane == k, es[k] / den, topw)
        topi = jnp.where(lane == k, idxs[k], topi)
    topw_ref[rs, :] = topw
    topi_ref[:, rs] = topi.T[0:TOPI_ROWS, :]


def _mix_route(x2d, yssd, ypool, proj, wso, wmo, fg, wr2, br, tm=512):
    t = x2d.shape[0]
    const = lambda i: (0, 0)
    return pl.pallas_call(
        _mix_route_kernel,
        grid=(t // tm,),
        in_specs=[
            pl.BlockSpec((tm, D_MODEL), lambda i: (i, 0)),
            pl.BlockSpec((tm, D_INNER), lambda i: (i, 0)),
            pl.BlockSpec((tm, POOL_WIDTH), lambda i: (i, 0)),
            pl.BlockSpec((tm, 2 * D_MODEL), lambda i: (i, 3)),
            pl.BlockSpec((D_INNER, D_MODEL), const),
            pl.BlockSpec((D_MODEL, D_MODEL), const),
            pl.BlockSpec((1, D_MODEL), const),
            pl.BlockSpec((2 * D_MODEL, LANES), const),
            pl.BlockSpec((1, LANES), const),
        ],
        out_specs=[
            pl.BlockSpec((tm, D_MODEL), lambda i: (i, 0)),
            pl.BlockSpec((tm * TILE_ROWS, LANES), lambda i: (i, 0)),
            pl.BlockSpec((tm, LANES), lambda i: (i, 0)),
            pl.BlockSpec((TOPI_ROWS, tm), lambda i: (0, i)),
        ],
        out_shape=[
            jax.ShapeDtypeStruct((t, D_MODEL), F32),
            jax.ShapeDtypeStruct((t * TILE_ROWS, LANES), jnp.uint32),
            jax.ShapeDtypeStruct((t, LANES), F32),
            jax.ShapeDtypeStruct((TOPI_ROWS, t), jnp.int32),
        ],
        compiler_params=pltpu.CompilerParams(
            dimension_semantics=("arbitrary",), vmem_limit_bytes=VMEM_LIMIT),
        name="mix_route",
    )(x2d, yssd, ypool, proj, wso, wmo, fg, wr2, br)


def _moe_kernel(be_ref, nvalid_ref,
                tok0_ref, toknext_ref, dstprev_ref,
                h_hbm, wgu_ref, bgu_ref, wd_ref, bd_ref,
                slots_hbm,
                xbuf, ybuf, xb_sc, wgu_bf, wd_bf, gsem, ssem, *, nb):
    i = pl.program_id(0)
    nvalid = nvalid_ref[0]
    slot = lax.rem(i, 2)
    has_next = i + 1 < nvalid

    def tile(off, n=1):
        return pl.ds(pl.multiple_of(off, TILE_ROWS), n * TILE_ROWS)

    def gather_copy(tok, r, s):
        return pltpu.make_async_copy(h_hbm.at[tile(tok)], xbuf.at[s, tile(r * TILE_ROWS)], gsem.at[s])

    def scatter_copy(dst, r, s):
        return pltpu.make_async_copy(ybuf.at[s, tile(r * TILE_ROWS)], slots_hbm.at[tile(dst)], ssem.at[s])

    def wait_gather(s):
        pltpu.make_async_copy(h_hbm.at[tile(0, MOE_ROWS)], xbuf.at[s, tile(0, MOE_ROWS)], gsem.at[s]).wait()

    def wait_scatter(s):
        pltpu.make_async_copy(ybuf.at[s, tile(0, MOE_ROWS)], slots_hbm.at[tile(0, MOE_ROWS)], ssem.at[s]).wait()

    def scatter_whole_block(s):
        pltpu.make_async_copy(ybuf.at[s, tile(0, MOE_ROWS)], slots_hbm.at[tile(dstprev_ref[0, 0, 0], MOE_ROWS)],
                              ssem.at[s]).start()

    def issue(s, gather_ref, scatter_ref):
        for r in range(MOE_ROWS):
            if gather_ref is not None:
                gather_copy(gather_ref[0, 0, r], r, s).start()
            if scatter_ref is not None:
                scatter_copy(scatter_ref[0, 0, r], r, s).start()

    @pl.when(i == 0)
    def _():
        ybuf[...] = jnp.zeros_like(ybuf)
        issue(0, tok0_ref, None)

    for par in (0, 1):
        other = 1 - par
        mine = slot == par

        pl.when(jnp.logical_and(mine, i >= 2))(functools.partial(wait_scatter, par))
        pl.when(jnp.logical_and(mine, i < nvalid))(functools.partial(wait_gather, par))
        pl.when(jnp.logical_and(mine, jnp.logical_and(has_next, i == 0)))(
            functools.partial(issue, other, toknext_ref, None))
        pl.when(jnp.logical_and(mine, jnp.logical_and(has_next, i >= 1)))(
            functools.partial(issue, other, toknext_ref, dstprev_ref))
        pl.when(jnp.logical_and(mine, jnp.logical_and(jnp.logical_not(has_next), jnp.logical_and(i >= 1, i <= nvalid))))(
            functools.partial(issue, other, None, dstprev_ref))
        pl.when(jnp.logical_and(mine, i > nvalid))(functools.partial(scatter_whole_block, other))
        pl.when(jnp.logical_and(mine, i == nb))(functools.partial(wait_scatter, other))

    @pl.when(i < nvalid)
    def _():
        @pl.when(jnp.logical_or(i == 0, be_ref[i] != be_ref[jnp.maximum(i - 1, 0)]))
        def _():
            wgu_bf[...] = wgu_ref[0].astype(BF16)
            wd_bf[...] = wd_ref[0].astype(BF16)

        xb_sc[...] = _load_token_tiles(xbuf.at[slot], MOE_ROWS).astype(BF16)
        gu = jnp.dot(xb_sc[...], wgu_bf[...], preferred_element_type=F32) + bgu_ref[0]
        gate = jnp.minimum(gu[:, 0:D_EXPERT], SWIGLU_LIMIT)
        up = jnp.clip(gu[:, D_EXPERT:2 * D_EXPERT], -SWIGLU_LIMIT, SWIGLU_LIMIT)
        act = (up + 1.0) * gate * jax.nn.sigmoid(SWIGLU_ALPHA * gate)
        y = jnp.dot(act.astype(BF16), wd_bf[...], preferred_element_type=F32) + bd_ref[0]
        _store_token_tiles(ybuf.at[slot], y)


def _moe(block_expert, nvalid, row_token3, row_dest3, h2, wgu, bgu3, wd, bd3, n_slot_rows):
    nb = block_expert.shape[0]
    expert_of = lambda i, be: be[jnp.minimum(i, nb - 1)]
    grid_spec = pltpu.PrefetchScalarGridSpec(
        num_scalar_prefetch=2,
        grid=(nb + 1,),
        in_specs=[
            pl.BlockSpec((1, 1, MOE_ROWS), lambda i, be, nv: (0, 0, 0), memory_space=pltpu.SMEM),
            pl.BlockSpec((1, 1, MOE_ROWS), lambda i, be, nv: (jnp.minimum(i + 1, nb - 1), 0, 0),
                         memory_space=pltpu.SMEM),
            pl.BlockSpec((1, 1, MOE_ROWS), lambda i, be, nv: (jnp.clip(i - 1, 0, nb - 1), 0, 0),
                         memory_space=pltpu.SMEM),
            pl.BlockSpec(memory_space=pl.ANY),
            pl.BlockSpec((1, D_MODEL, 2 * D_EXPERT), lambda i, be, nv: (expert_of(i, be), 0, 0)),
            pl.BlockSpec((1, 1, 2 * D_EXPERT), lambda i, be, nv: (expert_of(i, be), 0, 0)),
            pl.BlockSpec((1, D_EXPERT, D_MODEL), lambda i, be, nv: (expert_of(i, be), 0, 0)),
            pl.BlockSpec((1, 1, D_MODEL), lambda i, be, nv: (expert_of(i, be), 0, 0)),
        ],
        out_specs=pl.BlockSpec(memory_space=pl.ANY),
        scratch_shapes=[
            pltpu.VMEM((2, MOE_ROWS * TILE_ROWS, LANES), jnp.uint32),
            pltpu.VMEM((2, MOE_ROWS * TILE_ROWS, LANES), jnp.uint32),
            pltpu.VMEM((MOE_ROWS, D_MODEL), BF16),
            pltpu.VMEM((D_MODEL, 2 * D_EXPERT), BF16),
            pltpu.VMEM((D_EXPERT, D_MODEL), BF16),
            pltpu.SemaphoreType.DMA((2,)),
            pltpu.SemaphoreType.DMA((2,)),
        ],
    )
    return pl.pallas_call(
        functools.partial(_moe_kernel, nb=nb),
        grid_spec=grid_spec,
        out_shape=jax.ShapeDtypeStruct((n_slot_rows * TILE_ROWS, LANES), jnp.uint32),
        compiler_params=pltpu.CompilerParams(
            dimension_semantics=("arbitrary",), vmem_limit_bytes=VMEM_LIMIT),
        name="moe_experts",
    )(block_expert, nvalid, row_token3, row_token3, row_dest3, h2, wgu, bgu3, wd, bd3)


def _combine_kernel(x1_ref, s0_ref, s1_ref, s2_ref, s3_ref, topw_ref, p_ref, pg_ref, wpg_ref, wpp_ref, fg_ref,
                    out_ref):
    x2 = x1_ref[...]
    topw = topw_ref[...]
    for k, s_ref in enumerate((s0_ref, s1_ref, s2_ref, s3_ref)):
        x2 = x2 + _load_token_tiles(s_ref, x2.shape[0]) * topw[:, k:k + 1]
    n = _rms(x2, pg_ref[...])
    gate = jax.nn.sigmoid(jnp.dot(n.astype(BF16), wpg_ref[...], preferred_element_type=F32))
    pp = jnp.dot(p_ref[...].astype(BF16), wpp_ref[...], preferred_element_type=F32)
    x3 = x2 + gate * pp
    out_ref[...] = _rms(x3, fg_ref[...])


def _combine(x1, slots, topw, p2d, pg, wpg, wpp, fg, tm=512):
    t = x1.shape[0]
    nt = t // tm
    const = lambda i: (0, 0)
    slot_specs = [pl.BlockSpec((tm * TILE_ROWS, LANES), functools.partial(lambda k, i: (k * nt + i, 0), k))
                  for k in range(TOP_K)]
    return pl.pallas_call(
        _combine_kernel,
        grid=(nt,),
        in_specs=[
            pl.BlockSpec((tm, D_MODEL), lambda i: (i, 0)),
            *slot_specs,
            pl.BlockSpec((tm, LANES), lambda i: (i, 0)),
            pl.BlockSpec((tm, D_PLE), lambda i: (i, 0)),
            pl.BlockSpec((1, D_MODEL), const),
            pl.BlockSpec((D_MODEL, D_MODEL), const),
            pl.BlockSpec((D_PLE, D_MODEL), const),
            pl.BlockSpec((1, D_MODEL), const),
        ],
        out_specs=pl.BlockSpec((tm, D_MODEL), lambda i: (i, 0)),
        out_shape=jax.ShapeDtypeStruct((t, D_MODEL), F32),
        compiler_params=pltpu.CompilerParams(
            dimension_semantics=("arbitrary",), vmem_limit_bytes=VMEM_LIMIT),
        name="combine_ple",
    )(x1, slots, slots, slots, slots, topw, p2d, pg, wpg, wpp, fg)


def _routing_tables(top_idx_t, n_tok):
    n_assign = n_tok * TOP_K
    expert_flat = top_idx_t.reshape(-1)
    assert N_EXPERTS * n_assign <= 2 ** 31
    keys = jnp.sort(expert_flat * n_assign + jnp.arange(n_assign, dtype=jnp.int32))
    order = keys % n_assign
    counts = jnp.bincount(expert_flat, length=N_EXPERTS).astype(jnp.int32)
    start = jnp.cumsum(counts) - counts
    padded = (counts + MOE_ROWS - 1) // MOE_ROWS * MOE_ROWS
    pend = jnp.cumsum(padded)
    pstart = pend - padded
    n_rows = n_assign + N_EXPERTS * MOE_ROWS
    n_blocks = n_rows // MOE_ROWS
    block_start = jnp.arange(n_blocks, dtype=jnp.int32) * MOE_ROWS
    block_expert = jnp.minimum(jnp.sum(block_start[:, None] >= pend[None, :], axis=1),
                               N_EXPERTS - 1).astype(jnp.int32)
    nvalid = (pend[-1] // MOE_ROWS).astype(jnp.int32).reshape(1)
    is_e = block_expert[:, None] == jnp.arange(N_EXPERTS, dtype=jnp.int32)[None, :]
    per_block = lambda v: jnp.sum(jnp.where(is_e, v[None, :], 0), axis=1)
    nreal = jnp.clip(per_block(pstart + counts) - block_start, 0, MOE_ROWS).astype(jnp.int32)
    sorted_pos = (block_start + per_block(start - pstart))[:, None] + jnp.arange(MOE_ROWS, dtype=jnp.int32)[None, :]
    assign = order[jnp.clip(sorted_pos, 0, n_assign - 1)]
    row_token = assign % n_tok
    is_real = jnp.arange(MOE_ROWS, dtype=jnp.int32)[None, :] < nreal[:, None]
    row_q = block_start[:, None] + jnp.arange(MOE_ROWS, dtype=jnp.int32)[None, :]
    spare = n_assign + row_q - per_block(start + counts)[:, None]
    row_dest = jnp.where(is_real, assign, spare)
    return (block_expert, nvalid, (row_token * TILE_ROWS).reshape(n_blocks, 1, MOE_ROWS),
            (row_dest * TILE_ROWS).reshape(n_blocks, 1, MOE_ROWS), n_rows)


def _layer(x2d, p2d, bsz, seq, mix_norm_g, w_in, conv_w, conv_b, dt_bias, a_log, d_skip, ssd_norm_g,
           w_ssd_out, pool_w, pool_scale, w_mix_out, ffn_norm_g, w_router, b_router,
           w_gate_up, b_gate_up, w_down, b_down, ple_norm_g, w_ple_gate, w_ple_proj, out_g):
    n_tok = x2d.shape[0]
    dt0 = D_INNER + D_CONV
    w_in_b = w_in.astype(BF16)
    w_main = jnp.concatenate([w_in_b[:, :dt0], w_in_b[:, dt0 + HEADS:]], axis=1)
    proj, dt_raw = _in_proj(x2d, mix_norm_g[None, :], w_main, _hilo_weight(w_in[:, dt0:dt0 + HEADS]))

    pad_h = lambda v: jnp.pad(v, (0, LANES - HEADS))[None, :]
    ltri = (jnp.arange(CHUNK)[:, None] >= jnp.arange(CHUNK)[None, :]).astype(BF16)
    ltri3 = jnp.concatenate([ltri, ltri, ltri], axis=1)
    e1 = (jnp.arange(LANES)[:, None] == (jnp.arange(D_INNER) // HEAD_DIM)[None, :]).astype(BF16)
    e2 = jnp.concatenate([e1, e1], axis=0)
    yssd, ypool = _mixers(
        proj, dt_raw, bsz, seq, conv_w, conv_b[None, :], pad_h(dt_bias), pad_h(a_log),
        jnp.repeat(d_skip, HEAD_DIM)[None, :], ssd_norm_g[None, :], ltri3, e2,
        pool_w.astype(BF16), pool_scale[None, :])

    br = jnp.pad(b_router, (0, LANES - N_EXPERTS))[None, :]
    x1, h2, topw, topi = _mix_route(x2d, yssd, ypool, proj, w_ssd_out.astype(BF16),
                                    w_mix_out.astype(BF16), ffn_norm_g[None, :], _hilo_weight(w_router), br)

    block_expert, nvalid, row_token3, row_dest3, n_slot_rows = _routing_tables(topi[:TOP_K], n_tok)
    slots = _moe(block_expert, nvalid, row_token3, row_dest3, h2, w_gate_up, b_gate_up[:, None, :],
                 w_down, b_down[:, None, :], n_slot_rows)

    return _combine(x1, slots, topw, p2d, ple_norm_g[None, :], w_ple_gate.astype(BF16),
                    w_ple_proj.astype(BF16), out_g[None, :])


def kernel(x, p, mix_norm_g, w_in, conv_w, conv_b, dt_bias, a_log, d_skip, ssd_norm_g, w_ssd_out, pool_w,
           pool_scale, w_mix_out, ffn_norm_g, w_router, b_router, w_gate_up, b_gate_up, w_down, b_down,
           ple_norm_g, w_ple_gate, w_ple_proj, final_norm_g):
    bsz, seq, d = x.shape
    depth = p.shape[0]
    assert depth == 1 and d == D_MODEL and p.shape[-1] == D_PLE
    assert seq % (MIX_CHUNKS * CHUNK) == 0 and (bsz * seq) % 1024 == 0
    x2d = x.reshape(bsz * seq, d)
    out = _layer(x2d, p[0].reshape(bsz * seq, D_PLE), bsz, seq, mix_norm_g[0], w_in[0], conv_w[0], conv_b[0],
                 dt_bias[0], a_log[0], d_skip[0], ssd_norm_g[0], w_ssd_out[0], pool_w[0], pool_scale[0],
                 w_mix_out[0], ffn_norm_g[0], w_router[0], b_router[0], w_gate_up[0], b_gate_up[0],
                 w_down[0], b_down[0], ple_norm_g[0], w_ple_gate[0], w_ple_proj[0], final_norm_g)
    return out.reshape(bsz, seq, d)
```

```python
import functools

import jax
import jax.numpy as jnp
from jax import lax
from jax.experimental import pallas as pl
from jax.experimental.pallas import tpu as pltpu

F32 = jnp.float32
BF16 = jnp.bfloat16

D_MODEL = 1024
D_INNER = 2048
HEAD_DIM = 64
HEADS = 32
GROUPS = 4
HEADS_PER_GROUP = HEADS // GROUPS
GROUP_DIM = D_INNER // GROUPS
D_STATE = 128
CONV_WIDTH = 4
CHUNK = 128
D_BC = 2 * GROUPS * D_STATE
D_CONV = D_INNER + D_BC
POOL_WIDTH = D_MODEL
POOL_WINDOWS = (2, 4, 8, 16)
POOL_GROUP_DIM = POOL_WIDTH // len(POOL_WINDOWS)
N_EXPERTS = 32
TOP_K = 4
D_EXPERT = D_MODEL
SWIGLU_LIMIT = 7.0
SWIGLU_ALPHA = 1.702
D_PLE = 256
EPS = 1e-6

LANES = 128
TILE_ROWS = D_MODEL // (2 * LANES)
TOPI_ROWS = 8
ROUTE_SUB = 256
MIX_CHUNKS = 4
HALO = 16
D_PROJ = D_INNER + D_CONV + POOL_WIDTH + 2 * D_MODEL
MOE_ROWS = 256
VMEM_LIMIT = 56 * 1024 * 1024


def _split2(v):
    hi = v.astype(BF16)
    lo = (v - hi.astype(F32)).astype(BF16)
    return hi, lo


def _split3(v):
    hi = v.astype(BF16)
    r = v - hi.astype(F32)
    mid = r.astype(BF16)
    lo = (r - mid.astype(F32)).astype(BF16)
    return hi, mid, lo


def _hilo_weight(w):
    n = w.shape[1]
    hi, lo = _split2(w)
    top = jnp.pad(jnp.concatenate([hi, lo], axis=1), ((0, 0), (0, LANES - 2 * n)))
    bottom = jnp.pad(hi, ((0, 0), (0, LANES - n)))
    return jnp.concatenate([top, bottom], axis=0)


def _hilo_dot(hi, lo, w2_ref, n):
    k = hi.shape[1]
    a = jnp.dot(hi, w2_ref[0:k, :], preferred_element_type=F32)
    b = jnp.dot(lo, w2_ref[k:2 * k, :], preferred_element_type=F32)
    return a + pltpu.roll(a, LANES - n, axis=1) + b


def _rms(x, g):
    return x * lax.rsqrt(jnp.mean(x * x, axis=-1, keepdims=True) + EPS) * g


def _store_token_tiles(ref2d, val, row0=0):
    rows = val.shape[0]
    for j in range(TILE_ROWS):
        c0 = 2 * j * LANES
        hi = lax.bitcast_convert_type(val[:, c0:c0 + LANES].astype(BF16).astype(F32), jnp.uint32)
        lo = lax.bitcast_convert_type(val[:, c0 + LANES:c0 + 2 * LANES].astype(BF16).astype(F32), jnp.uint32)
        ref2d[pl.ds(row0 * TILE_ROWS + j, rows, stride=TILE_ROWS), :] = hi | lax.shift_right_logical(lo, jnp.uint32(16))


def _load_token_tiles(ref2d, rows, row0=0):
    pieces = []
    for j in range(TILE_ROWS):
        words = ref2d[pl.ds(row0 * TILE_ROWS + j, rows, stride=TILE_ROWS), :]
        pieces.append(lax.bitcast_convert_type(words & jnp.uint32(0xFFFF0000), F32))
        pieces.append(lax.bitcast_convert_type(lax.shift_left(words, jnp.uint32(16)), F32))
    return jnp.concatenate(pieces, axis=1)


def _in_proj_kernel(x_ref, g_ref, w_ref, wdt_ref, proj_ref, dt_ref, h_sc):
    @pl.when(pl.program_id(1) == 0)
    def _():
        h = _rms(x_ref[...], g_ref[...])
        hi, lo = _split2(h)
        h_sc[...] = hi
        dt_ref[...] = _hilo_dot(hi, lo, wdt_ref, HEADS)

    proj_ref[...] = jnp.dot(h_sc[...], w_ref[...], preferred_element_type=F32).astype(BF16)


def _in_proj(x2d, g, w_main, wdt2, tm=1024, tn=2048):
    t = x2d.shape[0]
    return pl.pallas_call(
        _in_proj_kernel,
        grid=(t // tm, D_PROJ // tn),
        in_specs=[
            pl.BlockSpec((tm, D_MODEL), lambda i, j: (i, 0)),
            pl.BlockSpec((1, D_MODEL), lambda i, j: (0, 0)),
            pl.BlockSpec((D_MODEL, tn), lambda i, j: (0, j)),
            pl.BlockSpec((2 * D_MODEL, LANES), lambda i, j: (0, 0)),
        ],
        out_specs=[
            pl.BlockSpec((tm, tn), lambda i, j: (i, j)),
            pl.BlockSpec((tm, LANES), lambda i, j: (i, 0)),
        ],
        out_shape=[
            jax.ShapeDtypeStruct((t, D_PROJ), BF16),
            jax.ShapeDtypeStruct((t, LANES), F32),
        ],
        scratch_shapes=[pltpu.VMEM((tm, D_MODEL), BF16)],
        compiler_params=pltpu.CompilerParams(
            dimension_semantics=("arbitrary", "arbitrary"), vmem_limit_bytes=VMEM_LIMIT),
        name="in_proj",
    )(x2d, g, w_main, wdt2)


def _mixers_kernel(z_ref, xs_ref, bc_ref, u_ref, dt_ref,
                   cw_ref, cb_ref, dtb_ref, alog_ref, dskip_ref, ng_ref, ltri_ref, e2_ref,
                   shift_ref, band_ref, pw_ref, ps_ref,
                   yssd_ref, ypool_ref,
                   ext_sc, extu_sc, state_sc):
    c = pl.program_id(1)
    rows = MIX_CHUNKS * CHUNK

    @pl.when(c == 0)
    def _():
        ext_sc[0:HALO, :] = jnp.zeros((HALO, D_CONV), BF16)
        extu_sc[0:HALO, :] = jnp.zeros((HALO, POOL_WIDTH), BF16)
        state_sc[...] = jnp.zeros_like(state_sc)

    @pl.when(c > 0)
    def _():
        ext_sc[0:HALO, :] = ext_sc[rows:rows + HALO, :]
        extu_sc[0:HALO, :] = extu_sc[rows:rows + HALO, :]

    ext_sc[HALO:HALO + rows, 0:D_INNER] = xs_ref[...]
    ext_sc[HALO:HALO + rows, D_INNER:D_CONV] = bc_ref[...]
    extu_sc[HALO:HALO + rows, :] = u_ref[...]
    for ci in range(MIX_CHUNKS):
        _mixers_chunk(ci, c * MIX_CHUNKS + ci, z_ref, u_ref, dt_ref, cw_ref, cb_ref, dtb_ref, alog_ref, dskip_ref,
                      ng_ref, ltri_ref, e2_ref, shift_ref, band_ref, pw_ref, ps_ref, yssd_ref, ypool_ref,
                      ext_sc, extu_sc, state_sc)


def _mixers_chunk(ci, chunk_index, z_ref, u_ref, dt_ref, cw_ref, cb_ref, dtb_ref, alog_ref, dskip_ref,
                  ng_ref, ltri_ref, e2_ref, shift_ref, band_ref, pw_ref, ps_ref, yssd_ref, ypool_ref,
                  ext_sc, extu_sc, state_sc):
    r0 = ci * CHUNK
    rs = slice(r0, r0 + CHUNK)
    ext = ext_sc[r0:r0 + HALO + CHUNK, :]
    conv = cb_ref[...] + cw_ref[CONV_WIDTH - 1:CONV_WIDTH, :] * ext[HALO:HALO + CHUNK, :].astype(F32)
    for k in range(CONV_WIDTH - 1):
        conv = conv + cw_ref[k:k + 1, :] * jnp.dot(shift_ref[k], ext, preferred_element_type=F32)
    xc = conv * jax.nn.sigmoid(conv)
    xs = xc[:, 0:D_INNER]
    xs_b = xs.astype(BF16)

    dtv = jax.nn.softplus(dt_ref[rs, :] + dtb_ref[...])
    da = dtv * (-jnp.exp(alog_ref[...]))
    a_cum = jnp.dot(ltri_ref[...], jnp.concatenate(_split3(da), axis=0),
                    preferred_element_type=F32)
    expa = jnp.exp(a_cum)
    a_last = a_cum[CHUNK - 1:CHUNK, :]
    wst = dtv * jnp.exp(a_last - a_cum)
    a_cum_t = a_cum.T
    dt_t = dtv.T

    both = jnp.concatenate([wst, expa], axis=0)
    hi, lo = _split2(both)
    expd = jnp.dot(jnp.concatenate([hi, lo], axis=1), e2_ref[...],
                   preferred_element_type=F32)
    wst_x = expd[0:CHUNK, :]
    expa_x = expd[CHUNK:2 * CHUNK, :]
    xw_b = (xs * wst_x).astype(BF16)

    row = lax.broadcasted_iota(jnp.int32, (CHUNK, CHUNK), 0)
    col = lax.broadcasted_iota(jnp.int32, (CHUNK, CHUNK), 1)
    causal_bias = jnp.where(row >= col, 0.0, -jnp.inf).astype(F32)
    lane = lax.broadcasted_iota(jnp.int32, (CHUNK, LANES), 1)
    low_half = lane < HEAD_DIM

    y_groups = []
    for g in range(GROUPS):
        bg = xc[:, D_INNER + g * D_STATE:D_INNER + (g + 1) * D_STATE]
        cg = xc[:, D_INNER + GROUPS * D_STATE + g * D_STATE:D_INNER + GROUPS * D_STATE + (g + 1) * D_STATE]
        bg_b = bg.astype(BF16)
        cg_b = cg.astype(BF16)
        cbm = lax.dot_general(cg_b, bg_b, (((1,), (1,)), ((), ())), preferred_element_type=F32)
        gsl = slice(g * GROUP_DIM, (g + 1) * GROUP_DIM)

        prev_t = state_sc[g]
        y_off = jnp.dot(cg_b, prev_t.astype(BF16), preferred_element_type=F32) * expa_x[:, gsl]
        st_t = jnp.dot(bg.T.astype(BF16), xw_b[:, gsl], preferred_element_type=F32)
        state_sc[g] = prev_t * expa_x[CHUNK - 1:CHUNK, gsl] + st_t

        pairs = []
        for jp in range(HEADS_PER_GROUP // 2):
            ms = []
            for hh in range(2):
                h = g * HEADS_PER_GROUP + jp * 2 + hh
                seg = a_cum[:, h:h + 1] - a_cum_t[h:h + 1, :]
                dec = jnp.exp(seg + causal_bias)
                ms.append((cbm * dec * dt_t[h:h + 1, :]).astype(BF16))
            lhs = jnp.concatenate(ms, axis=1)
            c0 = g * GROUP_DIM + jp * LANES
            xp = xs_b[:, c0:c0 + LANES]
            zero = jnp.zeros_like(xp)
            rhs = jnp.concatenate([jnp.where(low_half, xp, zero), jnp.where(low_half, zero, xp)], axis=0)
            pairs.append(jnp.dot(lhs, rhs, preferred_element_type=F32))
        y_diag = jnp.concatenate(pairs, axis=1)

        yg = y_diag + y_off + dskip_ref[:, gsl] * xs[:, gsl]
        zg = z_ref[rs, gsl].astype(F32)
        yg = yg * (zg * jax.nn.sigmoid(zg))
        yg = yg * lax.rsqrt(jnp.mean(yg * yg, axis=-1, keepdims=True) + EPS) * ng_ref[:, gsl]
        y_groups.append(yg.astype(BF16))
    yssd_ref[rs, :] = jnp.concatenate(y_groups, axis=1)

    pos = chunk_index * CHUNK + lax.broadcasted_iota(jnp.int32, (CHUNK, 1), 0)
    outs = []
    for gi, w in enumerate(POOL_WINDOWS):
        psl = slice(gi * POOL_GROUP_DIM, (gi + 1) * POOL_GROUP_DIM)
        s = jnp.dot(band_ref[gi], extu_sc[r0:r0 + HALO + CHUNK, psl], preferred_element_type=F32)
        cnt = jnp.minimum(pos + 1, w).astype(F32)
        pooled = s / cnt - u_ref[rs, psl].astype(F32)
        outs.append(jnp.dot(pooled.astype(BF16), pw_ref[gi], preferred_element_type=F32))
    ypool_ref[rs, :] = (jnp.concatenate(outs, axis=1) * ps_ref[...]).astype(BF16)


def _mixers(proj, dt_raw, bsz, seq, cw, cb, dtb, alog, dskip, ng, ltri3, e2, pw, ps):
    rows = MIX_CHUNKS * CHUNK
    nc = seq // rows
    t = bsz * seq
    rowmap = lambda b, c: b * nc + c
    const2 = lambda b, c: (0, 0)
    const3 = lambda b, c: (0, 0, 0)
    trow = jnp.arange(CHUNK)[:, None] + HALO
    jcol = jnp.arange(HALO + CHUNK)[None, :]
    shifts = jnp.stack([(jcol == trow - (CONV_WIDTH - 1) + k) for k in range(CONV_WIDTH - 1)]).astype(BF16)
    bands = jnp.stack([(jcol <= trow) & (jcol > trow - w) for w in POOL_WINDOWS]).astype(BF16)
    return pl.pallas_call(
        _mixers_kernel,
        grid=(bsz, nc),
        in_specs=[
            pl.BlockSpec((rows, D_INNER), lambda b, c: (rowmap(b, c), 0)),
            pl.BlockSpec((rows, D_INNER), lambda b, c: (rowmap(b, c), 1)),
            pl.BlockSpec((rows, D_BC), lambda b, c: (rowmap(b, c), 4)),
            pl.BlockSpec((rows, POOL_WIDTH), lambda b, c: (rowmap(b, c), 5)),
            pl.BlockSpec((rows, LANES), lambda b, c: (rowmap(b, c), 0)),
            pl.BlockSpec((CONV_WIDTH, D_CONV), const2),
            pl.BlockSpec((1, D_CONV), const2),
            pl.BlockSpec((1, LANES), const2),
            pl.BlockSpec((1, LANES), const2),
            pl.BlockSpec((1, D_INNER), const2),
            pl.BlockSpec((1, D_INNER), const2),
            pl.BlockSpec((CHUNK, 3 * CHUNK), const2),
            pl.BlockSpec((2 * LANES, D_INNER), const2),
            pl.BlockSpec((CONV_WIDTH - 1, CHUNK, HALO + CHUNK), const3),
            pl.BlockSpec((len(POOL_WINDOWS), CHUNK, HALO + CHUNK), const3),
            pl.BlockSpec((len(POOL_WINDOWS), POOL_GROUP_DIM, POOL_GROUP_DIM), const3),
            pl.BlockSpec((1, POOL_WIDTH), const2),
        ],
        out_specs=[
            pl.BlockSpec((rows, D_INNER), lambda b, c: (rowmap(b, c), 0)),
            pl.BlockSpec((rows, POOL_WIDTH), lambda b, c: (rowmap(b, c), 0)),
        ],
        out_shape=[
            jax.ShapeDtypeStruct((t, D_INNER), BF16),
            jax.ShapeDtypeStruct((t, POOL_WIDTH), BF16),
        ],
        scratch_shapes=[
            pltpu.VMEM((HALO + rows, D_CONV), BF16),
            pltpu.VMEM((HALO + rows, POOL_WIDTH), BF16),
            pltpu.VMEM((GROUPS, D_STATE, GROUP_DIM), F32),
        ],
        compiler_params=pltpu.CompilerParams(
            dimension_semantics=("arbitrary", "arbitrary"), vmem_limit_bytes=VMEM_LIMIT),
        name="mixers",
    )(proj, proj, proj, proj, dt_raw, cw, cb, dtb, alog, dskip, ng, ltri3, e2, shifts, bands, pw, ps)


def _mix_route_kernel(x_ref, yssd_ref, ypool_ref, gates_ref, wso_ref, wmo_ref, fg_ref, wr_ref, br_ref,
                      x1_ref, h2_ref, topw_ref, topi_ref):
    for r0 in range(0, x_ref.shape[0], ROUTE_SUB):
        _mix_route_rows(slice(r0, r0 + ROUTE_SUB), x_ref, yssd_ref, ypool_ref, gates_ref, wso_ref, wmo_ref, fg_ref,
                        wr_ref, br_ref, x1_ref, h2_ref, topw_ref, topi_ref)


def _mix_route_rows(rs, x_ref, yssd_ref, ypool_ref, gates_ref, wso_ref, wmo_ref, fg_ref, wr_ref, br_ref,
                    x1_ref, h2_ref, topw_ref, topi_ref):
    y_ssd = jnp.dot(yssd_ref[rs, :], wso_ref[...], preferred_element_type=F32)
    gates = jax.nn.sigmoid(gates_ref[rs, :].astype(F32))
    mixed = gates[:, 0:D_MODEL] * y_ssd + gates[:, D_MODEL:2 * D_MODEL] * ypool_ref[rs, :].astype(F32)
    x1 = x_ref[rs, :] + jnp.dot(mixed.astype(BF16), wmo_ref[...], preferred_element_type=F32)
    x1_ref[rs, :] = x1
    h2 = _rms(x1, fg_ref[...])
    _store_token_tiles(h2_ref, h2, rs.start)

    hi, lo = _split2(h2)
    logits = _hilo_dot(hi, lo, wr_ref, N_EXPERTS) + br_ref[...]
    tm = logits.shape[0]
    lane = lax.broadcasted_iota(jnp.int32, (tm, LANES), 1)
    neg = jnp.float32(-jnp.inf)
    work = jnp.where(lane < N_EXPERTS, logits, neg)
    vals = []
    idxs = []
    for _ in range(TOP_K):
        m = jnp.max(work, axis=-1, keepdims=True)
        idx = jnp.min(jnp.where(work == m, lane, LANES), axis=-1, keepdims=True)
        vals.append(m)
        idxs.append(idx)
        work = jnp.where(lane == idx, neg, work)
    es = [jnp.exp(v - vals[0]) for v in vals]
    den = es[0] + es[1] + es[2] + es[3]
    topw = jnp.zeros((tm, LANES), F32)
    topi = jnp.zeros((tm, LANES), jnp.int32)
    for k in range(TOP_K):
        topw = jnp.where(lane == k, es[k] / den, topw)
        topi = jnp.where(lane == k, idxs[k], topi)
    topw_ref[rs, :] = topw
    topi_ref[:, rs] = topi.T[0:TOPI_ROWS, :]


def _mix_route(x2d, yssd, ypool, proj, wso, wmo, fg, wr2, br, tm=512):
    t = x2d.shape[0]
    const = lambda i: (0, 0)
    return pl.pallas_call(
        _mix_route_kernel,
        grid=(t // tm,),
        in_specs=[
            pl.BlockSpec((tm, D_MODEL), lambda i: (i, 0)),
            pl.BlockSpec((tm, D_INNER), lambda i: (i, 0)),
            pl.BlockSpec((tm, POOL_WIDTH), lambda i: (i, 0)),
            pl.BlockSpec((tm, 2 * D_MODEL), lambda i: (i, 3)),
            pl.BlockSpec((D_INNER, D_MODEL), const),
            pl.BlockSpec((D_MODEL, D_MODEL), const),
            pl.BlockSpec((1, D_MODEL), const),
            pl.BlockSpec((2 * D_MODEL, LANES), const),
            pl.BlockSpec((1, LANES), const),
        ],
        out_specs=[
            pl.BlockSpec((tm, D_MODEL), lambda i: (i, 0)),
            pl.BlockSpec((tm * TILE_ROWS, LANES), lambda i: (i, 0)),
            pl.BlockSpec((tm, LANES), lambda i: (i, 0)),
            pl.BlockSpec((TOPI_ROWS, tm), lambda i: (0, i)),
        ],
        out_shape=[
            jax.ShapeDtypeStruct((t, D_MODEL), F32),
            jax.ShapeDtypeStruct((t * TILE_ROWS, LANES), jnp.uint32),
            jax.ShapeDtypeStruct((t, LANES), F32),
            jax.ShapeDtypeStruct((TOPI_ROWS, t), jnp.int32),
        ],
        compiler_params=pltpu.CompilerParams(
            dimension_semantics=("arbitrary",), vmem_limit_bytes=VMEM_LIMIT),
        name="mix_route",
    )(x2d, yssd, ypool, proj, wso, wmo, fg, wr2, br)


def _moe_kernel(be_ref, nvalid_ref,
                tok0_ref, toknext_ref, dstprev_ref,
                h_hbm, wgu_ref, bgu_ref, wd_ref, bd_ref,
                slots_hbm,
                xbuf, ybuf, xb_sc, wgu_bf, wd_bf, gsem, ssem, *, nb):
    i = pl.program_id(0)
    nvalid = nvalid_ref[0]
    slot = lax.rem(i, 2)
    has_next = i + 1 < nvalid

    def tile(off, n=1):
        return pl.ds(pl.multiple_of(off, TILE_ROWS), n * TILE_ROWS)

    def gather_copy(tok, r, s):
        return pltpu.make_async_copy(h_hbm.at[tile(tok)], xbuf.at[s, tile(r * TILE_ROWS)], gsem.at[s])

    def scatter_copy(dst, r, s):
        return pltpu.make_async_copy(ybuf.at[s, tile(r * TILE_ROWS)], slots_hbm.at[tile(dst)], ssem.at[s])

    def wait_gather(s):
        pltpu.make_async_copy(h_hbm.at[tile(0, MOE_ROWS)], xbuf.at[s, tile(0, MOE_ROWS)], gsem.at[s]).wait()

    def wait_scatter(s):
        pltpu.make_async_copy(ybuf.at[s, tile(0, MOE_ROWS)], slots_hbm.at[tile(0, MOE_ROWS)], ssem.at[s]).wait()

    def scatter_whole_block(s):
        pltpu.make_async_copy(ybuf.at[s, tile(0, MOE_ROWS)], slots_hbm.at[tile(dstprev_ref[0, 0, 0], MOE_ROWS)],
                              ssem.at[s]).start()

    def issue(s, gather_ref, scatter_ref):
        for r in range(MOE_ROWS):
            if gather_ref is not None:
                gather_copy(gather_ref[0, 0, r], r, s).start()
            if scatter_ref is not None:
                scatter_copy(scatter_ref[0, 0, r], r, s).start()

    @pl.when(i == 0)
    def _():
        ybuf[...] = jnp.zeros_like(ybuf)
        issue(0, tok0_ref, None)

    for par in (0, 1):
        other = 1 - par
        mine = slot == par

        pl.when(jnp.logical_and(mine, i >= 2))(functools.partial(wait_scatter, par))
        pl.when(jnp.logical_and(mine, i < nvalid))(functools.partial(wait_gather, par))
        pl.when(jnp.logical_and(mine, jnp.logical_and(has_next, i == 0)))(
            functools.partial(issue, other, toknext_ref, None))
        pl.when(jnp.logical_and(mine, jnp.logical_and(has_next, i >= 1)))(
            functools.partial(issue, other, toknext_ref, dstprev_ref))
        pl.when(jnp.logical_and(mine, jnp.logical_and(jnp.logical_not(has_next), jnp.logical_and(i >= 1, i <= nvalid))))(
            functools.partial(issue, other, None, dstprev_ref))
        pl.when(jnp.logical_and(mine, i > nvalid))(functools.partial(scatter_whole_block, other))
        pl.when(jnp.logical_and(mine, i == nb))(functools.partial(wait_scatter, other))

    @pl.when(i < nvalid)
    def _():
        @pl.when(jnp.logical_or(i == 0, be_ref[i] != be_ref[jnp.maximum(i - 1, 0)]))
        def _():
            wgu_bf[...] = wgu_ref[0].astype(BF16)
            wd_bf[...] = wd_ref[0].astype(BF16)

        xb_sc[...] = _load_token_tiles(xbuf.at[slot], MOE_ROWS).astype(BF16)
        gu = jnp.dot(xb_sc[...], wgu_bf[...], preferred_element_type=F32) + bgu_ref[0]
        gate = jnp.minimum(gu[:, 0:D_EXPERT], SWIGLU_LIMIT)
        up = jnp.clip(gu[:, D_EXPERT:2 * D_EXPERT], -SWIGLU_LIMIT, SWIGLU_LIMIT)
        act = (up + 1.0) * gate * jax.nn.sigmoid(SWIGLU_ALPHA * gate)
        y = jnp.dot(act.astype(BF16), wd_bf[...], preferred_element_type=F32) + bd_ref[0]
        _store_token_tiles(ybuf.at[slot], y)


def _moe(block_expert, nvalid, row_token3, row_dest3, h2, wgu, bgu3, wd, bd3, n_slot_rows):
    nb = block_expert.shape[0]
    expert_of = lambda i, be: be[jnp.minimum(i, nb - 1)]
    grid_spec = pltpu.PrefetchScalarGridSpec(
        num_scalar_prefetch=2,
        grid=(nb + 1,),
        in_specs=[
            pl.BlockSpec((1, 1, MOE_ROWS), lambda i, be, nv: (0, 0, 0), memory_space=pltpu.SMEM),
            pl.BlockSpec((1, 1, MOE_ROWS), lambda i, be, nv: (jnp.minimum(i + 1, nb - 1), 0, 0),
                         memory_space=pltpu.SMEM),
            pl.BlockSpec((1, 1, MOE_ROWS), lambda i, be, nv: (jnp.clip(i - 1, 0, nb - 1), 0, 0),
                         memory_space=pltpu.SMEM),
            pl.BlockSpec(memory_space=pl.ANY),
            pl.BlockSpec((1, D_MODEL, 2 * D_EXPERT), lambda i, be, nv: (expert_of(i, be), 0, 0)),
            pl.BlockSpec((1, 1, 2 * D_EXPERT), lambda i, be, nv: (expert_of(i, be), 0, 0)),
            pl.BlockSpec((1, D_EXPERT, D_MODEL), lambda i, be, nv: (expert_of(i, be), 0, 0)),
            pl.BlockSpec((1, 1, D_MODEL), lambda i, be, nv: (expert_of(i, be), 0, 0)),
        ],
        out_specs=pl.BlockSpec(memory_space=pl.ANY),
        scratch_shapes=[
            pltpu.VMEM((2, MOE_ROWS * TILE_ROWS, LANES), jnp.uint32),
            pltpu.VMEM((2, MOE_ROWS * TILE_ROWS, LANES), jnp.uint32),
            pltpu.VMEM((MOE_ROWS, D_MODEL), BF16),
            pltpu.VMEM((D_MODEL, 2 * D_EXPERT), BF16),
            pltpu.VMEM((D_EXPERT, D_MODEL), BF16),
            pltpu.SemaphoreType.DMA((2,)),
            pltpu.SemaphoreType.DMA((2,)),
        ],
    )
    return pl.pallas_call(
        functools.partial(_moe_kernel, nb=nb),
        grid_spec=grid_spec,
        out_shape=jax.ShapeDtypeStruct((n_slot_rows * TILE_ROWS, LANES), jnp.uint32),
        compiler_params=pltpu.CompilerParams(
            dimension_semantics=("arbitrary",), vmem_limit_bytes=VMEM_LIMIT),
        name="moe_experts",
    )(block_expert, nvalid, row_token3, row_token3, row_dest3, h2, wgu, bgu3, wd, bd3)


def _combine_kernel(x1_ref, s0_ref, s1_ref, s2_ref, s3_ref, topw_ref, p_ref, pg_ref, wpg_ref, wpp_ref, fg_ref,
                    out_ref):
    for r0 in range(0, x1_ref.shape[0], ROUTE_SUB):
        rs = slice(r0, r0 + ROUTE_SUB)
        x2 = x1_ref[rs, :]
        topw = topw_ref[rs, :]
        for k, s_ref in enumerate((s0_ref, s1_ref, s2_ref, s3_ref)):
            x2 = x2 + _load_token_tiles(s_ref, ROUTE_SUB, r0) * topw[:, k:k + 1]
        n = _rms(x2, pg_ref[...])
        gate = jax.nn.sigmoid(jnp.dot(n.astype(BF16), wpg_ref[...], preferred_element_type=F32))
        pp = jnp.dot(p_ref[rs, :].astype(BF16), wpp_ref[...], preferred_element_type=F32)
        x3 = x2 + gate * pp
        out_ref[rs, :] = _rms(x3, fg_ref[...])


def _combine(x1, slots, topw, p2d, pg, wpg, wpp, fg, tm=512):
    t = x1.shape[0]
    nt = t // tm
    const = lambda i: (0, 0)
    slot_specs = [pl.BlockSpec((tm * TILE_ROWS, LANES), functools.partial(lambda k, i: (k * nt + i, 0), k))
                  for k in range(TOP_K)]
    return pl.pallas_call(
        _combine_kernel,
        grid=(nt,),
        in_specs=[
            pl.BlockSpec((tm, D_MODEL), lambda i: (i, 0)),
            *slot_specs,
            pl.BlockSpec((tm, LANES), lambda i: (i, 0)),
            pl.BlockSpec((tm, D_PLE), lambda i: (i, 0)),
            pl.BlockSpec((1, D_MODEL), const),
            pl.BlockSpec((D_MODEL, D_MODEL), const),
            pl.BlockSpec((D_PLE, D_MODEL), const),
            pl.BlockSpec((1, D_MODEL), const),
        ],
        out_specs=pl.BlockSpec((tm, D_MODEL), lambda i: (i, 0)),
        out_shape=jax.ShapeDtypeStruct((t, D_MODEL), F32),
        compiler_params=pltpu.CompilerParams(
            dimension_semantics=("arbitrary",), vmem_limit_bytes=VMEM_LIMIT),
        name="combine_ple",
    )(x1, slots, slots, slots, slots, topw, p2d, pg, wpg, wpp, fg)


def _routing_tables(top_idx_t, n_tok):
    n_assign = n_tok * TOP_K
    expert_flat = top_idx_t.reshape(-1)
    order = jnp.argsort(expert_flat, stable=True).astype(jnp.int32)
    counts = jnp.bincount(expert_flat, length=N_EXPERTS).astype(jnp.int32)
    start = jnp.cumsum(counts) - counts
    padded = (counts + MOE_ROWS - 1) // MOE_ROWS * MOE_ROWS
    pend = jnp.cumsum(padded)
    pstart = pend - padded
    n_rows = n_assign + N_EXPERTS * MOE_ROWS
    n_blocks = n_rows // MOE_ROWS
    block_start = jnp.arange(n_blocks, dtype=jnp.int32) * MOE_ROWS
    block_expert = jnp.minimum(jnp.sum(block_start[:, None] >= pend[None, :], axis=1),
                               N_EXPERTS - 1).astype(jnp.int32)
    nvalid = (pend[-1] // MOE_ROWS).astype(jnp.int32).reshape(1)
    is_e = block_expert[:, None] == jnp.arange(N_EXPERTS, dtype=jnp.int32)[None, :]
    per_block = lambda v: jnp.sum(jnp.where(is_e, v[None, :], 0), axis=1)
    nreal = jnp.clip(per_block(pstart + counts) - block_start, 0, MOE_ROWS).astype(jnp.int32)
    sorted_pos = (block_start + per_block(start - pstart))[:, None] + jnp.arange(MOE_ROWS, dtype=jnp.int32)[None, :]
    assign = order[jnp.clip(sorted_pos, 0, n_assign - 1)]
    row_token = assign % n_tok
    is_real = jnp.arange(MOE_ROWS, dtype=jnp.int32)[None, :] < nreal[:, None]
    row_q = block_start[:, None] + jnp.arange(MOE_ROWS, dtype=jnp.int32)[None, :]
    spare = n_assign + row_q - per_block(start + counts)[:, None]
    row_dest = jnp.where(is_real, assign, spare)
    return (block_expert, nvalid, (row_token * TILE_ROWS).reshape(n_blocks, 1, MOE_ROWS),
            (row_dest * TILE_ROWS).reshape(n_blocks, 1, MOE_ROWS), n_rows)


def _layer(x2d, p2d, bsz, seq, mix_norm_g, w_in, conv_w, conv_b, dt_bias, a_log, d_skip, ssd_norm_g,
           w_ssd_out, pool_w, pool_scale, w_mix_out, ffn_norm_g, w_router, b_router,
           w_gate_up, b_gate_up, w_down, b_down, ple_norm_g, w_ple_gate, w_ple_proj, out_g):
    n_tok = x2d.shape[0]
    dt0 = D_INNER + D_CONV
    w_main = jnp.concatenate([w_in[:, :dt0], w_in[:, dt0 + HEADS:]], axis=1).astype(BF16)
    proj, dt_raw = _in_proj(x2d, mix_norm_g[None, :], w_main, _hilo_weight(w_in[:, dt0:dt0 + HEADS]))

    pad_h = lambda v: jnp.pad(v, (0, LANES - HEADS))[None, :]
    ltri = (jnp.arange(CHUNK)[:, None] >= jnp.arange(CHUNK)[None, :]).astype(BF16)
    ltri3 = jnp.concatenate([ltri, ltri, ltri], axis=1)
    e1 = (jnp.arange(LANES)[:, None] == (jnp.arange(D_INNER) // HEAD_DIM)[None, :]).astype(BF16)
    e2 = jnp.concatenate([e1, e1], axis=0)
    yssd, ypool = _mixers(
        proj, dt_raw, bsz, seq, conv_w, conv_b[None, :], pad_h(dt_bias), pad_h(a_log),
        jnp.repeat(d_skip, HEAD_DIM)[None, :], ssd_norm_g[None, :], ltri3, e2,
        pool_w.astype(BF16), pool_scale[None, :])

    br = jnp.pad(b_router, (0, LANES - N_EXPERTS))[None, :]
    x1, h2, topw, topi = _mix_route(x2d, yssd, ypool, proj, w_ssd_out.astype(BF16),
                                    w_mix_out.astype(BF16), ffn_norm_g[None, :], _hilo_weight(w_router), br)

    block_expert, nvalid, row_token3, row_dest3, n_slot_rows = _routing_tables(topi[:TOP_K], n_tok)
    slots = _moe(block_expert, nvalid, row_token3, row_dest3, h2, w_gate_up, b_gate_up[:, None, :],
                 w_down, b_down[:, None, :], n_slot_rows)

    return _combine(x1, slots, topw, p2d, ple_norm_g[None, :], w_ple_gate.astype(BF16),
                    w_ple_proj.astype(BF16), out_g[None, :])


def kernel(x, p, mix_norm_g, w_in, conv_w, conv_b, dt_bias, a_log, d_skip, ssd_norm_g, w_ssd_out, pool_w,
           pool_scale, w_mix_out, ffn_norm_g, w_router, b_router, w_gate_up, b_gate_up, w_down, b_down,
           ple_norm_g, w_ple_gate, w_ple_proj, final_norm_g):
    bsz, seq, d = x.shape
    depth = p.shape[0]
    assert depth == 1 and d == D_MODEL and p.shape[-1] == D_PLE
    assert seq % (MIX_CHUNKS * CHUNK) == 0 and (bsz * seq) % 1024 == 0
    x2d = x.reshape(bsz * seq, d)
    out = _layer(x2d, p[0].reshape(bsz * seq, D_PLE), bsz, seq, mix_norm_g[0], w_in[0], conv_w[0], conv_b[0],
                 dt_bias[0], a_log[0], d_skip[0], ssd_norm_g[0], w_ssd_out[0], pool_w[0], pool_scale[0],
                 w_mix_out[0], ffn_norm_g[0], w_router[0], b_router[0], w_gate_up[0], b_gate_up[0],
                 w_down[0], b_down[0], ple_norm_g[0], w_ple_gate[0], w_ple_proj[0], final_norm_g)
    return out.reshape(bsz, seq, d)
```

```python
import functools

import jax
import jax.numpy as jnp
from jax import lax
from jax.experimental import pallas as pl
from jax.experimental.pallas import tpu as pltpu

F32 = jnp.float32
BF16 = jnp.bfloat16

D_MODEL = 1024
D_INNER = 2048
HEAD_DIM = 64
HEADS = 32
GROUPS = 4
HEADS_PER_GROUP = HEADS // GROUPS
GROUP_DIM = D_INNER // GROUPS
D_STATE = 128
CONV_WIDTH = 4
CHUNK = 128
D_BC = 2 * GROUPS * D_STATE
D_CONV = D_INNER + D_BC
POOL_WIDTH = D_MODEL
POOL_WINDOWS = (2, 4, 8, 16)
POOL_GROUP_DIM = POOL_WIDTH // len(POOL_WINDOWS)
N_EXPERTS = 32
TOP_K = 4
D_EXPERT = D_MODEL
SWIGLU_LIMIT = 7.0
SWIGLU_ALPHA = 1.702
D_PLE = 256
EPS = 1e-6

LANES = 128
TILE_ROWS = D_MODEL // (2 * LANES)
TOPI_ROWS = 8
ROUTE_SUB = 256
MIX_CHUNKS = 4
HALO = 16
D_PROJ = D_INNER + D_CONV + POOL_WIDTH + 2 * D_MODEL
MOE_ROWS = 256
VMEM_LIMIT = 56 * 1024 * 1024


def _split2(v):
    hi = v.astype(BF16)
    lo = (v - hi.astype(F32)).astype(BF16)
    return hi, lo


def _split3(v):
    hi = v.astype(BF16)
    r = v - hi.astype(F32)
    mid = r.astype(BF16)
    lo = (r - mid.astype(F32)).astype(BF16)
    return hi, mid, lo


def _hilo_weight(w):
    n = w.shape[1]
    hi, lo = _split2(w)
    top = jnp.pad(jnp.concatenate([hi, lo], axis=1), ((0, 0), (0, LANES - 2 * n)))
    bottom = jnp.pad(hi, ((0, 0), (0, LANES - n)))
    return jnp.concatenate([top, bottom], axis=0)


def _hilo_dot(hi, lo, w2_ref, n):
    k = hi.shape[1]
    a = jnp.dot(hi, w2_ref[0:k, :], preferred_element_type=F32)
    b = jnp.dot(lo, w2_ref[k:2 * k, :], preferred_element_type=F32)
    return a + pltpu.roll(a, LANES - n, axis=1) + b


def _rms(x, g):
    return x * lax.rsqrt(jnp.mean(x * x, axis=-1, keepdims=True) + EPS) * g


def _store_token_group(ref2d, val, j, row0=0):
    rows = val.shape[0]
    hi = lax.bitcast_convert_type(val[:, 0:LANES].astype(BF16).astype(F32), jnp.uint32)
    lo = lax.bitcast_convert_type(val[:, LANES:2 * LANES].astype(BF16).astype(F32), jnp.uint32)
    ref2d[pl.ds(row0 * TILE_ROWS + j, rows, stride=TILE_ROWS), :] = hi | lax.shift_right_logical(lo, jnp.uint32(16))


def _store_token_tiles(ref2d, val, row0=0):
    for j in range(TILE_ROWS):
        _store_token_group(ref2d, val[:, 2 * j * LANES:2 * (j + 1) * LANES], j, row0)


def _load_token_tiles(ref2d, rows, row0=0):
    pieces = []
    for j in range(TILE_ROWS):
        words = ref2d[pl.ds(row0 * TILE_ROWS + j, rows, stride=TILE_ROWS), :]
        pieces.append(lax.bitcast_convert_type(words & jnp.uint32(0xFFFF0000), F32))
        pieces.append(lax.bitcast_convert_type(lax.shift_left(words, jnp.uint32(16)), F32))
    return jnp.concatenate(pieces, axis=1)


def _in_proj_kernel(x_ref, g_ref, w_ref, wdt_ref, proj_ref, dt_ref, h_sc):
    @pl.when(pl.program_id(1) == 0)
    def _():
        h = _rms(x_ref[...], g_ref[...])
        hi, lo = _split2(h)
        h_sc[...] = hi
        dt_ref[...] = _hilo_dot(hi, lo, wdt_ref, HEADS)

    proj_ref[...] = jnp.dot(h_sc[...], w_ref[...], preferred_element_type=F32).astype(BF16)


def _in_proj(x2d, g, w_main, wdt2, tm=1024, tn=2048):
    t = x2d.shape[0]
    return pl.pallas_call(
        _in_proj_kernel,
        grid=(t // tm, D_PROJ // tn),
        in_specs=[
            pl.BlockSpec((tm, D_MODEL), lambda i, j: (i, 0)),
            pl.BlockSpec((1, D_MODEL), lambda i, j: (0, 0)),
            pl.BlockSpec((D_MODEL, tn), lambda i, j: (0, j)),
            pl.BlockSpec((2 * D_MODEL, LANES), lambda i, j: (0, 0)),
        ],
        out_specs=[
            pl.BlockSpec((tm, tn), lambda i, j: (i, j)),
            pl.BlockSpec((tm, LANES), lambda i, j: (i, 0)),
        ],
        out_shape=[
            jax.ShapeDtypeStruct((t, D_PROJ), BF16),
            jax.ShapeDtypeStruct((t, LANES), F32),
        ],
        scratch_shapes=[pltpu.VMEM((tm, D_MODEL), BF16)],
        compiler_params=pltpu.CompilerParams(
            dimension_semantics=("arbitrary", "arbitrary"), vmem_limit_bytes=VMEM_LIMIT),
        name="in_proj",
    )(x2d, g, w_main, wdt2)


def _mixers_kernel(z_ref, xs_ref, bc_ref, u_ref, dt_ref,
                   cw_ref, cb_ref, dtb_ref, alog_ref, dskip_ref, ng_ref, ltri_ref, e2_ref,
                   shift_ref, band_ref, pw_ref, ps_ref,
                   yssd_ref, ypool_ref,
                   ext_sc, extu_sc, state_sc):
    c = pl.program_id(1)
    rows = MIX_CHUNKS * CHUNK

    @pl.when(c == 0)
    def _():
        ext_sc[0:HALO, :] = jnp.zeros((HALO, D_CONV), BF16)
        extu_sc[0:HALO, :] = jnp.zeros((HALO, POOL_WIDTH), BF16)
        state_sc[...] = jnp.zeros_like(state_sc)

    @pl.when(c > 0)
    def _():
        ext_sc[0:HALO, :] = ext_sc[rows:rows + HALO, :]
        extu_sc[0:HALO, :] = extu_sc[rows:rows + HALO, :]

    ext_sc[HALO:HALO + rows, 0:D_INNER] = xs_ref[...]
    ext_sc[HALO:HALO + rows, D_INNER:D_CONV] = bc_ref[...]
    extu_sc[HALO:HALO + rows, :] = u_ref[...]
    for ci in range(MIX_CHUNKS):
        _mixers_chunk(ci, c * MIX_CHUNKS + ci, z_ref, u_ref, dt_ref, cw_ref, cb_ref, dtb_ref, alog_ref, dskip_ref,
                      ng_ref, ltri_ref, e2_ref, shift_ref, band_ref, pw_ref, ps_ref, yssd_ref, ypool_ref,
                      ext_sc, extu_sc, state_sc)


def _mixers_chunk(ci, chunk_index, z_ref, u_ref, dt_ref, cw_ref, cb_ref, dtb_ref, alog_ref, dskip_ref,
                  ng_ref, ltri_ref, e2_ref, shift_ref, band_ref, pw_ref, ps_ref, yssd_ref, ypool_ref,
                  ext_sc, extu_sc, state_sc):
    r0 = ci * CHUNK
    rs = slice(r0, r0 + CHUNK)
    ext = ext_sc[r0:r0 + HALO + CHUNK, :]
    conv = cb_ref[...] + cw_ref[CONV_WIDTH - 1:CONV_WIDTH, :] * ext[HALO:HALO + CHUNK, :].astype(F32)
    for k in range(CONV_WIDTH - 1):
        conv = conv + cw_ref[k:k + 1, :] * jnp.dot(shift_ref[k], ext, preferred_element_type=F32)
    xc = conv * jax.nn.sigmoid(conv)
    xs = xc[:, 0:D_INNER]
    xs_b = xs.astype(BF16)

    dtv = jax.nn.softplus(dt_ref[rs, :] + dtb_ref[...])
    da = dtv * (-jnp.exp(alog_ref[...]))
    a_cum = jnp.dot(ltri_ref[...], jnp.concatenate(_split3(da), axis=0),
                    preferred_element_type=F32)
    expa = jnp.exp(a_cum)
    a_last = a_cum[CHUNK - 1:CHUNK, :]
    wst = dtv * jnp.exp(a_last - a_cum)
    a_cum_t = a_cum.T
    dt_t = dtv.T

    both = jnp.concatenate([wst, expa], axis=0)
    hi, lo = _split2(both)
    expd = jnp.dot(jnp.concatenate([hi, lo], axis=1), e2_ref[...],
                   preferred_element_type=F32)
    wst_x = expd[0:CHUNK, :]
    expa_x = expd[CHUNK:2 * CHUNK, :]
    xw_b = (xs * wst_x).astype(BF16)

    row = lax.broadcasted_iota(jnp.int32, (CHUNK, CHUNK), 0)
    col = lax.broadcasted_iota(jnp.int32, (CHUNK, CHUNK), 1)
    causal_bias = jnp.where(row >= col, 0.0, -jnp.inf).astype(F32)
    lane = lax.broadcasted_iota(jnp.int32, (CHUNK, LANES), 1)
    low_half = lane < HEAD_DIM

    y_groups = []
    for g in range(GROUPS):
        bg = xc[:, D_INNER + g * D_STATE:D_INNER + (g + 1) * D_STATE]
        cg = xc[:, D_INNER + GROUPS * D_STATE + g * D_STATE:D_INNER + GROUPS * D_STATE + (g + 1) * D_STATE]
        bg_b = bg.astype(BF16)
        cg_b = cg.astype(BF16)
        cbm = lax.dot_general(cg_b, bg_b, (((1,), (1,)), ((), ())), preferred_element_type=F32)
        gsl = slice(g * GROUP_DIM, (g + 1) * GROUP_DIM)

        prev_t = state_sc[g]
        y_off = jnp.dot(cg_b, prev_t.astype(BF16), preferred_element_type=F32) * expa_x[:, gsl]
        st_t = jnp.dot(bg.T.astype(BF16), xw_b[:, gsl], preferred_element_type=F32)
        state_sc[g] = prev_t * expa_x[CHUNK - 1:CHUNK, gsl] + st_t

        pairs = []
        for jp in range(HEADS_PER_GROUP // 2):
            ms = []
            for hh in range(2):
                h = g * HEADS_PER_GROUP + jp * 2 + hh
                seg = a_cum[:, h:h + 1] - a_cum_t[h:h + 1, :]
                dec = jnp.exp(seg + causal_bias)
                ms.append((cbm * dec * dt_t[h:h + 1, :]).astype(BF16))
            lhs = jnp.concatenate(ms, axis=1)
            c0 = g * GROUP_DIM + jp * LANES
            xp = xs_b[:, c0:c0 + LANES]
            zero = jnp.zeros_like(xp)
            rhs = jnp.concatenate([jnp.where(low_half, xp, zero), jnp.where(low_half, zero, xp)], axis=0)
            pairs.append(jnp.dot(lhs, rhs, preferred_element_type=F32))
        y_diag = jnp.concatenate(pairs, axis=1)

        yg = y_diag + y_off + dskip_ref[:, gsl] * xs[:, gsl]
        zg = z_ref[rs, gsl].astype(F32)
        yg = yg * (zg * jax.nn.sigmoid(zg))
        yg = yg * lax.rsqrt(jnp.mean(yg * yg, axis=-1, keepdims=True) + EPS) * ng_ref[:, gsl]
        y_groups.append(yg.astype(BF16))
    yssd_ref[rs, :] = jnp.concatenate(y_groups, axis=1)

    pos = chunk_index * CHUNK + lax.broadcasted_iota(jnp.int32, (CHUNK, 1), 0)
    outs = []
    for gi, w in enumerate(POOL_WINDOWS):
        psl = slice(gi * POOL_GROUP_DIM, (gi + 1) * POOL_GROUP_DIM)
        s = jnp.dot(band_ref[gi], extu_sc[r0:r0 + HALO + CHUNK, psl], preferred_element_type=F32)
        cnt = jnp.minimum(pos + 1, w).astype(F32)
        pooled = s / cnt - u_ref[rs, psl].astype(F32)
        outs.append(jnp.dot(pooled.astype(BF16), pw_ref[gi], preferred_element_type=F32))
    ypool_ref[rs, :] = (jnp.concatenate(outs, axis=1) * ps_ref[...]).astype(BF16)


def _mixers(proj, dt_raw, bsz, seq, cw, cb, dtb, alog, dskip, ng, ltri3, e2, pw, ps):
    rows = MIX_CHUNKS * CHUNK
    nc = seq // rows
    t = bsz * seq
    rowmap = lambda b, c: b * nc + c
    const2 = lambda b, c: (0, 0)
    const3 = lambda b, c: (0, 0, 0)
    trow = jnp.arange(CHUNK)[:, None] + HALO
    jcol = jnp.arange(HALO + CHUNK)[None, :]
    shifts = jnp.stack([(jcol == trow - (CONV_WIDTH - 1) + k) for k in range(CONV_WIDTH - 1)]).astype(BF16)
    bands = jnp.stack([(jcol <= trow) & (jcol > trow - w) for w in POOL_WINDOWS]).astype(BF16)
    return pl.pallas_call(
        _mixers_kernel,
        grid=(bsz, nc),
        in_specs=[
            pl.BlockSpec((rows, D_INNER), lambda b, c: (rowmap(b, c), 0)),
            pl.BlockSpec((rows, D_INNER), lambda b, c: (rowmap(b, c), 1)),
            pl.BlockSpec((rows, D_BC), lambda b, c: (rowmap(b, c), 4)),
            pl.BlockSpec((rows, POOL_WIDTH), lambda b, c: (rowmap(b, c), 5)),
            pl.BlockSpec((rows, LANES), lambda b, c: (rowmap(b, c), 0)),
            pl.BlockSpec((CONV_WIDTH, D_CONV), const2),
            pl.BlockSpec((1, D_CONV), const2),
            pl.BlockSpec((1, LANES), const2),
            pl.BlockSpec((1, LANES), const2),
            pl.BlockSpec((1, D_INNER), const2),
            pl.BlockSpec((1, D_INNER), const2),
            pl.BlockSpec((CHUNK, 3 * CHUNK), const2),
            pl.BlockSpec((2 * LANES, D_INNER), const2),
            pl.BlockSpec((CONV_WIDTH - 1, CHUNK, HALO + CHUNK), const3),
            pl.BlockSpec((len(POOL_WINDOWS), CHUNK, HALO + CHUNK), const3),
            pl.BlockSpec((len(POOL_WINDOWS), POOL_GROUP_DIM, POOL_GROUP_DIM), const3),
            pl.BlockSpec((1, POOL_WIDTH), const2),
        ],
        out_specs=[
            pl.BlockSpec((rows, D_INNER), lambda b, c: (rowmap(b, c), 0)),
            pl.BlockSpec((rows, POOL_WIDTH), lambda b, c: (rowmap(b, c), 0)),
        ],
        out_shape=[
            jax.ShapeDtypeStruct((t, D_INNER), BF16),
            jax.ShapeDtypeStruct((t, POOL_WIDTH), BF16),
        ],
        scratch_shapes=[
            pltpu.VMEM((HALO + rows, D_CONV), BF16),
            pltpu.VMEM((HALO + rows, POOL_WIDTH), BF16),
            pltpu.VMEM((GROUPS, D_STATE, GROUP_DIM), F32),
        ],
        compiler_params=pltpu.CompilerParams(
            dimension_semantics=("arbitrary", "arbitrary"), vmem_limit_bytes=VMEM_LIMIT),
        name="mixers",
    )(proj, proj, proj, proj, dt_raw, cw, cb, dtb, alog, dskip, ng, ltri3, e2, shifts, bands, pw, ps)


def _mix_route_kernel(x_ref, yssd_ref, ypool_ref, gates_ref, wso_ref, wmo_ref, fg_ref, wr_ref, br_ref,
                      x1_ref, h2_ref, topw_ref, topi_ref):
    for r0 in range(0, x_ref.shape[0], ROUTE_SUB):
        _mix_route_rows(slice(r0, r0 + ROUTE_SUB), x_ref, yssd_ref, ypool_ref, gates_ref, wso_ref, wmo_ref, fg_ref,
                        wr_ref, br_ref, x1_ref, h2_ref, topw_ref, topi_ref)


def _mix_route_rows(rs, x_ref, yssd_ref, ypool_ref, gates_ref, wso_ref, wmo_ref, fg_ref, wr_ref, br_ref,
                    x1_ref, h2_ref, topw_ref, topi_ref):
    y_ssd = jnp.dot(yssd_ref[rs, :], wso_ref[...], preferred_element_type=F32)
    gates = jax.nn.sigmoid(gates_ref[rs, :].astype(F32))
    mixed = gates[:, 0:D_MODEL] * y_ssd + gates[:, D_MODEL:2 * D_MODEL] * ypool_ref[rs, :].astype(F32)
    x1 = x_ref[rs, :] + jnp.dot(mixed.astype(BF16), wmo_ref[...], preferred_element_type=F32)
    x1_ref[rs, :] = x1
    h2 = _rms(x1, fg_ref[...])
    _store_token_tiles(h2_ref, h2, rs.start)

    hi, lo = _split2(h2)
    logits = _hilo_dot(hi, lo, wr_ref, N_EXPERTS) + br_ref[...]
    tm = logits.shape[0]
    lane = lax.broadcasted_iota(jnp.int32, (tm, LANES), 1)
    neg = jnp.float32(-jnp.inf)
    work = jnp.where(lane < N_EXPERTS, logits, neg)
    vals = []
    idxs = []
    for _ in range(TOP_K):
        m = jnp.max(work, axis=-1, keepdims=True)
        idx = jnp.min(jnp.where(work == m, lane, LANES), axis=-1, keepdims=True)
        vals.append(m)
        idxs.append(idx)
        work = jnp.where(lane == idx, neg, work)
    es = [jnp.exp(v - vals[0]) for v in vals]
    den = es[0] + es[1] + es[2] + es[3]
    topw = jnp.zeros((tm, LANES), F32)
    topi = jnp.zeros((tm, LANES), jnp.int32)
    for k in range(TOP_K):
        topw = jnp.where(lane == k, es[k] / den, topw)
        topi = jnp.where(lane == k, idxs[k], topi)
    topw_ref[rs, :] = topw
    topi_ref[:, rs] = topi.T[0:TOPI_ROWS, :]


def _mix_route(x2d, yssd, ypool, proj, wso, wmo, fg, wr2, br, tm=512):
    t = x2d.shape[0]
    const = lambda i: (0, 0)
    return pl.pallas_call(
        _mix_route_kernel,
        grid=(t // tm,),
        in_specs=[
            pl.BlockSpec((tm, D_MODEL), lambda i: (i, 0)),
            pl.BlockSpec((tm, D_INNER), lambda i: (i, 0)),
            pl.BlockSpec((tm, POOL_WIDTH), lambda i: (i, 0)),
            pl.BlockSpec((tm, 2 * D_MODEL), lambda i: (i, 3)),
            pl.BlockSpec((D_INNER, D_MODEL), const),
            pl.BlockSpec((D_MODEL, D_MODEL), const),
            pl.BlockSpec((1, D_MODEL), const),
            pl.BlockSpec((2 * D_MODEL, LANES), const),
            pl.BlockSpec((1, LANES), const),
        ],
        out_specs=[
            pl.BlockSpec((tm, D_MODEL), lambda i: (i, 0)),
            pl.BlockSpec((tm * TILE_ROWS, LANES), lambda i: (i, 0)),
            pl.BlockSpec((tm, LANES), lambda i: (i, 0)),
            pl.BlockSpec((TOPI_ROWS, tm), lambda i: (0, i)),
        ],
        out_shape=[
            jax.ShapeDtypeStruct((t, D_MODEL), F32),
            jax.ShapeDtypeStruct((t * TILE_ROWS, LANES), jnp.uint32),
            jax.ShapeDtypeStruct((t, LANES), F32),
            jax.ShapeDtypeStruct((TOPI_ROWS, t), jnp.int32),
        ],
        compiler_params=pltpu.CompilerParams(
            dimension_semantics=("arbitrary",), vmem_limit_bytes=VMEM_LIMIT),
        name="mix_route",
    )(x2d, yssd, ypool, proj, wso, wmo, fg, wr2, br)


def _moe_kernel(be_ref, nvalid_ref,
                tok0_ref, toknext_ref, dstprev_ref,
                h_hbm, wgu_ref, bgu_ref, wd_ref, bd_ref,
                slots_hbm,
                xbuf, ybuf, xb_sc, wgu_bf, wd_bf, gsem, ssem, *, nb):
    i = pl.program_id(0)
    nvalid = nvalid_ref[0]
    slot = lax.rem(i, 2)
    has_next = i + 1 < nvalid

    def tile(off, n=1):
        return pl.ds(pl.multiple_of(off, TILE_ROWS), n * TILE_ROWS)

    def gather_copy(tok, r, s):
        return pltpu.make_async_copy(h_hbm.at[tile(tok)], xbuf.at[s, tile(r * TILE_ROWS)], gsem.at[s])

    def scatter_copy(dst, r, s):
        return pltpu.make_async_copy(ybuf.at[s, tile(r * TILE_ROWS)], slots_hbm.at[tile(dst)], ssem.at[s])

    def wait_gather(s):
        pltpu.make_async_copy(h_hbm.at[tile(0, MOE_ROWS)], xbuf.at[s, tile(0, MOE_ROWS)], gsem.at[s]).wait()

    def wait_scatter(s):
        pltpu.make_async_copy(ybuf.at[s, tile(0, MOE_ROWS)], slots_hbm.at[tile(0, MOE_ROWS)], ssem.at[s]).wait()

    def scatter_whole_block(s):
        pltpu.make_async_copy(ybuf.at[s, tile(0, MOE_ROWS)], slots_hbm.at[tile(dstprev_ref[0, 0, 0], MOE_ROWS)],
                              ssem.at[s]).start()

    def issue(s, gather_ref, scatter_ref):
        for r in range(MOE_ROWS):
            if gather_ref is not None:
                gather_copy(gather_ref[0, 0, r], r, s).start()
            if scatter_ref is not None:
                scatter_copy(scatter_ref[0, 0, r], r, s).start()

    @pl.when(i == 0)
    def _():
        ybuf[...] = jnp.zeros_like(ybuf)
        issue(0, tok0_ref, None)

    for par in (0, 1):
        other = 1 - par
        mine = slot == par

        pl.when(jnp.logical_and(mine, i >= 2))(functools.partial(wait_scatter, par))
        pl.when(jnp.logical_and(mine, i < nvalid))(functools.partial(wait_gather, par))
        pl.when(jnp.logical_and(mine, jnp.logical_and(has_next, i == 0)))(
            functools.partial(issue, other, toknext_ref, None))
        pl.when(jnp.logical_and(mine, jnp.logical_and(has_next, i >= 1)))(
            functools.partial(issue, other, toknext_ref, dstprev_ref))
        pl.when(jnp.logical_and(mine, jnp.logical_and(jnp.logical_not(has_next), jnp.logical_and(i >= 1, i <= nvalid))))(
            functools.partial(issue, other, None, dstprev_ref))
        pl.when(jnp.logical_and(mine, i > nvalid))(functools.partial(scatter_whole_block, other))
        pl.when(jnp.logical_and(mine, i == nb))(functools.partial(wait_scatter, other))

    @pl.when(i < nvalid)
    def _():
        @pl.when(jnp.logical_or(i == 0, be_ref[i] != be_ref[jnp.maximum(i - 1, 0)]))
        def _():
            wgu_bf[...] = wgu_ref[0].astype(BF16)
            wd_bf[...] = wd_ref[0].astype(BF16)

        xb_sc[...] = _load_token_tiles(xbuf.at[slot], MOE_ROWS).astype(BF16)
        gu = jnp.dot(xb_sc[...], wgu_bf[...], preferred_element_type=F32) + bgu_ref[0]
        gate = jnp.minimum(gu[:, 0:D_EXPERT], SWIGLU_LIMIT)
        up = jnp.clip(gu[:, D_EXPERT:2 * D_EXPERT], -SWIGLU_LIMIT, SWIGLU_LIMIT)
        act = (up + 1.0) * gate * jax.nn.sigmoid(SWIGLU_ALPHA * gate)
        act_b = act.astype(BF16)
        for j in range(TILE_ROWS):
            cols = slice(2 * j * LANES, 2 * (j + 1) * LANES)
            y = jnp.dot(act_b, wd_bf[:, cols], preferred_element_type=F32) + bd_ref[0, :, cols]
            _store_token_group(ybuf.at[slot], y, j)


def _moe(block_expert, nvalid, row_token3, row_dest3, h2, wgu, bgu3, wd, bd3, n_slot_rows):
    nb = block_expert.shape[0]
    expert_of = lambda i, be: be[jnp.minimum(i, nb - 1)]
    grid_spec = pltpu.PrefetchScalarGridSpec(
        num_scalar_prefetch=2,
        grid=(nb + 1,),
        in_specs=[
            pl.BlockSpec((1, 1, MOE_ROWS), lambda i, be, nv: (0, 0, 0), memory_space=pltpu.SMEM),
            pl.BlockSpec((1, 1, MOE_ROWS), lambda i, be, nv: (jnp.minimum(i + 1, nb - 1), 0, 0),
                         memory_space=pltpu.SMEM),
            pl.BlockSpec((1, 1, MOE_ROWS), lambda i, be, nv: (jnp.clip(i - 1, 0, nb - 1), 0, 0),
                         memory_space=pltpu.SMEM),
            pl.BlockSpec(memory_space=pl.ANY),
            pl.BlockSpec((1, D_MODEL, 2 * D_EXPERT), lambda i, be, nv: (expert_of(i, be), 0, 0)),
            pl.BlockSpec((1, 1, 2 * D_EXPERT), lambda i, be, nv: (expert_of(i, be), 0, 0)),
            pl.BlockSpec((1, D_EXPERT, D_MODEL), lambda i, be, nv: (expert_of(i, be), 0, 0)),
            pl.BlockSpec((1, 1, D_MODEL), lambda i, be, nv: (expert_of(i, be), 0, 0)),
        ],
        out_specs=pl.BlockSpec(memory_space=pl.ANY),
        scratch_shapes=[
            pltpu.VMEM((2, MOE_ROWS * TILE_ROWS, LANES), jnp.uint32),
            pltpu.VMEM((2, MOE_ROWS * TILE_ROWS, LANES), jnp.uint32),
            pltpu.VMEM((MOE_ROWS, D_MODEL), BF16),
            pltpu.VMEM((D_MODEL, 2 * D_EXPERT), BF16),
            pltpu.VMEM((D_EXPERT, D_MODEL), BF16),
            pltpu.SemaphoreType.DMA((2,)),
            pltpu.SemaphoreType.DMA((2,)),
        ],
    )
    return pl.pallas_call(
        functools.partial(_moe_kernel, nb=nb),
        grid_spec=grid_spec,
        out_shape=jax.ShapeDtypeStruct((n_slot_rows * TILE_ROWS, LANES), jnp.uint32),
        compiler_params=pltpu.CompilerParams(
            dimension_semantics=("arbitrary",), vmem_limit_bytes=VMEM_LIMIT),
        name="moe_experts",
    )(block_expert, nvalid, row_token3, row_token3, row_dest3, h2, wgu, bgu3, wd, bd3)


def _combine_kernel(x1_ref, s0_ref, s1_ref, s2_ref, s3_ref, topw_ref, p_ref, pg_ref, wpg_ref, wpp_ref, fg_ref,
                    out_ref):
    for r0 in range(0, x1_ref.shape[0], ROUTE_SUB):
        rs = slice(r0, r0 + ROUTE_SUB)
        x2 = x1_ref[rs, :]
        topw = topw_ref[rs, :]
        for k, s_ref in enumerate((s0_ref, s1_ref, s2_ref, s3_ref)):
            x2 = x2 + _load_token_tiles(s_ref, ROUTE_SUB, r0) * topw[:, k:k + 1]
        n = _rms(x2, pg_ref[...])
        gate = jax.nn.sigmoid(jnp.dot(n.astype(BF16), wpg_ref[...], preferred_element_type=F32))
        pp = jnp.dot(p_ref[rs, :].astype(BF16), wpp_ref[...], preferred_element_type=F32)
        x3 = x2 + gate * pp
        out_ref[rs, :] = _rms(x3, fg_ref[...])


def _combine(x1, slots, topw, p2d, pg, wpg, wpp, fg, tm=512):
    t = x1.shape[0]
    nt = t // tm
    const = lambda i: (0, 0)
    slot_specs = [pl.BlockSpec((tm * TILE_ROWS, LANES), functools.partial(lambda k, i: (k * nt + i, 0), k))
                  for k in range(TOP_K)]
    return pl.pallas_call(
        _combine_kernel,
        grid=(nt,),
        in_specs=[
            pl.BlockSpec((tm, D_MODEL), lambda i: (i, 0)),
            *slot_specs,
            pl.BlockSpec((tm, LANES), lambda i: (i, 0)),
            pl.BlockSpec((tm, D_PLE), lambda i: (i, 0)),
            pl.BlockSpec((1, D_MODEL), const),
            pl.BlockSpec((D_MODEL, D_MODEL), const),
            pl.BlockSpec((D_PLE, D_MODEL), const),
            pl.BlockSpec((1, D_MODEL), const),
        ],
        out_specs=pl.BlockSpec((tm, D_MODEL), lambda i: (i, 0)),
        out_shape=jax.ShapeDtypeStruct((t, D_MODEL), F32),
        compiler_params=pltpu.CompilerParams(
            dimension_semantics=("arbitrary",), vmem_limit_bytes=VMEM_LIMIT),
        name="combine_ple",
    )(x1, slots, slots, slots, slots, topw, p2d, pg, wpg, wpp, fg)


def _routing_tables(top_idx_t, n_tok):
    n_assign = n_tok * TOP_K
    expert_flat = top_idx_t.reshape(-1)
    order = jnp.argsort(expert_flat, stable=True).astype(jnp.int32)
    counts = jnp.bincount(expert_flat, length=N_EXPERTS).astype(jnp.int32)
    start = jnp.cumsum(counts) - counts
    padded = (counts + MOE_ROWS - 1) // MOE_ROWS * MOE_ROWS
    pend = jnp.cumsum(padded)
    pstart = pend - padded
    n_rows = n_assign + N_EXPERTS * MOE_ROWS
    n_blocks = n_rows // MOE_ROWS
    block_start = jnp.arange(n_blocks, dtype=jnp.int32) * MOE_ROWS
    block_expert = jnp.minimum(jnp.sum(block_start[:, None] >= pend[None, :], axis=1),
                               N_EXPERTS - 1).astype(jnp.int32)
    nvalid = (pend[-1] // MOE_ROWS).astype(jnp.int32).reshape(1)
    is_e = block_expert[:, None] == jnp.arange(N_EXPERTS, dtype=jnp.int32)[None, :]
    per_block = lambda v: jnp.sum(jnp.where(is_e, v[None, :], 0), axis=1)
    nreal = jnp.clip(per_block(pstart + counts) - block_start, 0, MOE_ROWS).astype(jnp.int32)
    sorted_pos = (block_start + per_block(start - pstart))[:, None] + jnp.arange(MOE_ROWS, dtype=jnp.int32)[None, :]
    assign = order[jnp.clip(sorted_pos, 0, n_assign - 1)]
    row_token = assign % n_tok
    is_real = jnp.arange(MOE_ROWS, dtype=jnp.int32)[None, :] < nreal[:, None]
    row_q = block_start[:, None] + jnp.arange(MOE_ROWS, dtype=jnp.int32)[None, :]
    spare = n_assign + row_q - per_block(start + counts)[:, None]
    row_dest = jnp.where(is_real, assign, spare)
    return (block_expert, nvalid, (row_token * TILE_ROWS).reshape(n_blocks, 1, MOE_ROWS),
            (row_dest * TILE_ROWS).reshape(n_blocks, 1, MOE_ROWS), n_rows)


def _layer(x2d, p2d, bsz, seq, mix_norm_g, w_in, conv_w, conv_b, dt_bias, a_log, d_skip, ssd_norm_g,
           w_ssd_out, pool_w, pool_scale, w_mix_out, ffn_norm_g, w_router, b_router,
           w_gate_up, b_gate_up, w_down, b_down, ple_norm_g, w_ple_gate, w_ple_proj, out_g):
    n_tok = x2d.shape[0]
    dt0 = D_INNER + D_CONV
    w_main = jnp.concatenate([w_in[:, :dt0], w_in[:, dt0 + HEADS:]], axis=1).astype(BF16)
    proj, dt_raw = _in_proj(x2d, mix_norm_g[None, :], w_main, _hilo_weight(w_in[:, dt0:dt0 + HEADS]))

    pad_h = lambda v: jnp.pad(v, (0, LANES - HEADS))[None, :]
    ltri = (jnp.arange(CHUNK)[:, None] >= jnp.arange(CHUNK)[None, :]).astype(BF16)
    ltri3 = jnp.concatenate([ltri, ltri, ltri], axis=1)
    e1 = (jnp.arange(LANES)[:, None] == (jnp.arange(D_INNER) // HEAD_DIM)[None, :]).astype(BF16)
    e2 = jnp.concatenate([e1, e1], axis=0)
    yssd, ypool = _mixers(
        proj, dt_raw, bsz, seq, conv_w, conv_b[None, :], pad_h(dt_bias), pad_h(a_log),
        jnp.repeat(d_skip, HEAD_DIM)[None, :], ssd_norm_g[None, :], ltri3, e2,
        pool_w.astype(BF16), pool_scale[None, :])

    br = jnp.pad(b_router, (0, LANES - N_EXPERTS))[None, :]
    x1, h2, topw, topi = _mix_route(x2d, yssd, ypool, proj, w_ssd_out.astype(BF16),
                                    w_mix_out.astype(BF16), ffn_norm_g[None, :], _hilo_weight(w_router), br)

    block_expert, nvalid, row_token3, row_dest3, n_slot_rows = _routing_tables(topi[:TOP_K], n_tok)
    slots = _moe(block_expert, nvalid, row_token3, row_dest3, h2, w_gate_up, b_gate_up[:, None, :],
                 w_down, b_down[:, None, :], n_slot_rows)

    return _combine(x1, slots, topw, p2d, ple_norm_g[None, :], w_ple_gate.astype(BF16),
                    w_ple_proj.astype(BF16), out_g[None, :])


def kernel(x, p, mix_norm_g, w_in, conv_w, conv_b, dt_bias, a_log, d_skip, ssd_norm_g, w_ssd_out, pool_w,
           pool_scale, w_mix_out, ffn_norm_g, w_router, b_router, w_gate_up, b_gate_up, w_down, b_down,
           ple_norm_g, w_ple_gate, w_ple_proj, final_norm_g):
    bsz, seq, d = x.shape
    depth = p.shape[0]
    assert depth == 1 and d == D_MODEL and p.shape[-1] == D_PLE
    assert seq % (MIX_CHUNKS * CHUNK) == 0 and (bsz * seq) % 1024 == 0
    x2d = x.reshape(bsz * seq, d)
    out = _layer(x2d, p[0].reshape(bsz * seq, D_PLE), bsz, seq, mix_norm_g[0], w_in[0], conv_w[0], conv_b[0],
                 dt_bias[0], a_log[0], d_skip[0], ssd_norm_g[0], w_ssd_out[0], pool_w[0], pool_scale[0],
                 w_mix_out[0], ffn_norm_g[0], w_router[0], b_router[0], w_gate_up[0], b_gate_up[0],
                 w_down[0], b_down[0], ple_norm_g[0], w_ple_gate[0], w_ple_proj[0], final_norm_g)
    return out.reshape(bsz, seq, d)
```

```python
import functools

import jax
import jax.numpy as jnp
from jax import lax
from jax.experimental import pallas as pl
from jax.experimental.pallas import tpu as pltpu

F32 = jnp.float32
BF16 = jnp.bfloat16

D_MODEL = 1024
D_INNER = 2048
HEAD_DIM = 64
HEADS = 32
GROUPS = 4
HEADS_PER_GROUP = HEADS // GROUPS
GROUP_DIM = D_INNER // GROUPS
D_STATE = 128
CONV_WIDTH = 4
CHUNK = 128
D_BC = 2 * GROUPS * D_STATE
D_CONV = D_INNER + D_BC
POOL_WIDTH = D_MODEL
POOL_WINDOWS = (2, 4, 8, 16)
POOL_GROUP_DIM = POOL_WIDTH // len(POOL_WINDOWS)
N_EXPERTS = 32
TOP_K = 4
D_EXPERT = D_MODEL
SWIGLU_LIMIT = 7.0
SWIGLU_ALPHA = 1.702
D_PLE = 256
EPS = 1e-6

LANES = 128
TILE_ROWS = D_MODEL // (2 * LANES)
TOPI_ROWS = 8
ROUTE_SUB = 256
MIX_CHUNKS = 4
HALO = 16
D_PROJ = D_INNER + D_CONV + POOL_WIDTH + 2 * D_MODEL
MOE_ROWS = 256
VMEM_LIMIT = 56 * 1024 * 1024


def _split2(v):
    hi = v.astype(BF16)
    lo = (v - hi.astype(F32)).astype(BF16)
    return hi, lo


def _split3(v):
    hi = v.astype(BF16)
    r = v - hi.astype(F32)
    mid = r.astype(BF16)
    lo = (r - mid.astype(F32)).astype(BF16)
    return hi, mid, lo


def _hilo_weight(w):
    n = w.shape[1]
    hi, lo = _split2(w)
    top = jnp.pad(jnp.concatenate([hi, lo], axis=1), ((0, 0), (0, LANES - 2 * n)))
    bottom = jnp.pad(hi, ((0, 0), (0, LANES - n)))
    return jnp.concatenate([top, bottom], axis=0)


def _hilo_dot(hi, lo, w2_ref, n):
    k = hi.shape[1]
    a = jnp.dot(hi, w2_ref[0:k, :], preferred_element_type=F32)
    b = jnp.dot(lo, w2_ref[k:2 * k, :], preferred_element_type=F32)
    return a + pltpu.roll(a, LANES - n, axis=1) + b


def _rms(x, g):
    return x * lax.rsqrt(jnp.mean(x * x, axis=-1, keepdims=True) + EPS) * g


def _store_token_tiles(ref2d, val, row0=0):
    rows = val.shape[0]
    for j in range(TILE_ROWS):
        c0 = 2 * j * LANES
        hi = lax.bitcast_convert_type(val[:, c0:c0 + LANES].astype(BF16).astype(F32), jnp.uint32)
        lo = lax.bitcast_convert_type(val[:, c0 + LANES:c0 + 2 * LANES].astype(BF16).astype(F32), jnp.uint32)
        ref2d[pl.ds(row0 * TILE_ROWS + j, rows, stride=TILE_ROWS), :] = hi | lax.shift_right_logical(lo, jnp.uint32(16))


def _load_token_tiles(ref2d, rows):
    pieces = []
    for j in range(TILE_ROWS):
        words = ref2d[pl.ds(j, rows, stride=TILE_ROWS), :]
        pieces.append(lax.bitcast_convert_type(words & jnp.uint32(0xFFFF0000), F32))
        pieces.append(lax.bitcast_convert_type(lax.shift_left(words, jnp.uint32(16)), F32))
    return jnp.concatenate(pieces, axis=1)


def _in_proj_kernel(x_ref, g_ref, w_ref, wdt_ref, proj_ref, dt_ref, h_sc):
    @pl.when(pl.program_id(1) == 0)
    def _():
        h = _rms(x_ref[...], g_ref[...])
        hi, lo = _split2(h)
        h_sc[...] = hi
        dt_ref[...] = _hilo_dot(hi, lo, wdt_ref, HEADS)

    proj_ref[...] = jnp.dot(h_sc[...], w_ref[...], preferred_element_type=F32).astype(BF16)


def _in_proj(x2d, g, w_main, wdt2, tm=1024, tn=2048):
    t = x2d.shape[0]
    return pl.pallas_call(
        _in_proj_kernel,
        grid=(t // tm, D_PROJ // tn),
        in_specs=[
            pl.BlockSpec((tm, D_MODEL), lambda i, j: (i, 0)),
            pl.BlockSpec((1, D_MODEL), lambda i, j: (0, 0)),
            pl.BlockSpec((D_MODEL, tn), lambda i, j: (0, j)),
            pl.BlockSpec((2 * D_MODEL, LANES), lambda i, j: (0, 0)),
        ],
        out_specs=[
            pl.BlockSpec((tm, tn), lambda i, j: (i, j)),
            pl.BlockSpec((tm, LANES), lambda i, j: (i, 0)),
        ],
        out_shape=[
            jax.ShapeDtypeStruct((t, D_PROJ), BF16),
            jax.ShapeDtypeStruct((t, LANES), F32),
        ],
        scratch_shapes=[pltpu.VMEM((tm, D_MODEL), BF16)],
        compiler_params=pltpu.CompilerParams(
            dimension_semantics=("arbitrary", "arbitrary"), vmem_limit_bytes=VMEM_LIMIT),
        name="in_proj",
    )(x2d, g, w_main, wdt2)


def _mixers_kernel(z_ref, xs_ref, bc_ref, u_ref, dt_ref,
                   cw_ref, cb_ref, dtb_ref, alog_ref, dskip_ref, ng_ref, ltri_ref, e2_ref,
                   shift_ref, band_ref, pw_ref, ps_ref,
                   yssd_ref, ypool_ref,
                   ext_sc, extu_sc, state_sc):
    c = pl.program_id(1)
    rows = MIX_CHUNKS * CHUNK

    @pl.when(c == 0)
    def _():
        ext_sc[0:HALO, :] = jnp.zeros((HALO, D_CONV), BF16)
        extu_sc[0:HALO, :] = jnp.zeros((HALO, POOL_WIDTH), BF16)
        state_sc[...] = jnp.zeros_like(state_sc)

    @pl.when(c > 0)
    def _():
        ext_sc[0:HALO, :] = ext_sc[rows:rows + HALO, :]
        extu_sc[0:HALO, :] = extu_sc[rows:rows + HALO, :]

    ext_sc[HALO:HALO + rows, 0:D_INNER] = xs_ref[...]
    ext_sc[HALO:HALO + rows, D_INNER:D_CONV] = bc_ref[...]
    extu_sc[HALO:HALO + rows, :] = u_ref[...]
    for ci in range(MIX_CHUNKS):
        _mixers_chunk(ci, c * MIX_CHUNKS + ci, z_ref, u_ref, dt_ref, cw_ref, cb_ref, dtb_ref, alog_ref, dskip_ref,
                      ng_ref, ltri_ref, e2_ref, shift_ref, band_ref, pw_ref, ps_ref, yssd_ref, ypool_ref,
                      ext_sc, extu_sc, state_sc)


def _mixers_chunk(ci, chunk_index, z_ref, u_ref, dt_ref, cw_ref, cb_ref, dtb_ref, alog_ref, dskip_ref,
                  ng_ref, ltri_ref, e2_ref, shift_ref, band_ref, pw_ref, ps_ref, yssd_ref, ypool_ref,
                  ext_sc, extu_sc, state_sc):
    r0 = ci * CHUNK
    rs = slice(r0, r0 + CHUNK)
    ext = ext_sc[r0:r0 + HALO + CHUNK, :]
    conv = cb_ref[...] + cw_ref[CONV_WIDTH - 1:CONV_WIDTH, :] * ext[HALO:HALO + CHUNK, :].astype(F32)
    taps = jnp.dot(shift_ref[...], ext, preferred_element_type=F32)
    for k in range(CONV_WIDTH - 1):
        conv = conv + cw_ref[k:k + 1, :] * taps[k * CHUNK:(k + 1) * CHUNK, :]
    xc = conv * jax.nn.sigmoid(conv)
    xs = xc[:, 0:D_INNER]
    xs_b = xs.astype(BF16)

    dtv = jax.nn.softplus(dt_ref[rs, :] + dtb_ref[...])
    da = dtv * (-jnp.exp(alog_ref[...]))
    a_cum = jnp.dot(ltri_ref[...], jnp.concatenate(_split3(da), axis=0),
                    preferred_element_type=F32)
    expa = jnp.exp(a_cum)
    a_last = a_cum[CHUNK - 1:CHUNK, :]
    wst = dtv * jnp.exp(a_last - a_cum)
    a_cum_t = a_cum.T
    dt_t = dtv.T

    both = jnp.concatenate([wst, expa], axis=0)
    hi, lo = _split2(both)
    expd = jnp.dot(jnp.concatenate([hi, lo], axis=1), e2_ref[...],
                   preferred_element_type=F32)
    wst_x = expd[0:CHUNK, :]
    expa_x = expd[CHUNK:2 * CHUNK, :]
    xw_b = (xs * wst_x).astype(BF16)

    row = lax.broadcasted_iota(jnp.int32, (CHUNK, CHUNK), 0)
    col = lax.broadcasted_iota(jnp.int32, (CHUNK, CHUNK), 1)
    causal_bias = jnp.where(row >= col, 0.0, -jnp.inf).astype(F32)
    lane = lax.broadcasted_iota(jnp.int32, (CHUNK, LANES), 1)
    low_half = lane < HEAD_DIM

    y_groups = []
    for g in range(GROUPS):
        bg = xc[:, D_INNER + g * D_STATE:D_INNER + (g + 1) * D_STATE]
        cg = xc[:, D_INNER + GROUPS * D_STATE + g * D_STATE:D_INNER + GROUPS * D_STATE + (g + 1) * D_STATE]
        bg_b = bg.astype(BF16)
        cg_b = cg.astype(BF16)
        cbm = lax.dot_general(cg_b, bg_b, (((1,), (1,)), ((), ())), preferred_element_type=F32)
        gsl = slice(g * GROUP_DIM, (g + 1) * GROUP_DIM)

        prev_t = state_sc[g]
        y_off = jnp.dot(cg_b, prev_t.astype(BF16), preferred_element_type=F32) * expa_x[:, gsl]
        st_t = jnp.dot(bg.T.astype(BF16), xw_b[:, gsl], preferred_element_type=F32)
        state_sc[g] = prev_t * expa_x[CHUNK - 1:CHUNK, gsl] + st_t

        pairs = []
        for jp in range(HEADS_PER_GROUP // 2):
            ms = []
            for hh in range(2):
                h = g * HEADS_PER_GROUP + jp * 2 + hh
                seg = a_cum[:, h:h + 1] - a_cum_t[h:h + 1, :]
                dec = jnp.exp(seg + causal_bias)
                ms.append((cbm * dec * dt_t[h:h + 1, :]).astype(BF16))
            lhs = jnp.concatenate(ms, axis=1)
            c0 = g * GROUP_DIM + jp * LANES
            xp = xs_b[:, c0:c0 + LANES]
            zero = jnp.zeros_like(xp)
            rhs = jnp.concatenate([jnp.where(low_half, xp, zero), jnp.where(low_half, zero, xp)], axis=0)
            pairs.append(jnp.dot(lhs, rhs, preferred_element_type=F32))
        y_diag = jnp.concatenate(pairs, axis=1)

        yg = y_diag + y_off + dskip_ref[:, gsl] * xs[:, gsl]
        zg = z_ref[rs, gsl].astype(F32)
        yg = yg * (zg * jax.nn.sigmoid(zg))
        yg = yg * lax.rsqrt(jnp.mean(yg * yg, axis=-1, keepdims=True) + EPS) * ng_ref[:, gsl]
        y_groups.append(yg.astype(BF16))
    yssd_ref[rs, :] = jnp.concatenate(y_groups, axis=1)

    pos = chunk_index * CHUNK + lax.broadcasted_iota(jnp.int32, (CHUNK, 1), 0)
    outs = []
    for gi, w in enumerate(POOL_WINDOWS):
        psl = slice(gi * POOL_GROUP_DIM, (gi + 1) * POOL_GROUP_DIM)
        s = jnp.dot(band_ref[gi], extu_sc[r0:r0 + HALO + CHUNK, psl], preferred_element_type=F32)
        cnt = jnp.minimum(pos + 1, w).astype(F32)
        pooled = s / cnt - u_ref[rs, psl].astype(F32)
        outs.append(jnp.dot(pooled.astype(BF16), pw_ref[gi], preferred_element_type=F32))
    ypool_ref[rs, :] = (jnp.concatenate(outs, axis=1) * ps_ref[...]).astype(BF16)


def _mixers(proj, dt_raw, bsz, seq, cw, cb, dtb, alog, dskip, ng, ltri3, e2, pw, ps):
    rows = MIX_CHUNKS * CHUNK
    nc = seq // rows
    t = bsz * seq
    rowmap = lambda b, c: b * nc + c
    const2 = lambda b, c: (0, 0)
    const3 = lambda b, c: (0, 0, 0)
    trow = jnp.arange(CHUNK)[:, None] + HALO
    jcol = jnp.arange(HALO + CHUNK)[None, :]
    shifts = jnp.concatenate([(jcol == trow - (CONV_WIDTH - 1) + k) for k in range(CONV_WIDTH - 1)],
                             axis=0).astype(BF16)
    bands = jnp.stack([(jcol <= trow) & (jcol > trow - w) for w in POOL_WINDOWS]).astype(BF16)
    return pl.pallas_call(
        _mixers_kernel,
        grid=(bsz, nc),
        in_specs=[
            pl.BlockSpec((rows, D_INNER), lambda b, c: (rowmap(b, c), 0)),
            pl.BlockSpec((rows, D_INNER), lambda b, c: (rowmap(b, c), 1)),
            pl.BlockSpec((rows, D_BC), lambda b, c: (rowmap(b, c), 4)),
            pl.BlockSpec((rows, POOL_WIDTH), lambda b, c: (rowmap(b, c), 5)),
            pl.BlockSpec((rows, LANES), lambda b, c: (rowmap(b, c), 0)),
            pl.BlockSpec((CONV_WIDTH, D_CONV), const2),
            pl.BlockSpec((1, D_CONV), const2),
            pl.BlockSpec((1, LANES), const2),
            pl.BlockSpec((1, LANES), const2),
            pl.BlockSpec((1, D_INNER), const2),
            pl.BlockSpec((1, D_INNER), const2),
            pl.BlockSpec((CHUNK, 3 * CHUNK), const2),
            pl.BlockSpec((2 * LANES, D_INNER), const2),
            pl.BlockSpec(((CONV_WIDTH - 1) * CHUNK, HALO + CHUNK), const2),
            pl.BlockSpec((len(POOL_WINDOWS), CHUNK, HALO + CHUNK), const3),
            pl.BlockSpec((len(POOL_WINDOWS), POOL_GROUP_DIM, POOL_GROUP_DIM), const3),
            pl.BlockSpec((1, POOL_WIDTH), const2),
        ],
        out_specs=[
            pl.BlockSpec((rows, D_INNER), lambda b, c: (rowmap(b, c), 0)),
            pl.BlockSpec((rows, POOL_WIDTH), lambda b, c: (rowmap(b, c), 0)),
        ],
        out_shape=[
            jax.ShapeDtypeStruct((t, D_INNER), BF16),
            jax.ShapeDtypeStruct((t, POOL_WIDTH), BF16),
        ],
        scratch_shapes=[
            pltpu.VMEM((HALO + rows, D_CONV), BF16),
            pltpu.VMEM((HALO + rows, POOL_WIDTH), BF16),
            pltpu.VMEM((GROUPS, D_STATE, GROUP_DIM), F32),
        ],
        compiler_params=pltpu.CompilerParams(
            dimension_semantics=("arbitrary", "arbitrary"), vmem_limit_bytes=VMEM_LIMIT),
        name="mixers",
    )(proj, proj, proj, proj, dt_raw, cw, cb, dtb, alog, dskip, ng, ltri3, e2, shifts, bands, pw, ps)


def _mix_route_kernel(x_ref, yssd_ref, ypool_ref, gates_ref, wso_ref, wmo_ref, fg_ref, wr_ref, br_ref,
                      x1_ref, h2_ref, topw_ref, topi_ref):
    for r0 in range(0, x_ref.shape[0], ROUTE_SUB):
        _mix_route_rows(slice(r0, r0 + ROUTE_SUB), x_ref, yssd_ref, ypool_ref, gates_ref, wso_ref, wmo_ref, fg_ref,
                        wr_ref, br_ref, x1_ref, h2_ref, topw_ref, topi_ref)


def _mix_route_rows(rs, x_ref, yssd_ref, ypool_ref, gates_ref, wso_ref, wmo_ref, fg_ref, wr_ref, br_ref,
                    x1_ref, h2_ref, topw_ref, topi_ref):
    y_ssd = jnp.dot(yssd_ref[rs, :], wso_ref[...], preferred_element_type=F32)
    gates = jax.nn.sigmoid(gates_ref[rs, :].astype(F32))
    mixed = gates[:, 0:D_MODEL] * y_ssd + gates[:, D_MODEL:2 * D_MODEL] * ypool_ref[rs, :].astype(F32)
    x1 = x_ref[rs, :] + jnp.dot(mixed.astype(BF16), wmo_ref[...], preferred_element_type=F32)
    x1_ref[rs, :] = x1
    h2 = _rms(x1, fg_ref[...])
    _store_token_tiles(h2_ref, h2, rs.start)

    hi, lo = _split2(h2)
    logits = _hilo_dot(hi, lo, wr_ref, N_EXPERTS) + br_ref[...]
    tm = logits.shape[0]
    lane = lax.broadcasted_iota(jnp.int32, (tm, LANES), 1)
    neg = jnp.float32(-jnp.inf)
    work = jnp.where(lane < N_EXPERTS, logits, neg)
    vals = []
    idxs = []
    for _ in range(TOP_K):
        m = jnp.max(work, axis=-1, keepdims=True)
        idx = jnp.min(jnp.where(work == m, lane, LANES), axis=-1, keepdims=True)
        vals.append(m)
        idxs.append(idx)
        work = jnp.where(lane == idx, neg, work)
    es = [jnp.exp(v - vals[0]) for v in vals]
    den = es[0] + es[1] + es[2] + es[3]
    topw = jnp.zeros((tm, LANES), F32)
    topi = jnp.zeros((tm, LANES), jnp.int32)
    for k in range(TOP_K):
        topw = jnp.where(lane == k, es[k] / den, topw)
        topi = jnp.where(lane == k, idxs[k], topi)
    topw_ref[rs, :] = topw
    topi_ref[:, rs] = topi.T[0:TOPI_ROWS, :]


def _mix_route(x2d, yssd, ypool, proj, wso, wmo, fg, wr2, br, tm=512):
    t = x2d.shape[0]
    const = lambda i: (0, 0)
    return pl.pallas_call(
        _mix_route_kernel,
        grid=(t // tm,),
        in_specs=[
            pl.BlockSpec((tm, D_MODEL), lambda i: (i, 0)),
            pl.BlockSpec((tm, D_INNER), lambda i: (i, 0)),
            pl.BlockSpec((tm, POOL_WIDTH), lambda i: (i, 0)),
            pl.BlockSpec((tm, 2 * D_MODEL), lambda i: (i, 3)),
            pl.BlockSpec((D_INNER, D_MODEL), const),
            pl.BlockSpec((D_MODEL, D_MODEL), const),
            pl.BlockSpec((1, D_MODEL), const),
            pl.BlockSpec((2 * D_MODEL, LANES), const),
            pl.BlockSpec((1, LANES), const),
        ],
        out_specs=[
            pl.BlockSpec((tm, D_MODEL), lambda i: (i, 0)),
            pl.BlockSpec((tm * TILE_ROWS, LANES), lambda i: (i, 0)),
            pl.BlockSpec((tm, LANES), lambda i: (i, 0)),
            pl.BlockSpec((TOPI_ROWS, tm), lambda i: (0, i)),
        ],
        out_shape=[
            jax.ShapeDtypeStruct((t, D_MODEL), F32),
            jax.ShapeDtypeStruct((t * TILE_ROWS, LANES), jnp.uint32),
            jax.ShapeDtypeStruct((t, LANES), F32),
            jax.ShapeDtypeStruct((TOPI_ROWS, t), jnp.int32),
        ],
        compiler_params=pltpu.CompilerParams(
            dimension_semantics=("arbitrary",), vmem_limit_bytes=VMEM_LIMIT),
        name="mix_route",
    )(x2d, yssd, ypool, proj, wso, wmo, fg, wr2, br)


def _moe_kernel(be_ref, nvalid_ref,
                tok0_ref, toknext_ref, dstprev_ref,
                h_hbm, wgu_ref, bgu_ref, wd_ref, bd_ref,
                slots_hbm,
                xbuf, ybuf, xb_sc, wgu_bf, wd_bf, gsem, ssem, *, nb):
    i = pl.program_id(0)
    nvalid = nvalid_ref[0]
    slot = lax.rem(i, 2)
    has_next = i + 1 < nvalid

    def tile(off, n=1):
        return pl.ds(pl.multiple_of(off, TILE_ROWS), n * TILE_ROWS)

    def gather_copy(tok, r, s):
        return pltpu.make_async_copy(h_hbm.at[tile(tok)], xbuf.at[s, tile(r * TILE_ROWS)], gsem.at[s])

    def scatter_copy(dst, r, s):
        return pltpu.make_async_copy(ybuf.at[s, tile(r * TILE_ROWS)], slots_hbm.at[tile(dst)], ssem.at[s])

    def wait_gather(s):
        pltpu.make_async_copy(h_hbm.at[tile(0, MOE_ROWS)], xbuf.at[s, tile(0, MOE_ROWS)], gsem.at[s]).wait()

    def wait_scatter(s):
        pltpu.make_async_copy(ybuf.at[s, tile(0, MOE_ROWS)], slots_hbm.at[tile(0, MOE_ROWS)], ssem.at[s]).wait()

    def scatter_whole_block(s):
        pltpu.make_async_copy(ybuf.at[s, tile(0, MOE_ROWS)], slots_hbm.at[tile(dstprev_ref[0, 0, 0], MOE_ROWS)],
                              ssem.at[s]).start()

    def issue(s, gather_ref, scatter_ref):
        for r in range(MOE_ROWS):
            if gather_ref is not None:
                gather_copy(gather_ref[0, 0, r], r, s).start()
            if scatter_ref is not None:
                scatter_copy(scatter_ref[0, 0, r], r, s).start()

    @pl.when(i == 0)
    def _():
        ybuf[...] = jnp.zeros_like(ybuf)
        issue(0, tok0_ref, None)

    for par in (0, 1):
        other = 1 - par
        mine = slot == par

        pl.when(jnp.logical_and(mine, i >= 2))(functools.partial(wait_scatter, par))
        pl.when(jnp.logical_and(mine, i < nvalid))(functools.partial(wait_gather, par))
        pl.when(jnp.logical_and(mine, jnp.logical_and(has_next, i == 0)))(
            functools.partial(issue, other, toknext_ref, None))
        pl.when(jnp.logical_and(mine, jnp.logical_and(has_next, i >= 1)))(
            functools.partial(issue, other, toknext_ref, dstprev_ref))
        pl.when(jnp.logical_and(mine, jnp.logical_and(jnp.logical_not(has_next), jnp.logical_and(i >= 1, i <= nvalid))))(
            functools.partial(issue, other, None, dstprev_ref))
        pl.when(jnp.logical_and(mine, i > nvalid))(functools.partial(scatter_whole_block, other))
        pl.when(jnp.logical_and(mine, i == nb))(functools.partial(wait_scatter, other))

    @pl.when(i < nvalid)
    def _():
        @pl.when(jnp.logical_or(i == 0, be_ref[i] != be_ref[jnp.maximum(i - 1, 0)]))
        def _():
            wgu_bf[...] = wgu_ref[0].astype(BF16)
            wd_bf[...] = wd_ref[0].astype(BF16)

        xb_sc[...] = _load_token_tiles(xbuf.at[slot], MOE_ROWS).astype(BF16)
        gu = jnp.dot(xb_sc[...], wgu_bf[...], preferred_element_type=F32) + bgu_ref[0]
        gate = jnp.minimum(gu[:, 0:D_EXPERT], SWIGLU_LIMIT)
        up = jnp.clip(gu[:, D_EXPERT:2 * D_EXPERT], -SWIGLU_LIMIT, SWIGLU_LIMIT)
        act = (up + 1.0) * gate * jax.nn.sigmoid(SWIGLU_ALPHA * gate)
        y = jnp.dot(act.astype(BF16), wd_bf[...], preferred_element_type=F32) + bd_ref[0]
        _store_token_tiles(ybuf.at[slot], y)


def _moe(block_expert, nvalid, row_token3, row_dest3, h2, wgu, bgu3, wd, bd3, n_slot_rows):
    nb = block_expert.shape[0]
    expert_of = lambda i, be: be[jnp.minimum(i, nb - 1)]
    grid_spec = pltpu.PrefetchScalarGridSpec(
        num_scalar_prefetch=2,
        grid=(nb + 1,),
        in_specs=[
            pl.BlockSpec((1, 1, MOE_ROWS), lambda i, be, nv: (0, 0, 0), memory_space=pltpu.SMEM),
            pl.BlockSpec((1, 1, MOE_ROWS), lambda i, be, nv: (jnp.minimum(i + 1, nb - 1), 0, 0),
                         memory_space=pltpu.SMEM),
            pl.BlockSpec((1, 1, MOE_ROWS), lambda i, be, nv: (jnp.clip(i - 1, 0, nb - 1), 0, 0),
                         memory_space=pltpu.SMEM),
            pl.BlockSpec(memory_space=pl.ANY),
            pl.BlockSpec((1, D_MODEL, 2 * D_EXPERT), lambda i, be, nv: (expert_of(i, be), 0, 0)),
            pl.BlockSpec((1, 1, 2 * D_EXPERT), lambda i, be, nv: (expert_of(i, be), 0, 0)),
            pl.BlockSpec((1, D_EXPERT, D_MODEL), lambda i, be, nv: (expert_of(i, be), 0, 0)),
            pl.BlockSpec((1, 1, D_MODEL), lambda i, be, nv: (expert_of(i, be), 0, 0)),
        ],
        out_specs=pl.BlockSpec(memory_space=pl.ANY),
        scratch_shapes=[
            pltpu.VMEM((2, MOE_ROWS * TILE_ROWS, LANES), jnp.uint32),
            pltpu.VMEM((2, MOE_ROWS * TILE_ROWS, LANES), jnp.uint32),
            pltpu.VMEM((MOE_ROWS, D_MODEL), BF16),
            pltpu.VMEM((D_MODEL, 2 * D_EXPERT), BF16),
            pltpu.VMEM((D_EXPERT, D_MODEL), BF16),
            pltpu.SemaphoreType.DMA((2,)),
            pltpu.SemaphoreType.DMA((2,)),
        ],
    )
    return pl.pallas_call(
        functools.partial(_moe_kernel, nb=nb),
        grid_spec=grid_spec,
        out_shape=jax.ShapeDtypeStruct((n_slot_rows * TILE_ROWS, LANES), jnp.uint32),
        compiler_params=pltpu.CompilerParams(
            dimension_semantics=("arbitrary",), vmem_limit_bytes=VMEM_LIMIT),
        name="moe_experts",
    )(block_expert, nvalid, row_token3, row_token3, row_dest3, h2, wgu, bgu3, wd, bd3)


def _combine_kernel(x1_ref, s0_ref, s1_ref, s2_ref, s3_ref, topw_ref, p_ref, pg_ref, wpg_ref, wpp_ref, fg_ref,
                    out_ref):
    x2 = x1_ref[...]
    topw = topw_ref[...]
    for k, s_ref in enumerate((s0_ref, s1_ref, s2_ref, s3_ref)):
        x2 = x2 + _load_token_tiles(s_ref, x2.shape[0]) * topw[:, k:k + 1]
    n = _rms(x2, pg_ref[...])
    gate = jax.nn.sigmoid(jnp.dot(n.astype(BF16), wpg_ref[...], preferred_element_type=F32))
    pp = jnp.dot(p_ref[...].astype(BF16), wpp_ref[...], preferred_element_type=F32)
    x3 = x2 + gate * pp
    out_ref[...] = _rms(x3, fg_ref[...])


def _combine(x1, slots, topw, p2d, pg, wpg, wpp, fg, tm=512):
    t = x1.shape[0]
    nt = t // tm
    const = lambda i: (0, 0)
    slot_specs = [pl.BlockSpec((tm * TILE_ROWS, LANES), functools.partial(lambda k, i: (k * nt + i, 0), k))
                  for k in range(TOP_K)]
    return pl.pallas_call(
        _combine_kernel,
        grid=(nt,),
        in_specs=[
            pl.BlockSpec((tm, D_MODEL), lambda i: (i, 0)),
            *slot_specs,
            pl.BlockSpec((tm, LANES), lambda i: (i, 0)),
            pl.BlockSpec((tm, D_PLE), lambda i: (i, 0)),
            pl.BlockSpec((1, D_MODEL), const),
            pl.BlockSpec((D_MODEL, D_MODEL), const),
            pl.BlockSpec((D_PLE, D_MODEL), const),
            pl.BlockSpec((1, D_MODEL), const),
        ],
        out_specs=pl.BlockSpec((tm, D_MODEL), lambda i: (i, 0)),
        out_shape=jax.ShapeDtypeStruct((t, D_MODEL), F32),
        compiler_params=pltpu.CompilerParams(
            dimension_semantics=("arbitrary",), vmem_limit_bytes=VMEM_LIMIT),
        name="combine_ple",
    )(x1, slots, slots, slots, slots, topw, p2d, pg, wpg, wpp, fg)


def _routing_tables(top_idx_t, n_tok):
    n_assign = n_tok * TOP_K
    expert_flat = top_idx_t.reshape(-1)
    order = jnp.argsort(expert_flat, stable=True).astype(jnp.int32)
    counts = jnp.bincount(expert_flat, length=N_EXPERTS).astype(jnp.int32)
    start = jnp.cumsum(counts) - counts
    padded = (counts + MOE_ROWS - 1) // MOE_ROWS * MOE_ROWS
    pend = jnp.cumsum(padded)
    pstart = pend - padded
    n_rows = n_assign + N_EXPERTS * MOE_ROWS
    n_blocks = n_rows // MOE_ROWS
    block_start = jnp.arange(n_blocks, dtype=jnp.int32) * MOE_ROWS
    block_expert = jnp.minimum(jnp.sum(block_start[:, None] >= pend[None, :], axis=1),
                               N_EXPERTS - 1).astype(jnp.int32)
    nvalid = (pend[-1] // MOE_ROWS).astype(jnp.int32).reshape(1)
    is_e = block_expert[:, None] == jnp.arange(N_EXPERTS, dtype=jnp.int32)[None, :]
    per_block = lambda v: jnp.sum(jnp.where(is_e, v[None, :], 0), axis=1)
    nreal = jnp.clip(per_block(pstart + counts) - block_start, 0, MOE_ROWS).astype(jnp.int32)
    sorted_pos = (block_start + per_block(start - pstart))[:, None] + jnp.arange(MOE_ROWS, dtype=jnp.int32)[None, :]
    assign = order[jnp.clip(sorted_pos, 0, n_assign - 1)]
    row_token = assign % n_tok
    is_real = jnp.arange(MOE_ROWS, dtype=jnp.int32)[None, :] < nreal[:, None]
    row_q = block_start[:, None] + jnp.arange(MOE_ROWS, dtype=jnp.int32)[None, :]
    spare = n_assign + row_q - per_block(start + counts)[:, None]
    row_dest = jnp.where(is_real, assign, spare)
    return (block_expert, nvalid, (row_token * TILE_ROWS).reshape(n_blocks, 1, MOE_ROWS),
            (row_dest * TILE_ROWS).reshape(n_blocks, 1, MOE_ROWS), n_rows)


def _layer(x2d, p2d, bsz, seq, mix_norm_g, w_in, conv_w, conv_b, dt_bias, a_log, d_skip, ssd_norm_g,
           w_ssd_out, pool_w, pool_scale, w_mix_out, ffn_norm_g, w_router, b_router,
           w_gate_up, b_gate_up, w_down, b_down, ple_norm_g, w_ple_gate, w_ple_proj, out_g):
    n_tok = x2d.shape[0]
    dt0 = D_INNER + D_CONV
    w_main = jnp.concatenate([w_in[:, :dt0], w_in[:, dt0 + HEADS:]], axis=1).astype(BF16)
    proj, dt_raw = _in_proj(x2d, mix_norm_g[None, :], w_main, _hilo_weight(w_in[:, dt0:dt0 + HEADS]))

    pad_h = lambda v: jnp.pad(v, (0, LANES - HEADS))[None, :]
    ltri = (jnp.arange(CHUNK)[:, None] >= jnp.arange(CHUNK)[None, :]).astype(BF16)
    ltri3 = jnp.concatenate([ltri, ltri, ltri], axis=1)
    e1 = (jnp.arange(LANES)[:, None] == (jnp.arange(D_INNER) // HEAD_DIM)[None, :]).astype(BF16)
    e2 = jnp.concatenate([e1, e1], axis=0)
    yssd, ypool = _mixers(
        proj, dt_raw, bsz, seq, conv_w, conv_b[None, :], pad_h(dt_bias), pad_h(a_log),
        jnp.repeat(d_skip, HEAD_DIM)[None, :], ssd_norm_g[None, :], ltri3, e2,
        pool_w.astype(BF16), pool_scale[None, :])

    br = jnp.pad(b_router, (0, LANES - N_EXPERTS))[None, :]
    x1, h2, topw, topi = _mix_route(x2d, yssd, ypool, proj, w_ssd_out.astype(BF16),
                                    w_mix_out.astype(BF16), ffn_norm_g[None, :], _hilo_weight(w_router), br)

    block_expert, nvalid, row_token3, row_dest3, n_slot_rows = _routing_tables(topi[:TOP_K], n_tok)
    slots = _moe(block_expert, nvalid, row_token3, row_dest3, h2, w_gate_up, b_gate_up[:, None, :],
                 w_down, b_down[:, None, :], n_slot_rows)

    return _combine(x1, slots, topw, p2d, ple_norm_g[None, :], w_ple_gate.astype(BF16),
                    w_ple_proj.astype(BF16), out_g[None, :])


def kernel(x, p, mix_norm_g, w_in, conv_w, conv_b, dt_bias, a_log, d_skip, ssd_norm_g, w_ssd_out, pool_w,
           pool_scale, w_mix_out, ffn_norm_g, w_router, b_router, w_gate_up, b_gate_up, w_down, b_down,
           ple_norm_g, w_ple_gate, w_ple_proj, final_norm_g):
    bsz, seq, d = x.shape
    depth = p.shape[0]
    assert depth == 1 and d == D_MODEL and p.shape[-1] == D_PLE
    assert seq % (MIX_CHUNKS * CHUNK) == 0 and (bsz * seq) % 1024 == 0
    x2d = x.reshape(bsz * seq, d)
    out = _layer(x2d, p[0].reshape(bsz * seq, D_PLE), bsz, seq, mix_norm_g[0], w_in[0], conv_w[0], conv_b[0],
                 dt_bias[0], a_log[0], d_skip[0], ssd_norm_g[0], w_ssd_out[0], pool_w[0], pool_scale[0],
                 w_mix_out[0], ffn_norm_g[0], w_router[0], b_router[0], w_gate_up[0], b_gate_up[0],
                 w_down[0], b_down[0], ple_norm_g[0], w_ple_gate[0], w_ple_proj[0], final_norm_g)
    return out.reshape(bsz, seq, d)
```

```python
import functools

import jax
import jax.numpy as jnp
from jax import lax
from jax.experimental import pallas as pl
from jax.experimental.pallas import tpu as pltpu

F32 = jnp.float32
BF16 = jnp.bfloat16

D_MODEL = 1024
D_INNER = 2048
HEAD_DIM = 64
HEADS = 32
GROUPS = 4
HEADS_PER_GROUP = HEADS // GROUPS
GROUP_DIM = D_INNER // GROUPS
D_STATE = 128
CONV_WIDTH = 4
CHUNK = 128
D_BC = 2 * GROUPS * D_STATE
D_CONV = D_INNER + D_BC
POOL_WIDTH = D_MODEL
POOL_WINDOWS = (2, 4, 8, 16)
POOL_GROUP_DIM = POOL_WIDTH // len(POOL_WINDOWS)
N_EXPERTS = 32
TOP_K = 4
D_EXPERT = D_MODEL
SWIGLU_LIMIT = 7.0
SWIGLU_ALPHA = 1.702
D_PLE = 256
EPS = 1e-6

LANES = 128
TILE_ROWS = D_MODEL // (2 * LANES)
TOPI_ROWS = 8
ROUTE_SUB = 256
MIX_CHUNKS = 4
HALO = 16
D_PROJ = D_INNER + D_CONV + POOL_WIDTH + 2 * D_MODEL
MOE_ROWS = 256
VMEM_LIMIT = 56 * 1024 * 1024


def _split2(v):
    hi = v.astype(BF16)
    lo = (v - hi.astype(F32)).astype(BF16)
    return hi, lo


def _split3(v):
    hi = v.astype(BF16)
    r = v - hi.astype(F32)
    mid = r.astype(BF16)
    lo = (r - mid.astype(F32)).astype(BF16)
    return hi, mid, lo


def _hilo_weight(w):
    n = w.shape[1]
    hi, lo = _split2(w)
    top = jnp.pad(jnp.concatenate([hi, lo], axis=1), ((0, 0), (0, LANES - 2 * n)))
    bottom = jnp.pad(hi, ((0, 0), (0, LANES - n)))
    return jnp.concatenate([top, bottom], axis=0)


def _hilo_dot(hi, lo, w2_ref, n):
    k = hi.shape[1]
    a = jnp.dot(hi, w2_ref[0:k, :], preferred_element_type=F32)
    b = jnp.dot(lo, w2_ref[k:2 * k, :], preferred_element_type=F32)
    return a + pltpu.roll(a, LANES - n, axis=1) + b


def _rms(x, g):
    return x * lax.rsqrt(jnp.mean(x * x, axis=-1, keepdims=True) + EPS) * g


def _store_token_tiles(ref2d, val, row0=0):
    rows = val.shape[0]
    for j in range(TILE_ROWS):
        c0 = 2 * j * LANES
        hi = lax.bitcast_convert_type(val[:, c0:c0 + LANES].astype(BF16).astype(F32), jnp.uint32)
        lo = lax.bitcast_convert_type(val[:, c0 + LANES:c0 + 2 * LANES].astype(BF16).astype(F32), jnp.uint32)
        ref2d[pl.ds(row0 * TILE_ROWS + j, rows, stride=TILE_ROWS), :] = hi | lax.shift_right_logical(lo, jnp.uint32(16))


def _load_token_tiles(ref2d, rows):
    pieces = []
    for j in range(TILE_ROWS):
        words = ref2d[pl.ds(j, rows, stride=TILE_ROWS), :]
        pieces.append(lax.bitcast_convert_type(words & jnp.uint32(0xFFFF0000), F32))
        pieces.append(lax.bitcast_convert_type(lax.shift_left(words, jnp.uint32(16)), F32))
    return jnp.concatenate(pieces, axis=1)


def _in_proj_kernel(x_ref, g_ref, w_ref, wdt_ref, proj_ref, dt_ref, h_sc):
    @pl.when(pl.program_id(1) == 0)
    def _():
        h = _rms(x_ref[...], g_ref[...])
        hi, lo = _split2(h)
        h_sc[...] = hi
        dt_ref[...] = _hilo_dot(hi, lo, wdt_ref, HEADS)

    proj_ref[...] = jnp.dot(h_sc[...], w_ref[...], preferred_element_type=F32).astype(BF16)


def _in_proj(x2d, g, w_main, wdt2, tm=1024, tn=2048):
    t = x2d.shape[0]
    return pl.pallas_call(
        _in_proj_kernel,
        grid=(t // tm, D_PROJ // tn),
        in_specs=[
            pl.BlockSpec((tm, D_MODEL), lambda i, j: (i, 0)),
            pl.BlockSpec((1, D_MODEL), lambda i, j: (0, 0)),
            pl.BlockSpec((D_MODEL, tn), lambda i, j: (0, j)),
            pl.BlockSpec((2 * D_MODEL, LANES), lambda i, j: (0, 0)),
        ],
        out_specs=[
            pl.BlockSpec((tm, tn), lambda i, j: (i, j)),
            pl.BlockSpec((tm, LANES), lambda i, j: (i, 0)),
        ],
        out_shape=[
            jax.ShapeDtypeStruct((t, D_PROJ), BF16),
            jax.ShapeDtypeStruct((t, LANES), F32),
        ],
        scratch_shapes=[pltpu.VMEM((tm, D_MODEL), BF16)],
        compiler_params=pltpu.CompilerParams(
            dimension_semantics=("arbitrary", "arbitrary"), vmem_limit_bytes=VMEM_LIMIT),
        name="in_proj",
    )(x2d, g, w_main, wdt2)


def _mixers_kernel(z_ref, xs_ref, bc_ref, u_ref, dt_ref,
                   cw_ref, cb_ref, dtb_ref, alog_ref, dskip_ref, ng_ref, ltri_ref, e2_ref,
                   shift_ref, band_ref, pw_ref, ps_ref,
                   yssd_ref, ypool_ref,
                   ext_sc, extu_sc, state_sc):
    c = pl.program_id(1)
    rows = MIX_CHUNKS * CHUNK

    @pl.when(c == 0)
    def _():
        ext_sc[0:HALO, :] = jnp.zeros((HALO, D_CONV), BF16)
        extu_sc[0:HALO, :] = jnp.zeros((HALO, POOL_WIDTH), BF16)
        state_sc[...] = jnp.zeros_like(state_sc)

    @pl.when(c > 0)
    def _():
        ext_sc[0:HALO, :] = ext_sc[rows:rows + HALO, :]
        extu_sc[0:HALO, :] = extu_sc[rows:rows + HALO, :]

    ext_sc[HALO:HALO + rows, 0:D_INNER] = xs_ref[...]
    ext_sc[HALO:HALO + rows, D_INNER:D_CONV] = bc_ref[...]
    extu_sc[HALO:HALO + rows, :] = u_ref[...]
    for ci in range(MIX_CHUNKS):
        _mixers_chunk(ci, c * MIX_CHUNKS + ci, z_ref, u_ref, dt_ref, cw_ref, cb_ref, dtb_ref, alog_ref, dskip_ref,
                      ng_ref, ltri_ref, e2_ref, shift_ref, band_ref, pw_ref, ps_ref, yssd_ref, ypool_ref,
                      ext_sc, extu_sc, state_sc)


def _mixers_chunk(ci, chunk_index, z_ref, u_ref, dt_ref, cw_ref, cb_ref, dtb_ref, alog_ref, dskip_ref,
                  ng_ref, ltri_ref, e2_ref, shift_ref, band_ref, pw_ref, ps_ref, yssd_ref, ypool_ref,
                  ext_sc, extu_sc, state_sc):
    r0 = ci * CHUNK
    rs = slice(r0, r0 + CHUNK)
    ext = ext_sc[r0:r0 + HALO + CHUNK, :]
    conv = cb_ref[...] + cw_ref[CONV_WIDTH - 1:CONV_WIDTH, :] * ext[HALO:HALO + CHUNK, :].astype(F32)
    for k in range(CONV_WIDTH - 1):
        conv = conv + cw_ref[k:k + 1, :] * jnp.dot(shift_ref[k], ext, preferred_element_type=F32)
    xc = conv * jax.nn.sigmoid(conv)
    xs = xc[:, 0:D_INNER]
    xs_b = xs.astype(BF16)

    dtv = jax.nn.softplus(dt_ref[rs, :] + dtb_ref[...])
    da = dtv * (-jnp.exp(alog_ref[...]))
    a_cum = jnp.dot(ltri_ref[...], jnp.concatenate(_split3(da), axis=0),
                    preferred_element_type=F32)
    expa = jnp.exp(a_cum)
    a_last = a_cum[CHUNK - 1:CHUNK, :]
    wst = dtv * jnp.exp(a_last - a_cum)
    a_cum_t = a_cum.T
    dt_t = dtv.T

    both = jnp.concatenate([wst, expa], axis=0)
    hi, lo = _split2(both)
    expd = jnp.dot(jnp.concatenate([hi, lo], axis=1), e2_ref[...],
                   preferred_element_type=F32)
    wst_x = expd[0:CHUNK, :]
    expa_x = expd[CHUNK:2 * CHUNK, :]
    xw_b = (xs * wst_x).astype(BF16)

    row = lax.broadcasted_iota(jnp.int32, (CHUNK, CHUNK), 0)
    col = lax.broadcasted_iota(jnp.int32, (CHUNK, CHUNK), 1)
    causal_bias = jnp.where(row >= col, 0.0, -jnp.inf).astype(F32)
    lane = lax.broadcasted_iota(jnp.int32, (CHUNK, LANES), 1)
    low_half = lane < HEAD_DIM

    y_groups = []
    for g in range(GROUPS):
        bg = xc[:, D_INNER + g * D_STATE:D_INNER + (g + 1) * D_STATE]
        cg = xc[:, D_INNER + GROUPS * D_STATE + g * D_STATE:D_INNER + GROUPS * D_STATE + (g + 1) * D_STATE]
        bg_b = bg.astype(BF16)
        cg_b = cg.astype(BF16)
        cbm = lax.dot_general(cg_b, bg_b, (((1,), (1,)), ((), ())), preferred_element_type=F32)
        gsl = slice(g * GROUP_DIM, (g + 1) * GROUP_DIM)

        prev_t = state_sc[g]
        y_off = jnp.dot(cg_b, prev_t.astype(BF16), preferred_element_type=F32) * expa_x[:, gsl]
        st_t = jnp.dot(bg.T.astype(BF16), xw_b[:, gsl], preferred_element_type=F32)
        state_sc[g] = prev_t * expa_x[CHUNK - 1:CHUNK, gsl] + st_t

        pairs = []
        for jp in range(HEADS_PER_GROUP // 2):
            ms = []
            for hh in range(2):
                h = g * HEADS_PER_GROUP + jp * 2 + hh
                seg = a_cum[:, h:h + 1] - a_cum_t[h:h + 1, :]
                dec = jnp.exp(seg + causal_bias)
                ms.append((cbm * dec * dt_t[h:h + 1, :]).astype(BF16))
            lhs = jnp.concatenate(ms, axis=1)
            c0 = g * GROUP_DIM + jp * LANES
            xp = xs_b[:, c0:c0 + LANES]
            zero = jnp.zeros_like(xp)
            rhs = jnp.concatenate([jnp.where(low_half, xp, zero), jnp.where(low_half, zero, xp)], axis=0)
            pairs.append(jnp.dot(lhs, rhs, preferred_element_type=F32))
        y_diag = jnp.concatenate(pairs, axis=1)

        yg = y_diag + y_off + dskip_ref[:, gsl] * xs[:, gsl]
        zg = z_ref[rs, gsl].astype(F32)
        yg = yg * (zg * jax.nn.sigmoid(zg))
        yg = yg * lax.rsqrt(jnp.mean(yg * yg, axis=-1, keepdims=True) + EPS) * ng_ref[:, gsl]
        y_groups.append(yg.astype(BF16))
    yssd_ref[rs, :] = jnp.concatenate(y_groups, axis=1)

    pos = chunk_index * CHUNK + lax.broadcasted_iota(jnp.int32, (CHUNK, 1), 0)
    outs = []
    for gi, w in enumerate(POOL_WINDOWS):
        psl = slice(gi * POOL_GROUP_DIM, (gi + 1) * POOL_GROUP_DIM)
        s = jnp.dot(band_ref[gi], extu_sc[r0:r0 + HALO + CHUNK, psl], preferred_element_type=F32)
        cnt = jnp.minimum(pos + 1, w).astype(F32)
        pooled = s / cnt - u_ref[rs, psl].astype(F32)
        outs.append(jnp.dot(pooled.astype(BF16), pw_ref[gi], preferred_element_type=F32))
    ypool_ref[rs, :] = (jnp.concatenate(outs, axis=1) * ps_ref[...]).astype(BF16)


def _mixers(proj, dt_raw, bsz, seq, cw, cb, dtb, alog, dskip, ng, ltri3, e2, pw, ps):
    rows = MIX_CHUNKS * CHUNK
    nc = seq // rows
    t = bsz * seq
    rowmap = lambda b, c: b * nc + c
    const2 = lambda b, c: (0, 0)
    const3 = lambda b, c: (0, 0, 0)
    trow = jnp.arange(CHUNK)[:, None] + HALO
    jcol = jnp.arange(HALO + CHUNK)[None, :]
    shifts = jnp.stack([(jcol == trow - (CONV_WIDTH - 1) + k) for k in range(CONV_WIDTH - 1)]).astype(BF16)
    bands = jnp.stack([(jcol <= trow) & (jcol > trow - w) for w in POOL_WINDOWS]).astype(BF16)
    return pl.pallas_call(
        _mixers_kernel,
        grid=(bsz, nc),
        in_specs=[
            pl.BlockSpec((rows, D_INNER), lambda b, c: (rowmap(b, c), 0)),
            pl.BlockSpec((rows, D_INNER), lambda b, c: (rowmap(b, c), 1)),
            pl.BlockSpec((rows, D_BC), lambda b, c: (rowmap(b, c), 4)),
            pl.BlockSpec((rows, POOL_WIDTH), lambda b, c: (rowmap(b, c), 5)),
            pl.BlockSpec((rows, LANES), lambda b, c: (rowmap(b, c), 0)),
            pl.BlockSpec((CONV_WIDTH, D_CONV), const2),
            pl.BlockSpec((1, D_CONV), const2),
            pl.BlockSpec((1, LANES), const2),
            pl.BlockSpec((1, LANES), const2),
            pl.BlockSpec((1, D_INNER), const2),
            pl.BlockSpec((1, D_INNER), const2),
            pl.BlockSpec((CHUNK, 3 * CHUNK), const2),
            pl.BlockSpec((2 * LANES, D_INNER), const2),
            pl.BlockSpec((CONV_WIDTH - 1, CHUNK, HALO + CHUNK), const3),
            pl.BlockSpec((len(POOL_WINDOWS), CHUNK, HALO + CHUNK), const3),
            pl.BlockSpec((len(POOL_WINDOWS), POOL_GROUP_DIM, POOL_GROUP_DIM), const3),
            pl.BlockSpec((1, POOL_WIDTH), const2),
        ],
        out_specs=[
            pl.BlockSpec((rows, D_INNER), lambda b, c: (rowmap(b, c), 0)),
            pl.BlockSpec((rows, POOL_WIDTH), lambda b, c: (rowmap(b, c), 0)),
        ],
        out_shape=[
            jax.ShapeDtypeStruct((t, D_INNER), BF16),
            jax.ShapeDtypeStruct((t, POOL_WIDTH), BF16),
        ],
        scratch_shapes=[
            pltpu.VMEM((HALO + rows, D_CONV), BF16),
            pltpu.VMEM((HALO + rows, POOL_WIDTH), BF16),
            pltpu.VMEM((GROUPS, D_STATE, GROUP_DIM), F32),
        ],
        compiler_params=pltpu.CompilerParams(
            dimension_semantics=("arbitrary", "arbitrary"), vmem_limit_bytes=VMEM_LIMIT),
        name="mixers",
    )(proj, proj, proj, proj, dt_raw, cw, cb, dtb, alog, dskip, ng, ltri3, e2, shifts, bands, pw, ps)


def _mix_route_kernel(x_ref, yssd_ref, ypool_ref, gates_ref, wso_ref, wmo_ref, fg_ref, wr_ref, br_ref,
                      x1_ref, h2_ref, topw_ref, topi_ref):
    for r0 in range(0, x_ref.shape[0], ROUTE_SUB):
        _mix_route_rows(slice(r0, r0 + ROUTE_SUB), x_ref, yssd_ref, ypool_ref, gates_ref, wso_ref, wmo_ref, fg_ref,
                        wr_ref, br_ref, x1_ref, h2_ref, topw_ref, topi_ref)


def _mix_route_rows(rs, x_ref, yssd_ref, ypool_ref, gates_ref, wso_ref, wmo_ref, fg_ref, wr_ref, br_ref,
                    x1_ref, h2_ref, topw_ref, topi_ref):
    y_ssd = jnp.dot(yssd_ref[rs, :], wso_ref[...], preferred_element_type=F32)
    gates = jax.nn.sigmoid(gates_ref[rs, :].astype(F32))
    mixed = gates[:, 0:D_MODEL] * y_ssd + gates[:, D_MODEL:2 * D_MODEL] * ypool_ref[rs, :].astype(F32)
    x1 = x_ref[rs, :] + jnp.dot(mixed.astype(BF16), wmo_ref[...], preferred_element_type=F32)
    x1_ref[rs, :] = x1
    h2 = _rms(x1, fg_ref[...])
    _store_token_tiles(h2_ref, h2, rs.start)

    hi, lo = _split2(h2)
    logits = _hilo_dot(hi, lo, wr_ref, N_EXPERTS) + br_ref[...]
    tm = logits.shape[0]
    lane = lax.broadcasted_iota(jnp.int32, (tm, LANES), 1)
    neg = jnp.float32(-jnp.inf)
    work = jnp.where(lane < N_EXPERTS, logits, neg)
    vals = []
    idxs = []
    for _ in range(TOP_K):
        m = jnp.max(work, axis=-1, keepdims=True)
        idx = jnp.min(jnp.where(work == m, lane, LANES), axis=-1, keepdims=True)
        vals.append(m)
        idxs.append(idx)
        work = jnp.where(lane == idx, neg, work)
    es = [jnp.exp(v - vals[0]) for v in vals]
    den = es[0] + es[1] + es[2] + es[3]
    topw = jnp.zeros((tm, LANES), F32)
    topi = jnp.zeros((tm, LANES), jnp.int32)
    for k in range(TOP_K):
        topw = jnp.where(lane == k, es[k] / den, topw)
        topi = jnp.where(lane == k, idxs[k], topi)
    topw_ref[rs, :] = topw
    topi_ref[:, rs] = topi.T[0:TOPI_ROWS, :]


def _mix_route(x2d, yssd, ypool, proj, wso, wmo, fg, wr2, br, tm=512):
    t = x2d.shape[0]
    const = lambda i: (0, 0)
    return pl.pallas_call(
        _mix_route_kernel,
        grid=(t // tm,),
        in_specs=[
            pl.BlockSpec((tm, D_MODEL), lambda i: (i, 0)),
            pl.BlockSpec((tm, D_INNER), lambda i: (i, 0)),
            pl.BlockSpec((tm, POOL_WIDTH), lambda i: (i, 0)),
            pl.BlockSpec((tm, 2 * D_MODEL), lambda i: (i, 3)),
            pl.BlockSpec((D_INNER, D_MODEL), const),
            pl.BlockSpec((D_MODEL, D_MODEL), const),
            pl.BlockSpec((1, D_MODEL), const),
            pl.BlockSpec((2 * D_MODEL, LANES), const),
            pl.BlockSpec((1, LANES), const),
        ],
        out_specs=[
            pl.BlockSpec((tm, D_MODEL), lambda i: (i, 0)),
            pl.BlockSpec((tm * TILE_ROWS, LANES), lambda i: (i, 0)),
            pl.BlockSpec((tm, LANES), lambda i: (i, 0)),
            pl.BlockSpec((TOPI_ROWS, tm), lambda i: (0, i)),
        ],
        out_shape=[
            jax.ShapeDtypeStruct((t, D_MODEL), F32),
            jax.ShapeDtypeStruct((t * TILE_ROWS, LANES), jnp.uint32),
            jax.ShapeDtypeStruct((t, LANES), F32),
            jax.ShapeDtypeStruct((TOPI_ROWS, t), jnp.int32),
        ],
        compiler_params=pltpu.CompilerParams(
            dimension_semantics=("arbitrary",), vmem_limit_bytes=VMEM_LIMIT,
            allow_input_fusion=[False, False, False, False, True, True, False, False, False]),
        name="mix_route",
    )(x2d, yssd, ypool, proj, wso, wmo, fg, wr2, br)


def _moe_kernel(be_ref, nvalid_ref,
                tok0_ref, toknext_ref, dstprev_ref,
                h_hbm, wgu_ref, bgu_ref, wd_ref, bd_ref,
                slots_hbm,
                xbuf, ybuf, xb_sc, wgu_bf, wd_bf, gsem, ssem, *, nb):
    i = pl.program_id(0)
    nvalid = nvalid_ref[0]
    slot = lax.rem(i, 2)
    has_next = i + 1 < nvalid

    def tile(off, n=1):
        return pl.ds(pl.multiple_of(off, TILE_ROWS), n * TILE_ROWS)

    def gather_copy(tok, r, s):
        return pltpu.make_async_copy(h_hbm.at[tile(tok)], xbuf.at[s, tile(r * TILE_ROWS)], gsem.at[s])

    def scatter_copy(dst, r, s):
        return pltpu.make_async_copy(ybuf.at[s, tile(r * TILE_ROWS)], slots_hbm.at[tile(dst)], ssem.at[s])

    def wait_gather(s):
        pltpu.make_async_copy(h_hbm.at[tile(0, MOE_ROWS)], xbuf.at[s, tile(0, MOE_ROWS)], gsem.at[s]).wait()

    def wait_scatter(s):
        pltpu.make_async_copy(ybuf.at[s, tile(0, MOE_ROWS)], slots_hbm.at[tile(0, MOE_ROWS)], ssem.at[s]).wait()

    def scatter_whole_block(s):
        pltpu.make_async_copy(ybuf.at[s, tile(0, MOE_ROWS)], slots_hbm.at[tile(dstprev_ref[0, 0, 0], MOE_ROWS)],
                              ssem.at[s]).start()

    def issue(s, gather_ref, scatter_ref):
        for r in range(MOE_ROWS):
            if gather_ref is not None:
                gather_copy(gather_ref[0, 0, r], r, s).start()
            if scatter_ref is not None:
                scatter_copy(scatter_ref[0, 0, r], r, s).start()

    @pl.when(i == 0)
    def _():
        ybuf[...] = jnp.zeros_like(ybuf)
        issue(0, tok0_ref, None)

    for par in (0, 1):
        other = 1 - par
        mine = slot == par

        pl.when(jnp.logical_and(mine, i >= 2))(functools.partial(wait_scatter, par))
        pl.when(jnp.logical_and(mine, i < nvalid))(functools.partial(wait_gather, par))
        pl.when(jnp.logical_and(mine, jnp.logical_and(has_next, i == 0)))(
            functools.partial(issue, other, toknext_ref, None))
        pl.when(jnp.logical_and(mine, jnp.logical_and(has_next, i >= 1)))(
            functools.partial(issue, other, toknext_ref, dstprev_ref))
        pl.when(jnp.logical_and(mine, jnp.logical_and(jnp.logical_not(has_next), jnp.logical_and(i >= 1, i <= nvalid))))(
            functools.partial(issue, other, None, dstprev_ref))
        pl.when(jnp.logical_and(mine, i > nvalid))(functools.partial(scatter_whole_block, other))
        pl.when(jnp.logical_and(mine, i == nb))(functools.partial(wait_scatter, other))

    @pl.when(i < nvalid)
    def _():
        @pl.when(jnp.logical_or(i == 0, be_ref[i] != be_ref[jnp.maximum(i - 1, 0)]))
        def _():
            wgu_bf[...] = wgu_ref[0].astype(BF16)
            wd_bf[...] = wd_ref[0].astype(BF16)

        xb_sc[...] = _load_token_tiles(xbuf.at[slot], MOE_ROWS).astype(BF16)
        gu = jnp.dot(xb_sc[...], wgu_bf[...], preferred_element_type=F32) + bgu_ref[0]
        gate = jnp.minimum(gu[:, 0:D_EXPERT], SWIGLU_LIMIT)
        up = jnp.clip(gu[:, D_EXPERT:2 * D_EXPERT], -SWIGLU_LIMIT, SWIGLU_LIMIT)
        act = (up + 1.0) * gate * jax.nn.sigmoid(SWIGLU_ALPHA * gate)
        y = jnp.dot(act.astype(BF16), wd_bf[...], preferred_element_type=F32) + bd_ref[0]
        _store_token_tiles(ybuf.at[slot], y)


def _moe(block_expert, nvalid, row_token3, row_dest3, h2, wgu, bgu3, wd, bd3, n_slot_rows):
    nb = block_expert.shape[0]
    expert_of = lambda i, be: be[jnp.minimum(i, nb - 1)]
    grid_spec = pltpu.PrefetchScalarGridSpec(
        num_scalar_prefetch=2,
        grid=(nb + 1,),
        in_specs=[
            pl.BlockSpec((1, 1, MOE_ROWS), lambda i, be, nv: (0, 0, 0), memory_space=pltpu.SMEM),
            pl.BlockSpec((1, 1, MOE_ROWS), lambda i, be, nv: (jnp.minimum(i + 1, nb - 1), 0, 0),
                         memory_space=pltpu.SMEM),
            pl.BlockSpec((1, 1, MOE_ROWS), lambda i, be, nv: (jnp.clip(i - 1, 0, nb - 1), 0, 0),
                         memory_space=pltpu.SMEM),
            pl.BlockSpec(memory_space=pl.ANY),
            pl.BlockSpec((1, D_MODEL, 2 * D_EXPERT), lambda i, be, nv: (expert_of(i, be), 0, 0)),
            pl.BlockSpec((1, 1, 2 * D_EXPERT), lambda i, be, nv: (expert_of(i, be), 0, 0)),
            pl.BlockSpec((1, D_EXPERT, D_MODEL), lambda i, be, nv: (expert_of(i, be), 0, 0)),
            pl.BlockSpec((1, 1, D_MODEL), lambda i, be, nv: (expert_of(i, be), 0, 0)),
        ],
        out_specs=pl.BlockSpec(memory_space=pl.ANY),
        scratch_shapes=[
            pltpu.VMEM((2, MOE_ROWS * TILE_ROWS, LANES), jnp.uint32),
            pltpu.VMEM((2, MOE_ROWS * TILE_ROWS, LANES), jnp.uint32),
            pltpu.VMEM((MOE_ROWS, D_MODEL), BF16),
            pltpu.VMEM((D_MODEL, 2 * D_EXPERT), BF16),
            pltpu.VMEM((D_EXPERT, D_MODEL), BF16),
            pltpu.SemaphoreType.DMA((2,)),
            pltpu.SemaphoreType.DMA((2,)),
        ],
    )
    return pl.pallas_call(
        functools.partial(_moe_kernel, nb=nb),
        grid_spec=grid_spec,
        out_shape=jax.ShapeDtypeStruct((n_slot_rows * TILE_ROWS, LANES), jnp.uint32),
        compiler_params=pltpu.CompilerParams(
            dimension_semantics=("arbitrary",), vmem_limit_bytes=VMEM_LIMIT),
        name="moe_experts",
    )(block_expert, nvalid, row_token3, row_token3, row_dest3, h2, wgu, bgu3, wd, bd3)


def _combine_kernel(x1_ref, s0_ref, s1_ref, s2_ref, s3_ref, topw_ref, p_ref, pg_ref, wpg_ref, wpp_ref, fg_ref,
                    out_ref):
    x2 = x1_ref[...]
    topw = topw_ref[...]
    for k, s_ref in enumerate((s0_ref, s1_ref, s2_ref, s3_ref)):
        x2 = x2 + _load_token_tiles(s_ref, x2.shape[0]) * topw[:, k:k + 1]
    n = _rms(x2, pg_ref[...])
    gate = jax.nn.sigmoid(jnp.dot(n.astype(BF16), wpg_ref[...], preferred_element_type=F32))
    pp = jnp.dot(p_ref[...].astype(BF16), wpp_ref[...], preferred_element_type=F32)
    x3 = x2 + gate * pp
    out_ref[...] = _rms(x3, fg_ref[...])


def _combine(x1, slots, topw, p2d, pg, wpg, wpp, fg, tm=512):
    t = x1.shape[0]
    nt = t // tm
    const = lambda i: (0, 0)
    slot_specs = [pl.BlockSpec((tm * TILE_ROWS, LANES), functools.partial(lambda k, i: (k * nt + i, 0), k))
                  for k in range(TOP_K)]
    return pl.pallas_call(
        _combine_kernel,
        grid=(nt,),
        in_specs=[
            pl.BlockSpec((tm, D_MODEL), lambda i: (i, 0)),
            *slot_specs,
            pl.BlockSpec((tm, LANES), lambda i: (i, 0)),
            pl.BlockSpec((tm, D_PLE), lambda i: (i, 0)),
            pl.BlockSpec((1, D_MODEL), const),
            pl.BlockSpec((D_MODEL, D_MODEL), const),
            pl.BlockSpec((D_PLE, D_MODEL), const),
            pl.BlockSpec((1, D_MODEL), const),
        ],
        out_specs=pl.BlockSpec((tm, D_MODEL), lambda i: (i, 0)),
        out_shape=jax.ShapeDtypeStruct((t, D_MODEL), F32),
        compiler_params=pltpu.CompilerParams(
            dimension_semantics=("arbitrary",), vmem_limit_bytes=VMEM_LIMIT,
            allow_input_fusion=[False] * 8 + [True, True, False]),
        name="combine_ple",
    )(x1, slots, slots, slots, slots, topw, p2d, pg, wpg, wpp, fg)


def _routing_tables(top_idx_t, n_tok):
    n_assign = n_tok * TOP_K
    expert_flat = top_idx_t.reshape(-1)
    order = jnp.argsort(expert_flat, stable=True).astype(jnp.int32)
    counts = jnp.bincount(expert_flat, length=N_EXPERTS).astype(jnp.int32)
    start = jnp.cumsum(counts) - counts
    padded = (counts + MOE_ROWS - 1) // MOE_ROWS * MOE_ROWS
    pend = jnp.cumsum(padded)
    pstart = pend - padded
    n_rows = n_assign + N_EXPERTS * MOE_ROWS
    n_blocks = n_rows // MOE_ROWS
    block_start = jnp.arange(n_blocks, dtype=jnp.int32) * MOE_ROWS
    block_expert = jnp.minimum(jnp.sum(block_start[:, None] >= pend[None, :], axis=1),
                               N_EXPERTS - 1).astype(jnp.int32)
    nvalid = (pend[-1] // MOE_ROWS).astype(jnp.int32).reshape(1)
    is_e = block_expert[:, None] == jnp.arange(N_EXPERTS, dtype=jnp.int32)[None, :]
    per_block = lambda v: jnp.sum(jnp.where(is_e, v[None, :], 0), axis=1)
    nreal = jnp.clip(per_block(pstart + counts) - block_start, 0, MOE_ROWS).astype(jnp.int32)
    sorted_pos = (block_start + per_block(start - pstart))[:, None] + jnp.arange(MOE_ROWS, dtype=jnp.int32)[None, :]
    assign = order[jnp.clip(sorted_pos, 0, n_assign - 1)]
    row_token = assign % n_tok
    is_real = jnp.arange(MOE_ROWS, dtype=jnp.int32)[None, :] < nreal[:, None]
    row_q = block_start[:, None] + jnp.arange(MOE_ROWS, dtype=jnp.int32)[None, :]
    spare = n_assign + row_q - per_block(start + counts)[:, None]
    row_dest = jnp.where(is_real, assign, spare)
    return (block_expert, nvalid, (row_token * TILE_ROWS).reshape(n_blocks, 1, MOE_ROWS),
            (row_dest * TILE_ROWS).reshape(n_blocks, 1, MOE_ROWS), n_rows)


def _layer(x2d, p2d, bsz, seq, mix_norm_g, w_in, conv_w, conv_b, dt_bias, a_log, d_skip, ssd_norm_g,
           w_ssd_out, pool_w, pool_scale, w_mix_out, ffn_norm_g, w_router, b_router,
           w_gate_up, b_gate_up, w_down, b_down, ple_norm_g, w_ple_gate, w_ple_proj, out_g):
    n_tok = x2d.shape[0]
    dt0 = D_INNER + D_CONV
    w_main = jnp.concatenate([w_in[:, :dt0], w_in[:, dt0 + HEADS:]], axis=1).astype(BF16)
    proj, dt_raw = _in_proj(x2d, mix_norm_g[None, :], w_main, _hilo_weight(w_in[:, dt0:dt0 + HEADS]))

    pad_h = lambda v: jnp.pad(v, (0, LANES - HEADS))[None, :]
    ltri = (jnp.arange(CHUNK)[:, None] >= jnp.arange(CHUNK)[None, :]).astype(BF16)
    ltri3 = jnp.concatenate([ltri, ltri, ltri], axis=1)
    e1 = (jnp.arange(LANES)[:, None] == (jnp.arange(D_INNER) // HEAD_DIM)[None, :]).astype(BF16)
    e2 = jnp.concatenate([e1, e1], axis=0)
    yssd, ypool = _mixers(
        proj, dt_raw, bsz, seq, conv_w, conv_b[None, :], pad_h(dt_bias), pad_h(a_log),
        jnp.repeat(d_skip, HEAD_DIM)[None, :], ssd_norm_g[None, :], ltri3, e2,
        pool_w.astype(BF16), pool_scale[None, :])

    br = jnp.pad(b_router, (0, LANES - N_EXPERTS))[None, :]
    x1, h2, topw, topi = _mix_route(x2d, yssd, ypool, proj, w_ssd_out.astype(BF16),
                                    w_mix_out.astype(BF16), ffn_norm_g[None, :], _hilo_weight(w_router), br)

    block_expert, nvalid, row_token3, row_dest3, n_slot_rows = _routing_tables(topi[:TOP_K], n_tok)
    slots = _moe(block_expert, nvalid, row_token3, row_dest3, h2, w_gate_up, b_gate_up[:, None, :],
                 w_down, b_down[:, None, :], n_slot_rows)

    return _combine(x1, slots, topw, p2d, ple_norm_g[None, :], w_ple_gate.astype(BF16),
                    w_ple_proj.astype(BF16), out_g[None, :])


def kernel(x, p, mix_norm_g, w_in, conv_w, conv_b, dt_bias, a_log, d_skip, ssd_norm_g, w_ssd_out, pool_w,
           pool_scale, w_mix_out, ffn_norm_g, w_router, b_router, w_gate_up, b_gate_up, w_down, b_down,
           ple_norm_g, w_ple_gate, w_ple_proj, final_norm_g):
    bsz, seq, d = x.shape
    depth = p.shape[0]
    assert depth == 1 and d == D_MODEL and p.shape[-1] == D_PLE
    assert seq % (MIX_CHUNKS * CHUNK) == 0 and (bsz * seq) % 1024 == 0
    x2d = x.reshape(bsz * seq, d)
    out = _layer(x2d, p[0].reshape(bsz * seq, D_PLE), bsz, seq, mix_norm_g[0], w_in[0], conv_w[0], conv_b[0],
                 dt_bias[0], a_log[0], d_skip[0], ssd_norm_g[0], w_ssd_out[0], pool_w[0], pool_scale[0],
                 w_mix_out[0], ffn_norm_g[0], w_router[0], b_router[0], w_gate_up[0], b_gate_up[0],
                 w_down[0], b_down[0], ple_norm_g[0], w_ple_gate[0], w_ple_proj[0], final_norm_g)
    return out.reshape(bsz, seq, d)
```
